```python
import jax, jax.numpy as jnp
from jax import lax
import numpy as np

D_MODEL = 2048
BATCH = 16
SEQ = 256
DEPTH = 4
DEC_BATCH = 2
DEC_SEQ = 1024
PAST_LEN = 512

GRID_W = 64
N_MIXERS = 3
HEAD_DIM = 128
EPS = 1e-6
NEG_INF = -1e30
ROPE_THETA = 10000.0
WIN_HEADS = D_MODEL // HEAD_DIM
WIN_KV_HEADS = WIN_HEADS // 4
WIN_WIDTH = WIN_HEADS * HEAD_DIM
WIN_KV_WIDTH = WIN_KV_HEADS * HEAD_DIM
WINDOW = 128
BLOCK = 128
NAT_HEADS = D_MODEL // HEAD_DIM
NAT_WIDTH = NAT_HEADS * HEAD_DIM
NAT_ROWS = 8
NAT_COLS = 16
NAT_QCOLS = 16
NAT_KCOLS = NAT_QCOLS + NAT_COLS
GMLP_WIDTH = 2 * D_MODEL
GMLP_GROUPS = 16
CHUNK = 128

kernel_name = 'hybrid_diffusion_prefix_step'


def _n_of_kind(kind):
    return len(range(kind, DEPTH, N_MIXERS))


def _rmsnorm(x, g):
    xf = x.astype(jnp.float32)
    y = xf * lax.rsqrt(jnp.mean(xf * xf, axis=-1, keepdims=True) + EPS)
    return (y * g.astype(jnp.float32)).astype(x.dtype)


def _adaln(cond, w, b):
    m = jax.nn.silu(cond) @ w + b
    return jnp.split(m[:, None, :], 3, axis=-1)


def _axial_rope(x):
    b, s, h, dh = x.shape
    nf = dh // 4
    t = jnp.arange(s)
    row = (t // GRID_W).astype(jnp.float32)
    col = (t % GRID_W).astype(jnp.float32)
    inv = ROPE_THETA ** (-jnp.arange(nf, dtype=jnp.float32) / nf)
    ang = jnp.stack([row[:, None] * inv, col[:, None] * inv], axis=1)
    cos = jnp.cos(ang)[None, :, None]
    sin = jnp.sin(ang)[None, :, None]
    xs = x.astype(jnp.float32).reshape(b, s, h, 2, 2, nf)
    x1, x2 = xs[..., 0, :], xs[..., 1, :]
    out = jnp.stack([x1 * cos - x2 * sin, x2 * cos + x1 * sin], axis=-2)
    return out.reshape(b, s, h, dh).astype(x.dtype)


def _dense_ctx_attn(q, k, v, sink):
    b, p, h, dh = q.shape
    kvh = k.shape[2]
    g = h // kvh
    nb = p // BLOCK
    scale = dh ** -0.5
    qb = q.reshape(b, nb, BLOCK, kvh, g, dh).transpose(1, 0, 2, 3, 4, 5)

    def one_block(qblk):
        s = jnp.einsum('bqkgd,bpkd->bkgqp', qblk, k).astype(jnp.float32) * scale
        if sink is not None:
            sk = jnp.broadcast_to(sink.astype(jnp.float32).reshape(kvh, g, 1, 1), s.shape[:-1] + (1,))
            pr = jax.nn.softmax(jnp.concatenate([s, sk], axis=-1), axis=-1)[..., :-1]
        else:
            pr = jax.nn.softmax(s, axis=-1)
        return jnp.einsum('bkgqp,bpkd->bqkgd', pr.astype(v.dtype), v)

    o = lax.map(one_block, qb)
    return o.transpose(1, 0, 2, 3, 4, 5).reshape(b, p, h * dh)


def _window_attn(q, k, v, kc, vc, sink):
    b, s, h, dh = q.shape
    kvh = k.shape[2]
    g = h // kvh
    nb = s // BLOCK
    p = kc.shape[1]
    scale = dh ** -0.5
    pad = ((0, 0), (BLOCK, BLOCK), (0, 0), (0, 0))
    kp = jnp.pad(k, pad).reshape(b, nb + 2, BLOCK, kvh, dh)
    vp = jnp.pad(v, pad).reshape(b, nb + 2, BLOCK, kvh, dh)
    kb = jnp.concatenate([kp[:, 0:nb], kp[:, 1:nb + 1], kp[:, 2:nb + 2]], axis=2)
    vb = jnp.concatenate([vp[:, 0:nb], vp[:, 1:nb + 1], vp[:, 2:nb + 2]], axis=2)
    n = jnp.arange(nb)
    qpos = n[:, None] * BLOCK + jnp.arange(BLOCK)[None]
    kpos = n[:, None] * BLOCK - BLOCK + jnp.arange(3 * BLOCK)[None]
    diff = kpos[:, None, :] - qpos[:, :, None]
    valid = (jnp.abs(diff) <= WINDOW) & (kpos[:, None, :] >= 0) & (kpos[:, None, :] < s)
    qb = q.reshape(b, nb, BLOCK, kvh, g, dh)
    s_band = jnp.einsum('bnqkgd,bnjkd->bnkgqj', qb, kb).astype(jnp.float32) * scale
    s_band = jnp.where(valid[None, :, None, None], s_band, NEG_INF)
    s_ctx = jnp.einsum('bnqkgd,bpkd->bnkgqp', qb, kc).astype(jnp.float32) * scale
    sk = jnp.broadcast_to(sink.astype(jnp.float32).reshape(1, 1, kvh, g, 1, 1), s_band.shape[:-1] + (1,))
    pr = jax.nn.softmax(jnp.concatenate([s_band, s_ctx, sk], axis=-1), axis=-1)
    nw = 3 * BLOCK
    p_band = pr[..., :nw].astype(v.dtype)
    p_ctx = pr[..., nw:nw + p].astype(v.dtype)
    o = jnp.einsum('bnkgqj,bnjkd->bnqkgd', p_band, vb) + jnp.einsum('bnkgqp,bpkd->bnqkgd', p_ctx, vc)
    return o.reshape(b, s, h * dh)


def _neigh_attn(q, k, v, kc, vc, rel_bias):
    b, s, h, dh = q.shape
    rows = s // GRID_W
    kr = min(NAT_ROWS, rows)
    nj = GRID_W // NAT_QCOLS
    scale = dh ** -0.5
    r = jnp.arange(rows)
    rs = jnp.clip(r - kr // 2, 0, rows - kr)
    row_idx = rs[:, None] + jnp.arange(kr)[None]
    j = jnp.arange(nj)
    cstart = jnp.clip(j * NAT_QCOLS - NAT_COLS // 2, 0, GRID_W - NAT_KCOLS)
    col_idx = cstart[:, None] + jnp.arange(NAT_KCOLS)[None]
    qcol = j[:, None] * NAT_QCOLS + jnp.arange(NAT_QCOLS)[None]
    cs = jnp.clip(qcol - NAT_COLS // 2, 0, GRID_W - NAT_COLS)
    kcol = col_idx[:, None, :]
    valid = (kcol >= cs[:, :, None]) & (kcol < cs[:, :, None] + NAT_COLS)
    dr = row_idx - r[:, None]
    dc = kcol - qcol[:, :, None]
    bias = rel_bias[:, (dr + NAT_ROWS - 1)[:, None, None, :, None],
                    jnp.clip(dc + NAT_COLS - 1, 0, 2 * NAT_COLS - 2)[None, :, :, None, :]]
    bias = bias.astype(jnp.float32).transpose(1, 2, 0, 3, 4, 5)
    bias = bias + jnp.where(valid, 0.0, NEG_INF)[None, :, None, :, None, :]
    bias = bias.reshape(rows, nj, h, NAT_QCOLS, kr * NAT_KCOLS)
    kg = k.reshape(b, rows, GRID_W, h, dh)
    vg = v.reshape(b, rows, GRID_W, h, dh)
    ri = row_idx[:, None, :, None]
    ci = col_idx[None, :, None, :]
    kn = kg[:, ri, ci].reshape(b, rows, nj, kr * NAT_KCOLS, h, dh)
    vn = vg[:, ri, ci].reshape(b, rows, nj, kr * NAT_KCOLS, h, dh)
    qn = q.reshape(b, rows, nj, NAT_QCOLS, h, dh)
    s_nb = jnp.einsum('brjqhd,brjkhd->brjhqk', qn, kn).astype(jnp.float32) * scale + bias[None]
    s_ctx = jnp.einsum('brjqhd,bphd->brjhqp', qn, kc).astype(jnp.float32) * scale
    pr = jax.nn.softmax(jnp.concatenate([s_nb, s_ctx], axis=-1), axis=-1)
    nk = kr * NAT_KCOLS
    o = (jnp.einsum('brjhqk,brjkhd->brjqhd', pr[..., :nk].astype(v.dtype), vn)
         + jnp.einsum('brjhqp,bphd->brjqhd', pr[..., nk:].astype(v.dtype), vc))
    return o.reshape(b, s, h * dh)


def _win_qkvz(hn, w_in, qg, kg, rope):
    b, s, _ = hn.shape
    q, k, v, z = jnp.split(hn @ w_in, [WIN_WIDTH, WIN_WIDTH + WIN_KV_WIDTH, WIN_WIDTH + 2 * WIN_KV_WIDTH], axis=-1)
    q = _rmsnorm(q.reshape(b, s, WIN_HEADS, HEAD_DIM), qg)
    k = _rmsnorm(k.reshape(b, s, WIN_KV_HEADS, HEAD_DIM), kg)
    v = v.reshape(b, s, WIN_KV_HEADS, HEAD_DIM)
    if rope:
        q, k = _axial_rope(q), _axial_rope(k)
    return q, k, v, z


def _nat_qkvz(hn, w_in, qg, kg):
    b, s, _ = hn.shape
    q, k, v, z = jnp.split(hn @ w_in, [NAT_WIDTH, 2 * NAT_WIDTH, 3 * NAT_WIDTH], axis=-1)
    q = _rmsnorm(q.reshape(b, s, NAT_HEADS, HEAD_DIM), qg)
    k = _rmsnorm(k.reshape(b, s, NAT_HEADS, HEAD_DIM), kg)
    v = v.reshape(b, s, NAT_HEADS, HEAD_DIM)
    return q, k, v, z


def _gmlp_branch(hn, w_in, ln_g, ln_b, w_s, b_s, w_out):
    b, s, _ = hn.shape
    nc = s // CHUNK
    proj = hn @ w_in
    uv = jax.nn.gelu(proj[..., :2 * GMLP_WIDTH])
    z = proj[..., 2 * GMLP_WIDTH:]
    u, v = jnp.split(uv, 2, axis=-1)
    vf = v.astype(jnp.float32)
    mu = jnp.mean(vf, axis=-1, keepdims=True)
    var = jnp.mean(jnp.square(vf - mu), axis=-1, keepdims=True)
    vn = ((vf - mu) * lax.rsqrt(var + EPS) * ln_g.astype(jnp.float32) + ln_b.astype(jnp.float32)).astype(v.dtype)
    vg = vn.reshape(b, nc, CHUNK, GMLP_GROUPS, GMLP_WIDTH // GMLP_GROUPS)
    sv = jnp.einsum('gij,bnjgc->bnigc', w_s, vg) + b_s.T[None, None, :, :, None]
    return (u * sv.reshape(b, s, GMLP_WIDTH) * jax.nn.silu(z)) @ w_out


def setup_inputs(seed: int = 0) -> dict:
    key = jax.random.key(seed)
    ks = iter(jax.random.split(key, 32))
    nrm = lambda shape: jax.random.normal(next(ks), shape, jnp.float32)
    n_win, n_nat, n_gm = _n_of_kind(0), _n_of_kind(1), _n_of_kind(2)
    d = D_MODEL
    return {
        'x_prompt': nrm((BATCH, SEQ, d)),
        'x_sample': nrm((DEC_BATCH, DEC_SEQ, d)),
        'cache_win_k': nrm((DEC_BATCH, n_win, PAST_LEN, WIN_KV_HEADS, HEAD_DIM)),
        'cache_win_v': nrm((DEC_BATCH, n_win, PAST_LEN, WIN_KV_HEADS, HEAD_DIM)),
        'cache_nat_k': nrm((DEC_BATCH, n_nat, PAST_LEN, NAT_HEADS, HEAD_DIM)),
        'cache_nat_v': nrm((DEC_BATCH, n_nat, PAST_LEN, NAT_HEADS, HEAD_DIM)),
        'c': nrm((DEC_BATCH, d)),
        'c_ctx': nrm((d,)),
        'norm_g': 1.0 + 0.02 * nrm((DEPTH, d)),
        'w_ada': nrm((DEPTH, d, 3 * d)) * (0.2 * d ** -0.5),
        'b_ada': 0.02 * nrm((DEPTH, 3 * d)),
        'win_w_in': nrm((n_win, d, 2 * WIN_WIDTH + 2 * WIN_KV_WIDTH)) * d ** -0.5,
        'win_q_norm': 1.0 + 0.02 * nrm((n_win, HEAD_DIM)),
        'win_k_norm': 1.0 + 0.02 * nrm((n_win, HEAD_DIM)),
        'win_sink': nrm((n_win, WIN_HEADS)),
        'win_w_out': nrm((n_win, WIN_WIDTH, d)) * WIN_WIDTH ** -0.5,
        'nat_w_in': nrm((n_nat, d, 4 * NAT_WIDTH)) * d ** -0.5,
        'nat_q_norm': 1.0 + 0.02 * nrm((n_nat, HEAD_DIM)),
        'nat_k_norm': 1.0 + 0.02 * nrm((n_nat, HEAD_DIM)),
        'nat_rel_bias': 0.5 * nrm((n_nat, NAT_HEADS, 2 * NAT_ROWS - 1, 2 * NAT_COLS - 1)),
        'nat_w_out': nrm((n_nat, NAT_WIDTH, d)) * NAT_WIDTH ** -0.5,
        'gmlp_w_in': nrm((n_gm, d, 3 * GMLP_WIDTH)) * d ** -0.5,
        'gmlp_ln_g': 1.0 + 0.02 * nrm((n_gm, GMLP_WIDTH)),
        'gmlp_ln_b': 0.02 * nrm((n_gm, GMLP_WIDTH)),
        'gmlp_w_s': nrm((n_gm, GMLP_GROUPS, CHUNK, CHUNK)) * CHUNK ** -0.5,
        'gmlp_b_s': 1.0 + 0.02 * nrm((n_gm, GMLP_GROUPS, CHUNK)),
        'gmlp_w_out': nrm((n_gm, GMLP_WIDTH, d)) * GMLP_WIDTH ** -0.5,
    }


def reference(x_prompt, x_sample, cache_win_k, cache_win_v, cache_nat_k, cache_nat_v, c, c_ctx,
              norm_g, w_ada, b_ada,
              win_w_in, win_q_norm, win_k_norm, win_sink, win_w_out,
              nat_w_in, nat_q_norm, nat_k_norm, nat_rel_bias, nat_w_out,
              gmlp_w_in, gmlp_ln_g, gmlp_ln_b, gmlp_w_s, gmlp_b_s, gmlp_w_out):
    xp, xs = x_prompt, x_sample
    new_win_k, new_win_v, new_nat_k, new_nat_v = [], [], [], []
    for i in range(DEPTH):
        kind = i % N_MIXERS
        li = i // N_MIXERS
        sh_p, sc_p, gt_p = _adaln(c_ctx[None], w_ada[i], b_ada[i])
        sh_s, sc_s, gt_s = _adaln(c, w_ada[i], b_ada[i])
        hp = _rmsnorm(xp, norm_g[i]) * (1.0 + sc_p) + sh_p
        hs = _rmsnorm(xs, norm_g[i]) * (1.0 + sc_s) + sh_s
        if kind == 0:
            qp, kp, vp, zp = _win_qkvz(hp, win_w_in[li], win_q_norm[li], win_k_norm[li], False)
            yp = (_dense_ctx_attn(qp, kp, vp, win_sink[li]) * jax.nn.silu(zp)) @ win_w_out[li]
            new_win_k.append(kp)
            new_win_v.append(vp)
            qs, ks_, vs, zs = _win_qkvz(hs, win_w_in[li], win_q_norm[li], win_k_norm[li], True)
            ys = (_window_attn(qs, ks_, vs, cache_win_k[:, li], cache_win_v[:, li], win_sink[li])
                  * jax.nn.silu(zs)) @ win_w_out[li]
        elif kind == 1:
            qp, kp, vp, zp = _nat_qkvz(hp, nat_w_in[li], nat_q_norm[li], nat_k_norm[li])
            yp = (_dense_ctx_attn(qp, kp, vp, None) * jax.nn.silu(zp)) @ nat_w_out[li]
            new_nat_k.append(kp)
            new_nat_v.append(vp)
            qs, ks_, vs, zs = _nat_qkvz(hs, nat_w_in[li], nat_q_norm[li], nat_k_norm[li])
            ys = (_neigh_attn(qs, ks_, vs, cache_nat_k[:, li], cache_nat_v[:, li], nat_rel_bias[li])
                  * jax.nn.silu(zs)) @ nat_w_out[li]
        else:
            yp = _gmlp_branch(hp, gmlp_w_in[li], gmlp_ln_g[li], gmlp_ln_b[li], gmlp_w_s[li], gmlp_b_s[li], gmlp_w_out[li])
            ys = _gmlp_branch(hs, gmlp_w_in[li], gmlp_ln_g[li], gmlp_ln_b[li], gmlp_w_s[li], gmlp_b_s[li], gmlp_w_out[li])
        xp = xp + gt_p * yp
        xs = xs + gt_s * ys
    return (xp, xs, jnp.stack(new_win_k, axis=1), jnp.stack(new_win_v, axis=1),
            jnp.stack(new_nat_k, axis=1), jnp.stack(new_nat_v, axis=1))
```

```python
import functools
import math

import jax
import jax.numpy as jnp
from jax import lax
from jax.experimental import pallas as pl
from jax.experimental.pallas import tpu as pltpu

F32 = jnp.float32
BF16 = jnp.bfloat16

D_MODEL = 2048
HEAD_DIM = 128
N_PROMPT = 16
PROMPT_SEQ = 256
N_SAMPLE = 2
SAMPLE_SEQ = 1024
PROMPT_TOKENS = N_PROMPT * PROMPT_SEQ
SAMPLE_TOKENS = N_SAMPLE * SAMPLE_SEQ
TOKENS = PROMPT_TOKENS + SAMPLE_TOKENS
PAST_LEN = 512
GRID_W = 64
EPS = 1e-6
NEG_INF = -1e30
ROPE_THETA = 10000.0
WINDOW = 128
NAT_ROWS = 8
NAT_COLS = 16
GMLP_WIDTH = 2 * D_MODEL
GMLP_GROUPS = 16
GMLP_GROUP_WIDTH = GMLP_WIDTH // GMLP_GROUPS
CHUNK = 128
CTX_COND_ROW = 2
COND_ROWS = 8
SM_SCALE = HEAD_DIM ** -0.5

VMEM_LIMIT = 56 * 1024 * 1024


def _params(n_axes, vmem=VMEM_LIMIT):
    return pltpu.CompilerParams(dimension_semantics=("arbitrary",) * n_axes,
                                vmem_limit_bytes=vmem)


def _cond_row(tok0):
    return jnp.where(tok0 < PROMPT_TOKENS, CTX_COND_ROW, (tok0 - PROMPT_TOKENS) // SAMPLE_SEQ)


def _silu(x):
    return x * (1.0 / (1.0 + jnp.exp(-x)))


def _gelu_tanh(x):
    return 0.5 * x * (1.0 + jnp.tanh(math.sqrt(2.0 / math.pi) * (x + 0.044715 * (x * x * x))))


def _ada_kernel(cond_ref, w_ref, b_ref, o_ref):
    s = _silu(cond_ref[...]).astype(BF16)
    o_ref[...] = jnp.dot(s, w_ref[...].astype(BF16), preferred_element_type=F32) + b_ref[...]


def _ada(cond, w_ada, b_ada, tn=1024):
    depth = w_ada.shape[0]
    n = w_ada.shape[2]
    return pl.pallas_call(
        _ada_kernel,
        grid=(depth, n // tn),
        in_specs=[pl.BlockSpec((COND_ROWS, D_MODEL), lambda l, j: (0, 0)),
                  pl.BlockSpec((None, D_MODEL, tn), lambda l, j: (l, 0, j)),
                  pl.BlockSpec((None, 1, tn), lambda l, j: (l, 0, j))],
        out_specs=pl.BlockSpec((None, COND_ROWS, tn), lambda l, j: (l, 0, j)),
        out_shape=jax.ShapeDtypeStruct((depth, COND_ROWS, n), F32),
        compiler_params=_params(2),
        name="ada_mod",
    )(cond, w_ada, b_ada.reshape(depth, 1, n))


def _norm_kernel(x_ref, g_ref, shift_ref, scale_ref, o_ref):
    x = x_ref[...]
    y = x * lax.rsqrt(jnp.mean(x * x, axis=-1, keepdims=True) + EPS) * g_ref[...]
    o_ref[...] = (y * (1.0 + scale_ref[...]) + shift_ref[...]).astype(BF16)


def _norm_mod(x, norm_g3, mod4, layer, tm=512):
    return pl.pallas_call(
        _norm_kernel,
        grid=(TOKENS // tm,),
        in_specs=[pl.BlockSpec((tm, D_MODEL), lambda i: (i, 0)),
                  pl.BlockSpec((None, 1, D_MODEL), lambda i: (layer, 0, 0)),
                  pl.BlockSpec((None, None, 1, D_MODEL), lambda i: (layer, _cond_row(i * tm), 0, 0)),
                  pl.BlockSpec((None, None, 1, D_MODEL), lambda i: (layer, _cond_row(i * tm), 0, 1))],
        out_specs=pl.BlockSpec((tm, D_MODEL), lambda i: (i, 0)),
        out_shape=jax.ShapeDtypeStruct((TOKENS, D_MODEL), BF16),
        compiler_params=_params(1),
        name="norm_mod",
    )(x, norm_g3, mod4, mod4)


def _proj_kernel(*refs, epilogue, n_extra):
    a_ref, w_ref = refs[0], refs[1]
    extras = refs[2:2 + n_extra]
    o_ref = refs[2 + n_extra]
    wbf_ref = refs[3 + n_extra]

    @pl.when(pl.program_id(1) == 0)
    def _():
        wbf_ref[...] = w_ref[...].astype(BF16)

    acc = jnp.dot(a_ref[...], wbf_ref[...], preferred_element_type=F32)
    epilogue(acc, extras, o_ref)


def _proj(a, w, layer, col_blk0, n_blk, epilogue, out_dtype, name, extras=(), extra_specs=(),
          tm=1024, tn=512):
    m, k = a.shape
    kern = functools.partial(_proj_kernel, epilogue=epilogue, n_extra=len(extras))
    return pl.pallas_call(
        kern,
        grid=(n_blk, m // tm),
        in_specs=[pl.BlockSpec((tm, k), lambda j, i: (i, 0)),
                  pl.BlockSpec((None, k, tn), lambda j, i: (layer, 0, col_blk0 + j)),
                  *extra_specs],
        out_specs=pl.BlockSpec((tm, tn), lambda j, i: (i, j)),
        out_shape=jax.ShapeDtypeStruct((m, n_blk * tn), out_dtype),
        scratch_shapes=[pltpu.VMEM((k, tn), BF16)],
        compiler_params=_params(2),
        name=name,
    )(a, w, *extras)


def _epi_plain(acc, extras, o_ref):
    o_ref[...] = acc.astype(o_ref.dtype)


def _epi_silu(acc, extras, o_ref):
    o_ref[...] = _silu(acc).astype(o_ref.dtype)


def _epi_gelu(acc, extras, o_ref):
    o_ref[...] = _gelu_tanh(acc).astype(o_ref.dtype)


def _rope_partner(y):
    lane = lax.broadcasted_iota(jnp.int32, y.shape, 1)
    return jnp.where((lane & 63) < 32, pltpu.roll(y, 96, 1), pltpu.roll(y, 32, 1))


def _epi_headnorm(acc, extras, o_ref, *, rope, tm):
    g = extras[0][...]
    n_heads = acc.shape[1] // HEAD_DIM

    def body(apply_rope):
        for hh in range(n_heads):
            sl = slice(hh * HEAD_DIM, (hh + 1) * HEAD_DIM)
            a = acc[:, sl]
            y = a * lax.rsqrt(jnp.mean(a * a, axis=-1, keepdims=True) + EPS) * g
            if apply_rope:
                y = y * extras[1][...] + _rope_partner(y) * extras[2][...]
            o_ref[:, sl] = y.astype(o_ref.dtype)

    if rope:
        is_latent = pl.program_id(1) >= PROMPT_TOKENS // tm
        pl.when(is_latent)(lambda: body(True))
        pl.when(jnp.logical_not(is_latent))(lambda: body(False))
    else:
        body(False)


def _epi_residual(acc, extras, o_ref):
    o_ref[...] = extras[0][...] + extras[1][...] * acc


def _rope_tables():
    nf = HEAD_DIM // 4
    t = jnp.arange(SAMPLE_SEQ)
    row = (t // GRID_W).astype(F32)
    col = (t % GRID_W).astype(F32)
    inv = ROPE_THETA ** (-jnp.arange(nf, dtype=F32) / nf)
    ang_r = row[:, None] * inv
    ang_c = col[:, None] * inv
    cos = jnp.concatenate([jnp.cos(ang_r), jnp.cos(ang_r), jnp.cos(ang_c), jnp.cos(ang_c)], axis=1)
    sin = jnp.concatenate([-jnp.sin(ang_r), jnp.sin(ang_r), -jnp.sin(ang_c), jnp.sin(ang_c)], axis=1)
    return cos, sin


def _headnorm_proj(hn, w, layer, col_blk0, n_blk, gain, rope_tabs, out_dtype, name, tm=1024, tn=512):
    extras = [gain.reshape(1, HEAD_DIM)]
    specs = [pl.BlockSpec((1, HEAD_DIM), lambda j, i: (0, 0))]
    if rope_tabs is not None:
        per_seq = SAMPLE_SEQ // tm
        tab_map = lambda j, i: (jnp.maximum(i - PROMPT_TOKENS // tm, 0) % per_seq, 0)
        extras += list(rope_tabs)
        specs += [pl.BlockSpec((tm, HEAD_DIM), tab_map)] * 2
    epi = functools.partial(_epi_headnorm, rope=rope_tabs is not None, tm=tm)
    return _proj(hn, w, layer, col_blk0, n_blk, epi, out_dtype, name, extras, specs, tm, tn)


def _out_proj(a, w, layer_in_kind, x, mod4, layer, name, tm, tn=512):
    gate_blk0 = 2 * D_MODEL // tn
    extras = [x, mod4]
    specs = [pl.BlockSpec((tm, tn), lambda j, i: (i, j)),
             pl.BlockSpec((None, None, 1, tn), lambda j, i: (layer, _cond_row(i * tm), 0, gate_blk0 + j))]
    return _proj(a, w, layer_in_kind, 0, D_MODEL // tn, _epi_residual, F32, name, extras, specs, tm, tn)


def _dot_nt(a, b):
    return lax.dot_general(a, b, (((1,), (1,)), ((), ())), preferred_element_type=F32)


def _head(ref, h, rows=slice(None)):
    return ref[rows, h * HEAD_DIM:(h + 1) * HEAD_DIM]


def _softmax_pv(score_blocks, value_blocks, sink_col):
    m = functools.reduce(jnp.maximum, [jnp.max(s, axis=-1, keepdims=True) for s in score_blocks])
    if sink_col is not None:
        m = jnp.maximum(m, sink_col)
    den = jnp.exp(sink_col - m) if sink_col is not None else 0.0
    o = 0.0
    for s, v in zip(score_blocks, value_blocks):
        p = jnp.exp(s - m)
        den = den + jnp.sum(p, axis=-1, keepdims=True)
        o = o + jnp.dot(p.astype(BF16), v, preferred_element_type=F32)
    return o * (1.0 / den)


def _attn_prompt_kernel(sink_ref, q_ref, k_ref, v_ref, z_ref, o_ref, *, n_heads, n_kv, use_sink):
    grp = n_heads // n_kv
    for g in range(n_kv):
        kg = _head(k_ref, g).astype(BF16)
        vg = _head(v_ref, g).astype(BF16)
        heads = [g * grp + t for t in range(grp)]
        qs = jnp.concatenate([_head(q_ref, h) for h in heads], axis=0) if grp > 1 else _head(q_ref, heads[0])
        s = _dot_nt(qs, kg) * SM_SCALE
        sink_col = None
        if use_sink:
            cols = [jnp.full((PROMPT_SEQ, 1), sink_ref[h], F32) for h in heads]
            sink_col = jnp.concatenate(cols, axis=0) if grp > 1 else cols[0]
        o = _softmax_pv([s], [vg], sink_col)
        for t, h in enumerate(heads):
            oh = o[t * PROMPT_SEQ:(t + 1) * PROMPT_SEQ]
            o_ref[:, h * HEAD_DIM:(h + 1) * HEAD_DIM] = (oh * _head(z_ref, h).astype(F32)).astype(BF16)


def _attn_prompt(sink, q, k, v, z, n_kv, use_sink, name):
    width = q.shape[1]
    kv_width = k.shape[1]
    kern = functools.partial(_attn_prompt_kernel, n_heads=width // HEAD_DIM, n_kv=n_kv, use_sink=use_sink)
    return pl.pallas_call(
        kern,
        grid=(N_PROMPT,),
        in_specs=[pl.BlockSpec(memory_space=pltpu.SMEM),
                  pl.BlockSpec((PROMPT_SEQ, width), lambda b: (b, 0)),
                  pl.BlockSpec((PROMPT_SEQ, kv_width), lambda b: (b, 0)),
                  pl.BlockSpec((PROMPT_SEQ, kv_width), lambda b: (b, 0)),
                  pl.BlockSpec((PROMPT_SEQ, width), lambda b: (b, 0))],
        out_specs=pl.BlockSpec((PROMPT_SEQ, width), lambda b: (b, 0)),
        out_shape=jax.ShapeDtypeStruct((TOKENS, width), BF16),
        compiler_params=_params(1),
        name=name,
    )(sink, q, k, v, z)


WIN_BAND = 3 * WINDOW


def _attn_win_kernel(sink_ref, q_ref, k_ref, v_ref, kc_ref, vc_ref, z_ref, a_in_ref, o_ref, *, n_heads, n_kv):
    del a_in_ref
    grp = n_heads // n_kv
    n = pl.program_id(1)
    start = pl.multiple_of(jnp.clip((n - 1) * WINDOW, 0, SAMPLE_SEQ - WIN_BAND), WINDOW)
    shape = (grp * WINDOW, WIN_BAND)
    qpos = n * WINDOW + (lax.broadcasted_iota(jnp.int32, shape, 0) & (WINDOW - 1))
    kpos = start + lax.broadcasted_iota(jnp.int32, shape, 1)
    valid = jnp.abs(kpos - qpos) <= WINDOW
    band = pl.ds(start, WIN_BAND)
    for g in range(n_kv):
        heads = [g * grp + t for t in range(grp)]
        qs = jnp.concatenate([_head(q_ref, h) for h in heads], axis=0)
        kb = _head(k_ref, g, band).astype(BF16)
        vb = _head(v_ref, g, band).astype(BF16)
        kc = _head(kc_ref, g).astype(BF16)
        vc = _head(vc_ref, g).astype(BF16)
        s_band = jnp.where(valid, _dot_nt(qs, kb) * SM_SCALE, NEG_INF)
        s_ctx = _dot_nt(qs, kc) * SM_SCALE
        sink_col = jnp.concatenate([jnp.full((WINDOW, 1), sink_ref[h], F32) for h in heads], axis=0)
        o = _softmax_pv([s_band, s_ctx], [vb, vc], sink_col)
        for t, h in enumerate(heads):
            oh = o[t * WINDOW:(t + 1) * WINDOW]
            o_ref[:, h * HEAD_DIM:(h + 1) * HEAD_DIM] = (oh * _head(z_ref, h).astype(F32)).astype(BF16)


def _attn_win(sink, q, k, v, cache_k, cache_v, layer_in_kind, z, a, n_kv):
    width = q.shape[1]
    kv_width = k.shape[1]
    blocks_per_seq = SAMPLE_SEQ // WINDOW
    q_map = lambda b, n: (PROMPT_TOKENS // WINDOW + b * blocks_per_seq + n, 0)
    kv_map = lambda b, n: (PROMPT_TOKENS // SAMPLE_SEQ + b, 0)
    cache_map = lambda b, n: (b, layer_in_kind, 0, 0)
    kern = functools.partial(_attn_win_kernel, n_heads=width // HEAD_DIM, n_kv=n_kv)
    return pl.pallas_call(
        kern,
        grid=(N_SAMPLE, blocks_per_seq),
        in_specs=[pl.BlockSpec(memory_space=pltpu.SMEM),
                  pl.BlockSpec((WINDOW, width), q_map),
                  pl.BlockSpec((SAMPLE_SEQ, kv_width), kv_map),
                  pl.BlockSpec((SAMPLE_SEQ, kv_width), kv_map),
                  pl.BlockSpec((None, None, PAST_LEN, kv_width), cache_map),
                  pl.BlockSpec((None, None, PAST_LEN, kv_width), cache_map),
                  pl.BlockSpec((WINDOW, width), q_map),
                  pl.BlockSpec(memory_space=pl.ANY)],
        out_specs=pl.BlockSpec((WINDOW, width), q_map),
        out_shape=jax.ShapeDtypeStruct(a.shape, a.dtype),
        input_output_aliases={7: 0},
        compiler_params=_params(2),
        name="attn_win_latent",
    )(sink, q, k, v, cache_k, cache_v, z, a)


NAT_QROWS = 4
NAT_KROWS = 12
NAT_QTOK = NAT_QROWS * GRID_W
NAT_KTOK = NAT_KROWS * GRID_W
GRID_ROWS = SAMPLE_SEQ // GRID_W
N_DR = 2 * NAT_ROWS - 1
N_DC = 2 * NAT_COLS - 1
MASKED_TILE = N_DR


def _attn_nat_kernel(bias_ref, q_ref, k_ref, v_ref, kc_ref, vc_ref, z_ref, a_in_ref, o_ref, left_ref, right_ref):
    del a_in_ref
    h = pl.program_id(0)
    qb = pl.program_id(2)

    @pl.when(jnp.logical_and(pl.program_id(1) == 0, qb == 0))
    def _():
        shape = (GRID_W, 2 * GRID_W)
        lane = lax.broadcasted_iota(jnp.int32, shape, 1)
        qc = lax.broadcasted_iota(jnp.int32, shape, 0)
        kc_ = lane & (GRID_W - 1)
        cs = jnp.clip(qc - NAT_COLS // 2, 0, GRID_W - NAT_COLS)
        col_ok = jnp.logical_and(kc_ >= cs, kc_ < cs + NAT_COLS)
        dci = kc_ - qc + (NAT_COLS - 1)
        is_left = lane < GRID_W
        for dri in range(N_DR):
            tile = jnp.zeros(shape, F32)
            for d in range(N_DC):
                tile = jnp.where(dci == d, bias_ref[(h * N_DR + dri) * N_DC + d], tile)
            tile = jnp.where(col_ok, tile, NEG_INF)
            left_ref[dri] = jnp.where(is_left, tile, 0.0)
            right_ref[dri] = jnp.where(is_left, 0.0, tile)
        left_ref[MASKED_TILE] = jnp.where(is_left, NEG_INF, 0.0)
        right_ref[MASKED_TILE] = jnp.where(is_left, 0.0, NEG_INF)

    krow0 = jnp.where(qb >= 2, GRID_ROWS - NAT_KROWS, 0)
    kwin = pl.ds(pl.multiple_of(krow0 * GRID_W, NAT_QTOK), NAT_KTOK)

    def tile_index(qr, kr):
        rs = jnp.clip(qr - NAT_ROWS // 2, 0, GRID_ROWS - NAT_ROWS)
        ok = jnp.logical_and(kr >= rs, kr < rs + NAT_ROWS)
        return jnp.where(ok, kr - qr + (NAT_ROWS - 1), MASKED_TILE)

    bias_rows = []
    for a in range(NAT_QROWS):
        qr = qb * NAT_QROWS + a
        tiles = []
        for tp in range(NAT_KROWS // 2):
            kr = krow0 + 2 * tp
            tiles.append(left_ref[tile_index(qr, kr)] + right_ref[tile_index(qr, kr + 1)])
        bias_rows.append(jnp.concatenate(tiles, axis=1))
    bias = jnp.concatenate(bias_rows, axis=0)

    q = q_ref[...]
    kb = k_ref[kwin, :].astype(BF16)
    vb = v_ref[kwin, :].astype(BF16)
    s_nb = _dot_nt(q, kb) * SM_SCALE + bias
    s_ctx = _dot_nt(q, kc_ref[...].astype(BF16)) * SM_SCALE
    o = _softmax_pv([s_nb, s_ctx], [vb, vc_ref[...].astype(BF16)], None)
    o_ref[...] = (o * z_ref[...].astype(F32)).astype(BF16)


def _attn_nat(rel_bias, q, k, v, cache_k, cache_v, layer_in_kind, z, a):
    n_heads = q.shape[1] // HEAD_DIM
    blocks_per_seq = SAMPLE_SEQ // NAT_QTOK
    q_map = lambda h, b, n: (PROMPT_TOKENS // NAT_QTOK + b * blocks_per_seq + n, h)
    kv_map = lambda h, b, n: (PROMPT_TOKENS // SAMPLE_SEQ + b, h)
    cache_map = lambda h, b, n: (b, layer_in_kind, 0, h)
    return pl.pallas_call(
        _attn_nat_kernel,
        grid=(n_heads, N_SAMPLE, blocks_per_seq),
        in_specs=[pl.BlockSpec(memory_space=pltpu.SMEM),
                  pl.BlockSpec((NAT_QTOK, HEAD_DIM), q_map),
                  pl.BlockSpec((SAMPLE_SEQ, HEAD_DIM), kv_map),
                  pl.BlockSpec((SAMPLE_SEQ, HEAD_DIM), kv_map),
                  pl.BlockSpec((None, None, PAST_LEN, HEAD_DIM), cache_map),
                  pl.BlockSpec((None, None, PAST_LEN, HEAD_DIM), cache_map),
                  pl.BlockSpec((NAT_QTOK, HEAD_DIM), q_map),
                  pl.BlockSpec(memory_space=pl.ANY)],
        out_specs=pl.BlockSpec((NAT_QTOK, HEAD_DIM), q_map),
        out_shape=jax.ShapeDtypeStruct(a.shape, a.dtype),
        scratch_shapes=[pltpu.VMEM((N_DR + 1, GRID_W, 2 * GRID_W), F32),
                        pltpu.VMEM((N_DR + 1, GRID_W, 2 * GRID_W), F32)],
        input_output_aliases={7: 0},
        compiler_params=_params(3),
        name="attn_nat_latent",
    )(rel_bias.reshape(-1), q, k, v, cache_k, cache_v, z, a)


def _spatial_kernel(u_ref, v_ref, z_ref, g_ref, b_ref, ws_ref, bs_ref, o_ref):
    v = v_ref[...].astype(F32)
    mu = jnp.mean(v, axis=-1, keepdims=True)
    vc = v - mu
    var = jnp.mean(vc * vc, axis=-1, keepdims=True)
    vn = (vc * lax.rsqrt(var + EPS) * g_ref[...] + b_ref[...]).astype(BF16)
    for g in range(GMLP_GROUPS):
        sl = slice(g * GMLP_GROUP_WIDTH, (g + 1) * GMLP_GROUP_WIDTH)
        sv = jnp.dot(ws_ref[g].astype(BF16), vn[:, sl], preferred_element_type=F32) + bs_ref[:, g:g + 1]
        o_ref[:, sl] = (u_ref[:, sl].astype(F32) * sv * z_ref[:, sl].astype(F32)).astype(BF16)


def _spatial(u, v, z, ln_g, ln_b, w_s, b_s):
    row = pl.BlockSpec((CHUNK, GMLP_WIDTH), lambda i: (i, 0))
    vec = pl.BlockSpec((1, GMLP_WIDTH), lambda i: (0, 0))
    return pl.pallas_call(
        _spatial_kernel,
        grid=(TOKENS // CHUNK,),
        in_specs=[row, row, row, vec, vec,
                  pl.BlockSpec((GMLP_GROUPS, CHUNK, CHUNK), lambda i: (0, 0, 0)),
                  pl.BlockSpec((CHUNK, GMLP_GROUPS), lambda i: (0, 0))],
        out_specs=row,
        out_shape=jax.ShapeDtypeStruct((TOKENS, GMLP_WIDTH), BF16),
        compiler_params=_params(1),
        name="gmlp_spatial",
    )(u, v, z, ln_g.reshape(1, -1), ln_b.reshape(1, -1), w_s, b_s.T)


def _prompt_cache(x, n_heads):
    return x[:PROMPT_TOKENS].reshape(N_PROMPT, PROMPT_SEQ, n_heads, HEAD_DIM)


def kernel(x_prompt, x_sample, cache_win_k, cache_win_v, cache_nat_k, cache_nat_v, c, c_ctx,
           norm_g, w_ada, b_ada,
           win_w_in, win_q_norm, win_k_norm, win_sink, win_w_out,
           nat_w_in, nat_q_norm, nat_k_norm, nat_rel_bias, nat_w_out,
           gmlp_w_in, gmlp_ln_g, gmlp_ln_b, gmlp_w_s, gmlp_b_s, gmlp_w_out):
    depth = norm_g.shape[0]
    x = jnp.concatenate([x_prompt.reshape(PROMPT_TOKENS, D_MODEL),
                         x_sample.reshape(SAMPLE_TOKENS, D_MODEL)], axis=0)
    cond = jnp.zeros((COND_ROWS, D_MODEL), F32).at[:N_SAMPLE].set(c).at[CTX_COND_ROW].set(c_ctx)
    mod4 = _ada(cond, w_ada, b_ada).reshape(depth, COND_ROWS, 1, 3 * D_MODEL)
    norm_g3 = norm_g.reshape(depth, 1, D_MODEL)
    rope_tabs = _rope_tables()
    no_sink = jnp.zeros((1,), F32)

    win_kv_heads = cache_win_k.shape[3]
    win_kv_width = win_kv_heads * HEAD_DIM
    cwk = cache_win_k.reshape(N_SAMPLE, -1, PAST_LEN, win_kv_width)
    cwv = cache_win_v.reshape(N_SAMPLE, -1, PAST_LEN, win_kv_width)
    nat_heads = cache_nat_k.shape[3]
    cnk = cache_nat_k.reshape(N_SAMPLE, -1, PAST_LEN, nat_heads * HEAD_DIM)
    cnv = cache_nat_v.reshape(N_SAMPLE, -1, PAST_LEN, nat_heads * HEAD_DIM)

    new_win_k, new_win_v, new_nat_k, new_nat_v = [], [], [], []
    tn = 512
    for layer in range(depth):
        kind = layer % 3
        li = layer // 3
        hn = _norm_mod(x, norm_g3, mod4, layer)
        if kind == 0:
            nq = D_MODEL // tn
            nkv = win_kv_width // tn
            q = _headnorm_proj(hn, win_w_in, li, 0, nq, win_q_norm[li], rope_tabs, BF16, "win_q")
            k = _headnorm_proj(hn, win_w_in, li, nq, nkv, win_k_norm[li], rope_tabs, F32, "win_k")
            v = _proj(hn, win_w_in, li, nq + nkv, nkv, _epi_plain, F32, "win_v")
            z = _proj(hn, win_w_in, li, nq + 2 * nkv, nq, _epi_silu, BF16, "win_z")
            new_win_k.append(_prompt_cache(k, win_kv_heads))
            new_win_v.append(_prompt_cache(v, win_kv_heads))
            a = _attn_prompt(win_sink[li], q, k, v, z, win_kv_heads, True, "attn_win_prompt")
            a = _attn_win(win_sink[li], q, k, v, cwk, cwv, li, z, a, win_kv_heads)
            x = _out_proj(a, win_w_out, li, x, mod4, layer, "win_out", tm=1024)
        elif kind == 1:
            nq = D_MODEL // tn
            q = _headnorm_proj(hn, nat_w_in, li, 0, nq, nat_q_norm[li], None, BF16, "nat_q")
            k = _headnorm_proj(hn, nat_w_in, li, nq, nq, nat_k_norm[li], None, F32, "nat_k")
            v = _proj(hn, nat_w_in, li, 2 * nq, nq, _epi_plain, F32, "nat_v")
            z = _proj(hn, nat_w_in, li, 3 * nq, nq, _epi_silu, BF16, "nat_z")
            new_nat_k.append(_prompt_cache(k, nat_heads))
            new_nat_v.append(_prompt_cache(v, nat_heads))
            a = _attn_prompt(no_sink, q, k, v, z, nat_heads, False, "attn_nat_prompt")
            a = _attn_nat(nat_rel_bias[li], q, k, v, cnk, cnv, li, z, a)
            x = _out_proj(a, nat_w_out, li, x, mod4, layer, "nat_out", tm=1024)
        else:
            nw = GMLP_WIDTH // tn
            u = _proj(hn, gmlp_w_in, li, 0, nw, _epi_gelu, BF16, "gmlp_u")
            v = _proj(hn, gmlp_w_in, li, nw, nw, _epi_gelu, BF16, "gmlp_v")
            z = _proj(hn, gmlp_w_in, li, 2 * nw, nw, _epi_silu, BF16, "gmlp_z")
            a = _spatial(u, v, z, gmlp_ln_g[li], gmlp_ln_b[li], gmlp_w_s[li], gmlp_b_s[li])
            x = _out_proj(a, gmlp_w_out, li, x, mod4, layer, "gmlp_out", tm=512)

    y_prompt = x[:PROMPT_TOKENS].reshape(N_PROMPT, PROMPT_SEQ, D_MODEL)
    y_sample = x[PROMPT_TOKENS:].reshape(N_SAMPLE, SAMPLE_SEQ, D_MODEL)
    return (y_prompt, y_sample,
            jnp.stack(new_win_k, axis=1), jnp.stack(new_win_v, axis=1),
            jnp.stack(new_nat_k, axis=1), jnp.stack(new_nat_v, axis=1))
```

```python
import functools
import math

import jax
import jax.numpy as jnp
from jax import lax
from jax.experimental import pallas as pl
from jax.experimental.pallas import tpu as pltpu

F32 = jnp.float32
BF16 = jnp.bfloat16

D_MODEL = 2048
HEAD_DIM = 128
N_PROMPT = 16
PROMPT_SEQ = 256
N_SAMPLE = 2
SAMPLE_SEQ = 1024
PROMPT_TOKENS = N_PROMPT * PROMPT_SEQ
SAMPLE_TOKENS = N_SAMPLE * SAMPLE_SEQ
TOKENS = PROMPT_TOKENS + SAMPLE_TOKENS
PAST_LEN = 512
GRID_W = 64
EPS = 1e-6
NEG_INF = -1e30
ROPE_THETA = 10000.0
WINDOW = 128
NAT_ROWS = 8
NAT_COLS = 16
GMLP_WIDTH = 2 * D_MODEL
GMLP_GROUPS = 16
GMLP_GROUP_WIDTH = GMLP_WIDTH // GMLP_GROUPS
CHUNK = 128
CTX_COND_ROW = 2
COND_ROWS = 8
SM_SCALE = HEAD_DIM ** -0.5

VMEM_LIMIT = 56 * 1024 * 1024


def _params(n_axes, vmem=VMEM_LIMIT):
    return pltpu.CompilerParams(dimension_semantics=("arbitrary",) * n_axes,
                                vmem_limit_bytes=vmem)


def _cond_row(tok0):
    return jnp.where(tok0 < PROMPT_TOKENS, CTX_COND_ROW, (tok0 - PROMPT_TOKENS) // SAMPLE_SEQ)


def _silu(x):
    return x * (1.0 / (1.0 + jnp.exp(-x)))


def _gelu_tanh(x):
    return 0.5 * x * (1.0 + jnp.tanh(math.sqrt(2.0 / math.pi) * (x + 0.044715 * (x * x * x))))


def _split_specs(tm, width, tile0=0):
    n_p = PROMPT_TOKENS // tm
    return [pl.BlockSpec((tm, width), lambda i: (jnp.minimum(i + tile0, n_p - 1), 0)),
            pl.BlockSpec((tm, width), lambda i: (jnp.maximum(i + tile0 - n_p, 0), 0))]


def _ada_kernel(cond_ref, w_ref, b_ref, o_ref):
    s = _silu(cond_ref[...]).astype(BF16)
    o_ref[...] = jnp.dot(s, w_ref[...].astype(BF16), preferred_element_type=F32) + b_ref[...]


def _ada(cond, w_ada, b_ada, tn=1024):
    depth = w_ada.shape[0]
    n = w_ada.shape[2]
    return pl.pallas_call(
        _ada_kernel,
        grid=(depth, n // tn),
        in_specs=[pl.BlockSpec((COND_ROWS, D_MODEL), lambda l, j: (0, 0)),
                  pl.BlockSpec((None, D_MODEL, tn), lambda l, j: (l, 0, j)),
                  pl.BlockSpec((None, 1, tn), lambda l, j: (l, 0, j))],
        out_specs=pl.BlockSpec((None, COND_ROWS, tn), lambda l, j: (l, 0, j)),
        out_shape=jax.ShapeDtypeStruct((depth, COND_ROWS, n), F32),
        compiler_params=_params(2),
        name="ada_mod",
    )(cond, w_ada, b_ada.reshape(depth, 1, n))


def _norm_kernel(xp_ref, xs_ref, g_ref, shift_ref, scale_ref, o_ref, *, tm):
    x = jnp.where(pl.program_id(0) < PROMPT_TOKENS // tm, xp_ref[...], xs_ref[...])
    y = x * lax.rsqrt(jnp.mean(x * x, axis=-1, keepdims=True) + EPS) * g_ref[...]
    o_ref[...] = (y * (1.0 + scale_ref[...]) + shift_ref[...]).astype(BF16)


def _norm_mod(xp, xs, norm_g3, mod4, layer, tm=512):
    return pl.pallas_call(
        functools.partial(_norm_kernel, tm=tm),
        grid=(TOKENS // tm,),
        in_specs=[*_split_specs(tm, D_MODEL),
                  pl.BlockSpec((None, 1, D_MODEL), lambda i: (layer, 0, 0)),
                  pl.BlockSpec((None, None, 1, D_MODEL), lambda i: (layer, _cond_row(i * tm), 0, 0)),
                  pl.BlockSpec((None, None, 1, D_MODEL), lambda i: (layer, _cond_row(i * tm), 0, 1))],
        out_specs=pl.BlockSpec((tm, D_MODEL), lambda i: (i, 0)),
        out_shape=jax.ShapeDtypeStruct((TOKENS, D_MODEL), BF16),
        compiler_params=_params(1),
        name="norm_mod",
    )(xp, xs, norm_g3, mod4, mod4)


PROJ_TM = 1024
PROJ_TN = 512
N_PROMPT_TILES = PROMPT_TOKENS // PROJ_TM


def _proj_kernel(*refs, epilogue, n_extra, n_out):
    a_ref, w_ref = refs[0], refs[1]
    extras = refs[2:2 + n_extra]
    outs = refs[2 + n_extra:2 + n_extra + n_out]
    wbf_ref = refs[2 + n_extra + n_out]

    @pl.when(pl.program_id(1) == 0)
    def _():
        wbf_ref[...] = w_ref[...].astype(BF16)

    acc = jnp.dot(a_ref[...], wbf_ref[...], preferred_element_type=F32)
    epilogue(acc, extras, outs)


def _proj(a, w, layer, col_blk0, n_blk, epilogue, out_shapes, out_specs, name,
          extras=(), extra_specs=()):
    m, k = a.shape
    kern = functools.partial(_proj_kernel, epilogue=epilogue, n_extra=len(extras), n_out=len(out_shapes))
    return pl.pallas_call(
        kern,
        grid=(n_blk, m // PROJ_TM),
        in_specs=[pl.BlockSpec((PROJ_TM, k), lambda j, i: (i, 0)),
                  pl.BlockSpec((None, k, PROJ_TN), lambda j, i: (layer, 0, col_blk0 + j)),
                  *extra_specs],
        out_specs=out_specs,
        out_shape=out_shapes,
        scratch_shapes=[pltpu.VMEM((k, PROJ_TN), BF16)],
        compiler_params=_params(2),
        name=name,
    )(a, w, *extras)


def _full_out(n_blk, dtype):
    return ([jax.ShapeDtypeStruct((TOKENS, n_blk * PROJ_TN), dtype)],
            [pl.BlockSpec((PROJ_TM, PROJ_TN), lambda j, i: (i, j))])


def _split_out(n_blk):
    seqs = PROJ_TM // PROMPT_SEQ
    shapes = [jax.ShapeDtypeStruct((N_PROMPT, PROMPT_SEQ, n_blk * PROJ_TN), F32),
              jax.ShapeDtypeStruct((SAMPLE_TOKENS, n_blk * PROJ_TN), BF16)]
    specs = [pl.BlockSpec((seqs, PROMPT_SEQ, PROJ_TN),
                          lambda j, i: (jnp.minimum(i, N_PROMPT_TILES - 1), 0, j)),
             pl.BlockSpec((PROJ_TM, PROJ_TN), lambda j, i: (jnp.maximum(i - N_PROMPT_TILES, 0), j))]
    return shapes, specs


def _epi_silu(acc, extras, outs):
    outs[0][...] = _silu(acc).astype(BF16)


def _epi_gelu(acc, extras, outs):
    outs[0][...] = _gelu_tanh(acc).astype(BF16)


def _store_split(y, outs, sl, is_latent_static):
    if is_latent_static:
        outs[1][:, sl] = y.astype(BF16)
    else:
        outs[0][:, :, sl] = y.reshape(PROJ_TM // PROMPT_SEQ, PROMPT_SEQ, y.shape[1])


def _by_tile_kind(body):
    is_latent = pl.program_id(1) >= N_PROMPT_TILES
    pl.when(is_latent)(lambda: body(True))
    pl.when(jnp.logical_not(is_latent))(lambda: body(False))


def _epi_value(acc, extras, outs):
    _by_tile_kind(lambda latent: _store_split(acc, outs, slice(None), latent))


def _rope_partner(y):
    lane = lax.broadcasted_iota(jnp.int32, y.shape, 1)
    return jnp.where((lane & 63) < 32, pltpu.roll(y, 96, 1), pltpu.roll(y, 32, 1))


def _head_rmsnorm(acc, hh, g):
    a = acc[:, hh * HEAD_DIM:(hh + 1) * HEAD_DIM]
    return a * lax.rsqrt(jnp.mean(a * a, axis=-1, keepdims=True) + EPS) * g


def _epi_query(acc, extras, outs, *, rope):
    g = extras[0][...]

    def body(latent):
        for hh in range(PROJ_TN // HEAD_DIM):
            y = _head_rmsnorm(acc, hh, g)
            if latent and rope:
                y = y * extras[1][...] + _rope_partner(y) * extras[2][...]
            outs[0][:, hh * HEAD_DIM:(hh + 1) * HEAD_DIM] = y.astype(BF16)

    if rope:
        _by_tile_kind(body)
    else:
        body(False)


def _epi_key(acc, extras, outs, *, rope):
    g = extras[0][...]

    def body(latent):
        for hh in range(PROJ_TN // HEAD_DIM):
            y = _head_rmsnorm(acc, hh, g)
            if latent and rope:
                y = y * extras[1][...] + _rope_partner(y) * extras[2][...]
            _store_split(y, outs, slice(hh * HEAD_DIM, (hh + 1) * HEAD_DIM), latent)

    _by_tile_kind(body)


def _rope_tables():
    nf = HEAD_DIM // 4
    t = jnp.arange(SAMPLE_SEQ)
    row = (t // GRID_W).astype(F32)
    col = (t % GRID_W).astype(F32)
    inv = ROPE_THETA ** (-jnp.arange(nf, dtype=F32) / nf)
    ang_r = row[:, None] * inv
    ang_c = col[:, None] * inv
    cos = jnp.concatenate([jnp.cos(ang_r), jnp.cos(ang_r), jnp.cos(ang_c), jnp.cos(ang_c)], axis=1)
    sin = jnp.concatenate([-jnp.sin(ang_r), jnp.sin(ang_r), -jnp.sin(ang_c), jnp.sin(ang_c)], axis=1)
    return cos, sin


def _gain_extras(gain, rope_tabs):
    extras = [gain.reshape(1, HEAD_DIM)]
    specs = [pl.BlockSpec((1, HEAD_DIM), lambda j, i: (0, 0))]
    if rope_tabs is not None:
        per_seq = SAMPLE_SEQ // PROJ_TM
        tab_map = lambda j, i: (jnp.maximum(i - N_PROMPT_TILES, 0) % per_seq, 0)
        extras += list(rope_tabs)
        specs += [pl.BlockSpec((PROJ_TM, HEAD_DIM), tab_map)] * 2
    return extras, specs


def _query_proj(hn, w, layer, col_blk0, n_blk, gain, rope_tabs, name):
    extras, specs = _gain_extras(gain, rope_tabs)
    epi = functools.partial(_epi_query, rope=rope_tabs is not None)
    return _proj(hn, w, layer, col_blk0, n_blk, epi, *_full_out(n_blk, BF16), name, extras, specs)[0]


def _key_proj(hn, w, layer, col_blk0, n_blk, gain, rope_tabs, name):
    extras, specs = _gain_extras(gain, rope_tabs)
    epi = functools.partial(_epi_key, rope=rope_tabs is not None)
    return _proj(hn, w, layer, col_blk0, n_blk, epi, *_split_out(n_blk), name, extras, specs)


def _value_proj(hn, w, layer, col_blk0, n_blk, name):
    return _proj(hn, w, layer, col_blk0, n_blk, _epi_value, *_split_out(n_blk), name)


def _act_proj(hn, w, layer, col_blk0, n_blk, epi, name):
    return _proj(hn, w, layer, col_blk0, n_blk, epi, *_full_out(n_blk, BF16), name)[0]


OUT_COLS = 512
W_STAGE_ROWS = 256


def _load_weight_bf16(w_hbm, layer, wbf_ref, stage_ref, sem):
    k = wbf_ref.shape[0]
    n_chunks = k // W_STAGE_ROWS

    def copy(c, slot):
        return pltpu.make_async_copy(w_hbm.at[layer, pl.ds(c * W_STAGE_ROWS, W_STAGE_ROWS), :],
                                     stage_ref.at[slot], sem.at[slot])

    copy(0, 0).start()
    for c in range(n_chunks):
        slot = c % 2
        if c + 1 < n_chunks:
            copy(c + 1, 1 - slot).start()
        copy(c, slot).wait()
        wbf_ref[c * W_STAGE_ROWS:(c + 1) * W_STAGE_ROWS, :] = stage_ref[slot].astype(BF16)


def _out_kernel(*refs, layer_in_kind, tile0, tm, split_a, split_x, fuse_norm):
    it = iter(refs)
    a_refs = [next(it), next(it)] if split_a else [next(it)]
    w_hbm = next(it)
    x_refs = [next(it), next(it)] if split_x else [next(it)]
    gate_ref = next(it)
    if fuse_norm:
        g_ref, shift_ref, scale_ref = next(it), next(it), next(it)
    xnew_ref = next(it)
    hn_ref = next(it) if fuse_norm else None
    wbf_ref, stage_ref, sem = next(it), next(it), next(it)

    i = pl.program_id(0)

    @pl.when(i == 0)
    def _():
        _load_weight_bf16(w_hbm, layer_in_kind, wbf_ref, stage_ref, sem)

    is_prompt = i + tile0 < PROMPT_TOKENS // tm
    a = jnp.where(is_prompt, a_refs[0][...], a_refs[1][...]) if split_a else a_refs[0][...]
    ssq = jnp.zeros((tm, 1), F32)
    for cb in range(D_MODEL // OUT_COLS):
        sl = slice(cb * OUT_COLS, (cb + 1) * OUT_COLS)
        acc = jnp.dot(a, wbf_ref[:, sl], preferred_element_type=F32)
        if split_x:
            x = jnp.where(is_prompt, x_refs[0][:, sl], x_refs[1][:, sl])
        else:
            x = x_refs[0][:, sl]
        xn = x + gate_ref[:, sl] * acc
        xnew_ref[:, sl] = xn
        if fuse_norm:
            ssq = ssq + jnp.sum(xn * xn, axis=-1, keepdims=True)
    if fuse_norm:
        rs = lax.rsqrt(ssq * (1.0 / D_MODEL) + EPS)
        for cb in range(D_MODEL // OUT_COLS):
            sl = slice(cb * OUT_COLS, (cb + 1) * OUT_COLS)
            gmul = g_ref[:, sl] * (1.0 + scale_ref[:, sl])
            hn_ref[:, sl] = (xnew_ref[:, sl] * rs * gmul + shift_ref[:, sl]).astype(BF16)


def _out_proj(a, w, layer_in_kind, x, mod4, layer, norm_g3, name, tm, rows=None):
    split_a = isinstance(a, tuple)
    k = a[0].shape[1] if split_a else a.shape[1]
    tok0, n_tok = rows if rows is not None else (0, TOKENS)
    tile0 = tok0 // tm
    split_x = isinstance(x, tuple)
    fuse_norm = norm_g3 is not None
    mod_spec = lambda part, lyr: pl.BlockSpec(
        (None, None, 1, D_MODEL), lambda i: (lyr, _cond_row((i + tile0) * tm), 0, part))
    row_spec = lambda width: pl.BlockSpec((tm, width), lambda i: (i + tile0, 0))
    operands = [*a, w] if split_a else [a, w]
    in_specs = [*(_split_specs(tm, k, tile0) if split_a else [row_spec(k)]),
                pl.BlockSpec(memory_space=pl.ANY)]
    if split_x:
        assert rows is None
        operands += list(x)
        in_specs += _split_specs(tm, D_MODEL)
    else:
        operands.append(x)
        in_specs.append(row_spec(D_MODEL))
    operands.append(mod4)
    in_specs.append(mod_spec(2, layer))
    out_shapes = [jax.ShapeDtypeStruct((n_tok, D_MODEL), F32)]
    out_specs = [pl.BlockSpec((tm, D_MODEL), lambda i: (i, 0))]
    if fuse_norm:
        operands += [norm_g3, mod4, mod4]
        in_specs += [pl.BlockSpec((None, 1, D_MODEL), lambda i: (layer + 1, 0, 0)),
                     mod_spec(0, layer + 1), mod_spec(1, layer + 1)]
        out_shapes.append(jax.ShapeDtypeStruct((n_tok, D_MODEL), BF16))
        out_specs.append(pl.BlockSpec((tm, D_MODEL), lambda i: (i, 0)))
    kern = functools.partial(_out_kernel, layer_in_kind=layer_in_kind, tile0=tile0, tm=tm,
                             split_a=split_a, split_x=split_x, fuse_norm=fuse_norm)
    return pl.pallas_call(
        kern,
        grid=(n_tok // tm,),
        in_specs=in_specs,
        out_specs=out_specs,
        out_shape=out_shapes,
        scratch_shapes=[pltpu.VMEM((k, D_MODEL), BF16),
                        pltpu.VMEM((2, W_STAGE_ROWS, D_MODEL), F32),
                        pltpu.SemaphoreType.DMA((2,))],
        compiler_params=_params(1),
        name=name,
    )(*operands)


def _dot_nt(a, b):
    return lax.dot_general(a, b, (((1,), (1,)), ((), ())), preferred_element_type=F32)


def _head(ref, h, rows=slice(None)):
    return ref[rows, h * HEAD_DIM:(h + 1) * HEAD_DIM]


def _with_ones(v):
    return jnp.concatenate([v, jnp.ones(v.shape, v.dtype)], axis=1)


def _softmax_pv(score_blocks, value_blocks, sink_col):
    m = functools.reduce(jnp.maximum, [jnp.max(s, axis=-1, keepdims=True) for s in score_blocks])
    if sink_col is not None:
        m = jnp.maximum(m, sink_col)
    o = 0.0
    for s, v in zip(score_blocks, value_blocks):
        o = o + jnp.dot(jnp.exp(s - m).astype(BF16), v, preferred_element_type=F32)
    den = o[:, HEAD_DIM:]
    if sink_col is not None:
        den = den + jnp.exp(sink_col - m)
    return o[:, :HEAD_DIM] * (1.0 / den)


def _attn_prompt_kernel(sink_ref, q_ref, k_ref, v_ref, z_ref, o_ref, *, n_heads, n_kv, use_sink):
    grp = n_heads // n_kv
    for g in range(n_kv):
        kg = _head(k_ref, g).astype(BF16)
        vg = _with_ones(_head(v_ref, g).astype(BF16))
        heads = [g * grp + t for t in range(grp)]
        qs = jnp.concatenate([_head(q_ref, h) for h in heads], axis=0) if grp > 1 else _head(q_ref, heads[0])
        s = _dot_nt(qs, kg) * SM_SCALE
        sink_col = None
        if use_sink:
            cols = [jnp.full((PROMPT_SEQ, 1), sink_ref[h], F32) for h in heads]
            sink_col = jnp.concatenate(cols, axis=0) if grp > 1 else cols[0]
        o = _softmax_pv([s], [vg], sink_col)
        for t, h in enumerate(heads):
            oh = o[t * PROMPT_SEQ:(t + 1) * PROMPT_SEQ]
            o_ref[:, h * HEAD_DIM:(h + 1) * HEAD_DIM] = (oh * _head(z_ref, h).astype(F32)).astype(BF16)


def _attn_prompt(sink, q, k_cache, v_cache, z, n_kv, use_sink, name):
    width = q.shape[1]
    kv_width = k_cache.shape[2]
    kern = functools.partial(_attn_prompt_kernel, n_heads=width // HEAD_DIM, n_kv=n_kv, use_sink=use_sink)
    cache_spec = pl.BlockSpec((None, PROMPT_SEQ, kv_width), lambda b: (b, 0, 0))
    return pl.pallas_call(
        kern,
        grid=(N_PROMPT,),
        in_specs=[pl.BlockSpec(memory_space=pltpu.SMEM),
                  pl.BlockSpec((PROMPT_SEQ, width), lambda b: (b, 0)),
                  cache_spec, cache_spec,
                  pl.BlockSpec((PROMPT_SEQ, width), lambda b: (b, 0))],
        out_specs=pl.BlockSpec((PROMPT_SEQ, width), lambda b: (b, 0)),
        out_shape=jax.ShapeDtypeStruct((PROMPT_TOKENS, width), BF16),
        compiler_params=_params(1),
        name=name,
    )(sink, q, k_cache, v_cache, z)


WIN_BAND = 3 * WINDOW


def _attn_win_kernel(sink_ref, q_ref, k_ref, v_ref, kc_ref, vc_ref, z_ref, o_ref, *, n_heads, n_kv):
    grp = n_heads // n_kv
    n = pl.program_id(1)
    start = pl.multiple_of(jnp.clip((n - 1) * WINDOW, 0, SAMPLE_SEQ - WIN_BAND), WINDOW)
    shape = (grp * WINDOW, WIN_BAND)
    qpos = n * WINDOW + (lax.broadcasted_iota(jnp.int32, shape, 0) & (WINDOW - 1))
    kpos = start + lax.broadcasted_iota(jnp.int32, shape, 1)
    valid = jnp.abs(kpos - qpos) <= WINDOW
    band = pl.ds(start, WIN_BAND)
    for g in range(n_kv):
        heads = [g * grp + t for t in range(grp)]
        qs = jnp.concatenate([_head(q_ref, h) for h in heads], axis=0)
        kb = _head(k_ref, g, band)
        vb = _with_ones(_head(v_ref, g, band))
        kc = _head(kc_ref, g).astype(BF16)
        vc = _with_ones(_head(vc_ref, g).astype(BF16))
        s_band = jnp.where(valid, _dot_nt(qs, kb) * SM_SCALE, NEG_INF)
        s_ctx = _dot_nt(qs, kc) * SM_SCALE
        sink_col = jnp.concatenate([jnp.full((WINDOW, 1), sink_ref[h], F32) for h in heads], axis=0)
        o = _softmax_pv([s_band, s_ctx], [vb, vc], sink_col)
        for t, h in enumerate(heads):
            oh = o[t * WINDOW:(t + 1) * WINDOW]
            o_ref[:, h * HEAD_DIM:(h + 1) * HEAD_DIM] = (oh * _head(z_ref, h).astype(F32)).astype(BF16)


def _attn_win(sink, q, k, v, cache_k, cache_v, layer_in_kind, z, n_kv):
    width = q.shape[1]
    kv_width = k.shape[1]
    blocks_per_seq = SAMPLE_SEQ // WINDOW
    q_map = lambda b, n: (PROMPT_TOKENS // WINDOW + b * blocks_per_seq + n, 0)
    kv_map = lambda b, n: (b, 0)
    cache_map = lambda b, n: (b, layer_in_kind, 0, 0)
    kern = functools.partial(_attn_win_kernel, n_heads=width // HEAD_DIM, n_kv=n_kv)
    return pl.pallas_call(
        kern,
        grid=(N_SAMPLE, blocks_per_seq),
        in_specs=[pl.BlockSpec(memory_space=pltpu.SMEM),
                  pl.BlockSpec((WINDOW, width), q_map),
                  pl.BlockSpec((SAMPLE_SEQ, kv_width), kv_map),
                  pl.BlockSpec((SAMPLE_SEQ, kv_width), kv_map),
                  pl.BlockSpec((None, None, PAST_LEN, kv_width), cache_map),
                  pl.BlockSpec((None, None, PAST_LEN, kv_width), cache_map),
                  pl.BlockSpec((WINDOW, width), q_map)],
        out_specs=pl.BlockSpec((WINDOW, width), lambda b, n: (b * blocks_per_seq + n, 0)),
        out_shape=jax.ShapeDtypeStruct((SAMPLE_TOKENS, width), BF16),
        compiler_params=_params(2),
        name="attn_win_latent",
    )(sink, q, k, v, cache_k, cache_v, z)


NAT_QROWS = 4
NAT_KROWS = 12
NAT_QTOK = NAT_QROWS * GRID_W
NAT_KTOK = NAT_KROWS * GRID_W
GRID_ROWS = SAMPLE_SEQ // GRID_W
N_DR = 2 * NAT_ROWS - 1
N_DC = 2 * NAT_COLS - 1
MASKED_TILE = N_DR


def _attn_nat_kernel(bias_ref, q_ref, k_ref, v_ref, kc_ref, vc_ref, z_ref, o_ref, left_ref, right_ref):
    h = pl.program_id(0)
    qb = pl.program_id(2)

    @pl.when(jnp.logical_and(pl.program_id(1) == 0, qb == 0))
    def _():
        shape = (GRID_W, 2 * GRID_W)
        lane = lax.broadcasted_iota(jnp.int32, shape, 1)
        qc = lax.broadcasted_iota(jnp.int32, shape, 0)
        kc_ = lane & (GRID_W - 1)
        cs = jnp.clip(qc - NAT_COLS // 2, 0, GRID_W - NAT_COLS)
        col_ok = jnp.logical_and(kc_ >= cs, kc_ < cs + NAT_COLS)
        dci = kc_ - qc + (NAT_COLS - 1)
        is_left = lane < GRID_W
        for dri in range(N_DR):
            tile = jnp.zeros(shape, F32)
            for d in range(N_DC):
                tile = jnp.where(dci == d, bias_ref[(h * N_DR + dri) * N_DC + d], tile)
            tile = jnp.where(col_ok, tile, NEG_INF)
            left_ref[dri] = jnp.where(is_left, tile, 0.0)
            right_ref[dri] = jnp.where(is_left, 0.0, tile)
        left_ref[MASKED_TILE] = jnp.where(is_left, NEG_INF, 0.0)
        right_ref[MASKED_TILE] = jnp.where(is_left, 0.0, NEG_INF)

    krow0 = jnp.where(qb >= 2, GRID_ROWS - NAT_KROWS, 0)
    kwin = pl.ds(pl.multiple_of(krow0 * GRID_W, NAT_QTOK), NAT_KTOK)

    def tile_index(qr, kr):
        rs = jnp.clip(qr - NAT_ROWS // 2, 0, GRID_ROWS - NAT_ROWS)
        ok = jnp.logical_and(kr >= rs, kr < rs + NAT_ROWS)
        return jnp.where(ok, kr - qr + (NAT_ROWS - 1), MASKED_TILE)

    bias_rows = []
    for a in range(NAT_QROWS):
        qr = qb * NAT_QROWS + a
        tiles = []
        for tp in range(NAT_KROWS // 2):
            kr = krow0 + 2 * tp
            tiles.append(left_ref[tile_index(qr, kr)] + right_ref[tile_index(qr, kr + 1)])
        bias_rows.append(jnp.concatenate(tiles, axis=1))
    bias = jnp.concatenate(bias_rows, axis=0)

    q = q_ref[...]
    s_nb = _dot_nt(q, k_ref[kwin, :]) * SM_SCALE + bias
    s_ctx = _dot_nt(q, kc_ref[...].astype(BF16)) * SM_SCALE
    o = _softmax_pv([s_nb, s_ctx], [_with_ones(v_ref[kwin, :]), _with_ones(vc_ref[...].astype(BF16))], None)
    o_ref[...] = (o * z_ref[...].astype(F32)).astype(BF16)


def _attn_nat(rel_bias, q, k, v, cache_k, cache_v, layer_in_kind, z):
    n_heads = q.shape[1] // HEAD_DIM
    blocks_per_seq = SAMPLE_SEQ // NAT_QTOK
    q_map = lambda h, b, n: (PROMPT_TOKENS // NAT_QTOK + b * blocks_per_seq + n, h)
    kv_map = lambda h, b, n: (b, h)
    cache_map = lambda h, b, n: (b, layer_in_kind, 0, h)
    return pl.pallas_call(
        _attn_nat_kernel,
        grid=(n_heads, N_SAMPLE, blocks_per_seq),
        in_specs=[pl.BlockSpec(memory_space=pltpu.SMEM),
                  pl.BlockSpec((NAT_QTOK, HEAD_DIM), q_map),
                  pl.BlockSpec((SAMPLE_SEQ, HEAD_DIM), kv_map),
                  pl.BlockSpec((SAMPLE_SEQ, HEAD_DIM), kv_map),
                  pl.BlockSpec((None, None, PAST_LEN, HEAD_DIM), cache_map),
                  pl.BlockSpec((None, None, PAST_LEN, HEAD_DIM), cache_map),
                  pl.BlockSpec((NAT_QTOK, HEAD_DIM), q_map)],
        out_specs=pl.BlockSpec((NAT_QTOK, HEAD_DIM), lambda h, b, n: (b * blocks_per_seq + n, h)),
        out_shape=jax.ShapeDtypeStruct((SAMPLE_TOKENS, n_heads * HEAD_DIM), BF16),
        scratch_shapes=[pltpu.VMEM((N_DR + 1, GRID_W, 2 * GRID_W), F32),
                        pltpu.VMEM((N_DR + 1, GRID_W, 2 * GRID_W), F32)],
        compiler_params=_params(3),
        name="attn_nat_latent",
    )(rel_bias.reshape(-1), q, k, v, cache_k, cache_v, z)


def _spatial_kernel(u_ref, v_ref, z_ref, g_ref, b_ref, ws_ref, bs_ref, o_ref):
    v = v_ref[...].astype(F32)
    mu = jnp.mean(v, axis=-1, keepdims=True)
    vc = v - mu
    var = jnp.mean(vc * vc, axis=-1, keepdims=True)
    vn = (vc * lax.rsqrt(var + EPS) * g_ref[...] + b_ref[...]).astype(BF16)
    for g in range(GMLP_GROUPS):
        sl = slice(g * GMLP_GROUP_WIDTH, (g + 1) * GMLP_GROUP_WIDTH)
        sv = jnp.dot(ws_ref[g].astype(BF16), vn[:, sl], preferred_element_type=F32) + bs_ref[:, g:g + 1]
        o_ref[:, sl] = (u_ref[:, sl].astype(F32) * sv * z_ref[:, sl].astype(F32)).astype(BF16)


def _spatial(u, v, z, ln_g, ln_b, w_s, b_s):
    row = pl.BlockSpec((CHUNK, GMLP_WIDTH), lambda i: (i, 0))
    vec = pl.BlockSpec((1, GMLP_WIDTH), lambda i: (0, 0))
    return pl.pallas_call(
        _spatial_kernel,
        grid=(TOKENS // CHUNK,),
        in_specs=[row, row, row, vec, vec,
                  pl.BlockSpec((GMLP_GROUPS, CHUNK, CHUNK), lambda i: (0, 0, 0)),
                  pl.BlockSpec((CHUNK, GMLP_GROUPS), lambda i: (0, 0))],
        out_specs=row,
        out_shape=jax.ShapeDtypeStruct((TOKENS, GMLP_WIDTH), BF16),
        compiler_params=_params(1),
        name="gmlp_spatial",
    )(u, v, z, ln_g.reshape(1, -1), ln_b.reshape(1, -1), w_s, b_s.T)


def kernel(x_prompt, x_sample, cache_win_k, cache_win_v, cache_nat_k, cache_nat_v, c, c_ctx,
           norm_g, w_ada, b_ada,
           win_w_in, win_q_norm, win_k_norm, win_sink, win_w_out,
           nat_w_in, nat_q_norm, nat_k_norm, nat_rel_bias, nat_w_out,
           gmlp_w_in, gmlp_ln_g, gmlp_ln_b, gmlp_w_s, gmlp_b_s, gmlp_w_out):
    depth = norm_g.shape[0]
    xp = x_prompt.reshape(PROMPT_TOKENS, D_MODEL)
    xs = x_sample.reshape(SAMPLE_TOKENS, D_MODEL)
    cond = jnp.zeros((COND_ROWS, D_MODEL), F32).at[:N_SAMPLE].set(c).at[CTX_COND_ROW].set(c_ctx)
    mod4 = _ada(cond, w_ada, b_ada).reshape(depth, COND_ROWS, 1, 3 * D_MODEL)
    norm_g3 = norm_g.reshape(depth, 1, D_MODEL)
    rope_tabs = _rope_tables()
    no_sink = jnp.zeros((1,), F32)

    n_win = win_w_in.shape[0]
    n_nat = nat_w_in.shape[0]
    win_kv_heads = cache_win_k.shape[3]
    win_kv_width = win_kv_heads * HEAD_DIM
    cwk = cache_win_k.reshape(N_SAMPLE, n_win, PAST_LEN, win_kv_width)
    cwv = cache_win_v.reshape(N_SAMPLE, n_win, PAST_LEN, win_kv_width)
    nat_heads = cache_nat_k.shape[3]
    cnk = cache_nat_k.reshape(N_SAMPLE, n_nat, PAST_LEN, nat_heads * HEAD_DIM)
    cnv = cache_nat_v.reshape(N_SAMPLE, n_nat, PAST_LEN, nat_heads * HEAD_DIM)

    new_win_k, new_win_v, new_nat_k, new_nat_v = [], [], [], []
    x = (xp, xs)
    hn = _norm_mod(xp, xs, norm_g3, mod4, 0)
    for layer in range(depth):
        kind = layer % 3
        li = layer // 3
        if kind == 0:
            nq = D_MODEL // PROJ_TN
            nkv = win_kv_width // PROJ_TN
            q = _query_proj(hn, win_w_in, li, 0, nq, win_q_norm[li], rope_tabs, "win_q")
            kp, ks = _key_proj(hn, win_w_in, li, nq, nkv, win_k_norm[li], rope_tabs, "win_k")
            vp, vs = _value_proj(hn, win_w_in, li, nq + nkv, nkv, "win_v")
            new_win_k.append(kp)
            new_win_v.append(vp)
            z = _act_proj(hn, win_w_in, li, nq + 2 * nkv, nq, _epi_silu, "win_z")
            a = (_attn_prompt(win_sink[li], q, kp, vp, z, win_kv_heads, True, "attn_win_prompt"),
                 _attn_win(win_sink[li], q, ks, vs, cwk, cwv, li, z, win_kv_heads))
            w_out, out_tm, out_name = win_w_out, 512, "win_out"
        elif kind == 1:
            nq = D_MODEL // PROJ_TN
            q = _query_proj(hn, nat_w_in, li, 0, nq, nat_q_norm[li], None, "nat_q")
            kp, ks = _key_proj(hn, nat_w_in, li, nq, nq, nat_k_norm[li], None, "nat_k")
            vp, vs = _value_proj(hn, nat_w_in, li, 2 * nq, nq, "nat_v")
            new_nat_k.append(kp)
            new_nat_v.append(vp)
            z = _act_proj(hn, nat_w_in, li, 3 * nq, nq, _epi_silu, "nat_z")
            a = (_attn_prompt(no_sink, q, kp, vp, z, nat_heads, False, "attn_nat_prompt"),
                 _attn_nat(nat_rel_bias[li], q, ks, vs, cnk, cnv, li, z))
            w_out, out_tm, out_name = nat_w_out, 512, "nat_out"
        else:
            nw = GMLP_WIDTH // PROJ_TN
            u = _act_proj(hn, gmlp_w_in, li, 0, nw, _epi_gelu, "gmlp_u")
            v = _act_proj(hn, gmlp_w_in, li, nw, nw, _epi_gelu, "gmlp_v")
            z = _act_proj(hn, gmlp_w_in, li, 2 * nw, nw, _epi_silu, "gmlp_z")
            a = _spatial(u, v, z, gmlp_ln_g[li], gmlp_ln_b[li], gmlp_w_s[li], gmlp_b_s[li])
            w_out, out_tm, out_name = gmlp_w_out, 256, "gmlp_out"
        if layer + 1 < depth:
            x, hn = _out_proj(a, w_out, li, x, mod4, layer, norm_g3, out_name, out_tm)
        else:
            (yp,) = _out_proj(a, w_out, li, x, mod4, layer, None, out_name + "_prompt", out_tm,
                              rows=(0, PROMPT_TOKENS))
            (ys,) = _out_proj(a, w_out, li, x, mod4, layer, None, out_name + "_latent", out_tm,
                              rows=(PROMPT_TOKENS, SAMPLE_TOKENS))

    cache_shape = lambda layers, heads: jnp.stack(
        [c_.reshape(N_PROMPT, PROMPT_SEQ, heads, HEAD_DIM) for c_ in layers], axis=1)
    return (yp.reshape(N_PROMPT, PROMPT_SEQ, D_MODEL), ys.reshape(N_SAMPLE, SAMPLE_SEQ, D_MODEL),
            cache_shape(new_win_k, win_kv_heads), cache_shape(new_win_v, win_kv_heads),
            cache_shape(new_nat_k, nat_heads), cache_shape(new_nat_v, nat_heads))
```

```python
import functools
import math

import jax
import jax.numpy as jnp
from jax import lax
from jax.experimental import pallas as pl
from jax.experimental.pallas import tpu as pltpu

F32 = jnp.float32
BF16 = jnp.bfloat16

D_MODEL = 2048
HEAD_DIM = 128
N_PROMPT = 16
PROMPT_SEQ = 256
N_SAMPLE = 2
SAMPLE_SEQ = 1024
PROMPT_TOKENS = N_PROMPT * PROMPT_SEQ
SAMPLE_TOKENS = N_SAMPLE * SAMPLE_SEQ
TOKENS = PROMPT_TOKENS + SAMPLE_TOKENS
PAST_LEN = 512
GRID_W = 64
EPS = 1e-6
NEG_INF = -1e30
ROPE_THETA = 10000.0
WINDOW = 128
NAT_ROWS = 8
NAT_COLS = 16
GMLP_WIDTH = 2 * D_MODEL
GMLP_GROUPS = 16
GMLP_GROUP_WIDTH = GMLP_WIDTH // GMLP_GROUPS
CHUNK = 128
CTX_COND_ROW = 2
COND_ROWS = 8
SM_SCALE = HEAD_DIM ** -0.5
LOG2E = math.log2(math.e)
QUERY_SCALE = SM_SCALE * LOG2E

VMEM_LIMIT = 56 * 1024 * 1024


def _params(n_axes, vmem=VMEM_LIMIT):
    return pltpu.CompilerParams(dimension_semantics=("arbitrary",) * n_axes,
                                vmem_limit_bytes=vmem)


def _cond_row(tok0):
    return jnp.where(tok0 < PROMPT_TOKENS, CTX_COND_ROW, (tok0 - PROMPT_TOKENS) // SAMPLE_SEQ)


def _silu(x):
    return x * (0.5 + 0.5 * jnp.tanh(0.5 * x))


def _gelu_tanh(x):
    return 0.5 * x * (1.0 + jnp.tanh(math.sqrt(2.0 / math.pi) * (x + 0.044715 * (x * x * x))))


def _split_specs(tm, width, tile0=0):
    n_p = PROMPT_TOKENS // tm
    return [pl.BlockSpec((tm, width), lambda i: (jnp.minimum(i + tile0, n_p - 1), 0)),
            pl.BlockSpec((tm, width), lambda i: (jnp.maximum(i + tile0 - n_p, 0), 0))]


def _ada_kernel(cond_ref, w_ref, b_ref, o_ref):
    s = _silu(cond_ref[...]).astype(BF16)
    o_ref[...] = jnp.dot(s, w_ref[...].astype(BF16), preferred_element_type=F32) + b_ref[...]


def _ada(cond, w_ada, b_ada, tn=1024):
    depth = w_ada.shape[0]
    n = w_ada.shape[2]
    return pl.pallas_call(
        _ada_kernel,
        grid=(depth, n // tn),
        in_specs=[pl.BlockSpec((COND_ROWS, D_MODEL), lambda l, j: (0, 0)),
                  pl.BlockSpec((None, D_MODEL, tn), lambda l, j: (l, 0, j)),
                  pl.BlockSpec((None, 1, tn), lambda l, j: (l, 0, j))],
        out_specs=pl.BlockSpec((None, COND_ROWS, tn), lambda l, j: (l, 0, j)),
        out_shape=jax.ShapeDtypeStruct((depth, COND_ROWS, n), F32),
        compiler_params=_params(2),
        name="ada_mod",
    )(cond, w_ada, b_ada.reshape(depth, 1, n))


def _norm_kernel(xp_ref, xs_ref, g_ref, shift_ref, scale_ref, o_ref, *, tm):
    x = jnp.where(pl.program_id(0) < PROMPT_TOKENS // tm, xp_ref[...], xs_ref[...])
    y = x * lax.rsqrt(jnp.mean(x * x, axis=-1, keepdims=True) + EPS) * g_ref[...]
    o_ref[...] = (y * (1.0 + scale_ref[...]) + shift_ref[...]).astype(BF16)


def _norm_mod(xp, xs, norm_g3, mod4, layer, tm=512):
    return pl.pallas_call(
        functools.partial(_norm_kernel, tm=tm),
        grid=(TOKENS // tm,),
        in_specs=[*_split_specs(tm, D_MODEL),
                  pl.BlockSpec((None, 1, D_MODEL), lambda i: (layer, 0, 0)),
                  pl.BlockSpec((None, None, 1, D_MODEL), lambda i: (layer, _cond_row(i * tm), 0, 0)),
                  pl.BlockSpec((None, None, 1, D_MODEL), lambda i: (layer, _cond_row(i * tm), 0, 1))],
        out_specs=pl.BlockSpec((tm, D_MODEL), lambda i: (i, 0)),
        out_shape=jax.ShapeDtypeStruct((TOKENS, D_MODEL), BF16),
        compiler_params=_params(1),
        name="norm_mod",
    )(xp, xs, norm_g3, mod4, mod4)


PROJ_TM = 1024
PROJ_TN = 512
PROJ_ROWS = 512
N_PROMPT_TILES = PROMPT_TOKENS // PROJ_TM


def _proj_kernel(*refs, epilogue, n_extra, n_out, by_tile_kind):
    a_ref, w_ref = refs[0], refs[1]
    extras = refs[2:2 + n_extra]
    outs = refs[2 + n_extra:2 + n_extra + n_out]
    wbf_ref = refs[2 + n_extra + n_out]

    @pl.when(pl.program_id(1) == 0)
    def _():
        wbf_ref[...] = w_ref[...].astype(BF16)

    def body(latent):
        for rc in range(PROJ_TM // PROJ_ROWS):
            rows = slice(rc * PROJ_ROWS, (rc + 1) * PROJ_ROWS)
            acc = jnp.dot(a_ref[rows, :], wbf_ref[...], preferred_element_type=F32)
            epilogue(acc, extras, outs, rc, latent)

    if by_tile_kind:
        is_latent = pl.program_id(1) >= N_PROMPT_TILES
        pl.when(is_latent)(lambda: body(True))
        pl.when(jnp.logical_not(is_latent))(lambda: body(False))
    else:
        body(None)


def _proj(a, w, layer, col_blk0, n_blk, epilogue, out_shapes, out_specs, name,
          extras=(), extra_specs=(), by_tile_kind=False):
    m, k = a.shape
    kern = functools.partial(_proj_kernel, epilogue=epilogue, n_extra=len(extras), n_out=len(out_shapes),
                             by_tile_kind=by_tile_kind)
    return pl.pallas_call(
        kern,
        grid=(n_blk, m // PROJ_TM),
        in_specs=[pl.BlockSpec((PROJ_TM, k), lambda j, i: (i, 0)),
                  pl.BlockSpec((None, k, PROJ_TN), lambda j, i: (layer, 0, col_blk0 + j)),
                  *extra_specs],
        out_specs=out_specs,
        out_shape=out_shapes,
        scratch_shapes=[pltpu.VMEM((k, PROJ_TN), BF16)],
        compiler_params=_params(2),
        name=name,
    )(a, w, *extras)


def _full_out(n_blk, dtype):
    return ([jax.ShapeDtypeStruct((TOKENS, n_blk * PROJ_TN), dtype)],
            [pl.BlockSpec((PROJ_TM, PROJ_TN), lambda j, i: (i, j))])


def _split_out(n_blk):
    seqs = PROJ_TM // PROMPT_SEQ
    shapes = [jax.ShapeDtypeStruct((N_PROMPT, PROMPT_SEQ, n_blk * PROJ_TN), F32),
              jax.ShapeDtypeStruct((SAMPLE_TOKENS, n_blk * PROJ_TN), BF16)]
    specs = [pl.BlockSpec((seqs, PROMPT_SEQ, PROJ_TN),
                          lambda j, i: (jnp.minimum(i, N_PROMPT_TILES - 1), 0, j)),
             pl.BlockSpec((PROJ_TM, PROJ_TN), lambda j, i: (jnp.maximum(i - N_PROMPT_TILES, 0), j))]
    return shapes, specs


def _chunk_rows(rc):
    return slice(rc * PROJ_ROWS, (rc + 1) * PROJ_ROWS)


def _epi_silu(acc, extras, outs, rc, latent):
    outs[0][_chunk_rows(rc), :] = _silu(acc).astype(BF16)


def _epi_gelu(acc, extras, outs, rc, latent):
    outs[0][_chunk_rows(rc), :] = _gelu_tanh(acc).astype(BF16)


def _store_split(y, outs, rc, sl, latent):
    if latent:
        outs[1][_chunk_rows(rc), sl] = y.astype(BF16)
    else:
        seqs = PROJ_ROWS // PROMPT_SEQ
        outs[0][rc * seqs:(rc + 1) * seqs, :, sl] = y.reshape(seqs, PROMPT_SEQ, y.shape[1])


def _epi_value(acc, extras, outs, rc, latent):
    _store_split(acc, outs, rc, slice(None), latent)


def _head_rmsnorm(acc, hh, g):
    a = acc[:, hh * HEAD_DIM:(hh + 1) * HEAD_DIM]
    return a * lax.rsqrt(jnp.mean(a * a, axis=-1, keepdims=True) + EPS) * g


def _normed_heads(acc, extras, rc, rope):
    g = extras[0][...]
    ys = [_head_rmsnorm(acc, hh, g) for hh in range(PROJ_TN // HEAD_DIM)]
    if rope:
        rows = _chunk_rows(rc)
        cos, sin, swap = extras[1][rows, :], extras[2][rows, :], extras[3][...]
        for pair in range(len(ys) // 2):
            both = jnp.concatenate(ys[2 * pair:2 * pair + 2], axis=1).astype(BF16)
            partner = jnp.dot(both, swap, preferred_element_type=F32)
            for t in range(2):
                hh = 2 * pair + t
                ys[hh] = ys[hh] * cos + partner[:, t * HEAD_DIM:(t + 1) * HEAD_DIM] * sin
    for hh, y in enumerate(ys):
        yield slice(hh * HEAD_DIM, (hh + 1) * HEAD_DIM), y


def _epi_query(acc, extras, outs, rc, latent, *, rope):
    for sl, y in _normed_heads(acc, extras, rc, rope and latent):
        outs[0][_chunk_rows(rc), sl] = (y * QUERY_SCALE).astype(BF16)


def _epi_key(acc, extras, outs, rc, latent, *, rope):
    for sl, y in _normed_heads(acc, extras, rc, rope and latent):
        _store_split(y, outs, rc, sl, latent)


def _rope_tables():
    nf = HEAD_DIM // 4
    t = jnp.arange(SAMPLE_SEQ)
    row = (t // GRID_W).astype(F32)
    col = (t % GRID_W).astype(F32)
    inv = ROPE_THETA ** (-jnp.arange(nf, dtype=F32) / nf)
    ang_r = row[:, None] * inv
    ang_c = col[:, None] * inv
    cos = jnp.concatenate([jnp.cos(ang_r), jnp.cos(ang_r), jnp.cos(ang_c), jnp.cos(ang_c)], axis=1)
    sin = jnp.concatenate([-jnp.sin(ang_r), jnp.sin(ang_r), -jnp.sin(ang_c), jnp.sin(ang_c)], axis=1)
    lanes = jnp.arange(2 * HEAD_DIM)
    swap = (lanes[:, None] == (lanes[None, :] ^ (HEAD_DIM // 4))).astype(BF16)
    return cos, sin, swap


def _gain_extras(gain, rope_tabs):
    extras = [gain.reshape(1, HEAD_DIM)]
    specs = [pl.BlockSpec((1, HEAD_DIM), lambda j, i: (0, 0))]
    if rope_tabs is not None:
        per_seq = SAMPLE_SEQ // PROJ_TM
        tab_map = lambda j, i: (jnp.maximum(i - N_PROMPT_TILES, 0) % per_seq, 0)
        extras += list(rope_tabs)
        specs += [pl.BlockSpec((PROJ_TM, HEAD_DIM), tab_map)] * 2
        specs.append(pl.BlockSpec((2 * HEAD_DIM, 2 * HEAD_DIM), lambda j, i: (0, 0)))
    return extras, specs


def _query_proj(hn, w, layer, col_blk0, n_blk, gain, rope_tabs, name):
    extras, specs = _gain_extras(gain, rope_tabs)
    epi = functools.partial(_epi_query, rope=rope_tabs is not None)
    return _proj(hn, w, layer, col_blk0, n_blk, epi, *_full_out(n_blk, BF16), name, extras, specs,
                 by_tile_kind=rope_tabs is not None)[0]


def _key_proj(hn, w, layer, col_blk0, n_blk, gain, rope_tabs, name):
    extras, specs = _gain_extras(gain, rope_tabs)
    epi = functools.partial(_epi_key, rope=rope_tabs is not None)
    return _proj(hn, w, layer, col_blk0, n_blk, epi, *_split_out(n_blk), name, extras, specs,
                 by_tile_kind=True)


def _value_proj(hn, w, layer, col_blk0, n_blk, name):
    return _proj(hn, w, layer, col_blk0, n_blk, _epi_value, *_split_out(n_blk), name, by_tile_kind=True)


def _act_proj(hn, w, layer, col_blk0, n_blk, epi, name):
    return _proj(hn, w, layer, col_blk0, n_blk, epi, *_full_out(n_blk, BF16), name)[0]


OUT_COLS = 512
W_STAGE_ROWS = 256


def _load_weight_bf16(w_hbm, layer, wbf_ref, stage_ref, sem):
    k = wbf_ref.shape[0]
    n_chunks = k // W_STAGE_ROWS

    def copy(c, slot):
        return pltpu.make_async_copy(w_hbm.at[layer, pl.ds(c * W_STAGE_ROWS, W_STAGE_ROWS), :],
                                     stage_ref.at[slot], sem.at[slot])

    copy(0, 0).start()
    for c in range(n_chunks):
        slot = c % 2
        if c + 1 < n_chunks:
            copy(c + 1, 1 - slot).start()
        copy(c, slot).wait()
        wbf_ref[c * W_STAGE_ROWS:(c + 1) * W_STAGE_ROWS, :] = stage_ref[slot].astype(BF16)


def _out_kernel(*refs, layer_in_kind, tile0, tm, split_a, split_x, fuse_norm):
    it = iter(refs)
    a_refs = [next(it), next(it)] if split_a else [next(it)]
    w_hbm = next(it)
    x_refs = [next(it), next(it)] if split_x else [next(it)]
    gate_ref = next(it)
    if fuse_norm:
        g_ref, shift_ref, scale_ref = next(it), next(it), next(it)
    xnew_ref = next(it)
    hn_ref = next(it) if fuse_norm else None
    wbf_ref, stage_ref, sem = next(it), next(it), next(it)

    i = pl.program_id(0)

    @pl.when(i == 0)
    def _():
        _load_weight_bf16(w_hbm, layer_in_kind, wbf_ref, stage_ref, sem)

    is_prompt = i + tile0 < PROMPT_TOKENS // tm
    a = jnp.where(is_prompt, a_refs[0][...], a_refs[1][...]) if split_a else a_refs[0][...]
    ssq = jnp.zeros((tm, 1), F32)
    for cb in range(D_MODEL // OUT_COLS):
        sl = slice(cb * OUT_COLS, (cb + 1) * OUT_COLS)
        acc = jnp.dot(a, wbf_ref[:, sl], preferred_element_type=F32)
        if split_x:
            x = jnp.where(is_prompt, x_refs[0][:, sl], x_refs[1][:, sl])
        else:
            x = x_refs[0][:, sl]
        xn = x + gate_ref[:, sl] * acc
        xnew_ref[:, sl] = xn
        if fuse_norm:
            ssq = ssq + jnp.sum(xn * xn, axis=-1, keepdims=True)
    if fuse_norm:
        rs = lax.rsqrt(ssq * (1.0 / D_MODEL) + EPS)
        for cb in range(D_MODEL // OUT_COLS):
            sl = slice(cb * OUT_COLS, (cb + 1) * OUT_COLS)
            gmul = g_ref[:, sl] * (1.0 + scale_ref[:, sl])
            hn_ref[:, sl] = (xnew_ref[:, sl] * rs * gmul + shift_ref[:, sl]).astype(BF16)


def _out_proj(a, w, layer_in_kind, x, mod4, layer, norm_g3, name, tm, rows=None):
    split_a = isinstance(a, tuple)
    k = a[0].shape[1] if split_a else a.shape[1]
    tok0, n_tok = rows if rows is not None else (0, TOKENS)
    tile0 = tok0 // tm
    split_x = isinstance(x, tuple)
    fuse_norm = norm_g3 is not None
    mod_spec = lambda part, lyr: pl.BlockSpec(
        (None, None, 1, D_MODEL), lambda i: (lyr, _cond_row((i + tile0) * tm), 0, part))
    row_spec = lambda width: pl.BlockSpec((tm, width), lambda i: (i + tile0, 0))
    operands = [*a, w] if split_a else [a, w]
    in_specs = [*(_split_specs(tm, k, tile0) if split_a else [row_spec(k)]),
                pl.BlockSpec(memory_space=pl.ANY)]
    if split_x:
        assert rows is None
        operands += list(x)
        in_specs += _split_specs(tm, D_MODEL)
    else:
        operands.append(x)
        in_specs.append(row_spec(D_MODEL))
    operands.append(mod4)
    in_specs.append(mod_spec(2, layer))
    out_shapes = [jax.ShapeDtypeStruct((n_tok, D_MODEL), F32)]
    out_specs = [pl.BlockSpec((tm, D_MODEL), lambda i: (i, 0))]
    if fuse_norm:
        operands += [norm_g3, mod4, mod4]
        in_specs += [pl.BlockSpec((None, 1, D_MODEL), lambda i: (layer + 1, 0, 0)),
                     mod_spec(0, layer + 1), mod_spec(1, layer + 1)]
        out_shapes.append(jax.ShapeDtypeStruct((n_tok, D_MODEL), BF16))
        out_specs.append(pl.BlockSpec((tm, D_MODEL), lambda i: (i, 0)))
    kern = functools.partial(_out_kernel, layer_in_kind=layer_in_kind, tile0=tile0, tm=tm,
                             split_a=split_a, split_x=split_x, fuse_norm=fuse_norm)
    return pl.pallas_call(
        kern,
        grid=(n_tok // tm,),
        in_specs=in_specs,
        out_specs=out_specs,
        out_shape=out_shapes,
        scratch_shapes=[pltpu.VMEM((k, D_MODEL), BF16),
                        pltpu.VMEM((2, W_STAGE_ROWS, D_MODEL), F32),
                        pltpu.SemaphoreType.DMA((2,))],
        compiler_params=_params(1),
        name=name,
    )(*operands)


def _dot_nt(a, b):
    return lax.dot_general(a, b, (((1,), (1,)), ((), ())), preferred_element_type=F32)


def _head(ref, h, rows=slice(None)):
    return ref[rows, h * HEAD_DIM:(h + 1) * HEAD_DIM]


def _with_ones(v):
    return jnp.concatenate([v, jnp.ones(v.shape, v.dtype)], axis=1)


def _lane_chunks(s):
    return [s[:, c * HEAD_DIM:(c + 1) * HEAD_DIM] for c in range(s.shape[1] // HEAD_DIM)]


def _softmax_pv(score_blocks, value_blocks, sink):
    rows = score_blocks[0].shape[0]
    mx = functools.reduce(jnp.maximum, [c for s in score_blocks for c in _lane_chunks(s)])
    m = jnp.broadcast_to(jnp.max(mx, axis=-1, keepdims=True), (rows, HEAD_DIM))
    if sink is not None:
        m = jnp.maximum(m, sink)
    o = None
    for s, v in zip(score_blocks, value_blocks):
        p = jnp.concatenate([jnp.exp2(c - m) for c in _lane_chunks(s)], axis=1).astype(BF16)
        part = jnp.dot(p, v, preferred_element_type=F32)
        o = part if o is None else o + part
    den = o[:, HEAD_DIM:]
    if sink is not None:
        den = den + jnp.exp2(sink - m)
    return o[:, :HEAD_DIM] * (1.0 / den)


def _attn_prompt_kernel(sink_ref, q_ref, k_ref, v_ref, z_ref, o_ref, *, n_heads, n_kv, use_sink):
    grp = n_heads // n_kv
    for g in range(n_kv):
        kg = _head(k_ref, g).astype(BF16)
        vg = _with_ones(_head(v_ref, g).astype(BF16))
        for h in range(g * grp, (g + 1) * grp):
            s = _dot_nt(_head(q_ref, h), kg)
            sink = jnp.full((PROMPT_SEQ, HEAD_DIM), sink_ref[h], F32) if use_sink else None
            o = _softmax_pv([s], [vg], sink)
            o_ref[:, h * HEAD_DIM:(h + 1) * HEAD_DIM] = (o * _head(z_ref, h).astype(F32)).astype(BF16)


def _attn_prompt(sink, q, k_cache, v_cache, z, n_kv, use_sink, name):
    width = q.shape[1]
    kv_width = k_cache.shape[2]
    kern = functools.partial(_attn_prompt_kernel, n_heads=width // HEAD_DIM, n_kv=n_kv, use_sink=use_sink)
    cache_spec = pl.BlockSpec((None, PROMPT_SEQ, kv_width), lambda b: (b, 0, 0))
    return pl.pallas_call(
        kern,
        grid=(N_PROMPT,),
        in_specs=[pl.BlockSpec(memory_space=pltpu.SMEM),
                  pl.BlockSpec((PROMPT_SEQ, width), lambda b: (b, 0)),
                  cache_spec, cache_spec,
                  pl.BlockSpec((PROMPT_SEQ, width), lambda b: (b, 0))],
        out_specs=pl.BlockSpec((PROMPT_SEQ, width), lambda b: (b, 0)),
        out_shape=jax.ShapeDtypeStruct((PROMPT_TOKENS, width), BF16),
        compiler_params=_params(1),
        name=name,
    )(sink, q, k_cache, v_cache, z)


WIN_BAND = 3 * WINDOW


def _attn_win_kernel(sink_ref, q_ref, k_ref, v_ref, kc_ref, vc_ref, z_ref, o_ref, *, n_heads, n_kv):
    grp = n_heads // n_kv
    n = pl.program_id(1)
    start = pl.multiple_of(jnp.clip((n - 1) * WINDOW, 0, SAMPLE_SEQ - WIN_BAND), WINDOW)
    shape = (grp * WINDOW, WIN_BAND)
    qpos = n * WINDOW + (lax.broadcasted_iota(jnp.int32, shape, 0) & (WINDOW - 1))
    kpos = start + lax.broadcasted_iota(jnp.int32, shape, 1)
    valid = jnp.abs(kpos - qpos) <= WINDOW
    band = pl.ds(start, WIN_BAND)
    for g in range(n_kv):
        heads = [g * grp + t for t in range(grp)]
        qs = jnp.concatenate([_head(q_ref, h) for h in heads], axis=0)
        kb = _head(k_ref, g, band)
        vb = _with_ones(_head(v_ref, g, band))
        kc = _head(kc_ref, g).astype(BF16)
        vc = _with_ones(_head(vc_ref, g).astype(BF16))
        s_band = jnp.where(valid, _dot_nt(qs, kb), NEG_INF)
        s_ctx = _dot_nt(qs, kc)
        sink = jnp.concatenate([jnp.full((WINDOW, HEAD_DIM), sink_ref[h], F32) for h in heads], axis=0)
        o = _softmax_pv([s_band, s_ctx], [vb, vc], sink)
        for t, h in enumerate(heads):
            oh = o[t * WINDOW:(t + 1) * WINDOW]
            o_ref[:, h * HEAD_DIM:(h + 1) * HEAD_DIM] = (oh * _head(z_ref, h).astype(F32)).astype(BF16)


def _attn_win(sink, q, k, v, cache_k, cache_v, layer_in_kind, z, n_kv):
    width = q.shape[1]
    kv_width = k.shape[1]
    blocks_per_seq = SAMPLE_SEQ // WINDOW
    q_map = lambda b, n: (PROMPT_TOKENS // WINDOW + b * blocks_per_seq + n, 0)
    kv_map = lambda b, n: (b, 0)
    cache_map = lambda b, n: (b, layer_in_kind, 0, 0)
    kern = functools.partial(_attn_win_kernel, n_heads=width // HEAD_DIM, n_kv=n_kv)
    return pl.pallas_call(
        kern,
        grid=(N_SAMPLE, blocks_per_seq),
        in_specs=[pl.BlockSpec(memory_space=pltpu.SMEM),
                  pl.BlockSpec((WINDOW, width), q_map),
                  pl.BlockSpec((SAMPLE_SEQ, kv_width), kv_map),
                  pl.BlockSpec((SAMPLE_SEQ, kv_width), kv_map),
                  pl.BlockSpec((None, None, PAST_LEN, kv_width), cache_map),
                  pl.BlockSpec((None, None, PAST_LEN, kv_width), cache_map),
                  pl.BlockSpec((WINDOW, width), q_map)],
        out_specs=pl.BlockSpec((WINDOW, width), lambda b, n: (b * blocks_per_seq + n, 0)),
        out_shape=jax.ShapeDtypeStruct((SAMPLE_TOKENS, width), BF16),
        compiler_params=_params(2),
        name="attn_win_latent",
    )(sink, q, k, v, cache_k, cache_v, z)


NAT_QROWS = 4
GRID_ROWS = SAMPLE_SEQ // GRID_W
N_DR = 2 * NAT_ROWS - 1
MASKED_TILE = N_DR


def _nat_row_start(qr):
    return min(max(qr - NAT_ROWS // 2, 0), GRID_ROWS - NAT_ROWS)


def _attn_nat_kernel(bias_ref, q_ref, k_ref, v_ref, kc_ref, vc_ref, z_ref, o_ref, left_ref, right_ref):
    @pl.when(pl.program_id(1) == 0)
    def _():
        shape = (GRID_W, 2 * GRID_W)
        lane = lax.broadcasted_iota(jnp.int32, shape, 1)
        qc = lax.broadcasted_iota(jnp.int32, shape, 0)
        kc_ = lane & (GRID_W - 1)
        cs = jnp.clip(qc - NAT_COLS // 2, 0, GRID_W - NAT_COLS)
        col_ok = jnp.logical_and(kc_ >= cs, kc_ < cs + NAT_COLS)
        is_left = lane < GRID_W
        lanes = 2 * GRID_W
        for dri in range(N_DR):
            row = jnp.broadcast_to(bias_ref[dri:dri + 1, :], shape)
            on_left = pltpu.roll(row, lanes - (NAT_COLS - 1), 1, stride=1, stride_axis=0)
            on_right = pltpu.roll(row, GRID_W - (NAT_COLS - 1), 1, stride=1, stride_axis=0)
            left_ref[dri] = jnp.where(is_left, jnp.where(col_ok, on_left, NEG_INF), 0.0)
            right_ref[dri] = jnp.where(is_left, 0.0, jnp.where(col_ok, on_right, NEG_INF))
        left_ref[MASKED_TILE] = jnp.where(is_left, NEG_INF, 0.0)
        right_ref[MASKED_TILE] = jnp.where(is_left, 0.0, NEG_INF)

    kc = kc_ref[...].astype(BF16)
    vc = _with_ones(vc_ref[...].astype(BF16))
    for qb in range(GRID_ROWS // NAT_QROWS):
        qrows = range(qb * NAT_QROWS, (qb + 1) * NAT_QROWS)
        krow0 = _nat_row_start(qrows[0]) // 2 * 2
        krow1 = -(-(_nat_row_start(qrows[-1]) + NAT_ROWS) // 2) * 2
        kwin = slice(krow0 * GRID_W, krow1 * GRID_W)
        qwin = slice(qrows[0] * GRID_W, (qrows[-1] + 1) * GRID_W)

        def tile_index(qr, kr):
            rs = _nat_row_start(qr)
            return kr - qr + (NAT_ROWS - 1) if rs <= kr < rs + NAT_ROWS else MASKED_TILE

        bias = jnp.concatenate(
            [jnp.concatenate([left_ref[tile_index(qr, kr)] + right_ref[tile_index(qr, kr + 1)]
                              for kr in range(krow0, krow1, 2)], axis=1)
             for qr in qrows], axis=0)

        q = q_ref[qwin, :]
        s_nb = _dot_nt(q, k_ref[kwin, :]) + bias
        s_ctx = _dot_nt(q, kc)
        o = _softmax_pv([s_nb, s_ctx], [_with_ones(v_ref[kwin, :]), vc], None)
        o_ref[qwin, :] = (o * z_ref[qwin, :].astype(F32)).astype(BF16)


def _attn_nat(rel_bias, q, k, v, cache_k, cache_v, layer_in_kind, z):
    n_heads = q.shape[1] // HEAD_DIM
    bias_rows = jnp.pad(rel_bias, ((0, 0), (0, N_DR + 1 - rel_bias.shape[1]),
                                   (0, 2 * GRID_W - rel_bias.shape[2])))
    q_map = lambda h, b: (PROMPT_TOKENS // SAMPLE_SEQ + b, h)
    kv_map = lambda h, b: (b, h)
    cache_map = lambda h, b: (b, layer_in_kind, 0, h)
    return pl.pallas_call(
        _attn_nat_kernel,
        grid=(n_heads, N_SAMPLE),
        in_specs=[pl.BlockSpec((None, N_DR + 1, 2 * GRID_W), lambda h, b: (h, 0, 0)),
                  pl.BlockSpec((SAMPLE_SEQ, HEAD_DIM), q_map),
                  pl.BlockSpec((SAMPLE_SEQ, HEAD_DIM), kv_map),
                  pl.BlockSpec((SAMPLE_SEQ, HEAD_DIM), kv_map),
                  pl.BlockSpec((None, None, PAST_LEN, HEAD_DIM), cache_map),
                  pl.BlockSpec((None, None, PAST_LEN, HEAD_DIM), cache_map),
                  pl.BlockSpec((SAMPLE_SEQ, HEAD_DIM), q_map)],
        out_specs=pl.BlockSpec((SAMPLE_SEQ, HEAD_DIM), kv_map),
        out_shape=jax.ShapeDtypeStruct((SAMPLE_TOKENS, n_heads * HEAD_DIM), BF16),
        scratch_shapes=[pltpu.VMEM((N_DR + 1, GRID_W, 2 * GRID_W), F32),
                        pltpu.VMEM((N_DR + 1, GRID_W, 2 * GRID_W), F32)],
        compiler_params=_params(2),
        name="attn_nat_latent",
    )(bias_rows, q, k, v, cache_k, cache_v, z)


def _spatial_kernel(u_ref, v_ref, z_ref, g_ref, b_ref, ws_ref, bs_ref, o_ref):
    v = v_ref[...].astype(F32)
    mu = jnp.mean(v, axis=-1, keepdims=True)
    vc = v - mu
    var = jnp.mean(vc * vc, axis=-1, keepdims=True)
    vn = (vc * lax.rsqrt(var + EPS) * g_ref[...] + b_ref[...]).astype(BF16)
    for g in range(GMLP_GROUPS):
        sl = slice(g * GMLP_GROUP_WIDTH, (g + 1) * GMLP_GROUP_WIDTH)
        sv = jnp.dot(ws_ref[g].astype(BF16), vn[:, sl], preferred_element_type=F32) + bs_ref[:, g:g + 1]
        o_ref[:, sl] = (u_ref[:, sl].astype(F32) * sv * z_ref[:, sl].astype(F32)).astype(BF16)


def _spatial(u, v, z, ln_g, ln_b, w_s, b_s):
    row = pl.BlockSpec((CHUNK, GMLP_WIDTH), lambda i: (i, 0))
    vec = pl.BlockSpec((1, GMLP_WIDTH), lambda i: (0, 0))
    return pl.pallas_call(
        _spatial_kernel,
        grid=(TOKENS // CHUNK,),
        in_specs=[row, row, row, vec, vec,
                  pl.BlockSpec((GMLP_GROUPS, CHUNK, CHUNK), lambda i: (0, 0, 0)),
                  pl.BlockSpec((CHUNK, GMLP_GROUPS), lambda i: (0, 0))],
        out_specs=row,
        out_shape=jax.ShapeDtypeStruct((TOKENS, GMLP_WIDTH), BF16),
        compiler_params=_params(1),
        name="gmlp_spatial",
    )(u, v, z, ln_g.reshape(1, -1), ln_b.reshape(1, -1), w_s, b_s.T)


def kernel(x_prompt, x_sample, cache_win_k, cache_win_v, cache_nat_k, cache_nat_v, c, c_ctx,
           norm_g, w_ada, b_ada,
           win_w_in, win_q_norm, win_k_norm, win_sink, win_w_out,
           nat_w_in, nat_q_norm, nat_k_norm, nat_rel_bias, nat_w_out,
           gmlp_w_in, gmlp_ln_g, gmlp_ln_b, gmlp_w_s, gmlp_b_s, gmlp_w_out):
    depth = norm_g.shape[0]
    xp = x_prompt.reshape(PROMPT_TOKENS, D_MODEL)
    xs = x_sample.reshape(SAMPLE_TOKENS, D_MODEL)
    cond = jnp.zeros((COND_ROWS, D_MODEL), F32).at[:N_SAMPLE].set(c).at[CTX_COND_ROW].set(c_ctx)
    mod4 = _ada(cond, w_ada, b_ada).reshape(depth, COND_ROWS, 1, 3 * D_MODEL)
    norm_g3 = norm_g.reshape(depth, 1, D_MODEL)
    rope_tabs = _rope_tables()
    no_sink = jnp.zeros((1,), F32)

    n_win = win_w_in.shape[0]
    n_nat = nat_w_in.shape[0]
    win_kv_heads = cache_win_k.shape[3]
    win_kv_width = win_kv_heads * HEAD_DIM
    cwk = cache_win_k.reshape(N_SAMPLE, n_win, PAST_LEN, win_kv_width)
    cwv = cache_win_v.reshape(N_SAMPLE, n_win, PAST_LEN, win_kv_width)
    nat_heads = cache_nat_k.shape[3]
    cnk = cache_nat_k.reshape(N_SAMPLE, n_nat, PAST_LEN, nat_heads * HEAD_DIM)
    cnv = cache_nat_v.reshape(N_SAMPLE, n_nat, PAST_LEN, nat_heads * HEAD_DIM)

    new_win_k, new_win_v, new_nat_k, new_nat_v = [], [], [], []
    x = (xp, xs)
    hn = _norm_mod(xp, xs, norm_g3, mod4, 0)
    for layer in range(depth):
        kind = layer % 3
        li = layer // 3
        if kind == 0:
            nq = D_MODEL // PROJ_TN
            nkv = win_kv_width // PROJ_TN
            q = _query_proj(hn, win_w_in, li, 0, nq, win_q_norm[li], rope_tabs, "win_q")
            kp, ks = _key_proj(hn, win_w_in, li, nq, nkv, win_k_norm[li], rope_tabs, "win_k")
            vp, vs = _value_proj(hn, win_w_in, li, nq + nkv, nkv, "win_v")
            new_win_k.append(kp)
            new_win_v.append(vp)
            z = _act_proj(hn, win_w_in, li, nq + 2 * nkv, nq, _epi_silu, "win_z")
            sink = win_sink[li] * LOG2E
            a = (_attn_prompt(sink, q, kp, vp, z, win_kv_heads, True, "attn_win_prompt"),
                 _attn_win(sink, q, ks, vs, cwk, cwv, li, z, win_kv_heads))
            w_out, out_tm, out_name = win_w_out, 512, "win_out"
        elif kind == 1:
            nq = D_MODEL // PROJ_TN
            q = _query_proj(hn, nat_w_in, li, 0, nq, nat_q_norm[li], None, "nat_q")
            kp, ks = _key_proj(hn, nat_w_in, li, nq, nq, nat_k_norm[li], None, "nat_k")
            vp, vs = _value_proj(hn, nat_w_in, li, 2 * nq, nq, "nat_v")
            new_nat_k.append(kp)
            new_nat_v.append(vp)
            z = _act_proj(hn, nat_w_in, li, 3 * nq, nq, _epi_silu, "nat_z")
            a = (_attn_prompt(no_sink, q, kp, vp, z, nat_heads, False, "attn_nat_prompt"),
                 _attn_nat(nat_rel_bias[li] * LOG2E, q, ks, vs, cnk, cnv, li, z))
            w_out, out_tm, out_name = nat_w_out, 512, "nat_out"
        else:
            nw = GMLP_WIDTH // PROJ_TN
            u = _act_proj(hn, gmlp_w_in, li, 0, nw, _epi_gelu, "gmlp_u")
            v = _act_proj(hn, gmlp_w_in, li, nw, nw, _epi_gelu, "gmlp_v")
            z = _act_proj(hn, gmlp_w_in, li, 2 * nw, nw, _epi_silu, "gmlp_z")
            a = _spatial(u, v, z, gmlp_ln_g[li], gmlp_ln_b[li], gmlp_w_s[li], gmlp_b_s[li])
            w_out, out_tm, out_name = gmlp_w_out, 256, "gmlp_out"
        if layer + 1 < depth:
            x, hn = _out_proj(a, w_out, li, x, mod4, layer, norm_g3, out_name, out_tm)
        else:
            (yp,) = _out_proj(a, w_out, li, x, mod4, layer, None, out_name + "_prompt", out_tm,
                              rows=(0, PROMPT_TOKENS))
            (ys,) = _out_proj(a, w_out, li, x, mod4, layer, None, out_name + "_latent", out_tm,
                              rows=(PROMPT_TOKENS, SAMPLE_TOKENS))

    cache_shape = lambda layers, heads: jnp.stack(
        [c_.reshape(N_PROMPT, PROMPT_SEQ, heads, HEAD_DIM) for c_ in layers], axis=1)
    return (yp.reshape(N_PROMPT, PROMPT_SEQ, D_MODEL), ys.reshape(N_SAMPLE, SAMPLE_SEQ, D_MODEL),
            cache_shape(new_win_k, win_kv_heads), cache_shape(new_win_v, win_kv_heads),
            cache_shape(new_nat_k, nat_heads), cache_shape(new_nat_v, nat_heads))
```

```python
import functools
import math

import jax
import jax.numpy as jnp
from jax import lax
from jax.experimental import pallas as pl
from jax.experimental.pallas import tpu as pltpu

F32 = jnp.float32
BF16 = jnp.bfloat16

D_MODEL = 2048
HEAD_DIM = 128
N_PROMPT = 16
PROMPT_SEQ = 256
N_SAMPLE = 2
SAMPLE_SEQ = 1024
PROMPT_TOKENS = N_PROMPT * PROMPT_SEQ
SAMPLE_TOKENS = N_SAMPLE * SAMPLE_SEQ
TOKENS = PROMPT_TOKENS + SAMPLE_TOKENS
PAST_LEN = 512
GRID_W = 64
EPS = 1e-6
NEG_INF = -1e30
ROPE_THETA = 10000.0
WINDOW = 128
NAT_ROWS = 8
NAT_COLS = 16
GMLP_WIDTH = 2 * D_MODEL
GMLP_GROUPS = 16
GMLP_GROUP_WIDTH = GMLP_WIDTH // GMLP_GROUPS
CHUNK = 128
CTX_COND_ROW = 2
COND_ROWS = 8
SM_SCALE = HEAD_DIM ** -0.5
LOG2E = math.log2(math.e)
QUERY_SCALE = SM_SCALE * LOG2E

VMEM_LIMIT = 56 * 1024 * 1024


def _params(n_axes, vmem=VMEM_LIMIT, flags=None):
    return pltpu.CompilerParams(dimension_semantics=("arbitrary",) * n_axes,
                                vmem_limit_bytes=vmem, flags=flags)


def _cond_row(tok0):
    return jnp.where(tok0 < PROMPT_TOKENS, CTX_COND_ROW, (tok0 - PROMPT_TOKENS) // SAMPLE_SEQ)


def _silu(x):
    return x * (0.5 + 0.5 * jnp.tanh(0.5 * x))


def _gelu_tanh(x):
    return 0.5 * x * (1.0 + jnp.tanh(math.sqrt(2.0 / math.pi) * (x + 0.044715 * (x * x * x))))


def _split_specs(tm, width, tile0=0):
    n_p = PROMPT_TOKENS // tm
    return [pl.BlockSpec((tm, width), lambda i: (jnp.minimum(i + tile0, n_p - 1), 0)),
            pl.BlockSpec((tm, width), lambda i: (jnp.maximum(i + tile0 - n_p, 0), 0))]


def _ada_kernel(cond_ref, w_ref, b_ref, o_ref):
    s = _silu(cond_ref[...]).astype(BF16)
    o_ref[...] = jnp.dot(s, w_ref[...].astype(BF16), preferred_element_type=F32) + b_ref[...]


def _ada(cond, w_ada, b_ada, tn=1024):
    depth = w_ada.shape[0]
    n = w_ada.shape[2]
    return pl.pallas_call(
        _ada_kernel,
        grid=(depth, n // tn),
        in_specs=[pl.BlockSpec((COND_ROWS, D_MODEL), lambda l, j: (0, 0)),
                  pl.BlockSpec((None, D_MODEL, tn), lambda l, j: (l, 0, j)),
                  pl.BlockSpec((None, 1, tn), lambda l, j: (l, 0, j))],
        out_specs=pl.BlockSpec((None, COND_ROWS, tn), lambda l, j: (l, 0, j)),
        out_shape=jax.ShapeDtypeStruct((depth, COND_ROWS, n), F32),
        compiler_params=_params(2),
        name="ada_mod",
    )(cond, w_ada, b_ada.reshape(depth, 1, n))


def _norm_kernel(xp_ref, xs_ref, g_ref, shift_ref, scale_ref, o_ref, *, tm):
    x = jnp.where(pl.program_id(0) < PROMPT_TOKENS // tm, xp_ref[...], xs_ref[...])
    y = x * lax.rsqrt(jnp.mean(x * x, axis=-1, keepdims=True) + EPS) * g_ref[...]
    o_ref[...] = (y * (1.0 + scale_ref[...]) + shift_ref[...]).astype(BF16)


def _norm_mod(xp, xs, norm_g3, mod4, layer, tm=512):
    return pl.pallas_call(
        functools.partial(_norm_kernel, tm=tm),
        grid=(TOKENS // tm,),
        in_specs=[*_split_specs(tm, D_MODEL),
                  pl.BlockSpec((None, 1, D_MODEL), lambda i: (layer, 0, 0)),
                  pl.BlockSpec((None, None, 1, D_MODEL), lambda i: (layer, _cond_row(i * tm), 0, 0)),
                  pl.BlockSpec((None, None, 1, D_MODEL), lambda i: (layer, _cond_row(i * tm), 0, 1))],
        out_specs=pl.BlockSpec((tm, D_MODEL), lambda i: (i, 0)),
        out_shape=jax.ShapeDtypeStruct((TOKENS, D_MODEL), BF16),
        compiler_params=_params(1),
        name="norm_mod",
    )(xp, xs, norm_g3, mod4, mod4)


PROJ_TM = 2048
PROJ_TN = 512
PROJ_ROWS = 512
N_PROMPT_TILES = PROMPT_TOKENS // PROJ_TM


def _proj_kernel(*refs, epilogue, n_extra, n_out, by_tile_kind):
    a_ref, w_ref = refs[0], refs[1]
    extras = refs[2:2 + n_extra]
    outs = refs[2 + n_extra:2 + n_extra + n_out]
    wbf_ref = refs[2 + n_extra + n_out]

    @pl.when(pl.program_id(1) == 0)
    def _():
        wbf_ref[...] = w_ref[...].astype(BF16)

    def body(latent):
        for rc in range(PROJ_TM // PROJ_ROWS):
            rows = slice(rc * PROJ_ROWS, (rc + 1) * PROJ_ROWS)
            acc = jnp.dot(a_ref[rows, :], wbf_ref[...], preferred_element_type=F32)
            epilogue(acc, extras, outs, rc, latent)

    if by_tile_kind:
        is_latent = pl.program_id(1) >= N_PROMPT_TILES
        pl.when(is_latent)(lambda: body(True))
        pl.when(jnp.logical_not(is_latent))(lambda: body(False))
    else:
        body(None)


def _proj(a, w, layer, col_blk0, n_blk, epilogue, out_shapes, out_specs, name,
          extras=(), extra_specs=(), by_tile_kind=False):
    m, k = a.shape
    kern = functools.partial(_proj_kernel, epilogue=epilogue, n_extra=len(extras), n_out=len(out_shapes),
                             by_tile_kind=by_tile_kind)
    return pl.pallas_call(
        kern,
        grid=(n_blk, m // PROJ_TM),
        in_specs=[pl.BlockSpec((PROJ_TM, k), lambda j, i: (i, 0)),
                  pl.BlockSpec((None, k, PROJ_TN), lambda j, i: (layer, 0, col_blk0 + j)),
                  *extra_specs],
        out_specs=out_specs,
        out_shape=out_shapes,
        scratch_shapes=[pltpu.VMEM((k, PROJ_TN), BF16)],
        compiler_params=_params(2),
        name=name,
    )(a, w, *extras)


def _full_out(n_blk, dtype):
    return ([jax.ShapeDtypeStruct((TOKENS, n_blk * PROJ_TN), dtype)],
            [pl.BlockSpec((PROJ_TM, PROJ_TN), lambda j, i: (i, j))])


def _split_out(n_blk):
    seqs = PROJ_TM // PROMPT_SEQ
    shapes = [jax.ShapeDtypeStruct((N_PROMPT, PROMPT_SEQ, n_blk * PROJ_TN), F32),
              jax.ShapeDtypeStruct((SAMPLE_TOKENS, n_blk * PROJ_TN), BF16)]
    specs = [pl.BlockSpec((seqs, PROMPT_SEQ, PROJ_TN),
                          lambda j, i: (jnp.minimum(i, N_PROMPT_TILES - 1), 0, j)),
             pl.BlockSpec((PROJ_TM, PROJ_TN), lambda j, i: (jnp.maximum(i - N_PROMPT_TILES, 0), j))]
    return shapes, specs


def _chunk_rows(rc):
    return slice(rc * PROJ_ROWS, (rc + 1) * PROJ_ROWS)


def _epi_silu(acc, extras, outs, rc, latent):
    outs[0][_chunk_rows(rc), :] = _silu(acc).astype(BF16)


def _epi_gelu(acc, extras, outs, rc, latent):
    outs[0][_chunk_rows(rc), :] = _gelu_tanh(acc).astype(BF16)


def _store_split(y, outs, rc, sl, latent):
    if latent:
        outs[1][_chunk_rows(rc), sl] = y.astype(BF16)
    else:
        seqs = PROJ_ROWS // PROMPT_SEQ
        outs[0][rc * seqs:(rc + 1) * seqs, :, sl] = y.reshape(seqs, PROMPT_SEQ, y.shape[1])


def _epi_value(acc, extras, outs, rc, latent):
    _store_split(acc, outs, rc, slice(None), latent)


def _head_rmsnorm(acc, hh, g):
    a = acc[:, hh * HEAD_DIM:(hh + 1) * HEAD_DIM]
    return a * lax.rsqrt(jnp.mean(a * a, axis=-1, keepdims=True) + EPS) * g


def _normed_heads(acc, extras, rc, rope):
    g = extras[0][...]
    ys = [_head_rmsnorm(acc, hh, g) for hh in range(PROJ_TN // HEAD_DIM)]
    if rope:
        rows = _chunk_rows(rc)
        cos, sin, swap = extras[1][rows, :], extras[2][rows, :], extras[3][...]
        for pair in range(len(ys) // 2):
            both = jnp.concatenate(ys[2 * pair:2 * pair + 2], axis=1).astype(BF16)
            partner = jnp.dot(both, swap, preferred_element_type=F32)
            for t in range(2):
                hh = 2 * pair + t
                ys[hh] = ys[hh] * cos + partner[:, t * HEAD_DIM:(t + 1) * HEAD_DIM] * sin
    for hh, y in enumerate(ys):
        yield slice(hh * HEAD_DIM, (hh + 1) * HEAD_DIM), y


def _epi_query(acc, extras, outs, rc, latent, *, rope):
    for sl, y in _normed_heads(acc, extras, rc, rope and latent):
        outs[0][_chunk_rows(rc), sl] = (y * QUERY_SCALE).astype(BF16)


def _epi_key(acc, extras, outs, rc, latent, *, rope):
    for sl, y in _normed_heads(acc, extras, rc, rope and latent):
        _store_split(y, outs, rc, sl, latent)


def _rope_tables():
    nf = HEAD_DIM // 4
    t = jnp.arange(SAMPLE_TOKENS) % SAMPLE_SEQ
    row = (t // GRID_W).astype(F32)
    col = (t % GRID_W).astype(F32)
    inv = ROPE_THETA ** (-jnp.arange(nf, dtype=F32) / nf)
    ang_r = row[:, None] * inv
    ang_c = col[:, None] * inv
    cos = jnp.concatenate([jnp.cos(ang_r), jnp.cos(ang_r), jnp.cos(ang_c), jnp.cos(ang_c)], axis=1)
    sin = jnp.concatenate([-jnp.sin(ang_r), jnp.sin(ang_r), -jnp.sin(ang_c), jnp.sin(ang_c)], axis=1)
    lanes = jnp.arange(2 * HEAD_DIM)
    swap = (lanes[:, None] == (lanes[None, :] ^ (HEAD_DIM // 4))).astype(BF16)
    return cos, sin, swap


def _gain_extras(gain, rope_tabs):
    extras = [gain.reshape(1, HEAD_DIM)]
    specs = [pl.BlockSpec((1, HEAD_DIM), lambda j, i: (0, 0))]
    if rope_tabs is not None:
        tab_map = lambda j, i: (jnp.maximum(i - N_PROMPT_TILES, 0), 0)
        extras += list(rope_tabs)
        specs += [pl.BlockSpec((PROJ_TM, HEAD_DIM), tab_map)] * 2
        specs.append(pl.BlockSpec((2 * HEAD_DIM, 2 * HEAD_DIM), lambda j, i: (0, 0)))
    return extras, specs


def _query_proj(hn, w, layer, col_blk0, n_blk, gain, rope_tabs, name):
    extras, specs = _gain_extras(gain, rope_tabs)
    epi = functools.partial(_epi_query, rope=rope_tabs is not None)
    return _proj(hn, w, layer, col_blk0, n_blk, epi, *_full_out(n_blk, BF16), name, extras, specs,
                 by_tile_kind=rope_tabs is not None)[0]


def _key_proj(hn, w, layer, col_blk0, n_blk, gain, rope_tabs, name):
    extras, specs = _gain_extras(gain, rope_tabs)
    epi = functools.partial(_epi_key, rope=rope_tabs is not None)
    return _proj(hn, w, layer, col_blk0, n_blk, epi, *_split_out(n_blk), name, extras, specs,
                 by_tile_kind=True)


def _value_proj(hn, w, layer, col_blk0, n_blk, name):
    return _proj(hn, w, layer, col_blk0, n_blk, _epi_value, *_split_out(n_blk), name, by_tile_kind=True)


def _act_proj(hn, w, layer, col_blk0, n_blk, epi, name):
    return _proj(hn, w, layer, col_blk0, n_blk, epi, *_full_out(n_blk, BF16), name)[0]


OUT_COLS = 512
W_STAGE_ROWS = 256


def _load_weight_bf16(w_hbm, layer, wbf_ref, stage_ref, sem):
    k = wbf_ref.shape[0]
    n_chunks = k // W_STAGE_ROWS

    def copy(c, slot):
        return pltpu.make_async_copy(w_hbm.at[layer, pl.ds(c * W_STAGE_ROWS, W_STAGE_ROWS), :],
                                     stage_ref.at[slot], sem.at[slot])

    copy(0, 0).start()
    for c in range(n_chunks):
        slot = c % 2
        if c + 1 < n_chunks:
            copy(c + 1, 1 - slot).start()
        copy(c, slot).wait()
        wbf_ref[c * W_STAGE_ROWS:(c + 1) * W_STAGE_ROWS, :] = stage_ref[slot].astype(BF16)


def _out_kernel(*refs, layer_in_kind, tile0, tm, split_a, split_x, fuse_norm):
    it = iter(refs)
    a_refs = [next(it), next(it)] if split_a else [next(it)]
    w_hbm = next(it)
    x_refs = [next(it), next(it)] if split_x else [next(it)]
    gate_ref = next(it)
    if fuse_norm:
        g_ref, shift_ref, scale_ref = next(it), next(it), next(it)
    xnew_ref = next(it)
    hn_ref = next(it) if fuse_norm else None
    wbf_ref, stage_ref, sem = next(it), next(it), next(it)

    i = pl.program_id(0)

    @pl.when(i == 0)
    def _():
        _load_weight_bf16(w_hbm, layer_in_kind, wbf_ref, stage_ref, sem)

    is_prompt = i + tile0 < PROMPT_TOKENS // tm
    a = jnp.where(is_prompt, a_refs[0][...], a_refs[1][...]) if split_a else a_refs[0][...]
    ssq = jnp.zeros((tm, 1), F32)
    for cb in range(D_MODEL // OUT_COLS):
        sl = slice(cb * OUT_COLS, (cb + 1) * OUT_COLS)
        acc = jnp.dot(a, wbf_ref[:, sl], preferred_element_type=F32)
        if split_x:
            x = jnp.where(is_prompt, x_refs[0][:, sl], x_refs[1][:, sl])
        else:
            x = x_refs[0][:, sl]
        xn = x + gate_ref[:, sl] * acc
        xnew_ref[:, sl] = xn
        if fuse_norm:
            ssq = ssq + jnp.sum(xn * xn, axis=-1, keepdims=True)
    if fuse_norm:
        rs = lax.rsqrt(ssq * (1.0 / D_MODEL) + EPS)
        for cb in range(D_MODEL // OUT_COLS):
            sl = slice(cb * OUT_COLS, (cb + 1) * OUT_COLS)
            gmul = g_ref[:, sl] * (1.0 + scale_ref[:, sl])
            hn_ref[:, sl] = (xnew_ref[:, sl] * rs * gmul + shift_ref[:, sl]).astype(BF16)


def _out_proj(a, w, layer_in_kind, x, mod4, layer, norm_g3, name, tm, rows=None):
    split_a = isinstance(a, tuple)
    k = a[0].shape[1] if split_a else a.shape[1]
    tok0, n_tok = rows if rows is not None else (0, TOKENS)
    tile0 = tok0 // tm
    split_x = isinstance(x, tuple)
    fuse_norm = norm_g3 is not None
    mod_spec = lambda part, lyr: pl.BlockSpec(
        (None, None, 1, D_MODEL), lambda i: (lyr, _cond_row((i + tile0) * tm), 0, part))
    row_spec = lambda width: pl.BlockSpec((tm, width), lambda i: (i + tile0, 0))
    operands = [*a, w] if split_a else [a, w]
    in_specs = [*(_split_specs(tm, k, tile0) if split_a else [row_spec(k)]),
                pl.BlockSpec(memory_space=pl.ANY)]
    if split_x:
        assert rows is None
        operands += list(x)
        in_specs += _split_specs(tm, D_MODEL)
    else:
        operands.append(x)
        in_specs.append(row_spec(D_MODEL))
    operands.append(mod4)
    in_specs.append(mod_spec(2, layer))
    out_shapes = [jax.ShapeDtypeStruct((n_tok, D_MODEL), F32)]
    out_specs = [pl.BlockSpec((tm, D_MODEL), lambda i: (i, 0))]
    if fuse_norm:
        operands += [norm_g3, mod4, mod4]
        in_specs += [pl.BlockSpec((None, 1, D_MODEL), lambda i: (layer + 1, 0, 0)),
                     mod_spec(0, layer + 1), mod_spec(1, layer + 1)]
        out_shapes.append(jax.ShapeDtypeStruct((n_tok, D_MODEL), BF16))
        out_specs.append(pl.BlockSpec((tm, D_MODEL), lambda i: (i, 0)))
    kern = functools.partial(_out_kernel, layer_in_kind=layer_in_kind, tile0=tile0, tm=tm,
                             split_a=split_a, split_x=split_x, fuse_norm=fuse_norm)
    return pl.pallas_call(
        kern,
        grid=(n_tok // tm,),
        in_specs=in_specs,
        out_specs=out_specs,
        out_shape=out_shapes,
        scratch_shapes=[pltpu.VMEM((k, D_MODEL), BF16),
                        pltpu.VMEM((2, W_STAGE_ROWS, D_MODEL), F32),
                        pltpu.SemaphoreType.DMA((2,))],
        compiler_params=_params(1),
        name=name,
    )(*operands)


def _dot_nt(a, b):
    return lax.dot_general(a, b, (((1,), (1,)), ((), ())), preferred_element_type=F32)


def _head(ref, h, rows=slice(None)):
    return ref[rows, h * HEAD_DIM:(h + 1) * HEAD_DIM]


def _with_ones(v):
    return jnp.concatenate([v, jnp.ones(v.shape, v.dtype)], axis=1)


def _lane_chunks(s):
    return [s[:, c * HEAD_DIM:(c + 1) * HEAD_DIM] for c in range(s.shape[1] // HEAD_DIM)]


def _softmax_pv(score_blocks, value_blocks, sink):
    rows = score_blocks[0].shape[0]
    mx = functools.reduce(jnp.maximum, [c for s in score_blocks for c in _lane_chunks(s)])
    m = jnp.broadcast_to(jnp.max(mx, axis=-1, keepdims=True), (rows, HEAD_DIM))
    if sink is not None:
        m = jnp.maximum(m, sink)
    o = None
    for s, v in zip(score_blocks, value_blocks):
        p = jnp.concatenate([jnp.exp2(c - m) for c in _lane_chunks(s)], axis=1).astype(BF16)
        part = jnp.dot(p, v, preferred_element_type=F32)
        o = part if o is None else o + part
    den = o[:, HEAD_DIM:]
    if sink is not None:
        den = den + jnp.exp2(sink - m)
    return o[:, :HEAD_DIM] * (1.0 / den)


def _attn_prompt_kernel(sink_ref, q_ref, k_ref, v_ref, z_ref, o_ref, *, n_heads, n_kv, use_sink):
    grp = n_heads // n_kv
    for g in range(n_kv):
        kg = _head(k_ref, g).astype(BF16)
        vg = _with_ones(_head(v_ref, g).astype(BF16))
        for h in range(g * grp, (g + 1) * grp):
            s = _dot_nt(_head(q_ref, h), kg)
            sink = jnp.full((PROMPT_SEQ, HEAD_DIM), sink_ref[h], F32) if use_sink else None
            o = _softmax_pv([s], [vg], sink)
            o_ref[:, h * HEAD_DIM:(h + 1) * HEAD_DIM] = (o * _head(z_ref, h).astype(F32)).astype(BF16)


def _attn_prompt(sink, q, k_cache, v_cache, z, n_kv, use_sink, name):
    width = q.shape[1]
    kv_width = k_cache.shape[2]
    kern = functools.partial(_attn_prompt_kernel, n_heads=width // HEAD_DIM, n_kv=n_kv, use_sink=use_sink)
    cache_spec = pl.BlockSpec((None, PROMPT_SEQ, kv_width), lambda b: (b, 0, 0))
    return pl.pallas_call(
        kern,
        grid=(N_PROMPT,),
        in_specs=[pl.BlockSpec(memory_space=pltpu.SMEM),
                  pl.BlockSpec((PROMPT_SEQ, width), lambda b: (b, 0)),
                  cache_spec, cache_spec,
                  pl.BlockSpec((PROMPT_SEQ, width), lambda b: (b, 0))],
        out_specs=pl.BlockSpec((PROMPT_SEQ, width), lambda b: (b, 0)),
        out_shape=jax.ShapeDtypeStruct((PROMPT_TOKENS, width), BF16),
        compiler_params=_params(1),
        name=name,
    )(sink, q, k_cache, v_cache, z)


WIN_BAND = 3 * WINDOW


def _attn_win_kernel(sink_ref, q_ref, k_ref, v_ref, kc_ref, vc_ref, z_ref, o_ref, *, n_heads, n_kv):
    grp = n_heads // n_kv
    n = pl.program_id(1)
    start = pl.multiple_of(jnp.clip((n - 1) * WINDOW, 0, SAMPLE_SEQ - WIN_BAND), WINDOW)
    shape = (grp * WINDOW, WIN_BAND)
    qpos = n * WINDOW + (lax.broadcasted_iota(jnp.int32, shape, 0) & (WINDOW - 1))
    kpos = start + lax.broadcasted_iota(jnp.int32, shape, 1)
    valid = jnp.abs(kpos - qpos) <= WINDOW
    band = pl.ds(start, WIN_BAND)
    for g in range(n_kv):
        heads = [g * grp + t for t in range(grp)]
        qs = jnp.concatenate([_head(q_ref, h) for h in heads], axis=0)
        kb = _head(k_ref, g, band)
        vb = _with_ones(_head(v_ref, g, band))
        kc = _head(kc_ref, g).astype(BF16)
        vc = _with_ones(_head(vc_ref, g).astype(BF16))
        s_band = jnp.where(valid, _dot_nt(qs, kb), NEG_INF)
        s_ctx = _dot_nt(qs, kc)
        sink = jnp.concatenate([jnp.full((WINDOW, HEAD_DIM), sink_ref[h], F32) for h in heads], axis=0)
        o = _softmax_pv([s_band, s_ctx], [vb, vc], sink)
        for t, h in enumerate(heads):
            oh = o[t * WINDOW:(t + 1) * WINDOW]
            o_ref[:, h * HEAD_DIM:(h + 1) * HEAD_DIM] = (oh * _head(z_ref, h).astype(F32)).astype(BF16)


def _attn_win(sink, q, k, v, cache_k, cache_v, layer_in_kind, z, n_kv):
    width = q.shape[1]
    kv_width = k.shape[1]
    blocks_per_seq = SAMPLE_SEQ // WINDOW
    q_map = lambda b, n: (PROMPT_TOKENS // WINDOW + b * blocks_per_seq + n, 0)
    kv_map = lambda b, n: (b, 0)
    cache_map = lambda b, n: (b, layer_in_kind, 0, 0)
    kern = functools.partial(_attn_win_kernel, n_heads=width // HEAD_DIM, n_kv=n_kv)
    return pl.pallas_call(
        kern,
        grid=(N_SAMPLE, blocks_per_seq),
        in_specs=[pl.BlockSpec(memory_space=pltpu.SMEM),
                  pl.BlockSpec((WINDOW, width), q_map),
                  pl.BlockSpec((SAMPLE_SEQ, kv_width), kv_map),
                  pl.BlockSpec((SAMPLE_SEQ, kv_width), kv_map),
                  pl.BlockSpec((None, None, PAST_LEN, kv_width), cache_map),
                  pl.BlockSpec((None, None, PAST_LEN, kv_width), cache_map),
                  pl.BlockSpec((WINDOW, width), q_map)],
        out_specs=pl.BlockSpec((WINDOW, width), lambda b, n: (b * blocks_per_seq + n, 0)),
        out_shape=jax.ShapeDtypeStruct((SAMPLE_TOKENS, width), BF16),
        compiler_params=_params(2),
        name="attn_win_latent",
    )(sink, q, k, v, cache_k, cache_v, z)


NAT_QROWS = 4
GRID_ROWS = SAMPLE_SEQ // GRID_W
N_DR = 2 * NAT_ROWS - 1
MASKED_TILE = N_DR


def _nat_row_start(qr):
    return min(max(qr - NAT_ROWS // 2, 0), GRID_ROWS - NAT_ROWS)


def _attn_nat_kernel(bias_ref, q_ref, k_ref, v_ref, kc_ref, vc_ref, z_ref, o_ref, left_ref, right_ref):
    @pl.when(pl.program_id(1) == 0)
    def _():
        shape = (GRID_W, 2 * GRID_W)
        lane = lax.broadcasted_iota(jnp.int32, shape, 1)
        qc = lax.broadcasted_iota(jnp.int32, shape, 0)
        kc_ = lane & (GRID_W - 1)
        cs = jnp.clip(qc - NAT_COLS // 2, 0, GRID_W - NAT_COLS)
        col_ok = jnp.logical_and(kc_ >= cs, kc_ < cs + NAT_COLS)
        is_left = lane < GRID_W
        lanes = 2 * GRID_W
        for dri in range(N_DR):
            row = jnp.broadcast_to(bias_ref[dri:dri + 1, :], shape)
            on_left = pltpu.roll(row, lanes - (NAT_COLS - 1), 1, stride=1, stride_axis=0)
            on_right = pltpu.roll(row, GRID_W - (NAT_COLS - 1), 1, stride=1, stride_axis=0)
            left_ref[dri] = jnp.where(is_left, jnp.where(col_ok, on_left, NEG_INF), 0.0)
            right_ref[dri] = jnp.where(is_left, 0.0, jnp.where(col_ok, on_right, NEG_INF))
        left_ref[MASKED_TILE] = jnp.where(is_left, NEG_INF, 0.0)
        right_ref[MASKED_TILE] = jnp.where(is_left, 0.0, NEG_INF)

    kc = kc_ref[...].astype(BF16)
    vc = _with_ones(vc_ref[...].astype(BF16))
    for qb in range(GRID_ROWS // NAT_QROWS):
        qrows = range(qb * NAT_QROWS, (qb + 1) * NAT_QROWS)
        krow0 = _nat_row_start(qrows[0]) // 2 * 2
        krow1 = -(-(_nat_row_start(qrows[-1]) + NAT_ROWS) // 2) * 2
        kwin = slice(krow0 * GRID_W, krow1 * GRID_W)
        qwin = slice(qrows[0] * GRID_W, (qrows[-1] + 1) * GRID_W)

        def tile_index(qr, kr):
            rs = _nat_row_start(qr)
            return kr - qr + (NAT_ROWS - 1) if rs <= kr < rs + NAT_ROWS else MASKED_TILE

        bias = jnp.concatenate(
            [jnp.concatenate([left_ref[tile_index(qr, kr)] + right_ref[tile_index(qr, kr + 1)]
                              for kr in range(krow0, krow1, 2)], axis=1)
             for qr in qrows], axis=0)

        q = q_ref[qwin, :]
        s_nb = _dot_nt(q, k_ref[kwin, :]) + bias
        s_ctx = _dot_nt(q, kc)
        o = _softmax_pv([s_nb, s_ctx], [_with_ones(v_ref[kwin, :]), vc], None)
        o_ref[qwin, :] = (o * z_ref[qwin, :].astype(F32)).astype(BF16)


def _attn_nat(rel_bias, q, k, v, cache_k, cache_v, layer_in_kind, z):
    n_heads = q.shape[1] // HEAD_DIM
    bias_rows = jnp.pad(rel_bias, ((0, 0), (0, N_DR + 1 - rel_bias.shape[1]),
                                   (0, 2 * GRID_W - rel_bias.shape[2])))
    q_map = lambda h, b: (PROMPT_TOKENS // SAMPLE_SEQ + b, h)
    kv_map = lambda h, b: (b, h)
    cache_map = lambda h, b: (b, layer_in_kind, 0, h)
    return pl.pallas_call(
        _attn_nat_kernel,
        grid=(n_heads, N_SAMPLE),
        in_specs=[pl.BlockSpec((None, N_DR + 1, 2 * GRID_W), lambda h, b: (h, 0, 0)),
                  pl.BlockSpec((SAMPLE_SEQ, HEAD_DIM), q_map),
                  pl.BlockSpec((SAMPLE_SEQ, HEAD_DIM), kv_map),
                  pl.BlockSpec((SAMPLE_SEQ, HEAD_DIM), kv_map),
                  pl.BlockSpec((None, None, PAST_LEN, HEAD_DIM), cache_map),
                  pl.BlockSpec((None, None, PAST_LEN, HEAD_DIM), cache_map),
                  pl.BlockSpec((SAMPLE_SEQ, HEAD_DIM), q_map)],
        out_specs=pl.BlockSpec((SAMPLE_SEQ, HEAD_DIM), kv_map),
        out_shape=jax.ShapeDtypeStruct((SAMPLE_TOKENS, n_heads * HEAD_DIM), BF16),
        scratch_shapes=[pltpu.VMEM((N_DR + 1, GRID_W, 2 * GRID_W), F32),
                        pltpu.VMEM((N_DR + 1, GRID_W, 2 * GRID_W), F32)],
        compiler_params=_params(2),
        name="attn_nat_latent",
    )(bias_rows, q, k, v, cache_k, cache_v, z)


SPATIAL_TOKENS = 2 * CHUNK


def _spatial_kernel(u_ref, v_ref, z_ref, g_ref, b_ref, ws_ref, bs_ref, o_ref):
    for c in range(SPATIAL_TOKENS // CHUNK):
        rows = slice(c * CHUNK, (c + 1) * CHUNK)
        v = v_ref[rows, :].astype(F32)
        mu = jnp.mean(v, axis=-1, keepdims=True)
        vc = v - mu
        var = jnp.mean(vc * vc, axis=-1, keepdims=True)
        vn = (vc * lax.rsqrt(var + EPS) * g_ref[...] + b_ref[...]).astype(BF16)
        for g in range(GMLP_GROUPS):
            sl = slice(g * GMLP_GROUP_WIDTH, (g + 1) * GMLP_GROUP_WIDTH)
            sv = jnp.dot(ws_ref[g], vn[:, sl], preferred_element_type=F32) + bs_ref[:, g:g + 1]
            o_ref[rows, sl] = u_ref[rows, sl] * sv.astype(BF16) * z_ref[rows, sl]


def _spatial(u, v, z, ln_g, ln_b, w_s, b_s):
    row = pl.BlockSpec((SPATIAL_TOKENS, GMLP_WIDTH), lambda i: (i, 0))
    vec = pl.BlockSpec((1, GMLP_WIDTH), lambda i: (0, 0))
    return pl.pallas_call(
        _spatial_kernel,
        grid=(TOKENS // SPATIAL_TOKENS,),
        in_specs=[row, row, row, vec, vec,
                  pl.BlockSpec((GMLP_GROUPS, CHUNK, CHUNK), lambda i: (0, 0, 0)),
                  pl.BlockSpec((CHUNK, GMLP_GROUPS), lambda i: (0, 0))],
        out_specs=row,
        out_shape=jax.ShapeDtypeStruct((TOKENS, GMLP_WIDTH), BF16),
        compiler_params=_params(1),
        name="gmlp_spatial",
    )(u, v, z, ln_g.reshape(1, -1), ln_b.reshape(1, -1), w_s.astype(BF16), b_s.T)


def kernel(x_prompt, x_sample, cache_win_k, cache_win_v, cache_nat_k, cache_nat_v, c, c_ctx,
           norm_g, w_ada, b_ada,
           win_w_in, win_q_norm, win_k_norm, win_sink, win_w_out,
           nat_w_in, nat_q_norm, nat_k_norm, nat_rel_bias, nat_w_out,
           gmlp_w_in, gmlp_ln_g, gmlp_ln_b, gmlp_w_s, gmlp_b_s, gmlp_w_out):
    depth = norm_g.shape[0]
    xp = x_prompt.reshape(PROMPT_TOKENS, D_MODEL)
    xs = x_sample.reshape(SAMPLE_TOKENS, D_MODEL)
    cond = jnp.zeros((COND_ROWS, D_MODEL), F32).at[:N_SAMPLE].set(c).at[CTX_COND_ROW].set(c_ctx)
    mod4 = _ada(cond, w_ada, b_ada).reshape(depth, COND_ROWS, 1, 3 * D_MODEL)
    norm_g3 = norm_g.reshape(depth, 1, D_MODEL)
    rope_tabs = _rope_tables()
    no_sink = jnp.zeros((1,), F32)

    n_win = win_w_in.shape[0]
    n_nat = nat_w_in.shape[0]
    win_kv_heads = cache_win_k.shape[3]
    win_kv_width = win_kv_heads * HEAD_DIM
    cwk = cache_win_k.reshape(N_SAMPLE, n_win, PAST_LEN, win_kv_width)
    cwv = cache_win_v.reshape(N_SAMPLE, n_win, PAST_LEN, win_kv_width)
    nat_heads = cache_nat_k.shape[3]
    cnk = cache_nat_k.reshape(N_SAMPLE, n_nat, PAST_LEN, nat_heads * HEAD_DIM)
    cnv = cache_nat_v.reshape(N_SAMPLE, n_nat, PAST_LEN, nat_heads * HEAD_DIM)

    new_win_k, new_win_v, new_nat_k, new_nat_v = [], [], [], []
    x = (xp, xs)
    hn = _norm_mod(xp, xs, norm_g3, mod4, 0)
    for layer in range(depth):
        kind = layer % 3
        li = layer // 3
        if kind == 0:
            nq = D_MODEL // PROJ_TN
            nkv = win_kv_width // PROJ_TN
            q = _query_proj(hn, win_w_in, li, 0, nq, win_q_norm[li], rope_tabs, "win_q")
            kp, ks = _key_proj(hn, win_w_in, li, nq, nkv, win_k_norm[li], rope_tabs, "win_k")
            vp, vs = _value_proj(hn, win_w_in, li, nq + nkv, nkv, "win_v")
            new_win_k.append(kp)
            new_win_v.append(vp)
            z = _act_proj(hn, win_w_in, li, nq + 2 * nkv, nq, _epi_silu, "win_z")
            sink = win_sink[li] * LOG2E
            a = (_attn_prompt(sink, q, kp, vp, z, win_kv_heads, True, "attn_win_prompt"),
                 _attn_win(sink, q, ks, vs, cwk, cwv, li, z, win_kv_heads))
            w_out, out_tm, out_name = win_w_out, 512, "win_out"
        elif kind == 1:
            nq = D_MODEL // PROJ_TN
            q = _query_proj(hn, nat_w_in, li, 0, nq, nat_q_norm[li], None, "nat_q")
            kp, ks = _key_proj(hn, nat_w_in, li, nq, nq, nat_k_norm[li], None, "nat_k")
            vp, vs = _value_proj(hn, nat_w_in, li, 2 * nq, nq, "nat_v")
            new_nat_k.append(kp)
            new_nat_v.append(vp)
            z = _act_proj(hn, nat_w_in, li, 3 * nq, nq, _epi_silu, "nat_z")
            a = (_attn_prompt(no_sink, q, kp, vp, z, nat_heads, False, "attn_nat_prompt"),
                 _attn_nat(nat_rel_bias[li] * LOG2E, q, ks, vs, cnk, cnv, li, z))
            w_out, out_tm, out_name = nat_w_out, 512, "nat_out"
        else:
            nw = GMLP_WIDTH // PROJ_TN
            u = _act_proj(hn, gmlp_w_in, li, 0, nw, _epi_gelu, "gmlp_u")
            v = _act_proj(hn, gmlp_w_in, li, nw, nw, _epi_gelu, "gmlp_v")
            z = _act_proj(hn, gmlp_w_in, li, 2 * nw, nw, _epi_silu, "gmlp_z")
            a = _spatial(u, v, z, gmlp_ln_g[li], gmlp_ln_b[li], gmlp_w_s[li], gmlp_b_s[li])
            w_out, out_tm, out_name = gmlp_w_out, 256, "gmlp_out"
        if layer + 1 < depth:
            x, hn = _out_proj(a, w_out, li, x, mod4, layer, norm_g3, out_name, out_tm)
        else:
            (yp,) = _out_proj(a, w_out, li, x, mod4, layer, None, out_name + "_prompt", out_tm,
                              rows=(0, PROMPT_TOKENS))
            (ys,) = _out_proj(a, w_out, li, x, mod4, layer, None, out_name + "_latent", out_tm,
                              rows=(PROMPT_TOKENS, SAMPLE_TOKENS))

    cache_shape = lambda layers, heads: jnp.stack(
        [c_.reshape(N_PROMPT, PROMPT_SEQ, heads, HEAD_DIM) for c_ in layers], axis=1)
    return (yp.reshape(N_PROMPT, PROMPT_SEQ, D_MODEL), ys.reshape(N_SAMPLE, SAMPLE_SEQ, D_MODEL),
            cache_shape(new_win_k, win_kv_heads), cache_shape(new_win_v, win_kv_heads),
            cache_shape(new_nat_k, nat_heads), cache_shape(new_nat_v, nat_heads))
```

```python
import functools
import math

import jax
import jax.numpy as jnp
from jax import lax
from jax.experimental import pallas as pl
from jax.experimental.pallas import tpu as pltpu

F32 = jnp.float32
BF16 = jnp.bfloat16

D_MODEL = 2048
HEAD_DIM = 128
N_PROMPT = 16
PROMPT_SEQ = 256
N_SAMPLE = 2
SAMPLE_SEQ = 1024
PROMPT_TOKENS = N_PROMPT * PROMPT_SEQ
SAMPLE_TOKENS = N_SAMPLE * SAMPLE_SEQ
TOKENS = PROMPT_TOKENS + SAMPLE_TOKENS
PAST_LEN = 512
GRID_W = 64
EPS = 1e-6
NEG_INF = -1e30
ROPE_THETA = 10000.0
WINDOW = 128
NAT_ROWS = 8
NAT_COLS = 16
GMLP_WIDTH = 2 * D_MODEL
GMLP_GROUPS = 16
GMLP_GROUP_WIDTH = GMLP_WIDTH // GMLP_GROUPS
CHUNK = 128
CTX_COND_ROW = 2
COND_ROWS = 8
SM_SCALE = HEAD_DIM ** -0.5
LOG2E = math.log2(math.e)
QUERY_SCALE = SM_SCALE * LOG2E

VMEM_LIMIT = 56 * 1024 * 1024


def _params(n_axes, vmem=VMEM_LIMIT, flags=None):
    return pltpu.CompilerParams(dimension_semantics=("arbitrary",) * n_axes,
                                vmem_limit_bytes=vmem, flags=flags)


def _cond_row(tok0):
    return jnp.where(tok0 < PROMPT_TOKENS, CTX_COND_ROW, (tok0 - PROMPT_TOKENS) // SAMPLE_SEQ)


def _silu(x):
    return x * (0.5 + 0.5 * jnp.tanh(0.5 * x))


def _gelu_tanh(x):
    return 0.5 * x * (1.0 + jnp.tanh(math.sqrt(2.0 / math.pi) * (x + 0.044715 * (x * x * x))))


def _split_specs(tm, width, tile0=0):
    n_p = PROMPT_TOKENS // tm
    return [pl.BlockSpec((tm, width), lambda i: (jnp.minimum(i + tile0, n_p - 1), 0)),
            pl.BlockSpec((tm, width), lambda i: (jnp.maximum(i + tile0 - n_p, 0), 0))]


def _ada_kernel(cond_ref, w_ref, b_ref, o_ref):
    s = _silu(cond_ref[...]).astype(BF16)
    o_ref[...] = jnp.dot(s, w_ref[...].astype(BF16), preferred_element_type=F32) + b_ref[...]


def _ada(cond, w_ada, b_ada, tn=1024):
    depth = w_ada.shape[0]
    n = w_ada.shape[2]
    return pl.pallas_call(
        _ada_kernel,
        grid=(depth, n // tn),
        in_specs=[pl.BlockSpec((COND_ROWS, D_MODEL), lambda l, j: (0, 0)),
                  pl.BlockSpec((None, D_MODEL, tn), lambda l, j: (l, 0, j)),
                  pl.BlockSpec((None, 1, tn), lambda l, j: (l, 0, j))],
        out_specs=pl.BlockSpec((None, COND_ROWS, tn), lambda l, j: (l, 0, j)),
        out_shape=jax.ShapeDtypeStruct((depth, COND_ROWS, n), F32),
        compiler_params=_params(2),
        name="ada_mod",
    )(cond, w_ada, b_ada.reshape(depth, 1, n))


def _norm_kernel(xp_ref, xs_ref, g_ref, shift_ref, scale_ref, o_ref, *, tm):
    x = jnp.where(pl.program_id(0) < PROMPT_TOKENS // tm, xp_ref[...], xs_ref[...])
    y = x * lax.rsqrt(jnp.mean(x * x, axis=-1, keepdims=True) + EPS) * g_ref[...]
    o_ref[...] = (y * (1.0 + scale_ref[...]) + shift_ref[...]).astype(BF16)


def _norm_mod(xp, xs, norm_g3, mod4, layer, tm=512):
    return pl.pallas_call(
        functools.partial(_norm_kernel, tm=tm),
        grid=(TOKENS // tm,),
        in_specs=[*_split_specs(tm, D_MODEL),
                  pl.BlockSpec((None, 1, D_MODEL), lambda i: (layer, 0, 0)),
                  pl.BlockSpec((None, None, 1, D_MODEL), lambda i: (layer, _cond_row(i * tm), 0, 0)),
                  pl.BlockSpec((None, None, 1, D_MODEL), lambda i: (layer, _cond_row(i * tm), 0, 1))],
        out_specs=pl.BlockSpec((tm, D_MODEL), lambda i: (i, 0)),
        out_shape=jax.ShapeDtypeStruct((TOKENS, D_MODEL), BF16),
        compiler_params=_params(1),
        name="norm_mod",
    )(xp, xs, norm_g3, mod4, mod4)


PROJ_TM = 2048
PROJ_TN = 512
PROJ_ROWS = 512
N_PROMPT_TILES = PROMPT_TOKENS // PROJ_TM


def _proj_kernel(*refs, epilogue, n_extra, n_out, by_tile_kind):
    a_ref, w_ref = refs[0], refs[1]
    extras = refs[2:2 + n_extra]
    outs = refs[2 + n_extra:2 + n_extra + n_out]
    wbf_ref = refs[2 + n_extra + n_out]

    @pl.when(pl.program_id(1) == 0)
    def _():
        wbf_ref[...] = w_ref[...].astype(BF16)

    def body(latent):
        for rc in range(PROJ_TM // PROJ_ROWS):
            rows = slice(rc * PROJ_ROWS, (rc + 1) * PROJ_ROWS)
            acc = jnp.dot(a_ref[rows, :], wbf_ref[...], preferred_element_type=F32)
            epilogue(acc, extras, outs, rc, latent)

    if by_tile_kind:
        is_latent = pl.program_id(1) >= N_PROMPT_TILES
        pl.when(is_latent)(lambda: body(True))
        pl.when(jnp.logical_not(is_latent))(lambda: body(False))
    else:
        body(None)


def _proj(a, w, layer, col_blk0, n_blk, epilogue, out_shapes, out_specs, name,
          extras=(), extra_specs=(), by_tile_kind=False):
    m, k = a.shape
    kern = functools.partial(_proj_kernel, epilogue=epilogue, n_extra=len(extras), n_out=len(out_shapes),
                             by_tile_kind=by_tile_kind)
    return pl.pallas_call(
        kern,
        grid=(n_blk, m // PROJ_TM),
        in_specs=[pl.BlockSpec((PROJ_TM, k), lambda j, i: (i, 0)),
                  pl.BlockSpec((None, k, PROJ_TN), lambda j, i: (layer, 0, col_blk0 + j)),
                  *extra_specs],
        out_specs=out_specs,
        out_shape=out_shapes,
        scratch_shapes=[pltpu.VMEM((k, PROJ_TN), BF16)],
        compiler_params=_params(2),
        name=name,
    )(a, w, *extras)


def _full_out(n_blk, dtype):
    return ([jax.ShapeDtypeStruct((TOKENS, n_blk * PROJ_TN), dtype)],
            [pl.BlockSpec((PROJ_TM, PROJ_TN), lambda j, i: (i, j))])


def _split_out(n_blk, head_rows):
    prompt_tile = lambda i: jnp.minimum(i, N_PROMPT_TILES - 1)
    if head_rows:
        assert n_blk == 1
        heads = PROJ_TN // HEAD_DIM
        cache_shape = jax.ShapeDtypeStruct((PROMPT_TOKENS * heads, HEAD_DIM), F32)
        cache_spec = pl.BlockSpec((PROJ_TM * heads, HEAD_DIM), lambda j, i: (prompt_tile(i), 0))
    else:
        cache_shape = jax.ShapeDtypeStruct((N_PROMPT, PROMPT_SEQ, n_blk * PROJ_TN), F32)
        cache_spec = pl.BlockSpec((PROJ_TM // PROMPT_SEQ, PROMPT_SEQ, PROJ_TN),
                                  lambda j, i: (prompt_tile(i), 0, j))
    shapes = [cache_shape, jax.ShapeDtypeStruct((SAMPLE_TOKENS, n_blk * PROJ_TN), BF16)]
    specs = [cache_spec,
             pl.BlockSpec((PROJ_TM, PROJ_TN), lambda j, i: (jnp.maximum(i - N_PROMPT_TILES, 0), j))]
    return shapes, specs


def _chunk_rows(rc):
    return slice(rc * PROJ_ROWS, (rc + 1) * PROJ_ROWS)


def _epi_silu(acc, extras, outs, rc, latent):
    outs[0][_chunk_rows(rc), :] = _silu(acc).astype(BF16)


def _epi_gelu(acc, extras, outs, rc, latent):
    outs[0][_chunk_rows(rc), :] = _gelu_tanh(acc).astype(BF16)


def _store_split(y, outs, rc, sl, latent):
    if latent:
        outs[1][_chunk_rows(rc), sl] = y.astype(BF16)
    elif len(outs[0].shape) == 2:
        heads = PROJ_TN // HEAD_DIM
        col0 = sl.start or 0
        for c in range(y.shape[1] // HEAD_DIM):
            head = col0 // HEAD_DIM + c
            rows = pl.ds(rc * PROJ_ROWS * heads + head, PROJ_ROWS, stride=heads)
            outs[0][rows, :] = y[:, c * HEAD_DIM:(c + 1) * HEAD_DIM]
    else:
        seqs = PROJ_ROWS // PROMPT_SEQ
        outs[0][rc * seqs:(rc + 1) * seqs, :, sl] = y.reshape(seqs, PROMPT_SEQ, y.shape[1])


def _epi_value(acc, extras, outs, rc, latent):
    _store_split(acc, outs, rc, slice(None), latent)


def _head_rmsnorm(acc, hh, g):
    a = acc[:, hh * HEAD_DIM:(hh + 1) * HEAD_DIM]
    return a * lax.rsqrt(jnp.mean(a * a, axis=-1, keepdims=True) + EPS) * g


def _normed_heads(acc, extras, rc, rope):
    g = extras[0][...]
    ys = [_head_rmsnorm(acc, hh, g) for hh in range(PROJ_TN // HEAD_DIM)]
    if rope:
        rows = _chunk_rows(rc)
        cos, sin, swap = extras[1][rows, :], extras[2][rows, :], extras[3][...]
        for pair in range(len(ys) // 2):
            both = jnp.concatenate(ys[2 * pair:2 * pair + 2], axis=1).astype(BF16)
            partner = jnp.dot(both, swap, preferred_element_type=F32)
            for t in range(2):
                hh = 2 * pair + t
                ys[hh] = ys[hh] * cos + partner[:, t * HEAD_DIM:(t + 1) * HEAD_DIM] * sin
    for hh, y in enumerate(ys):
        yield slice(hh * HEAD_DIM, (hh + 1) * HEAD_DIM), y


def _epi_query(acc, extras, outs, rc, latent, *, rope):
    for sl, y in _normed_heads(acc, extras, rc, rope and latent):
        outs[0][_chunk_rows(rc), sl] = (y * QUERY_SCALE).astype(BF16)


def _epi_key(acc, extras, outs, rc, latent, *, rope):
    for sl, y in _normed_heads(acc, extras, rc, rope and latent):
        _store_split(y, outs, rc, sl, latent)


def _rope_tables():
    nf = HEAD_DIM // 4
    t = jnp.arange(SAMPLE_TOKENS) % SAMPLE_SEQ
    row = (t // GRID_W).astype(F32)
    col = (t % GRID_W).astype(F32)
    inv = ROPE_THETA ** (-jnp.arange(nf, dtype=F32) / nf)
    ang_r = row[:, None] * inv
    ang_c = col[:, None] * inv
    cos = jnp.concatenate([jnp.cos(ang_r), jnp.cos(ang_r), jnp.cos(ang_c), jnp.cos(ang_c)], axis=1)
    sin = jnp.concatenate([-jnp.sin(ang_r), jnp.sin(ang_r), -jnp.sin(ang_c), jnp.sin(ang_c)], axis=1)
    lanes = jnp.arange(2 * HEAD_DIM)
    swap = (lanes[:, None] == (lanes[None, :] ^ (HEAD_DIM // 4))).astype(BF16)
    return cos, sin, swap


def _gain_extras(gain, rope_tabs):
    extras = [gain.reshape(1, HEAD_DIM)]
    specs = [pl.BlockSpec((1, HEAD_DIM), lambda j, i: (0, 0))]
    if rope_tabs is not None:
        tab_map = lambda j, i: (jnp.maximum(i - N_PROMPT_TILES, 0), 0)
        extras += list(rope_tabs)
        specs += [pl.BlockSpec((PROJ_TM, HEAD_DIM), tab_map)] * 2
        specs.append(pl.BlockSpec((2 * HEAD_DIM, 2 * HEAD_DIM), lambda j, i: (0, 0)))
    return extras, specs


def _query_proj(hn, w, layer, col_blk0, n_blk, gain, rope_tabs, name):
    extras, specs = _gain_extras(gain, rope_tabs)
    epi = functools.partial(_epi_query, rope=rope_tabs is not None)
    return _proj(hn, w, layer, col_blk0, n_blk, epi, *_full_out(n_blk, BF16), name, extras, specs,
                 by_tile_kind=rope_tabs is not None)[0]


def _key_proj(hn, w, layer, col_blk0, n_blk, gain, rope_tabs, head_rows, name):
    extras, specs = _gain_extras(gain, rope_tabs)
    epi = functools.partial(_epi_key, rope=rope_tabs is not None)
    return _proj(hn, w, layer, col_blk0, n_blk, epi, *_split_out(n_blk, head_rows), name, extras, specs,
                 by_tile_kind=True)


def _value_proj(hn, w, layer, col_blk0, n_blk, head_rows, name):
    return _proj(hn, w, layer, col_blk0, n_blk, _epi_value, *_split_out(n_blk, head_rows), name,
                 by_tile_kind=True)


def _act_proj(hn, w, layer, col_blk0, n_blk, epi, name):
    return _proj(hn, w, layer, col_blk0, n_blk, epi, *_full_out(n_blk, BF16), name)[0]


OUT_COLS = 512
W_STAGE_ROWS = 256


def _load_weight_bf16(w_hbm, layer, wbf_ref, stage_ref, sem):
    k = wbf_ref.shape[0]
    n_chunks = k // W_STAGE_ROWS

    def copy(c, slot):
        return pltpu.make_async_copy(w_hbm.at[layer, pl.ds(c * W_STAGE_ROWS, W_STAGE_ROWS), :],
                                     stage_ref.at[slot], sem.at[slot])

    copy(0, 0).start()
    for c in range(n_chunks):
        slot = c % 2
        if c + 1 < n_chunks:
            copy(c + 1, 1 - slot).start()
        copy(c, slot).wait()
        wbf_ref[c * W_STAGE_ROWS:(c + 1) * W_STAGE_ROWS, :] = stage_ref[slot].astype(BF16)


def _out_kernel(*refs, layer_in_kind, tile0, tm, split_a, split_x, fuse_norm):
    it = iter(refs)
    a_refs = [next(it), next(it)] if split_a else [next(it)]
    w_hbm = next(it)
    x_refs = [next(it), next(it)] if split_x else [next(it)]
    gate_ref = next(it)
    if fuse_norm:
        g_ref, shift_ref, scale_ref = next(it), next(it), next(it)
    xnew_ref = next(it)
    hn_ref = next(it) if fuse_norm else None
    wbf_ref, stage_ref, sem = next(it), next(it), next(it)

    i = pl.program_id(0)

    @pl.when(i == 0)
    def _():
        _load_weight_bf16(w_hbm, layer_in_kind, wbf_ref, stage_ref, sem)

    is_prompt = i + tile0 < PROMPT_TOKENS // tm
    a = jnp.where(is_prompt, a_refs[0][...], a_refs[1][...]) if split_a else a_refs[0][...]
    ssq = jnp.zeros((tm, 1), F32)
    for cb in range(D_MODEL // OUT_COLS):
        sl = slice(cb * OUT_COLS, (cb + 1) * OUT_COLS)
        acc = jnp.dot(a, wbf_ref[:, sl], preferred_element_type=F32)
        if split_x:
            x = jnp.where(is_prompt, x_refs[0][:, sl], x_refs[1][:, sl])
        else:
            x = x_refs[0][:, sl]
        xn = x + gate_ref[:, sl] * acc
        xnew_ref[:, sl] = xn
        if fuse_norm:
            ssq = ssq + jnp.sum(xn * xn, axis=-1, keepdims=True)
    if fuse_norm:
        rs = lax.rsqrt(ssq * (1.0 / D_MODEL) + EPS)
        for cb in range(D_MODEL // OUT_COLS):
            sl = slice(cb * OUT_COLS, (cb + 1) * OUT_COLS)
            gmul = g_ref[:, sl] * (1.0 + scale_ref[:, sl])
            hn_ref[:, sl] = (xnew_ref[:, sl] * rs * gmul + shift_ref[:, sl]).astype(BF16)


def _out_proj(a, w, layer_in_kind, x, mod4, layer, norm_g3, name, tm, rows=None):
    split_a = isinstance(a, tuple)
    k = a[0].shape[1] if split_a else a.shape[1]
    tok0, n_tok = rows if rows is not None else (0, TOKENS)
    tile0 = tok0 // tm
    split_x = isinstance(x, tuple)
    fuse_norm = norm_g3 is not None
    mod_spec = lambda part, lyr: pl.BlockSpec(
        (None, None, 1, D_MODEL), lambda i: (lyr, _cond_row((i + tile0) * tm), 0, part))
    row_spec = lambda width: pl.BlockSpec((tm, width), lambda i: (i + tile0, 0))
    operands = [*a, w] if split_a else [a, w]
    in_specs = [*(_split_specs(tm, k, tile0) if split_a else [row_spec(k)]),
                pl.BlockSpec(memory_space=pl.ANY)]
    if split_x:
        assert rows is None
        operands += list(x)
        in_specs += _split_specs(tm, D_MODEL)
    else:
        operands.append(x)
        in_specs.append(row_spec(D_MODEL))
    operands.append(mod4)
    in_specs.append(mod_spec(2, layer))
    out_shapes = [jax.ShapeDtypeStruct((n_tok, D_MODEL), F32)]
    out_specs = [pl.BlockSpec((tm, D_MODEL), lambda i: (i, 0))]
    if fuse_norm:
        operands += [norm_g3, mod4, mod4]
        in_specs += [pl.BlockSpec((None, 1, D_MODEL), lambda i: (layer + 1, 0, 0)),
                     mod_spec(0, layer + 1), mod_spec(1, layer + 1)]
        out_shapes.append(jax.ShapeDtypeStruct((n_tok, D_MODEL), BF16))
        out_specs.append(pl.BlockSpec((tm, D_MODEL), lambda i: (i, 0)))
    kern = functools.partial(_out_kernel, layer_in_kind=layer_in_kind, tile0=tile0, tm=tm,
                             split_a=split_a, split_x=split_x, fuse_norm=fuse_norm)
    return pl.pallas_call(
        kern,
        grid=(n_tok // tm,),
        in_specs=in_specs,
        out_specs=out_specs,
        out_shape=out_shapes,
        scratch_shapes=[pltpu.VMEM((k, D_MODEL), BF16),
                        pltpu.VMEM((2, W_STAGE_ROWS, D_MODEL), F32),
                        pltpu.SemaphoreType.DMA((2,))],
        compiler_params=_params(1),
        name=name,
    )(*operands)


def _dot_nt(a, b):
    return lax.dot_general(a, b, (((1,), (1,)), ((), ())), preferred_element_type=F32)


def _head(ref, h, rows=slice(None)):
    return ref[rows, h * HEAD_DIM:(h + 1) * HEAD_DIM]


def _with_ones(v):
    return jnp.concatenate([v, jnp.ones(v.shape, v.dtype)], axis=1)


def _lane_chunks(s):
    return [s[:, c * HEAD_DIM:(c + 1) * HEAD_DIM] for c in range(s.shape[1] // HEAD_DIM)]


def _softmax_pv(score_blocks, value_blocks, sink):
    rows = score_blocks[0].shape[0]
    mx = functools.reduce(jnp.maximum, [c for s in score_blocks for c in _lane_chunks(s)])
    m = jnp.broadcast_to(jnp.max(mx, axis=-1, keepdims=True), (rows, HEAD_DIM))
    if sink is not None:
        m = jnp.maximum(m, sink)
    o = None
    for s, v in zip(score_blocks, value_blocks):
        p = jnp.concatenate([jnp.exp2(c - m) for c in _lane_chunks(s)], axis=1).astype(BF16)
        part = jnp.dot(p, v, preferred_element_type=F32)
        o = part if o is None else o + part
    den = o[:, HEAD_DIM:]
    if sink is not None:
        den = den + jnp.exp2(sink - m)
    return o[:, :HEAD_DIM] * (1.0 / den)


def _attn_prompt_kernel(sink_ref, q_ref, k_ref, v_ref, z_ref, o_ref, *, n_heads, n_kv, use_sink):
    grp = n_heads // n_kv
    head_rows = k_ref.shape[1] == HEAD_DIM

    def kv_head(ref, g):
        return ref[pl.ds(g, PROMPT_SEQ, stride=n_kv), :] if head_rows else _head(ref, g)

    for g in range(n_kv):
        kg = kv_head(k_ref, g).astype(BF16)
        vg = _with_ones(kv_head(v_ref, g).astype(BF16))
        for h in range(g * grp, (g + 1) * grp):
            s = _dot_nt(_head(q_ref, h), kg)
            sink = jnp.full((PROMPT_SEQ, HEAD_DIM), sink_ref[h], F32) if use_sink else None
            o = _softmax_pv([s], [vg], sink)
            o_ref[:, h * HEAD_DIM:(h + 1) * HEAD_DIM] = (o * _head(z_ref, h).astype(F32)).astype(BF16)


def _attn_prompt(sink, q, k_cache, v_cache, z, n_kv, use_sink, name):
    width = q.shape[1]
    kern = functools.partial(_attn_prompt_kernel, n_heads=width // HEAD_DIM, n_kv=n_kv, use_sink=use_sink)
    if k_cache.ndim == 2:
        cache_spec = pl.BlockSpec((PROMPT_SEQ * n_kv, HEAD_DIM), lambda b: (b, 0))
    else:
        cache_spec = pl.BlockSpec((None, PROMPT_SEQ, k_cache.shape[2]), lambda b: (b, 0, 0))
    return pl.pallas_call(
        kern,
        grid=(N_PROMPT,),
        in_specs=[pl.BlockSpec(memory_space=pltpu.SMEM),
                  pl.BlockSpec((PROMPT_SEQ, width), lambda b: (b, 0)),
                  cache_spec, cache_spec,
                  pl.BlockSpec((PROMPT_SEQ, width), lambda b: (b, 0))],
        out_specs=pl.BlockSpec((PROMPT_SEQ, width), lambda b: (b, 0)),
        out_shape=jax.ShapeDtypeStruct((PROMPT_TOKENS, width), BF16),
        compiler_params=_params(1),
        name=name,
    )(sink, q, k_cache, v_cache, z)


WIN_BAND = 3 * WINDOW


def _attn_win_kernel(sink_ref, q_ref, k_ref, v_ref, kc_ref, vc_ref, z_ref, o_ref, *, n_heads, n_kv):
    grp = n_heads // n_kv
    n = pl.program_id(1)
    start = pl.multiple_of(jnp.clip((n - 1) * WINDOW, 0, SAMPLE_SEQ - WIN_BAND), WINDOW)
    shape = (grp * WINDOW, WIN_BAND)
    qpos = n * WINDOW + (lax.broadcasted_iota(jnp.int32, shape, 0) & (WINDOW - 1))
    kpos = start + lax.broadcasted_iota(jnp.int32, shape, 1)
    valid = jnp.abs(kpos - qpos) <= WINDOW
    band = pl.ds(start, WIN_BAND)
    for g in range(n_kv):
        heads = [g * grp + t for t in range(grp)]
        qs = jnp.concatenate([_head(q_ref, h) for h in heads], axis=0)
        kb = _head(k_ref, g, band)
        vb = _with_ones(_head(v_ref, g, band))
        kc = _head(kc_ref, g).astype(BF16)
        vc = _with_ones(_head(vc_ref, g).astype(BF16))
        s_band = jnp.where(valid, _dot_nt(qs, kb), NEG_INF)
        s_ctx = _dot_nt(qs, kc)
        sink = jnp.concatenate([jnp.full((WINDOW, HEAD_DIM), sink_ref[h], F32) for h in heads], axis=0)
        o = _softmax_pv([s_band, s_ctx], [vb, vc], sink)
        for t, h in enumerate(heads):
            oh = o[t * WINDOW:(t + 1) * WINDOW]
            o_ref[:, h * HEAD_DIM:(h + 1) * HEAD_DIM] = (oh * _head(z_ref, h).astype(F32)).astype(BF16)


def _attn_win(sink, q, k, v, cache_k, cache_v, layer_in_kind, z, n_kv):
    width = q.shape[1]
    kv_width = k.shape[1]
    blocks_per_seq = SAMPLE_SEQ // WINDOW
    q_map = lambda b, n: (PROMPT_TOKENS // WINDOW + b * blocks_per_seq + n, 0)
    kv_map = lambda b, n: (b, 0)
    cache_map = lambda b, n: (b, layer_in_kind, 0, 0)
    kern = functools.partial(_attn_win_kernel, n_heads=width // HEAD_DIM, n_kv=n_kv)
    return pl.pallas_call(
        kern,
        grid=(N_SAMPLE, blocks_per_seq),
        in_specs=[pl.BlockSpec(memory_space=pltpu.SMEM),
                  pl.BlockSpec((WINDOW, width), q_map),
                  pl.BlockSpec((SAMPLE_SEQ, kv_width), kv_map),
                  pl.BlockSpec((SAMPLE_SEQ, kv_width), kv_map),
                  pl.BlockSpec((None, None, PAST_LEN, kv_width), cache_map),
                  pl.BlockSpec((None, None, PAST_LEN, kv_width), cache_map),
                  pl.BlockSpec((WINDOW, width), q_map)],
        out_specs=pl.BlockSpec((WINDOW, width), lambda b, n: (b * blocks_per_seq + n, 0)),
        out_shape=jax.ShapeDtypeStruct((SAMPLE_TOKENS, width), BF16),
        compiler_params=_params(2),
        name="attn_win_latent",
    )(sink, q, k, v, cache_k, cache_v, z)


NAT_QROWS = 4
GRID_ROWS = SAMPLE_SEQ // GRID_W
N_DR = 2 * NAT_ROWS - 1
MASKED_TILE = N_DR


def _nat_row_start(qr):
    return min(max(qr - NAT_ROWS // 2, 0), GRID_ROWS - NAT_ROWS)


def _attn_nat_kernel(bias_ref, q_ref, k_ref, v_ref, kc_ref, vc_ref, z_ref, o_ref, left_ref, right_ref):
    @pl.when(pl.program_id(1) == 0)
    def _():
        shape = (GRID_W, 2 * GRID_W)
        lane = lax.broadcasted_iota(jnp.int32, shape, 1)
        qc = lax.broadcasted_iota(jnp.int32, shape, 0)
        kc_ = lane & (GRID_W - 1)
        cs = jnp.clip(qc - NAT_COLS // 2, 0, GRID_W - NAT_COLS)
        col_ok = jnp.logical_and(kc_ >= cs, kc_ < cs + NAT_COLS)
        is_left = lane < GRID_W
        lanes = 2 * GRID_W
        for dri in range(N_DR):
            row = jnp.broadcast_to(bias_ref[dri:dri + 1, :], shape)
            on_left = pltpu.roll(row, lanes - (NAT_COLS - 1), 1, stride=1, stride_axis=0)
            on_right = pltpu.roll(row, GRID_W - (NAT_COLS - 1), 1, stride=1, stride_axis=0)
            left_ref[dri] = jnp.where(is_left, jnp.where(col_ok, on_left, NEG_INF), 0.0)
            right_ref[dri] = jnp.where(is_left, 0.0, jnp.where(col_ok, on_right, NEG_INF))
        left_ref[MASKED_TILE] = jnp.where(is_left, NEG_INF, 0.0)
        right_ref[MASKED_TILE] = jnp.where(is_left, 0.0, NEG_INF)

    kc = kc_ref[...].astype(BF16)
    vc = _with_ones(vc_ref[...].astype(BF16))
    for qb in range(GRID_ROWS // NAT_QROWS):
        qrows = range(qb * NAT_QROWS, (qb + 1) * NAT_QROWS)
        krow0 = _nat_row_start(qrows[0]) // 2 * 2
        krow1 = -(-(_nat_row_start(qrows[-1]) + NAT_ROWS) // 2) * 2
        kwin = slice(krow0 * GRID_W, krow1 * GRID_W)
        qwin = slice(qrows[0] * GRID_W, (qrows[-1] + 1) * GRID_W)

        def tile_index(qr, kr):
            rs = _nat_row_start(qr)
            return kr - qr + (NAT_ROWS - 1) if rs <= kr < rs + NAT_ROWS else MASKED_TILE

        bias = jnp.concatenate(
            [jnp.concatenate([left_ref[tile_index(qr, kr)] + right_ref[tile_index(qr, kr + 1)]
                              for kr in range(krow0, krow1, 2)], axis=1)
             for qr in qrows], axis=0)

        q = q_ref[qwin, :]
        s_nb = _dot_nt(q, k_ref[kwin, :]) + bias
        s_ctx = _dot_nt(q, kc)
        o = _softmax_pv([s_nb, s_ctx], [_with_ones(v_ref[kwin, :]), vc], None)
        o_ref[qwin, :] = (o * z_ref[qwin, :].astype(F32)).astype(BF16)


def _attn_nat(rel_bias, q, k, v, cache_k, cache_v, layer_in_kind, z):
    n_heads = q.shape[1] // HEAD_DIM
    bias_rows = jnp.pad(rel_bias, ((0, 0), (0, N_DR + 1 - rel_bias.shape[1]),
                                   (0, 2 * GRID_W - rel_bias.shape[2])))
    q_map = lambda h, b: (PROMPT_TOKENS // SAMPLE_SEQ + b, h)
    kv_map = lambda h, b: (b, h)
    cache_map = lambda h, b: (b, layer_in_kind, 0, h)
    return pl.pallas_call(
        _attn_nat_kernel,
        grid=(n_heads, N_SAMPLE),
        in_specs=[pl.BlockSpec((None, N_DR + 1, 2 * GRID_W), lambda h, b: (h, 0, 0)),
                  pl.BlockSpec((SAMPLE_SEQ, HEAD_DIM), q_map),
                  pl.BlockSpec((SAMPLE_SEQ, HEAD_DIM), kv_map),
                  pl.BlockSpec((SAMPLE_SEQ, HEAD_DIM), kv_map),
                  pl.BlockSpec((None, None, PAST_LEN, HEAD_DIM), cache_map),
                  pl.BlockSpec((None, None, PAST_LEN, HEAD_DIM), cache_map),
                  pl.BlockSpec((SAMPLE_SEQ, HEAD_DIM), q_map)],
        out_specs=pl.BlockSpec((SAMPLE_SEQ, HEAD_DIM), kv_map),
        out_shape=jax.ShapeDtypeStruct((SAMPLE_TOKENS, n_heads * HEAD_DIM), BF16),
        scratch_shapes=[pltpu.VMEM((N_DR + 1, GRID_W, 2 * GRID_W), F32),
                        pltpu.VMEM((N_DR + 1, GRID_W, 2 * GRID_W), F32)],
        compiler_params=_params(2),
        name="attn_nat_latent",
    )(bias_rows, q, k, v, cache_k, cache_v, z)


SPATIAL_TOKENS = 2 * CHUNK


def _spatial_kernel(u_ref, v_ref, z_ref, g_ref, b_ref, ws_ref, bs_ref, o_ref):
    for c in range(SPATIAL_TOKENS // CHUNK):
        rows = slice(c * CHUNK, (c + 1) * CHUNK)
        v = v_ref[rows, :].astype(F32)
        mu = jnp.mean(v, axis=-1, keepdims=True)
        vc = v - mu
        var = jnp.mean(vc * vc, axis=-1, keepdims=True)
        vn = (vc * lax.rsqrt(var + EPS) * g_ref[...] + b_ref[...]).astype(BF16)
        for g in range(GMLP_GROUPS):
            sl = slice(g * GMLP_GROUP_WIDTH, (g + 1) * GMLP_GROUP_WIDTH)
            sv = jnp.dot(ws_ref[g], vn[:, sl], preferred_element_type=F32) + bs_ref[:, g:g + 1]
            o_ref[rows, sl] = u_ref[rows, sl] * sv.astype(BF16) * z_ref[rows, sl]


def _spatial(u, v, z, ln_g, ln_b, w_s, b_s):
    row = pl.BlockSpec((SPATIAL_TOKENS, GMLP_WIDTH), lambda i: (i, 0))
    vec = pl.BlockSpec((1, GMLP_WIDTH), lambda i: (0, 0))
    return pl.pallas_call(
        _spatial_kernel,
        grid=(TOKENS // SPATIAL_TOKENS,),
        in_specs=[row, row, row, vec, vec,
                  pl.BlockSpec((GMLP_GROUPS, CHUNK, CHUNK), lambda i: (0, 0, 0)),
                  pl.BlockSpec((CHUNK, GMLP_GROUPS), lambda i: (0, 0))],
        out_specs=row,
        out_shape=jax.ShapeDtypeStruct((TOKENS, GMLP_WIDTH), BF16),
        compiler_params=_params(1),
        name="gmlp_spatial",
    )(u, v, z, ln_g.reshape(1, -1), ln_b.reshape(1, -1), w_s.astype(BF16), b_s.T)


def kernel(x_prompt, x_sample, cache_win_k, cache_win_v, cache_nat_k, cache_nat_v, c, c_ctx,
           norm_g, w_ada, b_ada,
           win_w_in, win_q_norm, win_k_norm, win_sink, win_w_out,
           nat_w_in, nat_q_norm, nat_k_norm, nat_rel_bias, nat_w_out,
           gmlp_w_in, gmlp_ln_g, gmlp_ln_b, gmlp_w_s, gmlp_b_s, gmlp_w_out):
    depth = norm_g.shape[0]
    xp = x_prompt.reshape(PROMPT_TOKENS, D_MODEL)
    xs = x_sample.reshape(SAMPLE_TOKENS, D_MODEL)
    cond = jnp.zeros((COND_ROWS, D_MODEL), F32).at[:N_SAMPLE].set(c).at[CTX_COND_ROW].set(c_ctx)
    mod4 = _ada(cond, w_ada, b_ada).reshape(depth, COND_ROWS, 1, 3 * D_MODEL)
    norm_g3 = norm_g.reshape(depth, 1, D_MODEL)
    rope_tabs = _rope_tables()
    no_sink = jnp.zeros((1,), F32)

    n_win = win_w_in.shape[0]
    n_nat = nat_w_in.shape[0]
    win_kv_heads = cache_win_k.shape[3]
    win_kv_width = win_kv_heads * HEAD_DIM
    cwk = cache_win_k.reshape(N_SAMPLE, n_win, PAST_LEN, win_kv_width)
    cwv = cache_win_v.reshape(N_SAMPLE, n_win, PAST_LEN, win_kv_width)
    nat_heads = cache_nat_k.shape[3]
    cnk = cache_nat_k.reshape(N_SAMPLE, n_nat, PAST_LEN, nat_heads * HEAD_DIM)
    cnv = cache_nat_v.reshape(N_SAMPLE, n_nat, PAST_LEN, nat_heads * HEAD_DIM)

    new_win_k, new_win_v, new_nat_k, new_nat_v = [], [], [], []
    x = (xp, xs)
    hn = _norm_mod(xp, xs, norm_g3, mod4, 0)
    for layer in range(depth):
        kind = layer % 3
        li = layer // 3
        if kind == 0:
            nq = D_MODEL // PROJ_TN
            nkv = win_kv_width // PROJ_TN
            q = _query_proj(hn, win_w_in, li, 0, nq, win_q_norm[li], rope_tabs, "win_q")
            kp, ks = _key_proj(hn, win_w_in, li, nq, nkv, win_k_norm[li], rope_tabs, True, "win_k")
            vp, vs = _value_proj(hn, win_w_in, li, nq + nkv, nkv, True, "win_v")
            new_win_k.append(kp)
            new_win_v.append(vp)
            z = _act_proj(hn, win_w_in, li, nq + 2 * nkv, nq, _epi_silu, "win_z")
            sink = win_sink[li] * LOG2E
            a = (_attn_prompt(sink, q, kp, vp, z, win_kv_heads, True, "attn_win_prompt"),
                 _attn_win(sink, q, ks, vs, cwk, cwv, li, z, win_kv_heads))
            w_out, out_tm, out_name = win_w_out, 512, "win_out"
        elif kind == 1:
            nq = D_MODEL // PROJ_TN
            q = _query_proj(hn, nat_w_in, li, 0, nq, nat_q_norm[li], None, "nat_q")
            kp, ks = _key_proj(hn, nat_w_in, li, nq, nq, nat_k_norm[li], None, False, "nat_k")
            vp, vs = _value_proj(hn, nat_w_in, li, 2 * nq, nq, False, "nat_v")
            new_nat_k.append(kp)
            new_nat_v.append(vp)
            z = _act_proj(hn, nat_w_in, li, 3 * nq, nq, _epi_silu, "nat_z")
            a = (_attn_prompt(no_sink, q, kp, vp, z, nat_heads, False, "attn_nat_prompt"),
                 _attn_nat(nat_rel_bias[li] * LOG2E, q, ks, vs, cnk, cnv, li, z))
            w_out, out_tm, out_name = nat_w_out, 512, "nat_out"
        else:
            nw = GMLP_WIDTH // PROJ_TN
            u = _act_proj(hn, gmlp_w_in, li, 0, nw, _epi_gelu, "gmlp_u")
            v = _act_proj(hn, gmlp_w_in, li, nw, nw, _epi_gelu, "gmlp_v")
            z = _act_proj(hn, gmlp_w_in, li, 2 * nw, nw, _epi_silu, "gmlp_z")
            a = _spatial(u, v, z, gmlp_ln_g[li], gmlp_ln_b[li], gmlp_w_s[li], gmlp_b_s[li])
            w_out, out_tm, out_name = gmlp_w_out, 256, "gmlp_out"
        if layer + 1 < depth:
            x, hn = _out_proj(a, w_out, li, x, mod4, layer, norm_g3, out_name, out_tm)
        else:
            (yp,) = _out_proj(a, w_out, li, x, mod4, layer, None, out_name + "_prompt", out_tm,
                              rows=(0, PROMPT_TOKENS))
            (ys,) = _out_proj(a, w_out, li, x, mod4, layer, None, out_name + "_latent", out_tm,
                              rows=(PROMPT_TOKENS, SAMPLE_TOKENS))

    cache_shape = lambda layers, heads: jnp.stack(
        [c_.reshape(N_PROMPT, PROMPT_SEQ, heads, HEAD_DIM) for c_ in layers], axis=1)
    return (yp.reshape(N_PROMPT, PROMPT_SEQ, D_MODEL), ys.reshape(N_SAMPLE, SAMPLE_SEQ, D_MODEL),
            cache_shape(new_win_k, win_kv_heads), cache_shape(new_win_v, win_kv_heads),
            cache_shape(new_nat_k, nat_heads), cache_shape(new_nat_v, nat_heads))
```

```python
import functools
import math

import jax
import jax.numpy as jnp
import numpy as np
from jax import lax
from jax.experimental import pallas as pl
from jax.experimental.pallas import tpu as pltpu

F32 = jnp.float32
BF16 = jnp.bfloat16

D_MODEL = 2048
HEAD_DIM = 128
N_PROMPT = 16
PROMPT_SEQ = 256
N_SAMPLE = 2
SAMPLE_SEQ = 1024
PROMPT_TOKENS = N_PROMPT * PROMPT_SEQ
SAMPLE_TOKENS = N_SAMPLE * SAMPLE_SEQ
TOKENS = PROMPT_TOKENS + SAMPLE_TOKENS
PAST_LEN = 512
GRID_W = 64
EPS = 1e-6
NEG_INF = -1e30
ROPE_THETA = 10000.0
WINDOW = 128
NAT_ROWS = 8
NAT_COLS = 16
GMLP_WIDTH = 2 * D_MODEL
GMLP_GROUPS = 16
GMLP_GROUP_WIDTH = GMLP_WIDTH // GMLP_GROUPS
CHUNK = 128
CTX_COND_ROW = 2
COND_ROWS = 8
SM_SCALE = HEAD_DIM ** -0.5
LOG2E = math.log2(math.e)
QUERY_SCALE = SM_SCALE * LOG2E

VMEM_LIMIT = 56 * 1024 * 1024


def _params(n_axes, vmem=VMEM_LIMIT, flags=None):
    return pltpu.CompilerParams(dimension_semantics=("arbitrary",) * n_axes,
                                vmem_limit_bytes=vmem, flags=flags)


def _cond_row(tok0):
    return jnp.where(tok0 < PROMPT_TOKENS, CTX_COND_ROW, (tok0 - PROMPT_TOKENS) // SAMPLE_SEQ)


def _silu(x):
    return x * (0.5 + 0.5 * jnp.tanh(0.5 * x))


def _gelu_tanh(x):
    return 0.5 * x * (1.0 + jnp.tanh(math.sqrt(2.0 / math.pi) * (x + 0.044715 * (x * x * x))))


def _split_specs(tm, width, tile0=0):
    n_p = PROMPT_TOKENS // tm
    return [pl.BlockSpec((tm, width), lambda i: (jnp.minimum(i + tile0, n_p - 1), 0)),
            pl.BlockSpec((tm, width), lambda i: (jnp.maximum(i + tile0 - n_p, 0), 0))]


def _ada_kernel(cond_ref, w_ref, b_ref, o_ref):
    s = _silu(cond_ref[...]).astype(BF16)
    o_ref[...] = jnp.dot(s, w_ref[...].astype(BF16), preferred_element_type=F32) + b_ref[...]


def _ada(cond, w_ada, b_ada, tn=1024):
    depth = w_ada.shape[0]
    n = w_ada.shape[2]
    return pl.pallas_call(
        _ada_kernel,
        grid=(depth, n // tn),
        in_specs=[pl.BlockSpec((COND_ROWS, D_MODEL), lambda l, j: (0, 0)),
                  pl.BlockSpec((None, D_MODEL, tn), lambda l, j: (l, 0, j)),
                  pl.BlockSpec((None, 1, tn), lambda l, j: (l, 0, j))],
        out_specs=pl.BlockSpec((None, COND_ROWS, tn), lambda l, j: (l, 0, j)),
        out_shape=jax.ShapeDtypeStruct((depth, COND_ROWS, n), F32),
        compiler_params=_params(2),
        name="ada_mod",
    )(cond, w_ada, b_ada.reshape(depth, 1, n))


def _norm_kernel(xp_ref, xs_ref, g_ref, shift_ref, scale_ref, o_ref, *, tm):
    x = jnp.where(pl.program_id(0) < PROMPT_TOKENS // tm, xp_ref[...], xs_ref[...])
    y = x * lax.rsqrt(jnp.mean(x * x, axis=-1, keepdims=True) + EPS) * g_ref[...]
    o_ref[...] = (y * (1.0 + scale_ref[...]) + shift_ref[...]).astype(BF16)


def _norm_mod(xp, xs, norm_g3, mod4, layer, tm=512):
    return pl.pallas_call(
        functools.partial(_norm_kernel, tm=tm),
        grid=(TOKENS // tm,),
        in_specs=[*_split_specs(tm, D_MODEL),
                  pl.BlockSpec((None, 1, D_MODEL), lambda i: (layer, 0, 0)),
                  pl.BlockSpec((None, None, 1, D_MODEL), lambda i: (layer, _cond_row(i * tm), 0, 0)),
                  pl.BlockSpec((None, None, 1, D_MODEL), lambda i: (layer, _cond_row(i * tm), 0, 1))],
        out_specs=pl.BlockSpec((tm, D_MODEL), lambda i: (i, 0)),
        out_shape=jax.ShapeDtypeStruct((TOKENS, D_MODEL), BF16),
        compiler_params=_params(1),
        name="norm_mod",
    )(xp, xs, norm_g3, mod4, mod4)


PROJ_TM = 2048
PROJ_TN = 512
PROJ_ROWS = 512
N_PROMPT_TILES = PROMPT_TOKENS // PROJ_TM


def _proj_kernel(*refs, epilogue, n_extra, n_out, by_tile_kind):
    a_ref, w_ref = refs[0], refs[1]
    extras = refs[2:2 + n_extra]
    outs = refs[2 + n_extra:2 + n_extra + n_out]
    wbf_ref = refs[2 + n_extra + n_out]

    @pl.when(pl.program_id(1) == 0)
    def _():
        wbf_ref[...] = w_ref[...].astype(BF16)

    def body(latent):
        for rc in range(PROJ_TM // PROJ_ROWS):
            rows = slice(rc * PROJ_ROWS, (rc + 1) * PROJ_ROWS)
            acc = jnp.dot(a_ref[rows, :], wbf_ref[...], preferred_element_type=F32)
            epilogue(acc, extras, outs, rc, latent)

    if by_tile_kind:
        is_latent = pl.program_id(1) >= N_PROMPT_TILES
        pl.when(is_latent)(lambda: body(True))
        pl.when(jnp.logical_not(is_latent))(lambda: body(False))
    else:
        body(None)


def _proj(a, w, layer, col_blk0, n_blk, epilogue, out_shapes, out_specs, name,
          extras=(), extra_specs=(), by_tile_kind=False):
    m, k = a.shape
    kern = functools.partial(_proj_kernel, epilogue=epilogue, n_extra=len(extras), n_out=len(out_shapes),
                             by_tile_kind=by_tile_kind)
    return pl.pallas_call(
        kern,
        grid=(n_blk, m // PROJ_TM),
        in_specs=[pl.BlockSpec((PROJ_TM, k), lambda j, i: (i, 0)),
                  pl.BlockSpec((None, k, PROJ_TN), lambda j, i: (layer, 0, col_blk0 + j)),
                  *extra_specs],
        out_specs=out_specs,
        out_shape=out_shapes,
        scratch_shapes=[pltpu.VMEM((k, PROJ_TN), BF16)],
        compiler_params=_params(2),
        name=name,
    )(a, w, *extras)


def _full_out(n_blk, dtype):
    return ([jax.ShapeDtypeStruct((TOKENS, n_blk * PROJ_TN), dtype)],
            [pl.BlockSpec((PROJ_TM, PROJ_TN), lambda j, i: (i, j))])


def _split_out(n_blk, head_rows):
    prompt_tile = lambda i: jnp.minimum(i, N_PROMPT_TILES - 1)
    if head_rows:
        assert n_blk == 1
        heads = PROJ_TN // HEAD_DIM
        cache_shape = jax.ShapeDtypeStruct((PROMPT_TOKENS * heads, HEAD_DIM), F32)
        cache_spec = pl.BlockSpec((PROJ_TM * heads, HEAD_DIM), lambda j, i: (prompt_tile(i), 0))
    else:
        cache_shape = jax.ShapeDtypeStruct((N_PROMPT, PROMPT_SEQ, n_blk * PROJ_TN), F32)
        cache_spec = pl.BlockSpec((PROJ_TM // PROMPT_SEQ, PROMPT_SEQ, PROJ_TN),
                                  lambda j, i: (prompt_tile(i), 0, j))
    shapes = [cache_shape, jax.ShapeDtypeStruct((TOKENS, n_blk * PROJ_TN), BF16)]
    specs = [cache_spec, pl.BlockSpec((PROJ_TM, PROJ_TN), lambda j, i: (i, j))]
    return shapes, specs


def _chunk_rows(rc):
    return slice(rc * PROJ_ROWS, (rc + 1) * PROJ_ROWS)


def _epi_silu(acc, extras, outs, rc, latent):
    outs[0][_chunk_rows(rc), :] = _silu(acc).astype(BF16)


def _epi_gelu(acc, extras, outs, rc, latent):
    outs[0][_chunk_rows(rc), :] = _gelu_tanh(acc).astype(BF16)


def _store_split(y, outs, rc, sl, latent):
    outs[1][_chunk_rows(rc), sl] = y.astype(BF16)
    if latent:
        return
    if len(outs[0].shape) == 2:
        heads = PROJ_TN // HEAD_DIM
        col0 = sl.start or 0
        for c in range(y.shape[1] // HEAD_DIM):
            head = col0 // HEAD_DIM + c
            rows = pl.ds(rc * PROJ_ROWS * heads + head, PROJ_ROWS, stride=heads)
            outs[0][rows, :] = y[:, c * HEAD_DIM:(c + 1) * HEAD_DIM]
    else:
        seqs = PROJ_ROWS // PROMPT_SEQ
        outs[0][rc * seqs:(rc + 1) * seqs, :, sl] = y.reshape(seqs, PROMPT_SEQ, y.shape[1])


def _epi_value(acc, extras, outs, rc, latent):
    _store_split(acc, outs, rc, slice(None), latent)


def _head_rmsnorm(acc, hh, g):
    a = acc[:, hh * HEAD_DIM:(hh + 1) * HEAD_DIM]
    return a * lax.rsqrt(jnp.mean(a * a, axis=-1, keepdims=True) + EPS) * g


def _normed_heads(acc, extras, rc, rope):
    g = extras[0][...]
    ys = [_head_rmsnorm(acc, hh, g) for hh in range(PROJ_TN // HEAD_DIM)]
    if rope:
        rows = _chunk_rows(rc)
        cos, sin, swap = extras[1][rows, :], extras[2][rows, :], extras[3][...]
        for pair in range(len(ys) // 2):
            both = jnp.concatenate(ys[2 * pair:2 * pair + 2], axis=1).astype(BF16)
            partner = jnp.dot(both, swap, preferred_element_type=F32)
            for t in range(2):
                hh = 2 * pair + t
                ys[hh] = ys[hh] * cos + partner[:, t * HEAD_DIM:(t + 1) * HEAD_DIM] * sin
    for hh, y in enumerate(ys):
        yield slice(hh * HEAD_DIM, (hh + 1) * HEAD_DIM), y


def _epi_query(acc, extras, outs, rc, latent, *, rope):
    for sl, y in _normed_heads(acc, extras, rc, rope and latent):
        outs[0][_chunk_rows(rc), sl] = (y * QUERY_SCALE).astype(BF16)


def _epi_key(acc, extras, outs, rc, latent, *, rope):
    for sl, y in _normed_heads(acc, extras, rc, rope and latent):
        _store_split(y, outs, rc, sl, latent)


def _rope_tables():
    nf = HEAD_DIM // 4
    t = np.arange(SAMPLE_TOKENS) % SAMPLE_SEQ
    row = (t // GRID_W).astype(np.float32)
    col = (t % GRID_W).astype(np.float32)
    inv = np.float32(ROPE_THETA) ** (-np.arange(nf, dtype=np.float32) / np.float32(nf))
    ang_r = row[:, None] * inv
    ang_c = col[:, None] * inv
    cos = np.concatenate([np.cos(ang_r), np.cos(ang_r), np.cos(ang_c), np.cos(ang_c)], axis=1)
    sin = np.concatenate([-np.sin(ang_r), np.sin(ang_r), -np.sin(ang_c), np.sin(ang_c)], axis=1)
    lanes = np.arange(2 * HEAD_DIM)
    swap = lanes[:, None] == (lanes[None, :] ^ (HEAD_DIM // 4))
    return jnp.asarray(cos, F32), jnp.asarray(sin, F32), jnp.asarray(swap, BF16)


def _gain_extras(gain, rope_tabs):
    extras = [gain.reshape(1, HEAD_DIM)]
    specs = [pl.BlockSpec((1, HEAD_DIM), lambda j, i: (0, 0))]
    if rope_tabs is not None:
        tab_map = lambda j, i: (jnp.maximum(i - N_PROMPT_TILES, 0), 0)
        extras += list(rope_tabs)
        specs += [pl.BlockSpec((PROJ_TM, HEAD_DIM), tab_map)] * 2
        specs.append(pl.BlockSpec((2 * HEAD_DIM, 2 * HEAD_DIM), lambda j, i: (0, 0)))
    return extras, specs


def _query_proj(hn, w, layer, col_blk0, n_blk, gain, rope_tabs, name):
    extras, specs = _gain_extras(gain, rope_tabs)
    epi = functools.partial(_epi_query, rope=rope_tabs is not None)
    return _proj(hn, w, layer, col_blk0, n_blk, epi, *_full_out(n_blk, BF16), name, extras, specs,
                 by_tile_kind=rope_tabs is not None)[0]


def _key_proj(hn, w, layer, col_blk0, n_blk, gain, rope_tabs, head_rows, name):
    extras, specs = _gain_extras(gain, rope_tabs)
    epi = functools.partial(_epi_key, rope=rope_tabs is not None)
    return _proj(hn, w, layer, col_blk0, n_blk, epi, *_split_out(n_blk, head_rows), name, extras, specs,
                 by_tile_kind=True)


def _value_proj(hn, w, layer, col_blk0, n_blk, head_rows, name):
    return _proj(hn, w, layer, col_blk0, n_blk, _epi_value, *_split_out(n_blk, head_rows), name,
                 by_tile_kind=True)


def _act_proj(hn, w, layer, col_blk0, n_blk, epi, name):
    return _proj(hn, w, layer, col_blk0, n_blk, epi, *_full_out(n_blk, BF16), name)[0]


OUT_COLS = 512
W_STAGE_ROWS = 256


def _load_weight_bf16(w_hbm, layer, wbf_ref, stage_ref, sem):
    k = wbf_ref.shape[0]
    n_chunks = k // W_STAGE_ROWS

    def copy(c, slot):
        return pltpu.make_async_copy(w_hbm.at[layer, pl.ds(c * W_STAGE_ROWS, W_STAGE_ROWS), :],
                                     stage_ref.at[slot], sem.at[slot])

    copy(0, 0).start()
    for c in range(n_chunks):
        slot = c % 2
        if c + 1 < n_chunks:
            copy(c + 1, 1 - slot).start()
        copy(c, slot).wait()
        wbf_ref[c * W_STAGE_ROWS:(c + 1) * W_STAGE_ROWS, :] = stage_ref[slot].astype(BF16)


def _out_kernel(*refs, layer_in_kind, tile0, tm, split_a, split_x, fuse_norm):
    it = iter(refs)
    a_refs = [next(it), next(it)] if split_a else [next(it)]
    w_hbm = next(it)
    x_refs = [next(it), next(it)] if split_x else [next(it)]
    gate_ref = next(it)
    if fuse_norm:
        g_ref, shift_ref, scale_ref = next(it), next(it), next(it)
    xnew_ref = next(it)
    hn_ref = next(it) if fuse_norm else None
    wbf_ref, stage_ref, sem = next(it), next(it), next(it)

    i = pl.program_id(0)

    @pl.when(i == 0)
    def _():
        _load_weight_bf16(w_hbm, layer_in_kind, wbf_ref, stage_ref, sem)

    is_prompt = i + tile0 < PROMPT_TOKENS // tm
    a = jnp.where(is_prompt, a_refs[0][...], a_refs[1][...]) if split_a else a_refs[0][...]
    ssq = jnp.zeros((tm, 1), F32)
    for cb in range(D_MODEL // OUT_COLS):
        sl = slice(cb * OUT_COLS, (cb + 1) * OUT_COLS)
        acc = jnp.dot(a, wbf_ref[:, sl], preferred_element_type=F32)
        if split_x:
            x = jnp.where(is_prompt, x_refs[0][:, sl], x_refs[1][:, sl])
        else:
            x = x_refs[0][:, sl]
        xn = x + gate_ref[:, sl] * acc
        xnew_ref[:, sl] = xn
        if fuse_norm:
            ssq = ssq + jnp.sum(xn * xn, axis=-1, keepdims=True)
    if fuse_norm:
        rs = lax.rsqrt(ssq * (1.0 / D_MODEL) + EPS)
        for cb in range(D_MODEL // OUT_COLS):
            sl = slice(cb * OUT_COLS, (cb + 1) * OUT_COLS)
            gmul = g_ref[:, sl] * (1.0 + scale_ref[:, sl])
            hn_ref[:, sl] = (xnew_ref[:, sl] * rs * gmul + shift_ref[:, sl]).astype(BF16)


def _out_proj(a, w, layer_in_kind, x, mod4, layer, norm_g3, name, tm, rows=None):
    split_a = isinstance(a, tuple)
    k = a[0].shape[1] if split_a else a.shape[1]
    tok0, n_tok = rows if rows is not None else (0, TOKENS)
    tile0 = tok0 // tm
    split_x = isinstance(x, tuple)
    fuse_norm = norm_g3 is not None
    mod_spec = lambda part, lyr: pl.BlockSpec(
        (None, None, 1, D_MODEL), lambda i: (lyr, _cond_row((i + tile0) * tm), 0, part))
    row_spec = lambda width: pl.BlockSpec((tm, width), lambda i: (i + tile0, 0))
    operands = [*a, w] if split_a else [a, w]
    in_specs = [*(_split_specs(tm, k, tile0) if split_a else [row_spec(k)]),
                pl.BlockSpec(memory_space=pl.ANY)]
    if split_x:
        assert rows is None
        operands += list(x)
        in_specs += _split_specs(tm, D_MODEL)
    else:
        operands.append(x)
        in_specs.append(row_spec(D_MODEL))
    operands.append(mod4)
    in_specs.append(mod_spec(2, layer))
    out_shapes = [jax.ShapeDtypeStruct((n_tok, D_MODEL), F32)]
    out_specs = [pl.BlockSpec((tm, D_MODEL), lambda i: (i, 0))]
    if fuse_norm:
        operands += [norm_g3, mod4, mod4]
        in_specs += [pl.BlockSpec((None, 1, D_MODEL), lambda i: (layer + 1, 0, 0)),
                     mod_spec(0, layer + 1), mod_spec(1, layer + 1)]
        out_shapes.append(jax.ShapeDtypeStruct((n_tok, D_MODEL), BF16))
        out_specs.append(pl.BlockSpec((tm, D_MODEL), lambda i: (i, 0)))
    kern = functools.partial(_out_kernel, layer_in_kind=layer_in_kind, tile0=tile0, tm=tm,
                             split_a=split_a, split_x=split_x, fuse_norm=fuse_norm)
    return pl.pallas_call(
        kern,
        grid=(n_tok // tm,),
        in_specs=in_specs,
        out_specs=out_specs,
        out_shape=out_shapes,
        scratch_shapes=[pltpu.VMEM((k, D_MODEL), BF16),
                        pltpu.VMEM((2, W_STAGE_ROWS, D_MODEL), F32),
                        pltpu.SemaphoreType.DMA((2,))],
        compiler_params=_params(1),
        name=name,
    )(*operands)


def _dot_nt(a, b):
    return lax.dot_general(a, b, (((1,), (1,)), ((), ())), preferred_element_type=F32)


def _head(ref, h, rows=slice(None)):
    return ref[rows, h * HEAD_DIM:(h + 1) * HEAD_DIM]


def _with_ones(v):
    return jnp.concatenate([v, jnp.ones(v.shape, v.dtype)], axis=1)


def _lane_chunks(s):
    return [s[:, c * HEAD_DIM:(c + 1) * HEAD_DIM] for c in range(s.shape[1] // HEAD_DIM)]


def _softmax_pv(score_blocks, value_blocks, sink):
    rows = score_blocks[0].shape[0]
    mx = functools.reduce(jnp.maximum, [c for s in score_blocks for c in _lane_chunks(s)])
    m = jnp.broadcast_to(jnp.max(mx, axis=-1, keepdims=True), (rows, HEAD_DIM))
    if sink is not None:
        m = jnp.maximum(m, sink)
    o = None
    for s, v in zip(score_blocks, value_blocks):
        p = jnp.concatenate([jnp.exp2(c - m) for c in _lane_chunks(s)], axis=1).astype(BF16)
        part = jnp.dot(p, v, preferred_element_type=F32)
        o = part if o is None else o + part
    den = o[:, HEAD_DIM:]
    if sink is not None:
        den = den + jnp.exp2(sink - m)
    return o[:, :HEAD_DIM] * (1.0 / den)


def _attn_prompt_kernel(sink_ref, q_ref, k_ref, v_ref, z_ref, o_ref, *, n_heads, n_kv, use_sink):
    grp = n_heads // n_kv
    for g in range(n_kv):
        kg = _head(k_ref, g)
        vg = _with_ones(_head(v_ref, g))
        for h in range(g * grp, (g + 1) * grp):
            s = _dot_nt(_head(q_ref, h), kg)
            sink = jnp.full((PROMPT_SEQ, HEAD_DIM), sink_ref[h], F32) if use_sink else None
            o = _softmax_pv([s], [vg], sink)
            o_ref[:, h * HEAD_DIM:(h + 1) * HEAD_DIM] = (o * _head(z_ref, h).astype(F32)).astype(BF16)


def _attn_prompt(sink, q, k, v, z, n_kv, use_sink, name):
    width = q.shape[1]
    kern = functools.partial(_attn_prompt_kernel, n_heads=width // HEAD_DIM, n_kv=n_kv, use_sink=use_sink)
    cache_spec = pl.BlockSpec((PROMPT_SEQ, k.shape[1]), lambda b: (b, 0))
    return pl.pallas_call(
        kern,
        grid=(N_PROMPT,),
        in_specs=[pl.BlockSpec(memory_space=pltpu.SMEM),
                  pl.BlockSpec((PROMPT_SEQ, width), lambda b: (b, 0)),
                  cache_spec, cache_spec,
                  pl.BlockSpec((PROMPT_SEQ, width), lambda b: (b, 0))],
        out_specs=pl.BlockSpec((PROMPT_SEQ, width), lambda b: (b, 0)),
        out_shape=jax.ShapeDtypeStruct((PROMPT_TOKENS, width), BF16),
        compiler_params=_params(1),
        name=name,
    )(sink, q, k, v, z)


WIN_BAND = 3 * WINDOW


def _attn_win_kernel(sink_ref, q_ref, k_ref, v_ref, kc_ref, vc_ref, z_ref, o_ref, *, n_heads, n_kv):
    grp = n_heads // n_kv
    n = pl.program_id(1)
    start = pl.multiple_of(jnp.clip((n - 1) * WINDOW, 0, SAMPLE_SEQ - WIN_BAND), WINDOW)
    shape = (grp * WINDOW, WIN_BAND)
    qpos = n * WINDOW + (lax.broadcasted_iota(jnp.int32, shape, 0) & (WINDOW - 1))
    kpos = start + lax.broadcasted_iota(jnp.int32, shape, 1)
    valid = jnp.abs(kpos - qpos) <= WINDOW
    band = pl.ds(start, WIN_BAND)
    for g in range(n_kv):
        heads = [g * grp + t for t in range(grp)]
        qs = jnp.concatenate([_head(q_ref, h) for h in heads], axis=0)
        kb = _head(k_ref, g, band)
        vb = _with_ones(_head(v_ref, g, band))
        ctx_rows = pl.ds(g, PAST_LEN, stride=n_kv)
        kc = kc_ref[ctx_rows, :].astype(BF16)
        vc = _with_ones(vc_ref[ctx_rows, :].astype(BF16))
        s_band = jnp.where(valid, _dot_nt(qs, kb), NEG_INF)
        s_ctx = _dot_nt(qs, kc)
        sink = jnp.concatenate([jnp.full((WINDOW, HEAD_DIM), sink_ref[h], F32) for h in heads], axis=0)
        o = _softmax_pv([s_band, s_ctx], [vb, vc], sink)
        for t, h in enumerate(heads):
            oh = o[t * WINDOW:(t + 1) * WINDOW]
            o_ref[:, h * HEAD_DIM:(h + 1) * HEAD_DIM] = (oh * _head(z_ref, h).astype(F32)).astype(BF16)


def _attn_win(sink, q, k, v, cache_k, cache_v, layer_in_kind, z, n_kv):
    width = q.shape[1]
    kv_width = k.shape[1]
    blocks_per_seq = SAMPLE_SEQ // WINDOW
    q_map = lambda b, n: (PROMPT_TOKENS // WINDOW + b * blocks_per_seq + n, 0)
    kv_map = lambda b, n: (PROMPT_TOKENS // SAMPLE_SEQ + b, 0)
    cache_map = lambda b, n: (b, layer_in_kind, 0, 0)
    kern = functools.partial(_attn_win_kernel, n_heads=width // HEAD_DIM, n_kv=n_kv)
    return pl.pallas_call(
        kern,
        grid=(N_SAMPLE, blocks_per_seq),
        in_specs=[pl.BlockSpec(memory_space=pltpu.SMEM),
                  pl.BlockSpec((WINDOW, width), q_map),
                  pl.BlockSpec((SAMPLE_SEQ, kv_width), kv_map),
                  pl.BlockSpec((SAMPLE_SEQ, kv_width), kv_map),
                  pl.BlockSpec((None, None, PAST_LEN * n_kv, HEAD_DIM), cache_map),
                  pl.BlockSpec((None, None, PAST_LEN * n_kv, HEAD_DIM), cache_map),
                  pl.BlockSpec((WINDOW, width), q_map)],
        out_specs=pl.BlockSpec((WINDOW, width), lambda b, n: (b * blocks_per_seq + n, 0)),
        out_shape=jax.ShapeDtypeStruct((SAMPLE_TOKENS, width), BF16),
        compiler_params=_params(2),
        name="attn_win_latent",
    )(sink, q, k, v, cache_k, cache_v, z)


NAT_QROWS = 4
GRID_ROWS = SAMPLE_SEQ // GRID_W
N_DR = 2 * NAT_ROWS - 1
MASKED_TILE = N_DR


def _nat_row_start(qr):
    return min(max(qr - NAT_ROWS // 2, 0), GRID_ROWS - NAT_ROWS)


def _attn_nat_kernel(bias_ref, q_ref, k_ref, v_ref, kc_ref, vc_ref, z_ref, o_ref, left_ref, right_ref):
    def build_bias_tiles():
        shape = (GRID_W, 2 * GRID_W)
        lane = lax.broadcasted_iota(jnp.int32, shape, 1)
        qc = lax.broadcasted_iota(jnp.int32, shape, 0)
        kc_ = lane & (GRID_W - 1)
        cs = jnp.clip(qc - NAT_COLS // 2, 0, GRID_W - NAT_COLS)
        col_ok = jnp.logical_and(kc_ >= cs, kc_ < cs + NAT_COLS)
        is_left = lane < GRID_W
        lanes = 2 * GRID_W
        for dri in range(N_DR):
            row = jnp.broadcast_to(bias_ref[dri:dri + 1, :], shape)
            on_left = pltpu.roll(row, lanes - (NAT_COLS - 1), 1, stride=1, stride_axis=0)
            on_right = pltpu.roll(row, GRID_W - (NAT_COLS - 1), 1, stride=1, stride_axis=0)
            left_ref[dri] = jnp.where(is_left, jnp.where(col_ok, on_left, NEG_INF), 0.0)
            right_ref[dri] = jnp.where(is_left, 0.0, jnp.where(col_ok, on_right, NEG_INF))
        left_ref[MASKED_TILE] = jnp.where(is_left, NEG_INF, 0.0)
        right_ref[MASKED_TILE] = jnp.where(is_left, 0.0, NEG_INF)

    build_bias_tiles()
    n_heads = kc_ref.shape[0] // PAST_LEN
    ctx_rows = pl.ds(pl.program_id(1), PAST_LEN, stride=n_heads)
    kc = kc_ref[ctx_rows, :].astype(BF16)
    vc = _with_ones(vc_ref[ctx_rows, :].astype(BF16))
    for qb in range(GRID_ROWS // NAT_QROWS):
        qrows = range(qb * NAT_QROWS, (qb + 1) * NAT_QROWS)
        krow0 = _nat_row_start(qrows[0]) // 2 * 2
        krow1 = -(-(_nat_row_start(qrows[-1]) + NAT_ROWS) // 2) * 2
        kwin = slice(krow0 * GRID_W, krow1 * GRID_W)
        qwin = slice(qrows[0] * GRID_W, (qrows[-1] + 1) * GRID_W)

        def tile_index(qr, kr):
            rs = _nat_row_start(qr)
            return kr - qr + (NAT_ROWS - 1) if rs <= kr < rs + NAT_ROWS else MASKED_TILE

        bias = jnp.concatenate(
            [jnp.concatenate([left_ref[tile_index(qr, kr)] + right_ref[tile_index(qr, kr + 1)]
                              for kr in range(krow0, krow1, 2)], axis=1)
             for qr in qrows], axis=0)

        q = q_ref[qwin, :]
        s_nb = _dot_nt(q, k_ref[kwin, :]) + bias
        s_ctx = _dot_nt(q, kc)
        o = _softmax_pv([s_nb, s_ctx], [_with_ones(v_ref[kwin, :]), vc], None)
        o_ref[qwin, :] = (o * z_ref[qwin, :].astype(F32)).astype(BF16)


def _attn_nat(rel_bias, q, k, v, cache_k, cache_v, layer_in_kind, z):
    n_heads = q.shape[1] // HEAD_DIM
    bias_rows = jnp.pad(rel_bias, ((0, 0), (0, N_DR + 1 - rel_bias.shape[1]),
                                   (0, 2 * GRID_W - rel_bias.shape[2])))
    q_map = lambda b, h: (PROMPT_TOKENS // SAMPLE_SEQ + b, h)
    kv_map = lambda b, h: (PROMPT_TOKENS // SAMPLE_SEQ + b, h)
    cache_map = lambda b, h: (b, layer_in_kind, 0, 0)
    return pl.pallas_call(
        _attn_nat_kernel,
        grid=(N_SAMPLE, n_heads),
        in_specs=[pl.BlockSpec((None, N_DR + 1, 2 * GRID_W), lambda b, h: (h, 0, 0)),
                  pl.BlockSpec((SAMPLE_SEQ, HEAD_DIM), q_map),
                  pl.BlockSpec((SAMPLE_SEQ, HEAD_DIM), kv_map),
                  pl.BlockSpec((SAMPLE_SEQ, HEAD_DIM), kv_map),
                  pl.BlockSpec((None, None, PAST_LEN * n_heads, HEAD_DIM), cache_map),
                  pl.BlockSpec((None, None, PAST_LEN * n_heads, HEAD_DIM), cache_map),
                  pl.BlockSpec((SAMPLE_SEQ, HEAD_DIM), q_map)],
        out_specs=pl.BlockSpec((SAMPLE_SEQ, HEAD_DIM), lambda b, h: (b, h)),
        out_shape=jax.ShapeDtypeStruct((SAMPLE_TOKENS, n_heads * HEAD_DIM), BF16),
        scratch_shapes=[pltpu.VMEM((N_DR + 1, GRID_W, 2 * GRID_W), F32),
                        pltpu.VMEM((N_DR + 1, GRID_W, 2 * GRID_W), F32)],
        compiler_params=_params(2),
        name="attn_nat_latent",
    )(bias_rows, q, k, v, cache_k, cache_v, z)


SPATIAL_TOKENS = 2 * CHUNK


def _spatial_kernel(u_ref, v_ref, z_ref, g_ref, b_ref, ws_ref, bs_ref, o_ref):
    for c in range(SPATIAL_TOKENS // CHUNK):
        rows = slice(c * CHUNK, (c + 1) * CHUNK)
        v = v_ref[rows, :].astype(F32)
        mu = jnp.mean(v, axis=-1, keepdims=True)
        vc = v - mu
        var = jnp.mean(vc * vc, axis=-1, keepdims=True)
        vn = (vc * lax.rsqrt(var + EPS) * g_ref[...] + b_ref[...]).astype(BF16)
        for g in range(GMLP_GROUPS):
            sl = slice(g * GMLP_GROUP_WIDTH, (g + 1) * GMLP_GROUP_WIDTH)
            sv = jnp.dot(ws_ref[g], vn[:, sl], preferred_element_type=F32) + bs_ref[:, g:g + 1]
            o_ref[rows, sl] = u_ref[rows, sl] * sv.astype(BF16) * z_ref[rows, sl]


def _spatial(u, v, z, ln_g, ln_b, w_s, b_s):
    row = pl.BlockSpec((SPATIAL_TOKENS, GMLP_WIDTH), lambda i: (i, 0))
    vec = pl.BlockSpec((1, GMLP_WIDTH), lambda i: (0, 0))
    return pl.pallas_call(
        _spatial_kernel,
        grid=(TOKENS // SPATIAL_TOKENS,),
        in_specs=[row, row, row, vec, vec,
                  pl.BlockSpec((GMLP_GROUPS, CHUNK, CHUNK), lambda i: (0, 0, 0)),
                  pl.BlockSpec((CHUNK, GMLP_GROUPS), lambda i: (0, 0))],
        out_specs=row,
        out_shape=jax.ShapeDtypeStruct((TOKENS, GMLP_WIDTH), BF16),
        compiler_params=_params(1),
        name="gmlp_spatial",
    )(u, v, z, ln_g.reshape(1, -1), ln_b.reshape(1, -1), w_s.astype(BF16), b_s.T)


def kernel(x_prompt, x_sample, cache_win_k, cache_win_v, cache_nat_k, cache_nat_v, c, c_ctx,
           norm_g, w_ada, b_ada,
           win_w_in, win_q_norm, win_k_norm, win_sink, win_w_out,
           nat_w_in, nat_q_norm, nat_k_norm, nat_rel_bias, nat_w_out,
           gmlp_w_in, gmlp_ln_g, gmlp_ln_b, gmlp_w_s, gmlp_b_s, gmlp_w_out):
    depth = norm_g.shape[0]
    xp = x_prompt.reshape(PROMPT_TOKENS, D_MODEL)
    xs = x_sample.reshape(SAMPLE_TOKENS, D_MODEL)
    cond = jnp.zeros((COND_ROWS, D_MODEL), F32).at[:N_SAMPLE].set(c).at[CTX_COND_ROW].set(c_ctx)
    mod4 = _ada(cond, w_ada, b_ada).reshape(depth, COND_ROWS, 1, 3 * D_MODEL)
    norm_g3 = norm_g.reshape(depth, 1, D_MODEL)
    rope_tabs = _rope_tables()
    no_sink = jnp.zeros((1,), F32)

    n_win = win_w_in.shape[0]
    n_nat = nat_w_in.shape[0]
    win_kv_heads = cache_win_k.shape[3]
    win_kv_width = win_kv_heads * HEAD_DIM
    cwk = cache_win_k.reshape(N_SAMPLE, n_win, PAST_LEN * win_kv_heads, HEAD_DIM)
    cwv = cache_win_v.reshape(N_SAMPLE, n_win, PAST_LEN * win_kv_heads, HEAD_DIM)
    nat_heads = cache_nat_k.shape[3]
    cnk = cache_nat_k.reshape(N_SAMPLE, n_nat, PAST_LEN * nat_heads, HEAD_DIM)
    cnv = cache_nat_v.reshape(N_SAMPLE, n_nat, PAST_LEN * nat_heads, HEAD_DIM)

    new_win_k, new_win_v, new_nat_k, new_nat_v = [], [], [], []
    x = (xp, xs)
    hn = _norm_mod(xp, xs, norm_g3, mod4, 0)
    for layer in range(depth):
        kind = layer % 3
        li = layer // 3
        if kind == 0:
            nq = D_MODEL // PROJ_TN
            nkv = win_kv_width // PROJ_TN
            q = _query_proj(hn, win_w_in, li, 0, nq, win_q_norm[li], rope_tabs, "win_q")
            kp, ks = _key_proj(hn, win_w_in, li, nq, nkv, win_k_norm[li], rope_tabs, True, "win_k")
            vp, vs = _value_proj(hn, win_w_in, li, nq + nkv, nkv, True, "win_v")
            new_win_k.append(kp)
            new_win_v.append(vp)
            z = _act_proj(hn, win_w_in, li, nq + 2 * nkv, nq, _epi_silu, "win_z")
            sink = win_sink[li] * LOG2E
            a = (_attn_prompt(sink, q, ks, vs, z, win_kv_heads, True, "attn_win_prompt"),
                 _attn_win(sink, q, ks, vs, cwk, cwv, li, z, win_kv_heads))
            w_out, out_tm, out_name = win_w_out, 512, "win_out"
        elif kind == 1:
            nq = D_MODEL // PROJ_TN
            q = _query_proj(hn, nat_w_in, li, 0, nq, nat_q_norm[li], None, "nat_q")
            kp, ks = _key_proj(hn, nat_w_in, li, nq, nq, nat_k_norm[li], None, False, "nat_k")
            vp, vs = _value_proj(hn, nat_w_in, li, 2 * nq, nq, False, "nat_v")
            new_nat_k.append(kp)
            new_nat_v.append(vp)
            z = _act_proj(hn, nat_w_in, li, 3 * nq, nq, _epi_silu, "nat_z")
            a = (_attn_prompt(no_sink, q, ks, vs, z, nat_heads, False, "attn_nat_prompt"),
                 _attn_nat(nat_rel_bias[li] * LOG2E, q, ks, vs, cnk, cnv, li, z))
            w_out, out_tm, out_name = nat_w_out, 512, "nat_out"
        else:
            nw = GMLP_WIDTH // PROJ_TN
            u = _act_proj(hn, gmlp_w_in, li, 0, nw, _epi_gelu, "gmlp_u")
            v = _act_proj(hn, gmlp_w_in, li, nw, nw, _epi_gelu, "gmlp_v")
            z = _act_proj(hn, gmlp_w_in, li, 2 * nw, nw, _epi_silu, "gmlp_z")
            a = _spatial(u, v, z, gmlp_ln_g[li], gmlp_ln_b[li], gmlp_w_s[li], gmlp_b_s[li])
            w_out, out_tm, out_name = gmlp_w_out, 256, "gmlp_out"
        if layer + 1 < depth:
            x, hn = _out_proj(a, w_out, li, x, mod4, layer, norm_g3, out_name, out_tm)
        else:
            (yp,) = _out_proj(a, w_out, li, x, mod4, layer, None, out_name + "_prompt", out_tm,
                              rows=(0, PROMPT_TOKENS))
            (ys,) = _out_proj(a, w_out, li, x, mod4, layer, None, out_name + "_latent", out_tm,
                              rows=(PROMPT_TOKENS, SAMPLE_TOKENS))

    cache_shape = lambda layers, heads: jnp.stack(
        [c_.reshape(N_PROMPT, PROMPT_SEQ, heads, HEAD_DIM) for c_ in layers], axis=1)
    return (yp.reshape(N_PROMPT, PROMPT_SEQ, D_MODEL), ys.reshape(N_SAMPLE, SAMPLE_SEQ, D_MODEL),
            cache_shape(new_win_k, win_kv_heads), cache_shape(new_win_v, win_kv_heads),
            cache_shape(new_nat_k, nat_heads), cache_shape(new_nat_v, nat_heads))
```

```python
import functools
import math

import jax
import jax.numpy as jnp
import numpy as np
from jax import lax
from jax.experimental import pallas as pl
from jax.experimental.pallas import tpu as pltpu

F32 = jnp.float32
BF16 = jnp.bfloat16

D_MODEL = 2048
HEAD_DIM = 128
N_PROMPT = 16
PROMPT_SEQ = 256
N_SAMPLE = 2
SAMPLE_SEQ = 1024
PROMPT_TOKENS = N_PROMPT * PROMPT_SEQ
SAMPLE_TOKENS = N_SAMPLE * SAMPLE_SEQ
TOKENS = PROMPT_TOKENS + SAMPLE_TOKENS
PAST_LEN = 512
GRID_W = 64
EPS = 1e-6
NEG_INF = -1e30
ROPE_THETA = 10000.0
WINDOW = 128
NAT_ROWS = 8
NAT_COLS = 16
GMLP_WIDTH = 2 * D_MODEL
GMLP_GROUPS = 16
GMLP_GROUP_WIDTH = GMLP_WIDTH // GMLP_GROUPS
CHUNK = 128
CTX_COND_ROW = 2
COND_ROWS = 8
SM_SCALE = HEAD_DIM ** -0.5
LOG2E = math.log2(math.e)
QUERY_SCALE = SM_SCALE * LOG2E

VMEM_LIMIT = 56 * 1024 * 1024


def _params(n_axes, vmem=VMEM_LIMIT, flags=None):
    return pltpu.CompilerParams(dimension_semantics=("arbitrary",) * n_axes,
                                vmem_limit_bytes=vmem, flags=flags)


def _cond_row(tok0):
    return jnp.where(tok0 < PROMPT_TOKENS, CTX_COND_ROW, (tok0 - PROMPT_TOKENS) // SAMPLE_SEQ)


def _silu(x):
    return x * (0.5 + 0.5 * jnp.tanh(0.5 * x))


def _gelu_tanh(x):
    return 0.5 * x * (1.0 + jnp.tanh(math.sqrt(2.0 / math.pi) * (x + 0.044715 * (x * x * x))))


def _split_specs(tm, width, tile0=0):
    n_p = PROMPT_TOKENS // tm
    return [pl.BlockSpec((tm, width), lambda i: (jnp.minimum(i + tile0, n_p - 1), 0)),
            pl.BlockSpec((tm, width), lambda i: (jnp.maximum(i + tile0 - n_p, 0), 0))]


def _ada_kernel(cond_ref, w_ref, b_ref, o_ref):
    s = _silu(cond_ref[...]).astype(BF16)
    o_ref[...] = jnp.dot(s, w_ref[...].astype(BF16), preferred_element_type=F32) + b_ref[...]


def _ada(cond, w_ada, b_ada, tn=1024):
    depth = w_ada.shape[0]
    n = w_ada.shape[2]
    return pl.pallas_call(
        _ada_kernel,
        grid=(depth, n // tn),
        in_specs=[pl.BlockSpec((COND_ROWS, D_MODEL), lambda l, j: (0, 0)),
                  pl.BlockSpec((None, D_MODEL, tn), lambda l, j: (l, 0, j)),
                  pl.BlockSpec((None, 1, tn), lambda l, j: (l, 0, j))],
        out_specs=pl.BlockSpec((None, COND_ROWS, tn), lambda l, j: (l, 0, j)),
        out_shape=jax.ShapeDtypeStruct((depth, COND_ROWS, n), F32),
        compiler_params=_params(2),
        name="ada_mod",
    )(cond, w_ada, b_ada.reshape(depth, 1, n))


def _norm_kernel(xp_ref, xs_ref, g_ref, shift_ref, scale_ref, o_ref, *, tm):
    x = jnp.where(pl.program_id(0) < PROMPT_TOKENS // tm, xp_ref[...], xs_ref[...])
    y = x * lax.rsqrt(jnp.mean(x * x, axis=-1, keepdims=True) + EPS) * g_ref[...]
    o_ref[...] = (y * (1.0 + scale_ref[...]) + shift_ref[...]).astype(BF16)


def _norm_mod(xp, xs, norm_g3, mod4, layer, tm=512):
    return pl.pallas_call(
        functools.partial(_norm_kernel, tm=tm),
        grid=(TOKENS // tm,),
        in_specs=[*_split_specs(tm, D_MODEL),
                  pl.BlockSpec((None, 1, D_MODEL), lambda i: (layer, 0, 0)),
                  pl.BlockSpec((None, None, 1, D_MODEL), lambda i: (layer, _cond_row(i * tm), 0, 0)),
                  pl.BlockSpec((None, None, 1, D_MODEL), lambda i: (layer, _cond_row(i * tm), 0, 1))],
        out_specs=pl.BlockSpec((tm, D_MODEL), lambda i: (i, 0)),
        out_shape=jax.ShapeDtypeStruct((TOKENS, D_MODEL), BF16),
        compiler_params=_params(1),
        name="norm_mod",
    )(xp, xs, norm_g3, mod4, mod4)


PROJ_TM = 2048
PROJ_TN = 512
PROJ_TN_WIDE = 1024
PROJ_ROWS = 512
N_PROMPT_TILES = PROMPT_TOKENS // PROJ_TM


def _proj_kernel(*refs, epilogue, n_extra, n_out, by_tile_kind):
    a_ref, w_ref = refs[0], refs[1]
    extras = refs[2:2 + n_extra]
    outs = refs[2 + n_extra:2 + n_extra + n_out]
    wbf_ref = refs[2 + n_extra + n_out]

    @pl.when(pl.program_id(1) == 0)
    def _():
        wbf_ref[...] = w_ref[...].astype(BF16)

    def body(latent):
        for rc in range(PROJ_TM // PROJ_ROWS):
            rows = slice(rc * PROJ_ROWS, (rc + 1) * PROJ_ROWS)
            for col0 in range(0, wbf_ref.shape[1], PROJ_TN):
                acc = jnp.dot(a_ref[rows, :], wbf_ref[:, col0:col0 + PROJ_TN], preferred_element_type=F32)
                epilogue(acc, extras, outs, rc, col0, latent)

    if by_tile_kind:
        is_latent = pl.program_id(1) >= N_PROMPT_TILES
        pl.when(is_latent)(lambda: body(True))
        pl.when(jnp.logical_not(is_latent))(lambda: body(False))
    else:
        body(None)


def _proj(a, w, layer, col_blk0, n_blk, epilogue, out_shapes, out_specs, name,
          extras=(), extra_specs=(), by_tile_kind=False, tn=PROJ_TN):
    m, k = a.shape
    wide = tn // PROJ_TN
    assert col_blk0 % wide == 0 and n_blk % wide == 0
    kern = functools.partial(_proj_kernel, epilogue=epilogue, n_extra=len(extras), n_out=len(out_shapes),
                             by_tile_kind=by_tile_kind)
    return pl.pallas_call(
        kern,
        grid=(n_blk // wide, m // PROJ_TM),
        in_specs=[pl.BlockSpec((PROJ_TM, k), lambda j, i: (i, 0)),
                  pl.BlockSpec((None, k, tn), lambda j, i: (layer, 0, col_blk0 // wide + j)),
                  *extra_specs],
        out_specs=out_specs,
        out_shape=out_shapes,
        scratch_shapes=[pltpu.VMEM((k, tn), BF16)],
        compiler_params=_params(2),
        name=name,
    )(a, w, *extras)


def _full_out(n_blk, dtype):
    return ([jax.ShapeDtypeStruct((TOKENS, n_blk * PROJ_TN), dtype)],
            [pl.BlockSpec((PROJ_TM, PROJ_TN_WIDE), lambda j, i: (i, j))])


def _split_out(n_blk, head_rows):
    prompt_tile = lambda i: jnp.minimum(i, N_PROMPT_TILES - 1)
    if head_rows:
        assert n_blk == 1
        heads = PROJ_TN // HEAD_DIM
        cache_shape = jax.ShapeDtypeStruct((PROMPT_TOKENS * heads, HEAD_DIM), F32)
        cache_spec = pl.BlockSpec((PROJ_TM * heads, HEAD_DIM), lambda j, i: (prompt_tile(i), 0))
    else:
        cache_shape = jax.ShapeDtypeStruct((N_PROMPT, PROMPT_SEQ, n_blk * PROJ_TN), F32)
        cache_spec = pl.BlockSpec((PROJ_TM // PROMPT_SEQ, PROMPT_SEQ, PROJ_TN),
                                  lambda j, i: (prompt_tile(i), 0, j))
    shapes = [cache_shape, jax.ShapeDtypeStruct((TOKENS, n_blk * PROJ_TN), BF16)]
    specs = [cache_spec, pl.BlockSpec((PROJ_TM, PROJ_TN), lambda j, i: (i, j))]
    return shapes, specs


def _chunk_rows(rc):
    return slice(rc * PROJ_ROWS, (rc + 1) * PROJ_ROWS)


def _epi_silu(acc, extras, outs, rc, col0, latent):
    outs[0][_chunk_rows(rc), col0:col0 + PROJ_TN] = _silu(acc).astype(BF16)


def _epi_gelu(acc, extras, outs, rc, col0, latent):
    outs[0][_chunk_rows(rc), col0:col0 + PROJ_TN] = _gelu_tanh(acc).astype(BF16)


def _store_split(y, outs, rc, sl, latent):
    outs[1][_chunk_rows(rc), sl] = y.astype(BF16)
    if latent:
        return
    if len(outs[0].shape) == 2:
        heads = PROJ_TN // HEAD_DIM
        col0 = sl.start or 0
        for c in range(y.shape[1] // HEAD_DIM):
            head = col0 // HEAD_DIM + c
            rows = pl.ds(rc * PROJ_ROWS * heads + head, PROJ_ROWS, stride=heads)
            outs[0][rows, :] = y[:, c * HEAD_DIM:(c + 1) * HEAD_DIM]
    else:
        seqs = PROJ_ROWS // PROMPT_SEQ
        outs[0][rc * seqs:(rc + 1) * seqs, :, sl] = y.reshape(seqs, PROMPT_SEQ, y.shape[1])


def _epi_value(acc, extras, outs, rc, col0, latent):
    _store_split(acc, outs, rc, slice(col0, col0 + PROJ_TN), latent)


def _head_rmsnorm(acc, hh, g):
    a = acc[:, hh * HEAD_DIM:(hh + 1) * HEAD_DIM]
    return a * lax.rsqrt(jnp.mean(a * a, axis=-1, keepdims=True) + EPS) * g


def _normed_heads(acc, extras, rc, rope):
    g = extras[0][...]
    ys = [_head_rmsnorm(acc, hh, g) for hh in range(PROJ_TN // HEAD_DIM)]
    if rope:
        rows = _chunk_rows(rc)
        cos, sin, swap = extras[1][rows, :], extras[2][rows, :], extras[3][...]
        for pair in range(len(ys) // 2):
            both = jnp.concatenate(ys[2 * pair:2 * pair + 2], axis=1).astype(BF16)
            partner = jnp.dot(both, swap, preferred_element_type=F32)
            for t in range(2):
                hh = 2 * pair + t
                ys[hh] = ys[hh] * cos + partner[:, t * HEAD_DIM:(t + 1) * HEAD_DIM] * sin
    for hh, y in enumerate(ys):
        yield slice(hh * HEAD_DIM, (hh + 1) * HEAD_DIM), y


def _epi_query(acc, extras, outs, rc, col0, latent, *, rope):
    for sl, y in _normed_heads(acc, extras, rc, rope and latent):
        outs[0][_chunk_rows(rc), col0 + sl.start:col0 + sl.stop] = (y * QUERY_SCALE).astype(BF16)


def _epi_key(acc, extras, outs, rc, col0, latent, *, rope):
    for sl, y in _normed_heads(acc, extras, rc, rope and latent):
        _store_split(y, outs, rc, slice(col0 + sl.start, col0 + sl.stop), latent)


def _rope_tables():
    nf = HEAD_DIM // 4
    t = np.arange(SAMPLE_TOKENS) % SAMPLE_SEQ
    row = (t // GRID_W).astype(np.float32)
    col = (t % GRID_W).astype(np.float32)
    inv = np.float32(ROPE_THETA) ** (-np.arange(nf, dtype=np.float32) / np.float32(nf))
    ang_r = row[:, None] * inv
    ang_c = col[:, None] * inv
    cos = np.concatenate([np.cos(ang_r), np.cos(ang_r), np.cos(ang_c), np.cos(ang_c)], axis=1)
    sin = np.concatenate([-np.sin(ang_r), np.sin(ang_r), -np.sin(ang_c), np.sin(ang_c)], axis=1)
    lanes = np.arange(2 * HEAD_DIM)
    swap = lanes[:, None] == (lanes[None, :] ^ (HEAD_DIM // 4))
    return jnp.asarray(cos, F32), jnp.asarray(sin, F32), jnp.asarray(swap, BF16)


def _gain_extras(gain, rope_tabs):
    extras = [gain.reshape(1, HEAD_DIM)]
    specs = [pl.BlockSpec((1, HEAD_DIM), lambda j, i: (0, 0))]
    if rope_tabs is not None:
        tab_map = lambda j, i: (jnp.maximum(i - N_PROMPT_TILES, 0), 0)
        extras += list(rope_tabs)
        specs += [pl.BlockSpec((PROJ_TM, HEAD_DIM), tab_map)] * 2
        specs.append(pl.BlockSpec((2 * HEAD_DIM, 2 * HEAD_DIM), lambda j, i: (0, 0)))
    return extras, specs


def _query_proj(hn, w, layer, col_blk0, n_blk, gain, rope_tabs, name):
    extras, specs = _gain_extras(gain, rope_tabs)
    epi = functools.partial(_epi_query, rope=rope_tabs is not None)
    return _proj(hn, w, layer, col_blk0, n_blk, epi, *_full_out(n_blk, BF16), name, extras, specs,
                 by_tile_kind=rope_tabs is not None, tn=PROJ_TN_WIDE)[0]


def _key_proj(hn, w, layer, col_blk0, n_blk, gain, rope_tabs, head_rows, name):
    extras, specs = _gain_extras(gain, rope_tabs)
    epi = functools.partial(_epi_key, rope=rope_tabs is not None)
    return _proj(hn, w, layer, col_blk0, n_blk, epi, *_split_out(n_blk, head_rows), name, extras, specs,
                 by_tile_kind=True)


def _value_proj(hn, w, layer, col_blk0, n_blk, head_rows, name):
    return _proj(hn, w, layer, col_blk0, n_blk, _epi_value, *_split_out(n_blk, head_rows), name,
                 by_tile_kind=True)


def _act_proj(hn, w, layer, col_blk0, n_blk, epi, name):
    return _proj(hn, w, layer, col_blk0, n_blk, epi, *_full_out(n_blk, BF16), name, tn=PROJ_TN_WIDE)[0]


OUT_COLS = 512
W_STAGE_ROWS = 256


def _load_weight_bf16(w_hbm, layer, wbf_ref, stage_ref, sem):
    k = wbf_ref.shape[0]
    n_chunks = k // W_STAGE_ROWS

    def copy(c, slot):
        return pltpu.make_async_copy(w_hbm.at[layer, pl.ds(c * W_STAGE_ROWS, W_STAGE_ROWS), :],
                                     stage_ref.at[slot], sem.at[slot])

    copy(0, 0).start()
    for c in range(n_chunks):
        slot = c % 2
        if c + 1 < n_chunks:
            copy(c + 1, 1 - slot).start()
        copy(c, slot).wait()
        wbf_ref[c * W_STAGE_ROWS:(c + 1) * W_STAGE_ROWS, :] = stage_ref[slot].astype(BF16)


def _out_kernel(*refs, layer_in_kind, tile0, tm, split_a, split_x, fuse_norm):
    it = iter(refs)
    a_refs = [next(it), next(it)] if split_a else [next(it)]
    w_hbm = next(it)
    x_refs = [next(it), next(it)] if split_x else [next(it)]
    gate_ref = next(it)
    if fuse_norm:
        g_ref, shift_ref, scale_ref = next(it), next(it), next(it)
    xnew_ref = next(it)
    hn_ref = next(it) if fuse_norm else None
    wbf_ref, stage_ref, sem = next(it), next(it), next(it)

    i = pl.program_id(0)

    @pl.when(i == 0)
    def _():
        _load_weight_bf16(w_hbm, layer_in_kind, wbf_ref, stage_ref, sem)

    is_prompt = i + tile0 < PROMPT_TOKENS // tm
    a = jnp.where(is_prompt, a_refs[0][...], a_refs[1][...]) if split_a else a_refs[0][...]
    ssq = jnp.zeros((tm, 1), F32)
    for cb in range(D_MODEL // OUT_COLS):
        sl = slice(cb * OUT_COLS, (cb + 1) * OUT_COLS)
        acc = jnp.dot(a, wbf_ref[:, sl], preferred_element_type=F32)
        if split_x:
            x = jnp.where(is_prompt, x_refs[0][:, sl], x_refs[1][:, sl])
        else:
            x = x_refs[0][:, sl]
        xn = x + gate_ref[:, sl] * acc
        xnew_ref[:, sl] = xn
        if fuse_norm:
            ssq = ssq + jnp.sum(xn * xn, axis=-1, keepdims=True)
    if fuse_norm:
        rs = lax.rsqrt(ssq * (1.0 / D_MODEL) + EPS)
        for cb in range(D_MODEL // OUT_COLS):
            sl = slice(cb * OUT_COLS, (cb + 1) * OUT_COLS)
            gmul = g_ref[:, sl] * (1.0 + scale_ref[:, sl])
            hn_ref[:, sl] = (xnew_ref[:, sl] * rs * gmul + shift_ref[:, sl]).astype(BF16)


def _out_proj(a, w, layer_in_kind, x, mod4, layer, norm_g3, name, tm, rows=None):
    split_a = isinstance(a, tuple)
    k = a[0].shape[1] if split_a else a.shape[1]
    tok0, n_tok = rows if rows is not None else (0, TOKENS)
    tile0 = tok0 // tm
    split_x = isinstance(x, tuple)
    fuse_norm = norm_g3 is not None
    mod_spec = lambda part, lyr: pl.BlockSpec(
        (None, None, 1, D_MODEL), lambda i: (lyr, _cond_row((i + tile0) * tm), 0, part))
    row_spec = lambda width: pl.BlockSpec((tm, width), lambda i: (i + tile0, 0))
    operands = [*a, w] if split_a else [a, w]
    in_specs = [*(_split_specs(tm, k, tile0) if split_a else [row_spec(k)]),
                pl.BlockSpec(memory_space=pl.ANY)]
    if split_x:
        assert rows is None
        operands += list(x)
        in_specs += _split_specs(tm, D_MODEL)
    else:
        operands.append(x)
        in_specs.append(row_spec(D_MODEL))
    operands.append(mod4)
    in_specs.append(mod_spec(2, layer))
    out_shapes = [jax.ShapeDtypeStruct((n_tok, D_MODEL), F32)]
    out_specs = [pl.BlockSpec((tm, D_MODEL), lambda i: (i, 0))]
    if fuse_norm:
        operands += [norm_g3, mod4, mod4]
        in_specs += [pl.BlockSpec((None, 1, D_MODEL), lambda i: (layer + 1, 0, 0)),
                     mod_spec(0, layer + 1), mod_spec(1, layer + 1)]
        out_shapes.append(jax.ShapeDtypeStruct((n_tok, D_MODEL), BF16))
        out_specs.append(pl.BlockSpec((tm, D_MODEL), lambda i: (i, 0)))
    kern = functools.partial(_out_kernel, layer_in_kind=layer_in_kind, tile0=tile0, tm=tm,
                             split_a=split_a, split_x=split_x, fuse_norm=fuse_norm)
    return pl.pallas_call(
        kern,
        grid=(n_tok // tm,),
        in_specs=in_specs,
        out_specs=out_specs,
        out_shape=out_shapes,
        scratch_shapes=[pltpu.VMEM((k, D_MODEL), BF16),
                        pltpu.VMEM((2, W_STAGE_ROWS, D_MODEL), F32),
                        pltpu.SemaphoreType.DMA((2,))],
        compiler_params=_params(1),
        name=name,
    )(*operands)


def _dot_nt(a, b):
    return lax.dot_general(a, b, (((1,), (1,)), ((), ())), preferred_element_type=F32)


def _head(ref, h, rows=slice(None)):
    return ref[rows, h * HEAD_DIM:(h + 1) * HEAD_DIM]


def _with_ones(v):
    return jnp.concatenate([v, jnp.ones(v.shape, v.dtype)], axis=1)


def _lane_chunks(s):
    return [s[:, c * HEAD_DIM:(c + 1) * HEAD_DIM] for c in range(s.shape[1] // HEAD_DIM)]


def _softmax_pv(score_blocks, value_blocks, sink):
    rows = score_blocks[0].shape[0]
    mx = functools.reduce(jnp.maximum, [c for s in score_blocks for c in _lane_chunks(s)])
    m = jnp.broadcast_to(jnp.max(mx, axis=-1, keepdims=True), (rows, HEAD_DIM))
    if sink is not None:
        m = jnp.maximum(m, sink)
    o = None
    for s, v in zip(score_blocks, value_blocks):
        p = jnp.concatenate([jnp.exp2(c - m) for c in _lane_chunks(s)], axis=1).astype(BF16)
        part = jnp.dot(p, v, preferred_element_type=F32)
        o = part if o is None else o + part
    den = o[:, HEAD_DIM:]
    if sink is not None:
        den = den + jnp.exp2(sink - m)
    return o[:, :HEAD_DIM] * (1.0 / den)


def _attn_prompt_kernel(sink_ref, q_ref, k_ref, v_ref, z_ref, o_ref, *, n_heads, n_kv, use_sink):
    grp = n_heads // n_kv
    for g in range(n_kv):
        kg = _head(k_ref, g)
        vg = _with_ones(_head(v_ref, g))
        for h in range(g * grp, (g + 1) * grp):
            s = _dot_nt(_head(q_ref, h), kg)
            sink = jnp.full((PROMPT_SEQ, HEAD_DIM), sink_ref[h], F32) if use_sink else None
            o = _softmax_pv([s], [vg], sink)
            o_ref[:, h * HEAD_DIM:(h + 1) * HEAD_DIM] = (o * _head(z_ref, h).astype(F32)).astype(BF16)


def _attn_prompt(sink, q, k, v, z, n_kv, use_sink, name):
    width = q.shape[1]
    kern = functools.partial(_attn_prompt_kernel, n_heads=width // HEAD_DIM, n_kv=n_kv, use_sink=use_sink)
    cache_spec = pl.BlockSpec((PROMPT_SEQ, k.shape[1]), lambda b: (b, 0))
    return pl.pallas_call(
        kern,
        grid=(N_PROMPT,),
        in_specs=[pl.BlockSpec(memory_space=pltpu.SMEM),
                  pl.BlockSpec((PROMPT_SEQ, width), lambda b: (b, 0)),
                  cache_spec, cache_spec,
                  pl.BlockSpec((PROMPT_SEQ, width), lambda b: (b, 0))],
        out_specs=pl.BlockSpec((PROMPT_SEQ, width), lambda b: (b, 0)),
        out_shape=jax.ShapeDtypeStruct((PROMPT_TOKENS, width), BF16),
        compiler_params=_params(1),
        name=name,
    )(sink, q, k, v, z)


WIN_BAND = 3 * WINDOW


def _attn_win_kernel(sink_ref, q_ref, k_ref, v_ref, kc_ref, vc_ref, z_ref, o_ref, *, n_heads, n_kv):
    grp = n_heads // n_kv
    n = pl.program_id(1)
    start = pl.multiple_of(jnp.clip((n - 1) * WINDOW, 0, SAMPLE_SEQ - WIN_BAND), WINDOW)
    shape = (grp * WINDOW, WIN_BAND)
    qpos = n * WINDOW + (lax.broadcasted_iota(jnp.int32, shape, 0) & (WINDOW - 1))
    kpos = start + lax.broadcasted_iota(jnp.int32, shape, 1)
    valid = jnp.abs(kpos - qpos) <= WINDOW
    band = pl.ds(start, WIN_BAND)
    for g in range(n_kv):
        heads = [g * grp + t for t in range(grp)]
        qs = jnp.concatenate([_head(q_ref, h) for h in heads], axis=0)
        kb = _head(k_ref, g, band)
        vb = _with_ones(_head(v_ref, g, band))
        ctx_rows = pl.ds(g, PAST_LEN, stride=n_kv)
        kc = kc_ref[ctx_rows, :].astype(BF16)
        vc = _with_ones(vc_ref[ctx_rows, :].astype(BF16))
        s_band = jnp.where(valid, _dot_nt(qs, kb), NEG_INF)
        s_ctx = _dot_nt(qs, kc)
        sink = jnp.concatenate([jnp.full((WINDOW, HEAD_DIM), sink_ref[h], F32) for h in heads], axis=0)
        o = _softmax_pv([s_band, s_ctx], [vb, vc], sink)
        for t, h in enumerate(heads):
            oh = o[t * WINDOW:(t + 1) * WINDOW]
            o_ref[:, h * HEAD_DIM:(h + 1) * HEAD_DIM] = (oh * _head(z_ref, h).astype(F32)).astype(BF16)


def _attn_win(sink, q, k, v, cache_k, cache_v, layer_in_kind, z, n_kv):
    width = q.shape[1]
    kv_width = k.shape[1]
    blocks_per_seq = SAMPLE_SEQ // WINDOW
    q_map = lambda b, n: (PROMPT_TOKENS // WINDOW + b * blocks_per_seq + n, 0)
    kv_map = lambda b, n: (PROMPT_TOKENS // SAMPLE_SEQ + b, 0)
    cache_map = lambda b, n: (b, layer_in_kind, 0, 0)
    kern = functools.partial(_attn_win_kernel, n_heads=width // HEAD_DIM, n_kv=n_kv)
    return pl.pallas_call(
        kern,
        grid=(N_SAMPLE, blocks_per_seq),
        in_specs=[pl.BlockSpec(memory_space=pltpu.SMEM),
                  pl.BlockSpec((WINDOW, width), q_map),
                  pl.BlockSpec((SAMPLE_SEQ, kv_width), kv_map),
                  pl.BlockSpec((SAMPLE_SEQ, kv_width), kv_map),
                  pl.BlockSpec((None, None, PAST_LEN * n_kv, HEAD_DIM), cache_map),
                  pl.BlockSpec((None, None, PAST_LEN * n_kv, HEAD_DIM), cache_map),
                  pl.BlockSpec((WINDOW, width), q_map)],
        out_specs=pl.BlockSpec((WINDOW, width), lambda b, n: (b * blocks_per_seq + n, 0)),
        out_shape=jax.ShapeDtypeStruct((SAMPLE_TOKENS, width), BF16),
        compiler_params=_params(2),
        name="attn_win_latent",
    )(sink, q, k, v, cache_k, cache_v, z)


NAT_QROWS = 4
GRID_ROWS = SAMPLE_SEQ // GRID_W
N_DR = 2 * NAT_ROWS - 1
MASKED_TILE = N_DR


def _nat_row_start(qr):
    return min(max(qr - NAT_ROWS // 2, 0), GRID_ROWS - NAT_ROWS)


def _attn_nat_kernel(bias_ref, q_ref, k_ref, v_ref, kc_ref, vc_ref, z_ref, o_ref, left_ref, right_ref):
    def build_bias_tiles():
        shape = (GRID_W, 2 * GRID_W)
        lane = lax.broadcasted_iota(jnp.int32, shape, 1)
        qc = lax.broadcasted_iota(jnp.int32, shape, 0)
        kc_ = lane & (GRID_W - 1)
        cs = jnp.clip(qc - NAT_COLS // 2, 0, GRID_W - NAT_COLS)
        col_ok = jnp.logical_and(kc_ >= cs, kc_ < cs + NAT_COLS)
        is_left = lane < GRID_W
        lanes = 2 * GRID_W
        for dri in range(N_DR):
            row = jnp.broadcast_to(bias_ref[dri:dri + 1, :], shape)
            on_left = pltpu.roll(row, lanes - (NAT_COLS - 1), 1, stride=1, stride_axis=0)
            on_right = pltpu.roll(row, GRID_W - (NAT_COLS - 1), 1, stride=1, stride_axis=0)
            left_ref[dri] = jnp.where(is_left, jnp.where(col_ok, on_left, NEG_INF), 0.0)
            right_ref[dri] = jnp.where(is_left, 0.0, jnp.where(col_ok, on_right, NEG_INF))
        left_ref[MASKED_TILE] = jnp.where(is_left, NEG_INF, 0.0)
        right_ref[MASKED_TILE] = jnp.where(is_left, 0.0, NEG_INF)

    build_bias_tiles()
    n_heads = kc_ref.shape[1] // PAST_LEN
    ctx_rows = pl.ds(pl.program_id(0), PAST_LEN, stride=n_heads)
    for b, qb in [(b, qb) for b in range(N_SAMPLE) for qb in range(GRID_ROWS // NAT_QROWS)]:
        if qb == 0:
            kc = kc_ref[b, ctx_rows, :].astype(BF16)
            vc = _with_ones(vc_ref[b, ctx_rows, :].astype(BF16))
        qrows = range(qb * NAT_QROWS, (qb + 1) * NAT_QROWS)
        krow0 = _nat_row_start(qrows[0]) // 2 * 2
        krow1 = -(-(_nat_row_start(qrows[-1]) + NAT_ROWS) // 2) * 2
        tok0 = b * SAMPLE_SEQ
        kwin = slice(tok0 + krow0 * GRID_W, tok0 + krow1 * GRID_W)
        qwin = slice(tok0 + qrows[0] * GRID_W, tok0 + (qrows[-1] + 1) * GRID_W)

        def tile_index(qr, kr):
            rs = _nat_row_start(qr)
            return kr - qr + (NAT_ROWS - 1) if rs <= kr < rs + NAT_ROWS else MASKED_TILE

        bias = jnp.concatenate(
            [jnp.concatenate([left_ref[tile_index(qr, kr)] + right_ref[tile_index(qr, kr + 1)]
                              for kr in range(krow0, krow1, 2)], axis=1)
             for qr in qrows], axis=0)

        q = q_ref[qwin, :]
        s_nb = _dot_nt(q, k_ref[kwin, :]) + bias
        s_ctx = _dot_nt(q, kc)
        o = _softmax_pv([s_nb, s_ctx], [_with_ones(v_ref[kwin, :]), vc], None)
        o_ref[qwin, :] = (o * z_ref[qwin, :].astype(F32)).astype(BF16)


def _attn_nat(rel_bias, q, k, v, cache_k, cache_v, layer_in_kind, z):
    n_heads = q.shape[1] // HEAD_DIM
    bias_rows = jnp.pad(rel_bias, ((0, 0), (0, N_DR + 1 - rel_bias.shape[1]),
                                   (0, 2 * GRID_W - rel_bias.shape[2])))
    latent_map = lambda h: (PROMPT_TOKENS // SAMPLE_TOKENS, h)
    latent_spec = pl.BlockSpec((SAMPLE_TOKENS, HEAD_DIM), latent_map)
    cache_spec = pl.BlockSpec((N_SAMPLE, None, PAST_LEN * n_heads, HEAD_DIM),
                              lambda h: (0, layer_in_kind, 0, 0))
    return pl.pallas_call(
        _attn_nat_kernel,
        grid=(n_heads,),
        in_specs=[pl.BlockSpec((None, N_DR + 1, 2 * GRID_W), lambda h: (h, 0, 0)),
                  latent_spec, latent_spec, latent_spec, cache_spec, cache_spec, latent_spec],
        out_specs=pl.BlockSpec((SAMPLE_TOKENS, HEAD_DIM), lambda h: (0, h)),
        out_shape=jax.ShapeDtypeStruct((SAMPLE_TOKENS, n_heads * HEAD_DIM), BF16),
        scratch_shapes=[pltpu.VMEM((N_DR + 1, GRID_W, 2 * GRID_W), F32),
                        pltpu.VMEM((N_DR + 1, GRID_W, 2 * GRID_W), F32)],
        compiler_params=_params(1),
        name="attn_nat_latent",
    )(bias_rows, q, k, v, cache_k, cache_v, z)


SPATIAL_TOKENS = 2 * CHUNK


def _spatial_kernel(u_ref, v_ref, z_ref, g_ref, b_ref, ws_ref, bs_ref, o_ref):
    for c in range(SPATIAL_TOKENS // CHUNK):
        rows = slice(c * CHUNK, (c + 1) * CHUNK)
        v = v_ref[rows, :].astype(F32)
        mu = jnp.mean(v, axis=-1, keepdims=True)
        vc = v - mu
        var = jnp.mean(vc * vc, axis=-1, keepdims=True)
        vn = (vc * lax.rsqrt(var + EPS) * g_ref[...] + b_ref[...]).astype(BF16)
        for g in range(GMLP_GROUPS):
            sl = slice(g * GMLP_GROUP_WIDTH, (g + 1) * GMLP_GROUP_WIDTH)
            sv = jnp.dot(ws_ref[g], vn[:, sl], preferred_element_type=F32) + bs_ref[:, g:g + 1]
            o_ref[rows, sl] = u_ref[rows, sl] * sv.astype(BF16) * z_ref[rows, sl]


def _spatial(u, v, z, ln_g, ln_b, w_s, b_s):
    row = pl.BlockSpec((SPATIAL_TOKENS, GMLP_WIDTH), lambda i: (i, 0))
    vec = pl.BlockSpec((1, GMLP_WIDTH), lambda i: (0, 0))
    return pl.pallas_call(
        _spatial_kernel,
        grid=(TOKENS // SPATIAL_TOKENS,),
        in_specs=[row, row, row, vec, vec,
                  pl.BlockSpec((GMLP_GROUPS, CHUNK, CHUNK), lambda i: (0, 0, 0)),
                  pl.BlockSpec((CHUNK, GMLP_GROUPS), lambda i: (0, 0))],
        out_specs=row,
        out_shape=jax.ShapeDtypeStruct((TOKENS, GMLP_WIDTH), BF16),
        compiler_params=_params(1),
        name="gmlp_spatial",
    )(u, v, z, ln_g.reshape(1, -1), ln_b.reshape(1, -1), w_s.astype(BF16), b_s.T)


def kernel(x_prompt, x_sample, cache_win_k, cache_win_v, cache_nat_k, cache_nat_v, c, c_ctx,
           norm_g, w_ada, b_ada,
           win_w_in, win_q_norm, win_k_norm, win_sink, win_w_out,
           nat_w_in, nat_q_norm, nat_k_norm, nat_rel_bias, nat_w_out,
           gmlp_w_in, gmlp_ln_g, gmlp_ln_b, gmlp_w_s, gmlp_b_s, gmlp_w_out):
    depth = norm_g.shape[0]
    xp = x_prompt.reshape(PROMPT_TOKENS, D_MODEL)
    xs = x_sample.reshape(SAMPLE_TOKENS, D_MODEL)
    cond = jnp.zeros((COND_ROWS, D_MODEL), F32).at[:N_SAMPLE].set(c).at[CTX_COND_ROW].set(c_ctx)
    mod4 = _ada(cond, w_ada, b_ada).reshape(depth, COND_ROWS, 1, 3 * D_MODEL)
    norm_g3 = norm_g.reshape(depth, 1, D_MODEL)
    rope_tabs = _rope_tables()
    no_sink = jnp.zeros((1,), F32)

    n_win = win_w_in.shape[0]
    n_nat = nat_w_in.shape[0]
    win_kv_heads = cache_win_k.shape[3]
    win_kv_width = win_kv_heads * HEAD_DIM
    cwk = cache_win_k.reshape(N_SAMPLE, n_win, PAST_LEN * win_kv_heads, HEAD_DIM)
    cwv = cache_win_v.reshape(N_SAMPLE, n_win, PAST_LEN * win_kv_heads, HEAD_DIM)
    nat_heads = cache_nat_k.shape[3]
    cnk = cache_nat_k.reshape(N_SAMPLE, n_nat, PAST_LEN * nat_heads, HEAD_DIM)
    cnv = cache_nat_v.reshape(N_SAMPLE, n_nat, PAST_LEN * nat_heads, HEAD_DIM)

    new_win_k, new_win_v, new_nat_k, new_nat_v = [], [], [], []
    x = (xp, xs)
    hn = _norm_mod(xp, xs, norm_g3, mod4, 0)
    for layer in range(depth):
        kind = layer % 3
        li = layer // 3
        if kind == 0:
            nq = D_MODEL // PROJ_TN
            nkv = win_kv_width // PROJ_TN
            q = _query_proj(hn, win_w_in, li, 0, nq, win_q_norm[li], rope_tabs, "win_q")
            kp, ks = _key_proj(hn, win_w_in, li, nq, nkv, win_k_norm[li], rope_tabs, True, "win_k")
            vp, vs = _value_proj(hn, win_w_in, li, nq + nkv, nkv, True, "win_v")
            new_win_k.append(kp)
            new_win_v.append(vp)
            z = _act_proj(hn, win_w_in, li, nq + 2 * nkv, nq, _epi_silu, "win_z")
            sink = win_sink[li] * LOG2E
            a = (_attn_prompt(sink, q, ks, vs, z, win_kv_heads, True, "attn_win_prompt"),
                 _attn_win(sink, q, ks, vs, cwk, cwv, li, z, win_kv_heads))
            w_out, out_tm, out_name = win_w_out, 512, "win_out"
        elif kind == 1:
            nq = D_MODEL // PROJ_TN
            q = _query_proj(hn, nat_w_in, li, 0, nq, nat_q_norm[li], None, "nat_q")
            kp, ks = _key_proj(hn, nat_w_in, li, nq, nq, nat_k_norm[li], None, False, "nat_k")
            vp, vs = _value_proj(hn, nat_w_in, li, 2 * nq, nq, False, "nat_v")
            new_nat_k.append(kp)
            new_nat_v.append(vp)
            z = _act_proj(hn, nat_w_in, li, 3 * nq, nq, _epi_silu, "nat_z")
            a = (_attn_prompt(no_sink, q, ks, vs, z, nat_heads, False, "attn_nat_prompt"),
                 _attn_nat(nat_rel_bias[li] * LOG2E, q, ks, vs, cnk, cnv, li, z))
            w_out, out_tm, out_name = nat_w_out, 512, "nat_out"
        else:
            nw = GMLP_WIDTH // PROJ_TN
            u = _act_proj(hn, gmlp_w_in, li, 0, nw, _epi_gelu, "gmlp_u")
            v = _act_proj(hn, gmlp_w_in, li, nw, nw, _epi_gelu, "gmlp_v")
            z = _act_proj(hn, gmlp_w_in, li, 2 * nw, nw, _epi_silu, "gmlp_z")
            a = _spatial(u, v, z, gmlp_ln_g[li], gmlp_ln_b[li], gmlp_w_s[li], gmlp_b_s[li])
            w_out, out_tm, out_name = gmlp_w_out, 256, "gmlp_out"
        if layer + 1 < depth:
            x, hn = _out_proj(a, w_out, li, x, mod4, layer, norm_g3, out_name, out_tm)
        else:
            (yp,) = _out_proj(a, w_out, li, x, mod4, layer, None, out_name + "_prompt", out_tm,
                              rows=(0, PROMPT_TOKENS))
            (ys,) = _out_proj(a, w_out, li, x, mod4, layer, None, out_name + "_latent", out_tm,
                              rows=(PROMPT_TOKENS, SAMPLE_TOKENS))

    cache_shape = lambda layers, heads: jnp.stack(
        [c_.reshape(N_PROMPT, PROMPT_SEQ, heads, HEAD_DIM) for c_ in layers], axis=1)
    return (yp.reshape(N_PROMPT, PROMPT_SEQ, D_MODEL), ys.reshape(N_SAMPLE, SAMPLE_SEQ, D_MODEL),
            cache_shape(new_win_k, win_kv_heads), cache_shape(new_win_v, win_kv_heads),
            cache_shape(new_nat_k, nat_heads), cache_shape(new_nat_v, nat_heads))
```

```python
import functools
import math
from typing import Callable, NamedTuple

import jax
import jax.numpy as jnp
import numpy as np
from jax import lax
from jax.experimental import pallas as pl
from jax.experimental.pallas import tpu as pltpu

F32 = jnp.float32
BF16 = jnp.bfloat16

D_MODEL = 2048
HEAD_DIM = 128
N_PROMPT = 16
PROMPT_SEQ = 256
N_SAMPLE = 2
SAMPLE_SEQ = 1024
PROMPT_TOKENS = N_PROMPT * PROMPT_SEQ
SAMPLE_TOKENS = N_SAMPLE * SAMPLE_SEQ
TOKENS = PROMPT_TOKENS + SAMPLE_TOKENS
PAST_LEN = 512
GRID_W = 64
EPS = 1e-6
NEG_INF = -1e30
ROPE_THETA = 10000.0
WINDOW = 128
NAT_ROWS = 8
NAT_COLS = 16
GMLP_WIDTH = 2 * D_MODEL
GMLP_GROUPS = 16
GMLP_GROUP_WIDTH = GMLP_WIDTH // GMLP_GROUPS
CHUNK = 128
CTX_COND_ROW = 2
COND_ROWS = 8
SM_SCALE = HEAD_DIM ** -0.5
LOG2E = math.log2(math.e)
QUERY_SCALE = SM_SCALE * LOG2E

VMEM_LIMIT = 56 * 1024 * 1024


def _params(n_axes, vmem=VMEM_LIMIT, flags=None):
    return pltpu.CompilerParams(dimension_semantics=("arbitrary",) * n_axes,
                                vmem_limit_bytes=vmem, flags=flags)


def _cond_row(tok0):
    return jnp.where(tok0 < PROMPT_TOKENS, CTX_COND_ROW, (tok0 - PROMPT_TOKENS) // SAMPLE_SEQ)


def _silu(x):
    return x * (0.5 + 0.5 * jnp.tanh(0.5 * x))


def _gelu_tanh(x):
    return 0.5 * x * (1.0 + jnp.tanh(math.sqrt(2.0 / math.pi) * (x + 0.044715 * (x * x * x))))


def _split_specs(tm, width, tile0=0):
    n_p = PROMPT_TOKENS // tm
    return [pl.BlockSpec((tm, width), lambda i: (jnp.minimum(i + tile0, n_p - 1), 0)),
            pl.BlockSpec((tm, width), lambda i: (jnp.maximum(i + tile0 - n_p, 0), 0))]


def _ada_kernel(cond_ref, w_ref, b_ref, o_ref):
    s = _silu(cond_ref[...]).astype(BF16)
    o_ref[...] = jnp.dot(s, w_ref[...].astype(BF16), preferred_element_type=F32) + b_ref[...]


def _ada(cond, w_ada, b_ada, tn=1024):
    depth = w_ada.shape[0]
    n = w_ada.shape[2]
    return pl.pallas_call(
        _ada_kernel,
        grid=(depth, n // tn),
        in_specs=[pl.BlockSpec((COND_ROWS, D_MODEL), lambda l, j: (0, 0)),
                  pl.BlockSpec((None, D_MODEL, tn), lambda l, j: (l, 0, j)),
                  pl.BlockSpec((None, 1, tn), lambda l, j: (l, 0, j))],
        out_specs=pl.BlockSpec((None, COND_ROWS, tn), lambda l, j: (l, 0, j)),
        out_shape=jax.ShapeDtypeStruct((depth, COND_ROWS, n), F32),
        compiler_params=_params(2),
        name="ada_mod",
    )(cond, w_ada, b_ada.reshape(depth, 1, n))


def _norm_kernel(xp_ref, xs_ref, g_ref, shift_ref, scale_ref, o_ref, *, tm):
    x = jnp.where(pl.program_id(0) < PROMPT_TOKENS // tm, xp_ref[...], xs_ref[...])
    y = x * lax.rsqrt(jnp.mean(x * x, axis=-1, keepdims=True) + EPS) * g_ref[...]
    o_ref[...] = (y * (1.0 + scale_ref[...]) + shift_ref[...]).astype(BF16)


def _norm_mod(xp, xs, norm_g3, mod4, layer, tm=512):
    return pl.pallas_call(
        functools.partial(_norm_kernel, tm=tm),
        grid=(TOKENS // tm,),
        in_specs=[*_split_specs(tm, D_MODEL),
                  pl.BlockSpec((None, 1, D_MODEL), lambda i: (layer, 0, 0)),
                  pl.BlockSpec((None, None, 1, D_MODEL), lambda i: (layer, _cond_row(i * tm), 0, 0)),
                  pl.BlockSpec((None, None, 1, D_MODEL), lambda i: (layer, _cond_row(i * tm), 0, 1))],
        out_specs=pl.BlockSpec((tm, D_MODEL), lambda i: (i, 0)),
        out_shape=jax.ShapeDtypeStruct((TOKENS, D_MODEL), BF16),
        compiler_params=_params(1),
        name="norm_mod",
    )(xp, xs, norm_g3, mod4, mod4)


PROJ_TM = 2048
PROJ_TN = 512
PROJ_TN_WIDE = 1024
PROJ_ROWS = 512
N_PROMPT_TILES = PROMPT_TOKENS // PROJ_TM


class _Segment(NamedTuple):
    col_blk0: int
    n_blk: int
    epilogue: Callable
    by_tile_kind: bool
    outs: tuple


def _segment_steps(segments, wide):
    spans, j0 = [], 0
    for seg in segments:
        assert seg.col_blk0 % wide == 0 and seg.n_blk % wide == 0
        spans.append((j0, seg.n_blk // wide))
        j0 += seg.n_blk // wide
    return spans


def _proj_kernel(*refs, segments, spans, n_extra, n_out):
    a_ref, w_ref = refs[0], refs[1]
    extras = refs[2:2 + n_extra]
    outs = refs[2 + n_extra:2 + n_extra + n_out]
    wbf_ref = refs[2 + n_extra + n_out]

    @pl.when(pl.program_id(1) == 0)
    def _():
        wbf_ref[...] = w_ref[...].astype(BF16)

    def run(seg):
        seg_outs = [outs[k] for k in seg.outs]

        def body(latent):
            for rc in range(PROJ_TM // PROJ_ROWS):
                rows = slice(rc * PROJ_ROWS, (rc + 1) * PROJ_ROWS)
                for col0 in range(0, wbf_ref.shape[1], PROJ_TN):
                    acc = jnp.dot(a_ref[rows, :], wbf_ref[:, col0:col0 + PROJ_TN],
                                  preferred_element_type=F32)
                    seg.epilogue(acc, extras, seg_outs, rc, col0, latent)

        if seg.by_tile_kind:
            is_latent = pl.program_id(1) >= N_PROMPT_TILES
            pl.when(is_latent)(lambda: body(True))
            pl.when(jnp.logical_not(is_latent))(lambda: body(False))
        else:
            body(None)

    j = pl.program_id(0)
    for seg, (j0, nj) in zip(segments, spans):
        if len(segments) == 1:
            run(seg)
        else:
            pl.when(jnp.logical_and(j >= j0, j < j0 + nj))(functools.partial(run, seg))


def _proj(a, w, layer, segments, out_shapes, out_specs, name, extras=(), extra_specs=(), tn=PROJ_TN):
    m, k = a.shape
    wide = tn // PROJ_TN
    spans = _segment_steps(segments, wide)

    def w_col(j):
        col = None
        for seg, (j0, _) in zip(segments, spans):
            c = seg.col_blk0 // wide + j - j0
            col = c if col is None else jnp.where(j >= j0, c, col)
        return col

    kern = functools.partial(_proj_kernel, segments=segments, spans=spans, n_extra=len(extras),
                             n_out=len(out_shapes))
    return pl.pallas_call(
        kern,
        grid=(sum(nj for _, nj in spans), m // PROJ_TM),
        in_specs=[pl.BlockSpec((PROJ_TM, k), lambda j, i: (i, 0)),
                  pl.BlockSpec((None, k, tn), lambda j, i: (layer, 0, w_col(j))),
                  *extra_specs],
        out_specs=out_specs,
        out_shape=out_shapes,
        scratch_shapes=[pltpu.VMEM((k, tn), BF16)],
        compiler_params=_params(2),
        name=name,
    )(a, w, *extras)


def _full_out(n_blk):
    return ([jax.ShapeDtypeStruct((TOKENS, n_blk * PROJ_TN), BF16)],
            [pl.BlockSpec((PROJ_TM, PROJ_TN_WIDE), lambda j, i: (i, j))])


def _held(span, j, inside, before, after):
    j0, nj = span
    pick = lambda a, b, c: jnp.where(j < j0, b, jnp.where(j >= j0 + nj, c, a))
    return tuple(pick(a, b, c) for a, b, c in zip(inside, before, after))


def _split_out(span, n_blk, head_rows):
    prompt_tile = lambda i: jnp.minimum(i, N_PROMPT_TILES - 1)
    last_prompt, last_tile = N_PROMPT_TILES - 1, TOKENS // PROJ_TM - 1
    jj = lambda j: j - span[0]
    if head_rows:
        assert n_blk == 1
        heads = PROJ_TN // HEAD_DIM
        cache_shape = jax.ShapeDtypeStruct((PROMPT_TOKENS * heads, HEAD_DIM), F32)
        cache_spec = pl.BlockSpec(
            (PROJ_TM * heads, HEAD_DIM),
            lambda j, i: _held(span, j, (prompt_tile(i), 0), (0, 0), (last_prompt, 0)))
    else:
        cache_shape = jax.ShapeDtypeStruct((N_PROMPT, PROMPT_SEQ, n_blk * PROJ_TN), F32)
        cache_spec = pl.BlockSpec(
            (PROJ_TM // PROMPT_SEQ, PROMPT_SEQ, PROJ_TN),
            lambda j, i: _held(span, j, (prompt_tile(i), 0, jj(j)), (0, 0, 0), (last_prompt, 0, n_blk - 1)))
    shapes = [cache_shape, jax.ShapeDtypeStruct((TOKENS, n_blk * PROJ_TN), BF16)]
    specs = [cache_spec,
             pl.BlockSpec((PROJ_TM, PROJ_TN),
                          lambda j, i: _held(span, j, (i, jj(j)), (0, 0), (last_tile, n_blk - 1)))]
    return shapes, specs


def _chunk_rows(rc):
    return slice(rc * PROJ_ROWS, (rc + 1) * PROJ_ROWS)


def _epi_silu(acc, extras, outs, rc, col0, latent):
    outs[0][_chunk_rows(rc), col0:col0 + PROJ_TN] = _silu(acc).astype(BF16)


def _epi_gelu(acc, extras, outs, rc, col0, latent):
    outs[0][_chunk_rows(rc), col0:col0 + PROJ_TN] = _gelu_tanh(acc).astype(BF16)


def _store_split(y, outs, rc, sl, latent):
    outs[1][_chunk_rows(rc), sl] = y.astype(BF16)
    if latent:
        return
    if len(outs[0].shape) == 2:
        heads = PROJ_TN // HEAD_DIM
        col0 = sl.start or 0
        for c in range(y.shape[1] // HEAD_DIM):
            head = col0 // HEAD_DIM + c
            rows = pl.ds(rc * PROJ_ROWS * heads + head, PROJ_ROWS, stride=heads)
            outs[0][rows, :] = y[:, c * HEAD_DIM:(c + 1) * HEAD_DIM]
    else:
        seqs = PROJ_ROWS // PROMPT_SEQ
        outs[0][rc * seqs:(rc + 1) * seqs, :, sl] = y.reshape(seqs, PROMPT_SEQ, y.shape[1])


def _epi_value(acc, extras, outs, rc, col0, latent):
    _store_split(acc, outs, rc, slice(col0, col0 + PROJ_TN), latent)


def _head_rmsnorm(acc, hh, g):
    a = acc[:, hh * HEAD_DIM:(hh + 1) * HEAD_DIM]
    return a * lax.rsqrt(jnp.mean(a * a, axis=-1, keepdims=True) + EPS) * g


def _normed_heads(acc, extras, rc, rope):
    g = extras[0][...]
    ys = [_head_rmsnorm(acc, hh, g) for hh in range(PROJ_TN // HEAD_DIM)]
    if rope:
        rows = _chunk_rows(rc)
        cos, sin, swap = extras[1][rows, :], extras[2][rows, :], extras[3][...]
        for pair in range(len(ys) // 2):
            both = jnp.concatenate(ys[2 * pair:2 * pair + 2], axis=1).astype(BF16)
            partner = jnp.dot(both, swap, preferred_element_type=F32)
            for t in range(2):
                hh = 2 * pair + t
                ys[hh] = ys[hh] * cos + partner[:, t * HEAD_DIM:(t + 1) * HEAD_DIM] * sin
    for hh, y in enumerate(ys):
        yield slice(hh * HEAD_DIM, (hh + 1) * HEAD_DIM), y


def _epi_query(acc, extras, outs, rc, col0, latent, *, rope):
    for sl, y in _normed_heads(acc, extras, rc, rope and latent):
        outs[0][_chunk_rows(rc), col0 + sl.start:col0 + sl.stop] = (y * QUERY_SCALE).astype(BF16)


def _epi_key(acc, extras, outs, rc, col0, latent, *, rope):
    for sl, y in _normed_heads(acc, extras, rc, rope and latent):
        _store_split(y, outs, rc, slice(col0 + sl.start, col0 + sl.stop), latent)


def _rope_tables():
    nf = HEAD_DIM // 4
    t = np.arange(SAMPLE_TOKENS) % SAMPLE_SEQ
    row = (t // GRID_W).astype(np.float32)
    col = (t % GRID_W).astype(np.float32)
    inv = np.float32(ROPE_THETA) ** (-np.arange(nf, dtype=np.float32) / np.float32(nf))
    ang_r = row[:, None] * inv
    ang_c = col[:, None] * inv
    cos = np.concatenate([np.cos(ang_r), np.cos(ang_r), np.cos(ang_c), np.cos(ang_c)], axis=1)
    sin = np.concatenate([-np.sin(ang_r), np.sin(ang_r), -np.sin(ang_c), np.sin(ang_c)], axis=1)
    lanes = np.arange(2 * HEAD_DIM)
    swap = lanes[:, None] == (lanes[None, :] ^ (HEAD_DIM // 4))
    return jnp.asarray(cos, F32), jnp.asarray(sin, F32), jnp.asarray(swap, BF16)


def _gain_extras(gain, rope_tabs):
    extras = [gain.reshape(1, HEAD_DIM)]
    specs = [pl.BlockSpec((1, HEAD_DIM), lambda j, i: (0, 0))]
    if rope_tabs is not None:
        tab_map = lambda j, i: (jnp.maximum(i - N_PROMPT_TILES, 0), 0)
        extras += list(rope_tabs)
        specs += [pl.BlockSpec((PROJ_TM, HEAD_DIM), tab_map)] * 2
        specs.append(pl.BlockSpec((2 * HEAD_DIM, 2 * HEAD_DIM), lambda j, i: (0, 0)))
    return extras, specs


def _qz_proj(hn, w, layer, q_blk0, z_blk0, n_blk, gain, rope_tabs, name):
    extras, specs = _gain_extras(gain, rope_tabs)
    rope = rope_tabs is not None
    segments = (_Segment(q_blk0, n_blk, functools.partial(_epi_query, rope=rope), rope, (0,)),
                _Segment(z_blk0, n_blk, _epi_silu, False, (0,)))
    return _proj(hn, w, layer, segments, *_full_out(2 * n_blk), name, extras, specs, tn=PROJ_TN_WIDE)[0]


def _kv_proj(hn, w, layer, k_blk0, n_blk, gain, rope_tabs, head_rows, name):
    extras, specs = _gain_extras(gain, rope_tabs)
    segments = (_Segment(k_blk0, n_blk, functools.partial(_epi_key, rope=rope_tabs is not None), True, (0, 1)),
                _Segment(k_blk0 + n_blk, n_blk, _epi_value, True, (2, 3)))
    shapes, out_specs = [], []
    for span in _segment_steps(segments, 1):
        seg_shapes, seg_specs = _split_out(span, n_blk, head_rows)
        shapes += seg_shapes
        out_specs += seg_specs
    return _proj(hn, w, layer, segments, shapes, out_specs, name, extras, specs)


def _uvz_proj(hn, w, layer, n_blk, name):
    segments = (_Segment(0, 2 * n_blk, _epi_gelu, False, (0,)),
                _Segment(2 * n_blk, n_blk, _epi_silu, False, (0,)))
    return _proj(hn, w, layer, segments, *_full_out(3 * n_blk), name, tn=PROJ_TN_WIDE)[0]


OUT_COLS = 512
W_STAGE_ROWS = 256


def _load_weight_bf16(w_hbm, layer, wbf_ref, stage_ref, sem):
    k = wbf_ref.shape[0]
    n_chunks = k // W_STAGE_ROWS

    def copy(c, slot):
        return pltpu.make_async_copy(w_hbm.at[layer, pl.ds(c * W_STAGE_ROWS, W_STAGE_ROWS), :],
                                     stage_ref.at[slot], sem.at[slot])

    copy(0, 0).start()
    for c in range(n_chunks):
        slot = c % 2
        if c + 1 < n_chunks:
            copy(c + 1, 1 - slot).start()
        copy(c, slot).wait()
        wbf_ref[c * W_STAGE_ROWS:(c + 1) * W_STAGE_ROWS, :] = stage_ref[slot].astype(BF16)


def _out_kernel(*refs, layer_in_kind, tile0, tm, split_a, split_x, fuse_norm):
    it = iter(refs)
    a_refs = [next(it), next(it)] if split_a else [next(it)]
    w_hbm = next(it)
    x_refs = [next(it), next(it)] if split_x else [next(it)]
    gate_ref = next(it)
    if fuse_norm:
        g_ref, shift_ref, scale_ref = next(it), next(it), next(it)
    xnew_ref = next(it)
    hn_ref = next(it) if fuse_norm else None
    wbf_ref, stage_ref, sem = next(it), next(it), next(it)

    i = pl.program_id(0)

    @pl.when(i == 0)
    def _():
        _load_weight_bf16(w_hbm, layer_in_kind, wbf_ref, stage_ref, sem)

    is_prompt = i + tile0 < PROMPT_TOKENS // tm
    a = jnp.where(is_prompt, a_refs[0][...], a_refs[1][...]) if split_a else a_refs[0][...]
    ssq = jnp.zeros((tm, 1), F32)
    for cb in range(D_MODEL // OUT_COLS):
        sl = slice(cb * OUT_COLS, (cb + 1) * OUT_COLS)
        acc = jnp.dot(a, wbf_ref[:, sl], preferred_element_type=F32)
        if split_x:
            x = jnp.where(is_prompt, x_refs[0][:, sl], x_refs[1][:, sl])
        else:
            x = x_refs[0][:, sl]
        xn = x + gate_ref[:, sl] * acc
        xnew_ref[:, sl] = xn
        if fuse_norm:
            ssq = ssq + jnp.sum(xn * xn, axis=-1, keepdims=True)
    if fuse_norm:
        rs = lax.rsqrt(ssq * (1.0 / D_MODEL) + EPS)
        for cb in range(D_MODEL // OUT_COLS):
            sl = slice(cb * OUT_COLS, (cb + 1) * OUT_COLS)
            gmul = g_ref[:, sl] * (1.0 + scale_ref[:, sl])
            hn_ref[:, sl] = (xnew_ref[:, sl] * rs * gmul + shift_ref[:, sl]).astype(BF16)


def _out_proj(a, w, layer_in_kind, x, mod4, layer, norm_g3, name, tm, rows=None):
    split_a = isinstance(a, tuple)
    k = a[0].shape[1] if split_a else a.shape[1]
    tok0, n_tok = rows if rows is not None else (0, TOKENS)
    tile0 = tok0 // tm
    split_x = isinstance(x, tuple)
    fuse_norm = norm_g3 is not None
    mod_spec = lambda part, lyr: pl.BlockSpec(
        (None, None, 1, D_MODEL), lambda i: (lyr, _cond_row((i + tile0) * tm), 0, part))
    row_spec = lambda width: pl.BlockSpec((tm, width), lambda i: (i + tile0, 0))
    operands = [*a, w] if split_a else [a, w]
    in_specs = [*(_split_specs(tm, k, tile0) if split_a else [row_spec(k)]),
                pl.BlockSpec(memory_space=pl.ANY)]
    if split_x:
        assert rows is None
        operands += list(x)
        in_specs += _split_specs(tm, D_MODEL)
    else:
        operands.append(x)
        in_specs.append(row_spec(D_MODEL))
    operands.append(mod4)
    in_specs.append(mod_spec(2, layer))
    out_shapes = [jax.ShapeDtypeStruct((n_tok, D_MODEL), F32)]
    out_specs = [pl.BlockSpec((tm, D_MODEL), lambda i: (i, 0))]
    if fuse_norm:
        operands += [norm_g3, mod4, mod4]
        in_specs += [pl.BlockSpec((None, 1, D_MODEL), lambda i: (layer + 1, 0, 0)),
                     mod_spec(0, layer + 1), mod_spec(1, layer + 1)]
        out_shapes.append(jax.ShapeDtypeStruct((n_tok, D_MODEL), BF16))
        out_specs.append(pl.BlockSpec((tm, D_MODEL), lambda i: (i, 0)))
    kern = functools.partial(_out_kernel, layer_in_kind=layer_in_kind, tile0=tile0, tm=tm,
                             split_a=split_a, split_x=split_x, fuse_norm=fuse_norm)
    return pl.pallas_call(
        kern,
        grid=(n_tok // tm,),
        in_specs=in_specs,
        out_specs=out_specs,
        out_shape=out_shapes,
        scratch_shapes=[pltpu.VMEM((k, D_MODEL), BF16),
                        pltpu.VMEM((2, W_STAGE_ROWS, D_MODEL), F32),
                        pltpu.SemaphoreType.DMA((2,))],
        compiler_params=_params(1),
        name=name,
    )(*operands)


def _dot_nt(a, b):
    return lax.dot_general(a, b, (((1,), (1,)), ((), ())), preferred_element_type=F32)


def _head(ref, h, rows=slice(None)):
    return ref[rows, h * HEAD_DIM:(h + 1) * HEAD_DIM]


def _with_ones(v):
    return jnp.concatenate([v, jnp.ones(v.shape, v.dtype)], axis=1)


def _lane_chunks(s):
    return [s[:, c * HEAD_DIM:(c + 1) * HEAD_DIM] for c in range(s.shape[1] // HEAD_DIM)]


def _softmax_pv(score_blocks, value_blocks, sink):
    rows = score_blocks[0].shape[0]
    mx = functools.reduce(jnp.maximum, [c for s in score_blocks for c in _lane_chunks(s)])
    m = jnp.broadcast_to(jnp.max(mx, axis=-1, keepdims=True), (rows, HEAD_DIM))
    if sink is not None:
        m = jnp.maximum(m, sink)
    o = None
    for s, v in zip(score_blocks, value_blocks):
        p = jnp.concatenate([jnp.exp2(c - m) for c in _lane_chunks(s)], axis=1).astype(BF16)
        part = jnp.dot(p, v, preferred_element_type=F32)
        o = part if o is None else o + part
    den = o[:, HEAD_DIM:]
    if sink is not None:
        den = den + jnp.exp2(sink - m)
    return o[:, :HEAD_DIM] * (1.0 / den)


def _attn_prompt_kernel(sink_ref, q_ref, k_ref, v_ref, z_ref, o_ref, *, n_heads, n_kv, use_sink):
    grp = n_heads // n_kv
    for g in range(n_kv):
        kg = _head(k_ref, g)
        vg = _with_ones(_head(v_ref, g))
        for h in range(g * grp, (g + 1) * grp):
            s = _dot_nt(_head(q_ref, h), kg)
            sink = jnp.full((PROMPT_SEQ, HEAD_DIM), sink_ref[h], F32) if use_sink else None
            o = _softmax_pv([s], [vg], sink)
            o_ref[:, h * HEAD_DIM:(h + 1) * HEAD_DIM] = (o * _head(z_ref, h).astype(F32)).astype(BF16)


def _attn_prompt(sink, qz, k, v, n_kv, use_sink, name):
    width = qz.shape[1] // 2
    kern = functools.partial(_attn_prompt_kernel, n_heads=width // HEAD_DIM, n_kv=n_kv, use_sink=use_sink)
    cache_spec = pl.BlockSpec((PROMPT_SEQ, k.shape[1]), lambda b: (b, 0))
    return pl.pallas_call(
        kern,
        grid=(N_PROMPT,),
        in_specs=[pl.BlockSpec(memory_space=pltpu.SMEM),
                  pl.BlockSpec((PROMPT_SEQ, width), lambda b: (b, 0)),
                  cache_spec, cache_spec,
                  pl.BlockSpec((PROMPT_SEQ, width), lambda b: (b, 1))],
        out_specs=pl.BlockSpec((PROMPT_SEQ, width), lambda b: (b, 0)),
        out_shape=jax.ShapeDtypeStruct((PROMPT_TOKENS, width), BF16),
        compiler_params=_params(1),
        name=name,
    )(sink, qz, k, v, qz)


WIN_BAND = 3 * WINDOW


def _attn_win_kernel(sink_ref, q_ref, k_ref, v_ref, kc_ref, vc_ref, z_ref, o_ref, *, n_heads, n_kv):
    grp = n_heads // n_kv
    n = pl.program_id(1)
    start = pl.multiple_of(jnp.clip((n - 1) * WINDOW, 0, SAMPLE_SEQ - WIN_BAND), WINDOW)
    shape = (grp * WINDOW, WIN_BAND)
    qpos = n * WINDOW + (lax.broadcasted_iota(jnp.int32, shape, 0) & (WINDOW - 1))
    kpos = start + lax.broadcasted_iota(jnp.int32, shape, 1)
    valid = jnp.abs(kpos - qpos) <= WINDOW
    band = pl.ds(start, WIN_BAND)
    for g in range(n_kv):
        heads = [g * grp + t for t in range(grp)]
        qs = jnp.concatenate([_head(q_ref, h) for h in heads], axis=0)
        kb = _head(k_ref, g, band)
        vb = _with_ones(_head(v_ref, g, band))
        ctx_rows = pl.ds(g, PAST_LEN, stride=n_kv)
        kc = kc_ref[ctx_rows, :].astype(BF16)
        vc = _with_ones(vc_ref[ctx_rows, :].astype(BF16))
        s_band = jnp.where(valid, _dot_nt(qs, kb), NEG_INF)
        s_ctx = _dot_nt(qs, kc)
        sink = jnp.concatenate([jnp.full((WINDOW, HEAD_DIM), sink_ref[h], F32) for h in heads], axis=0)
        o = _softmax_pv([s_band, s_ctx], [vb, vc], sink)
        for t, h in enumerate(heads):
            oh = o[t * WINDOW:(t + 1) * WINDOW]
            o_ref[:, h * HEAD_DIM:(h + 1) * HEAD_DIM] = (oh * _head(z_ref, h).astype(F32)).astype(BF16)


def _attn_win(sink, qz, k, v, cache_k, cache_v, layer_in_kind, n_kv):
    width = qz.shape[1] // 2
    kv_width = k.shape[1]
    blocks_per_seq = SAMPLE_SEQ // WINDOW
    q_map = lambda b, n: (PROMPT_TOKENS // WINDOW + b * blocks_per_seq + n, 0)
    z_map = lambda b, n: (PROMPT_TOKENS // WINDOW + b * blocks_per_seq + n, 1)
    kv_map = lambda b, n: (PROMPT_TOKENS // SAMPLE_SEQ + b, 0)
    cache_map = lambda b, n: (b, layer_in_kind, 0, 0)
    kern = functools.partial(_attn_win_kernel, n_heads=width // HEAD_DIM, n_kv=n_kv)
    return pl.pallas_call(
        kern,
        grid=(N_SAMPLE, blocks_per_seq),
        in_specs=[pl.BlockSpec(memory_space=pltpu.SMEM),
                  pl.BlockSpec((WINDOW, width), q_map),
                  pl.BlockSpec((SAMPLE_SEQ, kv_width), kv_map),
                  pl.BlockSpec((SAMPLE_SEQ, kv_width), kv_map),
                  pl.BlockSpec((None, None, PAST_LEN * n_kv, HEAD_DIM), cache_map),
                  pl.BlockSpec((None, None, PAST_LEN * n_kv, HEAD_DIM), cache_map),
                  pl.BlockSpec((WINDOW, width), z_map)],
        out_specs=pl.BlockSpec((WINDOW, width), lambda b, n: (b * blocks_per_seq + n, 0)),
        out_shape=jax.ShapeDtypeStruct((SAMPLE_TOKENS, width), BF16),
        compiler_params=_params(2),
        name="attn_win_latent",
    )(sink, qz, k, v, cache_k, cache_v, qz)


NAT_QROWS = 4
GRID_ROWS = SAMPLE_SEQ // GRID_W
N_DR = 2 * NAT_ROWS - 1
MASKED_TILE = N_DR


def _nat_row_start(qr):
    return min(max(qr - NAT_ROWS // 2, 0), GRID_ROWS - NAT_ROWS)


def _attn_nat_kernel(bias_ref, q_ref, k_ref, v_ref, kc_ref, vc_ref, z_ref, o_ref, left_ref, right_ref):
    def build_bias_tiles():
        shape = (GRID_W, 2 * GRID_W)
        lane = lax.broadcasted_iota(jnp.int32, shape, 1)
        qc = lax.broadcasted_iota(jnp.int32, shape, 0)
        kc_ = lane & (GRID_W - 1)
        cs = jnp.clip(qc - NAT_COLS // 2, 0, GRID_W - NAT_COLS)
        col_ok = jnp.logical_and(kc_ >= cs, kc_ < cs + NAT_COLS)
        is_left = lane < GRID_W
        lanes = 2 * GRID_W
        for dri in range(N_DR):
            row = jnp.broadcast_to(bias_ref[dri:dri + 1, :], shape)
            on_left = pltpu.roll(row, lanes - (NAT_COLS - 1), 1, stride=1, stride_axis=0)
            on_right = pltpu.roll(row, GRID_W - (NAT_COLS - 1), 1, stride=1, stride_axis=0)
            left_ref[dri] = jnp.where(is_left, jnp.where(col_ok, on_left, NEG_INF), 0.0)
            right_ref[dri] = jnp.where(is_left, 0.0, jnp.where(col_ok, on_right, NEG_INF))
        left_ref[MASKED_TILE] = jnp.where(is_left, NEG_INF, 0.0)
        right_ref[MASKED_TILE] = jnp.where(is_left, 0.0, NEG_INF)

    build_bias_tiles()
    n_heads = kc_ref.shape[1] // PAST_LEN
    ctx_rows = pl.ds(pl.program_id(0), PAST_LEN, stride=n_heads)
    for b, qb in [(b, qb) for b in range(N_SAMPLE) for qb in range(GRID_ROWS // NAT_QROWS)]:
        if qb == 0:
            kc = kc_ref[b, ctx_rows, :].astype(BF16)
            vc = _with_ones(vc_ref[b, ctx_rows, :].astype(BF16))
        qrows = range(qb * NAT_QROWS, (qb + 1) * NAT_QROWS)
        krow0 = _nat_row_start(qrows[0]) // 2 * 2
        krow1 = -(-(_nat_row_start(qrows[-1]) + NAT_ROWS) // 2) * 2
        tok0 = b * SAMPLE_SEQ
        kwin = slice(tok0 + krow0 * GRID_W, tok0 + krow1 * GRID_W)
        qwin = slice(tok0 + qrows[0] * GRID_W, tok0 + (qrows[-1] + 1) * GRID_W)

        def tile_index(qr, kr):
            rs = _nat_row_start(qr)
            return kr - qr + (NAT_ROWS - 1) if rs <= kr < rs + NAT_ROWS else MASKED_TILE

        bias = jnp.concatenate(
            [jnp.concatenate([left_ref[tile_index(qr, kr)] + right_ref[tile_index(qr, kr + 1)]
                              for kr in range(krow0, krow1, 2)], axis=1)
             for qr in qrows], axis=0)

        q = q_ref[qwin, :]
        s_nb = _dot_nt(q, k_ref[kwin, :]) + bias
        s_ctx = _dot_nt(q, kc)
        o = _softmax_pv([s_nb, s_ctx], [_with_ones(v_ref[kwin, :]), vc], None)
        o_ref[qwin, :] = (o * z_ref[qwin, :].astype(F32)).astype(BF16)


def _attn_nat(rel_bias, qz, k, v, cache_k, cache_v, layer_in_kind):
    n_heads = qz.shape[1] // (2 * HEAD_DIM)
    bias_rows = jnp.pad(rel_bias, ((0, 0), (0, N_DR + 1 - rel_bias.shape[1]),
                                   (0, 2 * GRID_W - rel_bias.shape[2])))
    latent_tile = PROMPT_TOKENS // SAMPLE_TOKENS
    latent_spec = pl.BlockSpec((SAMPLE_TOKENS, HEAD_DIM), lambda h: (latent_tile, h))
    z_spec = pl.BlockSpec((SAMPLE_TOKENS, HEAD_DIM), lambda h: (latent_tile, n_heads + h))
    cache_spec = pl.BlockSpec((N_SAMPLE, None, PAST_LEN * n_heads, HEAD_DIM),
                              lambda h: (0, layer_in_kind, 0, 0))
    return pl.pallas_call(
        _attn_nat_kernel,
        grid=(n_heads,),
        in_specs=[pl.BlockSpec((None, N_DR + 1, 2 * GRID_W), lambda h: (h, 0, 0)),
                  latent_spec, latent_spec, latent_spec, cache_spec, cache_spec, z_spec],
        out_specs=pl.BlockSpec((SAMPLE_TOKENS, HEAD_DIM), lambda h: (0, h)),
        out_shape=jax.ShapeDtypeStruct((SAMPLE_TOKENS, n_heads * HEAD_DIM), BF16),
        scratch_shapes=[pltpu.VMEM((N_DR + 1, GRID_W, 2 * GRID_W), F32),
                        pltpu.VMEM((N_DR + 1, GRID_W, 2 * GRID_W), F32)],
        compiler_params=_params(1),
        name="attn_nat_latent",
    )(bias_rows, qz, k, v, cache_k, cache_v, qz)


SPATIAL_TOKENS = 2 * CHUNK


def _spatial_kernel(u_ref, v_ref, z_ref, g_ref, b_ref, ws_ref, bs_ref, o_ref):
    for c in range(SPATIAL_TOKENS // CHUNK):
        rows = slice(c * CHUNK, (c + 1) * CHUNK)
        v = v_ref[rows, :].astype(F32)
        mu = jnp.mean(v, axis=-1, keepdims=True)
        vc = v - mu
        var = jnp.mean(vc * vc, axis=-1, keepdims=True)
        vn = (vc * lax.rsqrt(var + EPS) * g_ref[...] + b_ref[...]).astype(BF16)
        for g in range(GMLP_GROUPS):
            sl = slice(g * GMLP_GROUP_WIDTH, (g + 1) * GMLP_GROUP_WIDTH)
            sv = jnp.dot(ws_ref[g], vn[:, sl], preferred_element_type=F32) + bs_ref[:, g:g + 1]
            o_ref[rows, sl] = u_ref[rows, sl] * sv.astype(BF16) * z_ref[rows, sl]


def _spatial(uvz, ln_g, ln_b, w_s, b_s):
    part = lambda p: pl.BlockSpec((SPATIAL_TOKENS, GMLP_WIDTH), lambda i: (i, p))
    row = part(0)
    vec = pl.BlockSpec((1, GMLP_WIDTH), lambda i: (0, 0))
    return pl.pallas_call(
        _spatial_kernel,
        grid=(TOKENS // SPATIAL_TOKENS,),
        in_specs=[part(0), part(1), part(2), vec, vec,
                  pl.BlockSpec((GMLP_GROUPS, CHUNK, CHUNK), lambda i: (0, 0, 0)),
                  pl.BlockSpec((CHUNK, GMLP_GROUPS), lambda i: (0, 0))],
        out_specs=row,
        out_shape=jax.ShapeDtypeStruct((TOKENS, GMLP_WIDTH), BF16),
        compiler_params=_params(1),
        name="gmlp_spatial",
    )(uvz, uvz, uvz, ln_g.reshape(1, -1), ln_b.reshape(1, -1), w_s.astype(BF16), b_s.T)


def kernel(x_prompt, x_sample, cache_win_k, cache_win_v, cache_nat_k, cache_nat_v, c, c_ctx,
           norm_g, w_ada, b_ada,
           win_w_in, win_q_norm, win_k_norm, win_sink, win_w_out,
           nat_w_in, nat_q_norm, nat_k_norm, nat_rel_bias, nat_w_out,
           gmlp_w_in, gmlp_ln_g, gmlp_ln_b, gmlp_w_s, gmlp_b_s, gmlp_w_out):
    depth = norm_g.shape[0]
    xp = x_prompt.reshape(PROMPT_TOKENS, D_MODEL)
    xs = x_sample.reshape(SAMPLE_TOKENS, D_MODEL)
    cond = jnp.zeros((COND_ROWS, D_MODEL), F32).at[:N_SAMPLE].set(c).at[CTX_COND_ROW].set(c_ctx)
    mod4 = _ada(cond, w_ada, b_ada).reshape(depth, COND_ROWS, 1, 3 * D_MODEL)
    norm_g3 = norm_g.reshape(depth, 1, D_MODEL)
    rope_tabs = _rope_tables()
    no_sink = jnp.zeros((1,), F32)

    n_win = win_w_in.shape[0]
    n_nat = nat_w_in.shape[0]
    win_kv_heads = cache_win_k.shape[3]
    win_kv_width = win_kv_heads * HEAD_DIM
    cwk = cache_win_k.reshape(N_SAMPLE, n_win, PAST_LEN * win_kv_heads, HEAD_DIM)
    cwv = cache_win_v.reshape(N_SAMPLE, n_win, PAST_LEN * win_kv_heads, HEAD_DIM)
    nat_heads = cache_nat_k.shape[3]
    cnk = cache_nat_k.reshape(N_SAMPLE, n_nat, PAST_LEN * nat_heads, HEAD_DIM)
    cnv = cache_nat_v.reshape(N_SAMPLE, n_nat, PAST_LEN * nat_heads, HEAD_DIM)

    new_win_k, new_win_v, new_nat_k, new_nat_v = [], [], [], []
    x = (xp, xs)
    hn = _norm_mod(xp, xs, norm_g3, mod4, 0)
    for layer in range(depth):
        kind = layer % 3
        li = layer // 3
        if kind == 0:
            nq = D_MODEL // PROJ_TN
            nkv = win_kv_width // PROJ_TN
            qz = _qz_proj(hn, win_w_in, li, 0, nq + 2 * nkv, nq, win_q_norm[li], rope_tabs, "win_qz")
            kp, ks, vp, vs = _kv_proj(hn, win_w_in, li, nq, nkv, win_k_norm[li], rope_tabs, True, "win_kv")
            new_win_k.append(kp)
            new_win_v.append(vp)
            sink = win_sink[li] * LOG2E
            a = (_attn_prompt(sink, qz, ks, vs, win_kv_heads, True, "attn_win_prompt"),
                 _attn_win(sink, qz, ks, vs, cwk, cwv, li, win_kv_heads))
            w_out, out_tm, out_name = win_w_out, 512, "win_out"
        elif kind == 1:
            nq = D_MODEL // PROJ_TN
            qz = _qz_proj(hn, nat_w_in, li, 0, 3 * nq, nq, nat_q_norm[li], None, "nat_qz")
            kp, ks, vp, vs = _kv_proj(hn, nat_w_in, li, nq, nq, nat_k_norm[li], None, False, "nat_kv")
            new_nat_k.append(kp)
            new_nat_v.append(vp)
            a = (_attn_prompt(no_sink, qz, ks, vs, nat_heads, False, "attn_nat_prompt"),
                 _attn_nat(nat_rel_bias[li] * LOG2E, qz, ks, vs, cnk, cnv, li))
            w_out, out_tm, out_name = nat_w_out, 512, "nat_out"
        else:
            nw = GMLP_WIDTH // PROJ_TN
            uvz = _uvz_proj(hn, gmlp_w_in, li, nw, "gmlp_uvz")
            a = _spatial(uvz, gmlp_ln_g[li], gmlp_ln_b[li], gmlp_w_s[li], gmlp_b_s[li])
            w_out, out_tm, out_name = gmlp_w_out, 256, "gmlp_out"
        if layer + 1 < depth:
            x, hn = _out_proj(a, w_out, li, x, mod4, layer, norm_g3, out_name, out_tm)
        else:
            (yp,) = _out_proj(a, w_out, li, x, mod4, layer, None, out_name + "_prompt", out_tm,
                              rows=(0, PROMPT_TOKENS))
            (ys,) = _out_proj(a, w_out, li, x, mod4, layer, None, out_name + "_latent", out_tm,
                              rows=(PROMPT_TOKENS, SAMPLE_TOKENS))

    cache_shape = lambda layers, heads: jnp.stack(
        [c_.reshape(N_PROMPT, PROMPT_SEQ, heads, HEAD_DIM) for c_ in layers], axis=1)
    return (yp.reshape(N_PROMPT, PROMPT_SEQ, D_MODEL), ys.reshape(N_SAMPLE, SAMPLE_SEQ, D_MODEL),
            cache_shape(new_win_k, win_kv_heads), cache_shape(new_win_v, win_kv_heads),
            cache_shape(new_nat_k, nat_heads), cache_shape(new_nat_v, nat_heads))
```

```python
import functools
import math
from typing import Callable, NamedTuple

import jax
import jax.numpy as jnp
import numpy as np
from jax import lax
from jax.experimental import pallas as pl
from jax.experimental.pallas import tpu as pltpu

F32 = jnp.float32
BF16 = jnp.bfloat16

D_MODEL = 2048
HEAD_DIM = 128
N_PROMPT = 16
PROMPT_SEQ = 256
N_SAMPLE = 2
SAMPLE_SEQ = 1024
PROMPT_TOKENS = N_PROMPT * PROMPT_SEQ
SAMPLE_TOKENS = N_SAMPLE * SAMPLE_SEQ
TOKENS = PROMPT_TOKENS + SAMPLE_TOKENS
PAST_LEN = 512
GRID_W = 64
EPS = 1e-6
NEG_INF = -1e30
ROPE_THETA = 10000.0
WINDOW = 128
NAT_ROWS = 8
NAT_COLS = 16
GMLP_WIDTH = 2 * D_MODEL
GMLP_GROUPS = 16
GMLP_GROUP_WIDTH = GMLP_WIDTH // GMLP_GROUPS
CHUNK = 128
CTX_COND_ROW = 2
COND_ROWS = 8
SM_SCALE = HEAD_DIM ** -0.5
LOG2E = math.log2(math.e)
QUERY_SCALE = SM_SCALE * LOG2E

VMEM_LIMIT = 56 * 1024 * 1024


def _params(n_axes, vmem=VMEM_LIMIT, flags=None):
    return pltpu.CompilerParams(dimension_semantics=("arbitrary",) * n_axes,
                                vmem_limit_bytes=vmem, flags=flags)


def _cond_row(tok0):
    return jnp.where(tok0 < PROMPT_TOKENS, CTX_COND_ROW, (tok0 - PROMPT_TOKENS) // SAMPLE_SEQ)


def _silu(x):
    return x * (0.5 + 0.5 * jnp.tanh(0.5 * x))


def _gelu_tanh(x):
    return 0.5 * x * (1.0 + jnp.tanh(math.sqrt(2.0 / math.pi) * (x + 0.044715 * (x * x * x))))


def _split_specs(tm, width, tile0=0):
    n_p = PROMPT_TOKENS // tm
    return [pl.BlockSpec((tm, width), lambda i: (jnp.minimum(i + tile0, n_p - 1), 0)),
            pl.BlockSpec((tm, width), lambda i: (jnp.maximum(i + tile0 - n_p, 0), 0))]


def _ada_kernel(cond_ref, w_ref, b_ref, o_ref):
    s = _silu(cond_ref[...]).astype(BF16)
    o_ref[...] = jnp.dot(s, w_ref[...].astype(BF16), preferred_element_type=F32) + b_ref[...]


def _ada(cond, w_ada, b_ada, tn=1024):
    depth = w_ada.shape[0]
    n = w_ada.shape[2]
    return pl.pallas_call(
        _ada_kernel,
        grid=(depth, n // tn),
        in_specs=[pl.BlockSpec((COND_ROWS, D_MODEL), lambda l, j: (0, 0)),
                  pl.BlockSpec((None, D_MODEL, tn), lambda l, j: (l, 0, j)),
                  pl.BlockSpec((None, 1, tn), lambda l, j: (l, 0, j))],
        out_specs=pl.BlockSpec((None, COND_ROWS, tn), lambda l, j: (l, 0, j)),
        out_shape=jax.ShapeDtypeStruct((depth, COND_ROWS, n), F32),
        compiler_params=_params(2),
        name="ada_mod",
    )(cond, w_ada, b_ada.reshape(depth, 1, n))


def _norm_kernel(xp_ref, xs_ref, g_ref, shift_ref, scale_ref, o_ref, *, tm):
    x = jnp.where(pl.program_id(0) < PROMPT_TOKENS // tm, xp_ref[...], xs_ref[...])
    y = x * lax.rsqrt(jnp.mean(x * x, axis=-1, keepdims=True) + EPS) * g_ref[...]
    o_ref[...] = (y * (1.0 + scale_ref[...]) + shift_ref[...]).astype(BF16)


def _norm_mod(xp, xs, norm_g3, mod4, layer, tm=512):
    return pl.pallas_call(
        functools.partial(_norm_kernel, tm=tm),
        grid=(TOKENS // tm,),
        in_specs=[*_split_specs(tm, D_MODEL),
                  pl.BlockSpec((None, 1, D_MODEL), lambda i: (layer, 0, 0)),
                  pl.BlockSpec((None, None, 1, D_MODEL), lambda i: (layer, _cond_row(i * tm), 0, 0)),
                  pl.BlockSpec((None, None, 1, D_MODEL), lambda i: (layer, _cond_row(i * tm), 0, 1))],
        out_specs=pl.BlockSpec((tm, D_MODEL), lambda i: (i, 0)),
        out_shape=jax.ShapeDtypeStruct((TOKENS, D_MODEL), BF16),
        compiler_params=_params(1),
        name="norm_mod",
    )(xp, xs, norm_g3, mod4, mod4)


PROJ_TM = 2048
PROJ_TN = 512
PROJ_TN_WIDE = 1024
PROJ_ROWS = 512
N_PROMPT_TILES = PROMPT_TOKENS // PROJ_TM


class _Segment(NamedTuple):
    col_blk0: int
    n_blk: int
    epilogue: Callable
    by_tile_kind: bool
    outs: tuple


def _segment_steps(segments, wide):
    spans, j0 = [], 0
    for seg in segments:
        assert seg.col_blk0 % wide == 0 and seg.n_blk % wide == 0
        spans.append((j0, seg.n_blk // wide))
        j0 += seg.n_blk // wide
    return spans


def _proj_kernel(*refs, segments, spans, n_extra, n_out):
    a_ref, w_ref = refs[0], refs[1]
    extras = refs[2:2 + n_extra]
    outs = refs[2 + n_extra:2 + n_extra + n_out]
    wbf_ref = refs[2 + n_extra + n_out]

    @pl.when(pl.program_id(1) == 0)
    def _():
        wbf_ref[...] = w_ref[...].astype(BF16)

    def run(seg):
        seg_outs = [outs[k] for k in seg.outs]

        def body(latent):
            for rc in range(PROJ_TM // PROJ_ROWS):
                rows = slice(rc * PROJ_ROWS, (rc + 1) * PROJ_ROWS)
                for col0 in range(0, wbf_ref.shape[1], PROJ_TN):
                    acc = jnp.dot(a_ref[rows, :], wbf_ref[:, col0:col0 + PROJ_TN],
                                  preferred_element_type=F32)
                    seg.epilogue(acc, extras, seg_outs, rc, col0, latent)

        if seg.by_tile_kind:
            is_latent = pl.program_id(1) >= N_PROMPT_TILES
            pl.when(is_latent)(lambda: body(True))
            pl.when(jnp.logical_not(is_latent))(lambda: body(False))
        else:
            body(None)

    j = pl.program_id(0)
    for seg, (j0, nj) in zip(segments, spans):
        if len(segments) == 1:
            run(seg)
        else:
            pl.when(jnp.logical_and(j >= j0, j < j0 + nj))(functools.partial(run, seg))


def _proj(a, w, layer, segments, out_shapes, out_specs, name, extras=(), extra_specs=(), tn=PROJ_TN,
          run_after=None):
    m, k = a.shape
    if run_after is not None:
        extras = [*extras, run_after]
        extra_specs = [*extra_specs, pl.BlockSpec(memory_space=pl.ANY)]
    wide = tn // PROJ_TN
    spans = _segment_steps(segments, wide)

    def w_col(j):
        col = None
        for seg, (j0, _) in zip(segments, spans):
            c = seg.col_blk0 // wide + j - j0
            col = c if col is None else jnp.where(j >= j0, c, col)
        return col

    kern = functools.partial(_proj_kernel, segments=segments, spans=spans, n_extra=len(extras),
                             n_out=len(out_shapes))
    return pl.pallas_call(
        kern,
        grid=(sum(nj for _, nj in spans), m // PROJ_TM),
        in_specs=[pl.BlockSpec((PROJ_TM, k), lambda j, i: (i, 0)),
                  pl.BlockSpec((None, k, tn), lambda j, i: (layer, 0, w_col(j))),
                  *extra_specs],
        out_specs=out_specs,
        out_shape=out_shapes,
        scratch_shapes=[pltpu.VMEM((k, tn), BF16)],
        compiler_params=_params(2),
        name=name,
    )(a, w, *extras)


def _full_out(n_blk):
    return ([jax.ShapeDtypeStruct((TOKENS, n_blk * PROJ_TN), BF16)],
            [pl.BlockSpec((PROJ_TM, PROJ_TN_WIDE), lambda j, i: (i, j))])


def _held(span, j, inside, before, after):
    j0, nj = span
    pick = lambda a, b, c: jnp.where(j < j0, b, jnp.where(j >= j0 + nj, c, a))
    return tuple(pick(a, b, c) for a, b, c in zip(inside, before, after))


def _split_out(span, n_blk, head_rows):
    prompt_tile = lambda i: jnp.minimum(i, N_PROMPT_TILES - 1)
    last_prompt, last_tile = N_PROMPT_TILES - 1, TOKENS // PROJ_TM - 1
    jj = lambda j: j - span[0]
    if head_rows:
        assert n_blk == 1
        heads = PROJ_TN // HEAD_DIM
        cache_shape = jax.ShapeDtypeStruct((PROMPT_TOKENS * heads, HEAD_DIM), F32)
        cache_spec = pl.BlockSpec(
            (PROJ_TM * heads, HEAD_DIM),
            lambda j, i: _held(span, j, (prompt_tile(i), 0), (0, 0), (last_prompt, 0)))
    else:
        cache_shape = jax.ShapeDtypeStruct((N_PROMPT, PROMPT_SEQ, n_blk * PROJ_TN), F32)
        cache_spec = pl.BlockSpec(
            (PROJ_TM // PROMPT_SEQ, PROMPT_SEQ, PROJ_TN),
            lambda j, i: _held(span, j, (prompt_tile(i), 0, jj(j)), (0, 0, 0), (last_prompt, 0, n_blk - 1)))
    shapes = [cache_shape, jax.ShapeDtypeStruct((TOKENS, n_blk * PROJ_TN), BF16)]
    specs = [cache_spec,
             pl.BlockSpec((PROJ_TM, PROJ_TN),
                          lambda j, i: _held(span, j, (i, jj(j)), (0, 0), (last_tile, n_blk - 1)))]
    return shapes, specs


def _chunk_rows(rc):
    return slice(rc * PROJ_ROWS, (rc + 1) * PROJ_ROWS)


def _epi_silu(acc, extras, outs, rc, col0, latent):
    outs[0][_chunk_rows(rc), col0:col0 + PROJ_TN] = _silu(acc).astype(BF16)


def _epi_gelu(acc, extras, outs, rc, col0, latent):
    outs[0][_chunk_rows(rc), col0:col0 + PROJ_TN] = _gelu_tanh(acc).astype(BF16)


def _store_split(y, outs, rc, sl, latent):
    outs[1][_chunk_rows(rc), sl] = y.astype(BF16)
    if latent:
        return
    if len(outs[0].shape) == 2:
        heads = PROJ_TN // HEAD_DIM
        col0 = sl.start or 0
        for c in range(y.shape[1] // HEAD_DIM):
            head = col0 // HEAD_DIM + c
            rows = pl.ds(rc * PROJ_ROWS * heads + head, PROJ_ROWS, stride=heads)
            outs[0][rows, :] = y[:, c * HEAD_DIM:(c + 1) * HEAD_DIM]
    else:
        seqs = PROJ_ROWS // PROMPT_SEQ
        outs[0][rc * seqs:(rc + 1) * seqs, :, sl] = y.reshape(seqs, PROMPT_SEQ, y.shape[1])


def _epi_value(acc, extras, outs, rc, col0, latent):
    _store_split(acc, outs, rc, slice(col0, col0 + PROJ_TN), latent)


def _head_rmsnorm(acc, hh, g):
    a = acc[:, hh * HEAD_DIM:(hh + 1) * HEAD_DIM]
    return a * lax.rsqrt(jnp.mean(a * a, axis=-1, keepdims=True) + EPS) * g


def _normed_heads(acc, extras, rc, rope):
    g = extras[0][...]
    ys = [_head_rmsnorm(acc, hh, g) for hh in range(PROJ_TN // HEAD_DIM)]
    if rope:
        rows = _chunk_rows(rc)
        cos, sin, swap = extras[1][rows, :], extras[2][rows, :], extras[3][...]
        for pair in range(len(ys) // 2):
            both = jnp.concatenate(ys[2 * pair:2 * pair + 2], axis=1).astype(BF16)
            partner = jnp.dot(both, swap, preferred_element_type=F32)
            for t in range(2):
                hh = 2 * pair + t
                ys[hh] = ys[hh] * cos + partner[:, t * HEAD_DIM:(t + 1) * HEAD_DIM] * sin
    for hh, y in enumerate(ys):
        yield slice(hh * HEAD_DIM, (hh + 1) * HEAD_DIM), y


def _epi_query(acc, extras, outs, rc, col0, latent, *, rope):
    for sl, y in _normed_heads(acc, extras, rc, rope and latent):
        outs[0][_chunk_rows(rc), col0 + sl.start:col0 + sl.stop] = (y * QUERY_SCALE).astype(BF16)


def _epi_key(acc, extras, outs, rc, col0, latent, *, rope):
    for sl, y in _normed_heads(acc, extras, rc, rope and latent):
        _store_split(y, outs, rc, slice(col0 + sl.start, col0 + sl.stop), latent)


def _rope_tables():
    nf = HEAD_DIM // 4
    t = np.arange(SAMPLE_TOKENS) % SAMPLE_SEQ
    row = (t // GRID_W).astype(np.float32)
    col = (t % GRID_W).astype(np.float32)
    inv = np.float32(ROPE_THETA) ** (-np.arange(nf, dtype=np.float32) / np.float32(nf))
    ang_r = row[:, None] * inv
    ang_c = col[:, None] * inv
    cos = np.concatenate([np.cos(ang_r), np.cos(ang_r), np.cos(ang_c), np.cos(ang_c)], axis=1)
    sin = np.concatenate([-np.sin(ang_r), np.sin(ang_r), -np.sin(ang_c), np.sin(ang_c)], axis=1)
    lanes = np.arange(2 * HEAD_DIM)
    swap = lanes[:, None] == (lanes[None, :] ^ (HEAD_DIM // 4))
    return jnp.asarray(cos, F32), jnp.asarray(sin, F32), jnp.asarray(swap, BF16)


def _gain_extras(gain, rope_tabs):
    extras = [gain.reshape(1, HEAD_DIM)]
    specs = [pl.BlockSpec((1, HEAD_DIM), lambda j, i: (0, 0))]
    if rope_tabs is not None:
        tab_map = lambda j, i: (jnp.maximum(i - N_PROMPT_TILES, 0), 0)
        extras += list(rope_tabs)
        specs += [pl.BlockSpec((PROJ_TM, HEAD_DIM), tab_map)] * 2
        specs.append(pl.BlockSpec((2 * HEAD_DIM, 2 * HEAD_DIM), lambda j, i: (0, 0)))
    return extras, specs


def _qz_proj(hn, w, layer, q_blk0, z_blk0, n_blk, gain, rope_tabs, name):
    extras, specs = _gain_extras(gain, rope_tabs)
    rope = rope_tabs is not None
    segments = (_Segment(q_blk0, n_blk, functools.partial(_epi_query, rope=rope), rope, (0,)),
                _Segment(z_blk0, n_blk, _epi_silu, False, (0,)))
    return _proj(hn, w, layer, segments, *_full_out(2 * n_blk), name, extras, specs, tn=PROJ_TN_WIDE)[0]


def _kv_proj(hn, w, layer, k_blk0, n_blk, gain, rope_tabs, head_rows, name, run_after=None):
    extras, specs = _gain_extras(gain, rope_tabs)
    segments = (_Segment(k_blk0, n_blk, functools.partial(_epi_key, rope=rope_tabs is not None), True, (0, 1)),
                _Segment(k_blk0 + n_blk, n_blk, _epi_value, True, (2, 3)))
    shapes, out_specs = [], []
    for span in _segment_steps(segments, 1):
        seg_shapes, seg_specs = _split_out(span, n_blk, head_rows)
        shapes += seg_shapes
        out_specs += seg_specs
    return _proj(hn, w, layer, segments, shapes, out_specs, name, extras, specs, run_after=run_after)


def _uvz_proj(hn, w, layer, n_blk, name):
    segments = (_Segment(0, 2 * n_blk, _epi_gelu, False, (0,)),
                _Segment(2 * n_blk, n_blk, _epi_silu, False, (0,)))
    return _proj(hn, w, layer, segments, *_full_out(3 * n_blk), name, tn=PROJ_TN_WIDE)[0]


OUT_COLS = 512


def _cast_slab_specs(w, layer, n_slabs, slab_of_step):
    k, n = w.shape[1], w.shape[2]
    rows = k // n_slabs
    return (pl.BlockSpec((None, rows, n), lambda *g: (layer, slab_of_step(*g), 0)),
            pl.BlockSpec((rows, n), lambda *g: (slab_of_step(*g), 0)),
            jax.ShapeDtypeStruct((k, n), BF16))


def _out_kernel(*refs, tile0, tm, split_a, split_x, fuse_norm):
    it = iter(refs)
    a_refs = [next(it), next(it)] if split_a else [next(it)]
    wbf_ref = next(it)
    x_refs = [next(it), next(it)] if split_x else [next(it)]
    gate_ref = next(it)
    if fuse_norm:
        g_ref, shift_ref, scale_ref = next(it), next(it), next(it)
    xnew_ref = next(it)
    hn_ref = next(it) if fuse_norm else None

    i = pl.program_id(0)
    is_prompt = i + tile0 < PROMPT_TOKENS // tm
    a = jnp.where(is_prompt, a_refs[0][...], a_refs[1][...]) if split_a else a_refs[0][...]
    ssq = jnp.zeros((tm, 1), F32)
    for cb in range(D_MODEL // OUT_COLS):
        sl = slice(cb * OUT_COLS, (cb + 1) * OUT_COLS)
        acc = jnp.dot(a, wbf_ref[:, sl], preferred_element_type=F32)
        if split_x:
            x = jnp.where(is_prompt, x_refs[0][:, sl], x_refs[1][:, sl])
        else:
            x = x_refs[0][:, sl]
        xn = x + gate_ref[:, sl] * acc
        xnew_ref[:, sl] = xn
        if fuse_norm:
            ssq = ssq + jnp.sum(xn * xn, axis=-1, keepdims=True)
    if fuse_norm:
        rs = lax.rsqrt(ssq * (1.0 / D_MODEL) + EPS)
        for cb in range(D_MODEL // OUT_COLS):
            sl = slice(cb * OUT_COLS, (cb + 1) * OUT_COLS)
            gmul = g_ref[:, sl] * (1.0 + scale_ref[:, sl])
            hn_ref[:, sl] = (xnew_ref[:, sl] * rs * gmul + shift_ref[:, sl]).astype(BF16)


def _out_proj(a, wbf, x, mod4, layer, norm_g3, name, tm, rows=None):
    split_a = isinstance(a, tuple)
    k = a[0].shape[1] if split_a else a.shape[1]
    tok0, n_tok = rows if rows is not None else (0, TOKENS)
    tile0 = tok0 // tm
    split_x = isinstance(x, tuple)
    fuse_norm = norm_g3 is not None
    mod_spec = lambda part, lyr: pl.BlockSpec(
        (None, None, 1, D_MODEL), lambda i: (lyr, _cond_row((i + tile0) * tm), 0, part))
    row_spec = lambda width: pl.BlockSpec((tm, width), lambda i: (i + tile0, 0))
    operands = [*a, wbf] if split_a else [a, wbf]
    in_specs = [*(_split_specs(tm, k, tile0) if split_a else [row_spec(k)]),
                pl.BlockSpec((k, D_MODEL), lambda i: (0, 0))]
    if split_x:
        assert rows is None
        operands += list(x)
        in_specs += _split_specs(tm, D_MODEL)
    else:
        operands.append(x)
        in_specs.append(row_spec(D_MODEL))
    operands.append(mod4)
    in_specs.append(mod_spec(2, layer))
    out_shapes = [jax.ShapeDtypeStruct((n_tok, D_MODEL), F32)]
    out_specs = [pl.BlockSpec((tm, D_MODEL), lambda i: (i, 0))]
    if fuse_norm:
        operands += [norm_g3, mod4, mod4]
        in_specs += [pl.BlockSpec((None, 1, D_MODEL), lambda i: (layer + 1, 0, 0)),
                     mod_spec(0, layer + 1), mod_spec(1, layer + 1)]
        out_shapes.append(jax.ShapeDtypeStruct((n_tok, D_MODEL), BF16))
        out_specs.append(pl.BlockSpec((tm, D_MODEL), lambda i: (i, 0)))
    kern = functools.partial(_out_kernel, tile0=tile0, tm=tm,
                             split_a=split_a, split_x=split_x, fuse_norm=fuse_norm)
    return pl.pallas_call(
        kern,
        grid=(n_tok // tm,),
        in_specs=in_specs,
        out_specs=out_specs,
        out_shape=out_shapes,
        compiler_params=_params(1),
        name=name,
    )(*operands)


def _dot_nt(a, b):
    return lax.dot_general(a, b, (((1,), (1,)), ((), ())), preferred_element_type=F32)


def _head(ref, h, rows=slice(None)):
    return ref[rows, h * HEAD_DIM:(h + 1) * HEAD_DIM]


def _with_ones(v):
    return jnp.concatenate([v, jnp.ones(v.shape, v.dtype)], axis=1)


def _lane_chunks(s):
    return [s[:, c * HEAD_DIM:(c + 1) * HEAD_DIM] for c in range(s.shape[1] // HEAD_DIM)]


def _softmax_pv(score_blocks, value_blocks, sink):
    rows = score_blocks[0].shape[0]
    mx = functools.reduce(jnp.maximum, [c for s in score_blocks for c in _lane_chunks(s)])
    m = jnp.broadcast_to(jnp.max(mx, axis=-1, keepdims=True), (rows, HEAD_DIM))
    if sink is not None:
        m = jnp.maximum(m, sink)
    o = None
    for s, v in zip(score_blocks, value_blocks):
        p = jnp.concatenate([jnp.exp2(c - m) for c in _lane_chunks(s)], axis=1).astype(BF16)
        part = jnp.dot(p, v, preferred_element_type=F32)
        o = part if o is None else o + part
    den = o[:, HEAD_DIM:]
    if sink is not None:
        den = den + jnp.exp2(sink - m)
    return o[:, :HEAD_DIM] * (1.0 / den)


def _attn_prompt_kernel(sink_ref, q_ref, k_ref, v_ref, z_ref, o_ref, *, n_heads, n_kv, use_sink):
    grp = n_heads // n_kv
    for g in range(n_kv):
        kg = _head(k_ref, g)
        vg = _with_ones(_head(v_ref, g))
        for h in range(g * grp, (g + 1) * grp):
            s = _dot_nt(_head(q_ref, h), kg)
            sink = jnp.full((PROMPT_SEQ, HEAD_DIM), sink_ref[h], F32) if use_sink else None
            o = _softmax_pv([s], [vg], sink)
            o_ref[:, h * HEAD_DIM:(h + 1) * HEAD_DIM] = (o * _head(z_ref, h).astype(F32)).astype(BF16)


def _attn_prompt(sink, qz, k, v, n_kv, use_sink, name):
    width = qz.shape[1] // 2
    kern = functools.partial(_attn_prompt_kernel, n_heads=width // HEAD_DIM, n_kv=n_kv, use_sink=use_sink)
    cache_spec = pl.BlockSpec((PROMPT_SEQ, k.shape[1]), lambda b: (b, 0))
    return pl.pallas_call(
        kern,
        grid=(N_PROMPT,),
        in_specs=[pl.BlockSpec(memory_space=pltpu.SMEM),
                  pl.BlockSpec((PROMPT_SEQ, width), lambda b: (b, 0)),
                  cache_spec, cache_spec,
                  pl.BlockSpec((PROMPT_SEQ, width), lambda b: (b, 1))],
        out_specs=pl.BlockSpec((PROMPT_SEQ, width), lambda b: (b, 0)),
        out_shape=jax.ShapeDtypeStruct((PROMPT_TOKENS, width), BF16),
        compiler_params=_params(1),
        name=name,
    )(sink, qz, k, v, qz)


WIN_BAND = 3 * WINDOW


def _attn_win_kernel(sink_ref, q_ref, k_ref, v_ref, kc_ref, vc_ref, z_ref, w_ref, o_ref, wbf_ref, *,
                     n_heads, n_kv):
    wbf_ref[...] = w_ref[...].astype(BF16)
    grp = n_heads // n_kv
    n = pl.program_id(1)
    start = pl.multiple_of(jnp.clip((n - 1) * WINDOW, 0, SAMPLE_SEQ - WIN_BAND), WINDOW)
    shape = (grp * WINDOW, WIN_BAND)
    qpos = n * WINDOW + (lax.broadcasted_iota(jnp.int32, shape, 0) & (WINDOW - 1))
    kpos = start + lax.broadcasted_iota(jnp.int32, shape, 1)
    valid = jnp.abs(kpos - qpos) <= WINDOW
    band = pl.ds(start, WIN_BAND)
    for g in range(n_kv):
        heads = [g * grp + t for t in range(grp)]
        qs = jnp.concatenate([_head(q_ref, h) for h in heads], axis=0)
        kb = _head(k_ref, g, band)
        vb = _with_ones(_head(v_ref, g, band))
        ctx_rows = pl.ds(g, PAST_LEN, stride=n_kv)
        kc = kc_ref[ctx_rows, :].astype(BF16)
        vc = _with_ones(vc_ref[ctx_rows, :].astype(BF16))
        s_band = jnp.where(valid, _dot_nt(qs, kb), NEG_INF)
        s_ctx = _dot_nt(qs, kc)
        sink = jnp.concatenate([jnp.full((WINDOW, HEAD_DIM), sink_ref[h], F32) for h in heads], axis=0)
        o = _softmax_pv([s_band, s_ctx], [vb, vc], sink)
        for t, h in enumerate(heads):
            oh = o[t * WINDOW:(t + 1) * WINDOW]
            o_ref[:, h * HEAD_DIM:(h + 1) * HEAD_DIM] = (oh * _head(z_ref, h).astype(F32)).astype(BF16)


def _attn_win(sink, qz, k, v, cache_k, cache_v, layer_in_kind, n_kv, w_out):
    width = qz.shape[1] // 2
    kv_width = k.shape[1]
    blocks_per_seq = SAMPLE_SEQ // WINDOW
    q_map = lambda b, n: (PROMPT_TOKENS // WINDOW + b * blocks_per_seq + n, 0)
    z_map = lambda b, n: (PROMPT_TOKENS // WINDOW + b * blocks_per_seq + n, 1)
    kv_map = lambda b, n: (PROMPT_TOKENS // SAMPLE_SEQ + b, 0)
    cache_map = lambda b, n: (b, layer_in_kind, 0, 0)
    kern = functools.partial(_attn_win_kernel, n_heads=width // HEAD_DIM, n_kv=n_kv)
    w_spec, wbf_spec, wbf_shape = _cast_slab_specs(w_out, layer_in_kind, N_SAMPLE * blocks_per_seq,
                                                   lambda b, n: b * blocks_per_seq + n)
    return pl.pallas_call(
        kern,
        grid=(N_SAMPLE, blocks_per_seq),
        in_specs=[pl.BlockSpec(memory_space=pltpu.SMEM),
                  pl.BlockSpec((WINDOW, width), q_map),
                  pl.BlockSpec((SAMPLE_SEQ, kv_width), kv_map),
                  pl.BlockSpec((SAMPLE_SEQ, kv_width), kv_map),
                  pl.BlockSpec((None, None, PAST_LEN * n_kv, HEAD_DIM), cache_map),
                  pl.BlockSpec((None, None, PAST_LEN * n_kv, HEAD_DIM), cache_map),
                  pl.BlockSpec((WINDOW, width), z_map),
                  w_spec],
        out_specs=[pl.BlockSpec((WINDOW, width), lambda b, n: (b * blocks_per_seq + n, 0)), wbf_spec],
        out_shape=[jax.ShapeDtypeStruct((SAMPLE_TOKENS, width), BF16), wbf_shape],
        compiler_params=_params(2),
        name="attn_win_latent",
    )(sink, qz, k, v, cache_k, cache_v, qz, w_out)


NAT_QROWS = 4
GRID_ROWS = SAMPLE_SEQ // GRID_W
N_DR = 2 * NAT_ROWS - 1
MASKED_TILE = N_DR


def _nat_row_start(qr):
    return min(max(qr - NAT_ROWS // 2, 0), GRID_ROWS - NAT_ROWS)


def _attn_nat_kernel(bias_ref, q_ref, k_ref, v_ref, kc_ref, vc_ref, z_ref, w_ref, o_ref, wbf_ref,
                     left_ref, right_ref):
    wbf_ref[...] = w_ref[...].astype(BF16)

    def build_bias_tiles():
        shape = (GRID_W, 2 * GRID_W)
        lane = lax.broadcasted_iota(jnp.int32, shape, 1)
        qc = lax.broadcasted_iota(jnp.int32, shape, 0)
        kc_ = lane & (GRID_W - 1)
        cs = jnp.clip(qc - NAT_COLS // 2, 0, GRID_W - NAT_COLS)
        col_ok = jnp.logical_and(kc_ >= cs, kc_ < cs + NAT_COLS)
        is_left = lane < GRID_W
        lanes = 2 * GRID_W
        for dri in range(N_DR):
            row = jnp.broadcast_to(bias_ref[dri:dri + 1, :], shape)
            on_left = pltpu.roll(row, lanes - (NAT_COLS - 1), 1, stride=1, stride_axis=0)
            on_right = pltpu.roll(row, GRID_W - (NAT_COLS - 1), 1, stride=1, stride_axis=0)
            left_ref[dri] = jnp.where(is_left, jnp.where(col_ok, on_left, NEG_INF), 0.0)
            right_ref[dri] = jnp.where(is_left, 0.0, jnp.where(col_ok, on_right, NEG_INF))
        left_ref[MASKED_TILE] = jnp.where(is_left, NEG_INF, 0.0)
        right_ref[MASKED_TILE] = jnp.where(is_left, 0.0, NEG_INF)

    build_bias_tiles()
    n_heads = kc_ref.shape[1] // PAST_LEN
    ctx_rows = pl.ds(pl.program_id(0), PAST_LEN, stride=n_heads)
    for b, qb in [(b, qb) for b in range(N_SAMPLE) for qb in range(GRID_ROWS // NAT_QROWS)]:
        if qb == 0:
            kc = kc_ref[b, ctx_rows, :].astype(BF16)
            vc = _with_ones(vc_ref[b, ctx_rows, :].astype(BF16))
        qrows = range(qb * NAT_QROWS, (qb + 1) * NAT_QROWS)
        krow0 = _nat_row_start(qrows[0]) // 2 * 2
        krow1 = -(-(_nat_row_start(qrows[-1]) + NAT_ROWS) // 2) * 2
        tok0 = b * SAMPLE_SEQ
        kwin = slice(tok0 + krow0 * GRID_W, tok0 + krow1 * GRID_W)
        qwin = slice(tok0 + qrows[0] * GRID_W, tok0 + (qrows[-1] + 1) * GRID_W)

        def tile_index(qr, kr):
            rs = _nat_row_start(qr)
            return kr - qr + (NAT_ROWS - 1) if rs <= kr < rs + NAT_ROWS else MASKED_TILE

        bias = jnp.concatenate(
            [jnp.concatenate([left_ref[tile_index(qr, kr)] + right_ref[tile_index(qr, kr + 1)]
                              for kr in range(krow0, krow1, 2)], axis=1)
             for qr in qrows], axis=0)

        q = q_ref[qwin, :]
        s_nb = _dot_nt(q, k_ref[kwin, :]) + bias
        s_ctx = _dot_nt(q, kc)
        o = _softmax_pv([s_nb, s_ctx], [_with_ones(v_ref[kwin, :]), vc], None)
        o_ref[qwin, :] = (o * z_ref[qwin, :].astype(F32)).astype(BF16)


def _attn_nat(rel_bias, qz, k, v, cache_k, cache_v, layer_in_kind, w_out):
    n_heads = qz.shape[1] // (2 * HEAD_DIM)
    bias_rows = jnp.pad(rel_bias, ((0, 0), (0, N_DR + 1 - rel_bias.shape[1]),
                                   (0, 2 * GRID_W - rel_bias.shape[2])))
    latent_tile = PROMPT_TOKENS // SAMPLE_TOKENS
    latent_spec = pl.BlockSpec((SAMPLE_TOKENS, HEAD_DIM), lambda h: (latent_tile, h))
    z_spec = pl.BlockSpec((SAMPLE_TOKENS, HEAD_DIM), lambda h: (latent_tile, n_heads + h))
    cache_spec = pl.BlockSpec((N_SAMPLE, None, PAST_LEN * n_heads, HEAD_DIM),
                              lambda h: (0, layer_in_kind, 0, 0))
    w_spec, wbf_spec, wbf_shape = _cast_slab_specs(w_out, layer_in_kind, n_heads, lambda h: h)
    return pl.pallas_call(
        _attn_nat_kernel,
        grid=(n_heads,),
        in_specs=[pl.BlockSpec((None, N_DR + 1, 2 * GRID_W), lambda h: (h, 0, 0)),
                  latent_spec, latent_spec, latent_spec, cache_spec, cache_spec, z_spec, w_spec],
        out_specs=[pl.BlockSpec((SAMPLE_TOKENS, HEAD_DIM), lambda h: (0, h)), wbf_spec],
        out_shape=[jax.ShapeDtypeStruct((SAMPLE_TOKENS, n_heads * HEAD_DIM), BF16), wbf_shape],
        scratch_shapes=[pltpu.VMEM((N_DR + 1, GRID_W, 2 * GRID_W), F32),
                        pltpu.VMEM((N_DR + 1, GRID_W, 2 * GRID_W), F32)],
        compiler_params=_params(1),
        name="attn_nat_latent",
    )(bias_rows, qz, k, v, cache_k, cache_v, qz, w_out)


SPATIAL_TOKENS = 2 * CHUNK
SPATIAL_CAST_STEPS = 16


def _spatial_kernel(u_ref, v_ref, z_ref, g_ref, b_ref, ws_ref, bs_ref, w_ref, o_ref, wbf_ref):
    @pl.when(pl.program_id(0) < SPATIAL_CAST_STEPS)
    def _():
        wbf_ref[...] = w_ref[...].astype(BF16)

    for c in range(SPATIAL_TOKENS // CHUNK):
        rows = slice(c * CHUNK, (c + 1) * CHUNK)
        v = v_ref[rows, :].astype(F32)
        mu = jnp.mean(v, axis=-1, keepdims=True)
        vc = v - mu
        var = jnp.mean(vc * vc, axis=-1, keepdims=True)
        vn = (vc * lax.rsqrt(var + EPS) * g_ref[...] + b_ref[...]).astype(BF16)
        for g in range(GMLP_GROUPS):
            sl = slice(g * GMLP_GROUP_WIDTH, (g + 1) * GMLP_GROUP_WIDTH)
            sv = jnp.dot(ws_ref[g], vn[:, sl], preferred_element_type=F32) + bs_ref[:, g:g + 1]
            o_ref[rows, sl] = u_ref[rows, sl] * sv.astype(BF16) * z_ref[rows, sl]


def _spatial(uvz, ln_g, ln_b, w_s, b_s, w_out, layer_in_kind):
    w_spec, wbf_spec, wbf_shape = _cast_slab_specs(
        w_out, layer_in_kind, SPATIAL_CAST_STEPS, lambda i: jnp.minimum(i, SPATIAL_CAST_STEPS - 1))
    part = lambda p: pl.BlockSpec((SPATIAL_TOKENS, GMLP_WIDTH), lambda i: (i, p))
    row = part(0)
    vec = pl.BlockSpec((1, GMLP_WIDTH), lambda i: (0, 0))
    return pl.pallas_call(
        _spatial_kernel,
        grid=(TOKENS // SPATIAL_TOKENS,),
        in_specs=[part(0), part(1), part(2), vec, vec,
                  pl.BlockSpec((GMLP_GROUPS, CHUNK, CHUNK), lambda i: (0, 0, 0)),
                  pl.BlockSpec((CHUNK, GMLP_GROUPS), lambda i: (0, 0)),
                  w_spec],
        out_specs=[row, wbf_spec],
        out_shape=[jax.ShapeDtypeStruct((TOKENS, GMLP_WIDTH), BF16), wbf_shape],
        compiler_params=_params(1),
        name="gmlp_spatial",
    )(uvz, uvz, uvz, ln_g.reshape(1, -1), ln_b.reshape(1, -1), w_s.astype(BF16), b_s.T, w_out)


def kernel(x_prompt, x_sample, cache_win_k, cache_win_v, cache_nat_k, cache_nat_v, c, c_ctx,
           norm_g, w_ada, b_ada,
           win_w_in, win_q_norm, win_k_norm, win_sink, win_w_out,
           nat_w_in, nat_q_norm, nat_k_norm, nat_rel_bias, nat_w_out,
           gmlp_w_in, gmlp_ln_g, gmlp_ln_b, gmlp_w_s, gmlp_b_s, gmlp_w_out):
    depth = norm_g.shape[0]
    xp = x_prompt.reshape(PROMPT_TOKENS, D_MODEL)
    xs = x_sample.reshape(SAMPLE_TOKENS, D_MODEL)
    cond = jnp.zeros((COND_ROWS, D_MODEL), F32).at[:N_SAMPLE].set(c).at[CTX_COND_ROW].set(c_ctx)
    mod4 = _ada(cond, w_ada, b_ada).reshape(depth, COND_ROWS, 1, 3 * D_MODEL)
    norm_g3 = norm_g.reshape(depth, 1, D_MODEL)
    rope_tabs = _rope_tables()
    no_sink = jnp.zeros((1,), F32)

    n_win = win_w_in.shape[0]
    n_nat = nat_w_in.shape[0]
    win_kv_heads = cache_win_k.shape[3]
    win_kv_width = win_kv_heads * HEAD_DIM
    cwk = cache_win_k.reshape(N_SAMPLE, n_win, PAST_LEN * win_kv_heads, HEAD_DIM)
    cwv = cache_win_v.reshape(N_SAMPLE, n_win, PAST_LEN * win_kv_heads, HEAD_DIM)
    nat_heads = cache_nat_k.shape[3]
    cnk = cache_nat_k.reshape(N_SAMPLE, n_nat, PAST_LEN * nat_heads, HEAD_DIM)
    cnv = cache_nat_v.reshape(N_SAMPLE, n_nat, PAST_LEN * nat_heads, HEAD_DIM)

    new_win_k, new_win_v, new_nat_k, new_nat_v = [], [], [], []
    x = (xp, xs)
    hn = _norm_mod(xp, xs, norm_g3, mod4, 0)
    for layer in range(depth):
        kind = layer % 3
        li = layer // 3
        if kind == 0:
            nq = D_MODEL // PROJ_TN
            nkv = win_kv_width // PROJ_TN
            qz = _qz_proj(hn, win_w_in, li, 0, nq + 2 * nkv, nq, win_q_norm[li], rope_tabs, "win_qz")
            kp, ks, vp, vs = _kv_proj(hn, win_w_in, li, nq, nkv, win_k_norm[li], rope_tabs, True, "win_kv")
            new_win_k.append(kp)
            new_win_v.append(vp)
            sink = win_sink[li] * LOG2E
            a_latent, wbf = _attn_win(sink, qz, ks, vs, cwk, cwv, li, win_kv_heads, win_w_out)
            a = (_attn_prompt(sink, qz, ks, vs, win_kv_heads, True, "attn_win_prompt"), a_latent)
            out_tm, out_name = 512, "win_out"
        elif kind == 1:
            nq = D_MODEL // PROJ_TN
            qz = _qz_proj(hn, nat_w_in, li, 0, 3 * nq, nq, nat_q_norm[li], None, "nat_qz")
            kp, ks, vp, vs = _kv_proj(hn, nat_w_in, li, nq, nq, nat_k_norm[li], None, False, "nat_kv",
                                      run_after=qz)
            new_nat_k.append(kp)
            new_nat_v.append(vp)
            a_latent, wbf = _attn_nat(nat_rel_bias[li] * LOG2E, qz, ks, vs, cnk, cnv, li, nat_w_out)
            a = (_attn_prompt(no_sink, qz, ks, vs, nat_heads, False, "attn_nat_prompt"), a_latent)
            out_tm, out_name = 512, "nat_out"
        else:
            nw = GMLP_WIDTH // PROJ_TN
            uvz = _uvz_proj(hn, gmlp_w_in, li, nw, "gmlp_uvz")
            a, wbf = _spatial(uvz, gmlp_ln_g[li], gmlp_ln_b[li], gmlp_w_s[li], gmlp_b_s[li], gmlp_w_out, li)
            out_tm, out_name = 256, "gmlp_out"
        if layer + 1 < depth:
            x, hn = _out_proj(a, wbf, x, mod4, layer, norm_g3, out_name, out_tm)
        else:
            (yp,) = _out_proj(a, wbf, x, mod4, layer, None, out_name + "_prompt", out_tm,
                              rows=(0, PROMPT_TOKENS))
            (ys,) = _out_proj(a, wbf, x, mod4, layer, None, out_name + "_latent", out_tm,
                              rows=(PROMPT_TOKENS, SAMPLE_TOKENS))

    cache_shape = lambda layers, heads: jnp.stack(
        [c_.reshape(N_PROMPT, PROMPT_SEQ, heads, HEAD_DIM) for c_ in layers], axis=1)
    return (yp.reshape(N_PROMPT, PROMPT_SEQ, D_MODEL), ys.reshape(N_SAMPLE, SAMPLE_SEQ, D_MODEL),
            cache_shape(new_win_k, win_kv_heads), cache_shape(new_win_v, win_kv_heads),
            cache_shape(new_nat_k, nat_heads), cache_shape(new_nat_v, nat_heads))
```

```python
import functools
import math
from typing import Callable, NamedTuple

import jax
import jax.numpy as jnp
import numpy as np
from jax import lax
from jax.experimental import pallas as pl
from jax.experimental.pallas import tpu as pltpu

F32 = jnp.float32
BF16 = jnp.bfloat16

D_MODEL = 2048
HEAD_DIM = 128
N_PROMPT = 16
PROMPT_SEQ = 256
N_SAMPLE = 2
SAMPLE_SEQ = 1024
PROMPT_TOKENS = N_PROMPT * PROMPT_SEQ
SAMPLE_TOKENS = N_SAMPLE * SAMPLE_SEQ
TOKENS = PROMPT_TOKENS + SAMPLE_TOKENS
PAST_LEN = 512
GRID_W = 64
EPS = 1e-6
NEG_INF = -1e30
ROPE_THETA = 10000.0
WINDOW = 128
NAT_ROWS = 8
NAT_COLS = 16
GMLP_WIDTH = 2 * D_MODEL
GMLP_GROUPS = 16
GMLP_GROUP_WIDTH = GMLP_WIDTH // GMLP_GROUPS
CHUNK = 128
CTX_COND_ROW = 2
COND_ROWS = 8
SM_SCALE = HEAD_DIM ** -0.5
LOG2E = math.log2(math.e)
QUERY_SCALE = SM_SCALE * LOG2E

VMEM_LIMIT = 56 * 1024 * 1024


def _params(n_axes, vmem=VMEM_LIMIT, flags=None):
    return pltpu.CompilerParams(dimension_semantics=("arbitrary",) * n_axes,
                                vmem_limit_bytes=vmem, flags=flags)


def _cond_row(tok0):
    return jnp.where(tok0 < PROMPT_TOKENS, CTX_COND_ROW, (tok0 - PROMPT_TOKENS) // SAMPLE_SEQ)


def _silu(x):
    return x * (0.5 + 0.5 * jnp.tanh(0.5 * x))


def _gelu_tanh(x):
    return 0.5 * x * (1.0 + jnp.tanh(math.sqrt(2.0 / math.pi) * (x + 0.044715 * (x * x * x))))


def _split_specs(tm, width, tile0=0):
    n_p = PROMPT_TOKENS // tm
    return [pl.BlockSpec((tm, width), lambda i: (jnp.minimum(i + tile0, n_p - 1), 0)),
            pl.BlockSpec((tm, width), lambda i: (jnp.maximum(i + tile0 - n_p, 0), 0))]


def _ada_kernel(cond_ref, w_ref, b_ref, o_ref):
    s = _silu(cond_ref[...]).astype(BF16)
    o_ref[...] = jnp.dot(s, w_ref[...].astype(BF16), preferred_element_type=F32) + b_ref[...]


def _ada(cond, w_ada, b_ada, tn=1024):
    depth = w_ada.shape[0]
    n = w_ada.shape[2]
    return pl.pallas_call(
        _ada_kernel,
        grid=(depth, n // tn),
        in_specs=[pl.BlockSpec((COND_ROWS, D_MODEL), lambda l, j: (0, 0)),
                  pl.BlockSpec((None, D_MODEL, tn), lambda l, j: (l, 0, j)),
                  pl.BlockSpec((None, 1, tn), lambda l, j: (l, 0, j))],
        out_specs=pl.BlockSpec((None, COND_ROWS, tn), lambda l, j: (l, 0, j)),
        out_shape=jax.ShapeDtypeStruct((depth, COND_ROWS, n), F32),
        compiler_params=_params(2),
        name="ada_mod",
    )(cond, w_ada, b_ada.reshape(depth, 1, n))


def _norm_kernel(xp_ref, xs_ref, g_ref, shift_ref, scale_ref, o_ref, *, tm):
    x = jnp.where(pl.program_id(0) < PROMPT_TOKENS // tm, xp_ref[...], xs_ref[...])
    y = x * lax.rsqrt(jnp.mean(x * x, axis=-1, keepdims=True) + EPS) * g_ref[...]
    o_ref[...] = (y * (1.0 + scale_ref[...]) + shift_ref[...]).astype(BF16)


def _norm_mod(xp, xs, norm_g3, mod4, layer, tm=512):
    return pl.pallas_call(
        functools.partial(_norm_kernel, tm=tm),
        grid=(TOKENS // tm,),
        in_specs=[*_split_specs(tm, D_MODEL),
                  pl.BlockSpec((None, 1, D_MODEL), lambda i: (layer, 0, 0)),
                  pl.BlockSpec((None, None, 1, D_MODEL), lambda i: (layer, _cond_row(i * tm), 0, 0)),
                  pl.BlockSpec((None, None, 1, D_MODEL), lambda i: (layer, _cond_row(i * tm), 0, 1))],
        out_specs=pl.BlockSpec((tm, D_MODEL), lambda i: (i, 0)),
        out_shape=jax.ShapeDtypeStruct((TOKENS, D_MODEL), BF16),
        compiler_params=_params(1),
        name="norm_mod",
    )(xp, xs, norm_g3, mod4, mod4)


PROJ_TM = 2048
PROJ_TN = 512
PROJ_TN_WIDE = 1024
PROJ_ROWS = 512
N_PROMPT_TILES = PROMPT_TOKENS // PROJ_TM


class _Segment(NamedTuple):
    col_blk0: int
    n_blk: int
    epilogue: Callable
    by_tile_kind: bool
    outs: tuple


def _segment_steps(segments, wide):
    spans, j0 = [], 0
    for seg in segments:
        assert seg.col_blk0 % wide == 0 and seg.n_blk % wide == 0
        spans.append((j0, seg.n_blk // wide))
        j0 += seg.n_blk // wide
    return spans


def _proj_kernel(*refs, segments, spans, n_extra, n_out, cast_steps):
    a_ref, w_ref = refs[0], refs[1]
    extras = refs[2:2 + n_extra]
    outs = refs[2 + n_extra:2 + n_extra + n_out]
    wbf_ref = refs[2 + n_extra + n_out]

    @pl.when(pl.program_id(1) == 0)
    def _():
        wbf_ref[...] = w_ref[...].astype(BF16)

    if cast_steps:
        step = pl.program_id(0) * pl.num_programs(1) + pl.program_id(1)

        @pl.when(step < cast_steps)
        def _():
            outs[-1][...] = extras[-1][...].astype(BF16)

    def run(seg):
        seg_outs = [outs[k] for k in seg.outs]

        def body(latent):
            for rc in range(PROJ_TM // PROJ_ROWS):
                rows = slice(rc * PROJ_ROWS, (rc + 1) * PROJ_ROWS)
                for col0 in range(0, wbf_ref.shape[1], PROJ_TN):
                    acc = jnp.dot(a_ref[rows, :], wbf_ref[:, col0:col0 + PROJ_TN],
                                  preferred_element_type=F32)
                    seg.epilogue(acc, extras, seg_outs, rc, col0, latent)

        if seg.by_tile_kind:
            is_latent = pl.program_id(1) >= N_PROMPT_TILES
            pl.when(is_latent)(lambda: body(True))
            pl.when(jnp.logical_not(is_latent))(lambda: body(False))
        else:
            body(None)

    j = pl.program_id(0)
    for seg, (j0, nj) in zip(segments, spans):
        if len(segments) == 1:
            run(seg)
        else:
            pl.when(jnp.logical_and(j >= j0, j < j0 + nj))(functools.partial(run, seg))


def _proj(a, w, layer, segments, out_shapes, out_specs, name, extras=(), extra_specs=(), tn=PROJ_TN,
          run_after=None, cast=None):
    m, k = a.shape
    if run_after is not None:
        extras = [*extras, run_after]
        extra_specs = [*extra_specs, pl.BlockSpec(memory_space=pl.ANY)]
    wide = tn // PROJ_TN
    spans = _segment_steps(segments, wide)
    cast_steps = 0
    if cast is not None:
        w_out, out_layer, cast_steps = cast
        n_tiles = m // PROJ_TM
        assert cast_steps <= n_tiles * sum(nj for _, nj in spans)
        w_spec, wbf_spec, wbf_shape = _cast_slab_specs(
            w_out, out_layer, cast_steps, lambda j, i: jnp.minimum(j * n_tiles + i, cast_steps - 1))
        extras, extra_specs = [*extras, w_out], [*extra_specs, w_spec]
        out_shapes, out_specs = [*out_shapes, wbf_shape], [*out_specs, wbf_spec]

    def w_col(j):
        col = None
        for seg, (j0, _) in zip(segments, spans):
            c = seg.col_blk0 // wide + j - j0
            col = c if col is None else jnp.where(j >= j0, c, col)
        return col

    kern = functools.partial(_proj_kernel, segments=segments, spans=spans, n_extra=len(extras),
                             n_out=len(out_shapes), cast_steps=cast_steps)
    return pl.pallas_call(
        kern,
        grid=(sum(nj for _, nj in spans), m // PROJ_TM),
        in_specs=[pl.BlockSpec((PROJ_TM, k), lambda j, i: (i, 0)),
                  pl.BlockSpec((None, k, tn), lambda j, i: (layer, 0, w_col(j))),
                  *extra_specs],
        out_specs=out_specs,
        out_shape=out_shapes,
        scratch_shapes=[pltpu.VMEM((k, tn), BF16)],
        compiler_params=_params(2),
        name=name,
    )(a, w, *extras)


def _full_out(n_blk):
    return ([jax.ShapeDtypeStruct((TOKENS, n_blk * PROJ_TN), BF16)],
            [pl.BlockSpec((PROJ_TM, PROJ_TN_WIDE), lambda j, i: (i, j))])


def _held(span, j, inside, before, after):
    j0, nj = span
    pick = lambda a, b, c: jnp.where(j < j0, b, jnp.where(j >= j0 + nj, c, a))
    return tuple(pick(a, b, c) for a, b, c in zip(inside, before, after))


def _split_out(span, n_blk, head_rows):
    prompt_tile = lambda i: jnp.minimum(i, N_PROMPT_TILES - 1)
    last_prompt, last_tile = N_PROMPT_TILES - 1, TOKENS // PROJ_TM - 1
    jj = lambda j: j - span[0]
    if head_rows:
        assert n_blk == 1
        heads = PROJ_TN // HEAD_DIM
        cache_shape = jax.ShapeDtypeStruct((PROMPT_TOKENS * heads, HEAD_DIM), F32)
        cache_spec = pl.BlockSpec(
            (PROJ_TM * heads, HEAD_DIM),
            lambda j, i: _held(span, j, (prompt_tile(i), 0), (0, 0), (last_prompt, 0)))
    else:
        cache_shape = jax.ShapeDtypeStruct((N_PROMPT, PROMPT_SEQ, n_blk * PROJ_TN), F32)
        cache_spec = pl.BlockSpec(
            (PROJ_TM // PROMPT_SEQ, PROMPT_SEQ, PROJ_TN),
            lambda j, i: _held(span, j, (prompt_tile(i), 0, jj(j)), (0, 0, 0), (last_prompt, 0, n_blk - 1)))
    shapes = [cache_shape, jax.ShapeDtypeStruct((TOKENS, n_blk * PROJ_TN), BF16)]
    specs = [cache_spec,
             pl.BlockSpec((PROJ_TM, PROJ_TN),
                          lambda j, i: _held(span, j, (i, jj(j)), (0, 0), (last_tile, n_blk - 1)))]
    return shapes, specs


def _chunk_rows(rc):
    return slice(rc * PROJ_ROWS, (rc + 1) * PROJ_ROWS)


def _epi_silu(acc, extras, outs, rc, col0, latent):
    outs[0][_chunk_rows(rc), col0:col0 + PROJ_TN] = _silu(acc).astype(BF16)


def _epi_gelu(acc, extras, outs, rc, col0, latent):
    outs[0][_chunk_rows(rc), col0:col0 + PROJ_TN] = _gelu_tanh(acc).astype(BF16)


def _store_split(y, outs, rc, sl, latent):
    outs[1][_chunk_rows(rc), sl] = y.astype(BF16)
    if latent:
        return
    if len(outs[0].shape) == 2:
        heads = PROJ_TN // HEAD_DIM
        col0 = sl.start or 0
        for c in range(y.shape[1] // HEAD_DIM):
            head = col0 // HEAD_DIM + c
            rows = pl.ds(rc * PROJ_ROWS * heads + head, PROJ_ROWS, stride=heads)
            outs[0][rows, :] = y[:, c * HEAD_DIM:(c + 1) * HEAD_DIM]
    else:
        seqs = PROJ_ROWS // PROMPT_SEQ
        outs[0][rc * seqs:(rc + 1) * seqs, :, sl] = y.reshape(seqs, PROMPT_SEQ, y.shape[1])


def _epi_value(acc, extras, outs, rc, col0, latent):
    _store_split(acc, outs, rc, slice(col0, col0 + PROJ_TN), latent)


def _head_rmsnorm(acc, hh, g):
    a = acc[:, hh * HEAD_DIM:(hh + 1) * HEAD_DIM]
    return a * lax.rsqrt(jnp.mean(a * a, axis=-1, keepdims=True) + EPS) * g


def _normed_heads(acc, extras, rc, rope):
    g = extras[0][...]
    ys = [_head_rmsnorm(acc, hh, g) for hh in range(PROJ_TN // HEAD_DIM)]
    if rope:
        rows = _chunk_rows(rc)
        cos, sin, swap = extras[1][rows, :], extras[2][rows, :], extras[3][...]
        for pair in range(len(ys) // 2):
            both = jnp.concatenate(ys[2 * pair:2 * pair + 2], axis=1).astype(BF16)
            partner = jnp.dot(both, swap, preferred_element_type=F32)
            for t in range(2):
                hh = 2 * pair + t
                ys[hh] = ys[hh] * cos + partner[:, t * HEAD_DIM:(t + 1) * HEAD_DIM] * sin
    for hh, y in enumerate(ys):
        yield slice(hh * HEAD_DIM, (hh + 1) * HEAD_DIM), y


def _epi_query(acc, extras, outs, rc, col0, latent, *, rope):
    for sl, y in _normed_heads(acc, extras, rc, rope and latent):
        outs[0][_chunk_rows(rc), col0 + sl.start:col0 + sl.stop] = (y * QUERY_SCALE).astype(BF16)


def _epi_key(acc, extras, outs, rc, col0, latent, *, rope):
    for sl, y in _normed_heads(acc, extras, rc, rope and latent):
        _store_split(y, outs, rc, slice(col0 + sl.start, col0 + sl.stop), latent)


def _rope_tables():
    nf = HEAD_DIM // 4
    t = np.arange(SAMPLE_TOKENS) % SAMPLE_SEQ
    row = (t // GRID_W).astype(np.float32)
    col = (t % GRID_W).astype(np.float32)
    inv = np.float32(ROPE_THETA) ** (-np.arange(nf, dtype=np.float32) / np.float32(nf))
    ang_r = row[:, None] * inv
    ang_c = col[:, None] * inv
    cos = np.concatenate([np.cos(ang_r), np.cos(ang_r), np.cos(ang_c), np.cos(ang_c)], axis=1)
    sin = np.concatenate([-np.sin(ang_r), np.sin(ang_r), -np.sin(ang_c), np.sin(ang_c)], axis=1)
    lanes = np.arange(2 * HEAD_DIM)
    swap = lanes[:, None] == (lanes[None, :] ^ (HEAD_DIM // 4))
    return jnp.asarray(cos, F32), jnp.asarray(sin, F32), jnp.asarray(swap, BF16)


def _gain_extras(gain, rope_tabs):
    extras = [gain.reshape(1, HEAD_DIM)]
    specs = [pl.BlockSpec((1, HEAD_DIM), lambda j, i: (0, 0))]
    if rope_tabs is not None:
        tab_map = lambda j, i: (jnp.maximum(i - N_PROMPT_TILES, 0), 0)
        extras += list(rope_tabs)
        specs += [pl.BlockSpec((PROJ_TM, HEAD_DIM), tab_map)] * 2
        specs.append(pl.BlockSpec((2 * HEAD_DIM, 2 * HEAD_DIM), lambda j, i: (0, 0)))
    return extras, specs


def _qz_proj(hn, w, layer, q_blk0, z_blk0, n_blk, gain, rope_tabs, name):
    extras, specs = _gain_extras(gain, rope_tabs)
    rope = rope_tabs is not None
    segments = (_Segment(q_blk0, n_blk, functools.partial(_epi_query, rope=rope), rope, (0,)),
                _Segment(z_blk0, n_blk, _epi_silu, False, (0,)))
    return _proj(hn, w, layer, segments, *_full_out(2 * n_blk), name, extras, specs, tn=PROJ_TN_WIDE)[0]


def _kv_proj(hn, w, layer, k_blk0, n_blk, gain, rope_tabs, head_rows, name, run_after=None):
    extras, specs = _gain_extras(gain, rope_tabs)
    segments = (_Segment(k_blk0, n_blk, functools.partial(_epi_key, rope=rope_tabs is not None), True, (0, 1)),
                _Segment(k_blk0 + n_blk, n_blk, _epi_value, True, (2, 3)))
    shapes, out_specs = [], []
    for span in _segment_steps(segments, 1):
        seg_shapes, seg_specs = _split_out(span, n_blk, head_rows)
        shapes += seg_shapes
        out_specs += seg_specs
    return _proj(hn, w, layer, segments, shapes, out_specs, name, extras, specs, run_after=run_after)


UVZ_CAST_STEPS = 32


def _uvz_proj(hn, w, layer, n_blk, name, w_out):
    segments = (_Segment(0, 2 * n_blk, _epi_gelu, False, (0,)),
                _Segment(2 * n_blk, n_blk, _epi_silu, False, (0,)))
    return _proj(hn, w, layer, segments, *_full_out(3 * n_blk), name, tn=PROJ_TN_WIDE,
                 cast=(w_out, layer, UVZ_CAST_STEPS))


OUT_COLS = 512


def _cast_slab_specs(w, layer, n_slabs, slab_of_step):
    k, n = w.shape[1], w.shape[2]
    rows = k // n_slabs
    return (pl.BlockSpec((None, rows, n), lambda *g: (layer, slab_of_step(*g), 0)),
            pl.BlockSpec((rows, n), lambda *g: (slab_of_step(*g), 0)),
            jax.ShapeDtypeStruct((k, n), BF16))


def _out_kernel(*refs, tile0, tm, split_a, split_x, fuse_norm):
    it = iter(refs)
    a_refs = [next(it), next(it)] if split_a else [next(it)]
    wbf_ref = next(it)
    x_refs = [next(it), next(it)] if split_x else [next(it)]
    gate_ref = next(it)
    if fuse_norm:
        g_ref, shift_ref, scale_ref = next(it), next(it), next(it)
    xnew_ref = next(it)
    hn_ref = next(it) if fuse_norm else None

    i = pl.program_id(0)
    is_prompt = i + tile0 < PROMPT_TOKENS // tm
    a = jnp.where(is_prompt, a_refs[0][...], a_refs[1][...]) if split_a else a_refs[0][...]
    ssq = jnp.zeros((tm, 1), F32)
    for cb in range(D_MODEL // OUT_COLS):
        sl = slice(cb * OUT_COLS, (cb + 1) * OUT_COLS)
        acc = jnp.dot(a, wbf_ref[:, sl], preferred_element_type=F32)
        if split_x:
            x = jnp.where(is_prompt, x_refs[0][:, sl], x_refs[1][:, sl])
        else:
            x = x_refs[0][:, sl]
        xn = x + gate_ref[:, sl] * acc
        xnew_ref[:, sl] = xn
        if fuse_norm:
            ssq = ssq + jnp.sum(xn * xn, axis=-1, keepdims=True)
    if fuse_norm:
        rs = lax.rsqrt(ssq * (1.0 / D_MODEL) + EPS)
        for cb in range(D_MODEL // OUT_COLS):
            sl = slice(cb * OUT_COLS, (cb + 1) * OUT_COLS)
            gmul = g_ref[:, sl] * (1.0 + scale_ref[:, sl])
            hn_ref[:, sl] = (xnew_ref[:, sl] * rs * gmul + shift_ref[:, sl]).astype(BF16)


def _out_proj(a, wbf, x, mod4, layer, norm_g3, name, tm, rows=None):
    split_a = isinstance(a, tuple)
    k = a[0].shape[1] if split_a else a.shape[1]
    tok0, n_tok = rows if rows is not None else (0, TOKENS)
    tile0 = tok0 // tm
    split_x = isinstance(x, tuple)
    fuse_norm = norm_g3 is not None
    mod_spec = lambda part, lyr: pl.BlockSpec(
        (None, None, 1, D_MODEL), lambda i: (lyr, _cond_row((i + tile0) * tm), 0, part))
    row_spec = lambda width: pl.BlockSpec((tm, width), lambda i: (i + tile0, 0))
    operands = [*a, wbf] if split_a else [a, wbf]
    in_specs = [*(_split_specs(tm, k, tile0) if split_a else [row_spec(k)]),
                pl.BlockSpec((k, D_MODEL), lambda i: (0, 0))]
    if split_x:
        assert rows is None
        operands += list(x)
        in_specs += _split_specs(tm, D_MODEL)
    else:
        operands.append(x)
        in_specs.append(row_spec(D_MODEL))
    operands.append(mod4)
    in_specs.append(mod_spec(2, layer))
    out_shapes = [jax.ShapeDtypeStruct((n_tok, D_MODEL), F32)]
    out_specs = [pl.BlockSpec((tm, D_MODEL), lambda i: (i, 0))]
    if fuse_norm:
        operands += [norm_g3, mod4, mod4]
        in_specs += [pl.BlockSpec((None, 1, D_MODEL), lambda i: (layer + 1, 0, 0)),
                     mod_spec(0, layer + 1), mod_spec(1, layer + 1)]
        out_shapes.append(jax.ShapeDtypeStruct((n_tok, D_MODEL), BF16))
        out_specs.append(pl.BlockSpec((tm, D_MODEL), lambda i: (i, 0)))
    kern = functools.partial(_out_kernel, tile0=tile0, tm=tm,
                             split_a=split_a, split_x=split_x, fuse_norm=fuse_norm)
    return pl.pallas_call(
        kern,
        grid=(n_tok // tm,),
        in_specs=in_specs,
        out_specs=out_specs,
        out_shape=out_shapes,
        compiler_params=_params(1),
        name=name,
    )(*operands)


def _dot_nt(a, b):
    return lax.dot_general(a, b, (((1,), (1,)), ((), ())), preferred_element_type=F32)


def _head(ref, h, rows=slice(None)):
    return ref[rows, h * HEAD_DIM:(h + 1) * HEAD_DIM]


def _with_ones(v):
    return jnp.concatenate([v, jnp.ones(v.shape, v.dtype)], axis=1)


def _lane_chunks(s):
    return [s[:, c * HEAD_DIM:(c + 1) * HEAD_DIM] for c in range(s.shape[1] // HEAD_DIM)]


def _softmax_pv(score_blocks, value_blocks, sink):
    rows = score_blocks[0].shape[0]
    mx = functools.reduce(jnp.maximum, [c for s in score_blocks for c in _lane_chunks(s)])
    m = jnp.broadcast_to(jnp.max(mx, axis=-1, keepdims=True), (rows, HEAD_DIM))
    if sink is not None:
        m = jnp.maximum(m, sink)
    o = None
    for s, v in zip(score_blocks, value_blocks):
        p = jnp.concatenate([jnp.exp2(c - m) for c in _lane_chunks(s)], axis=1).astype(BF16)
        part = jnp.dot(p, v, preferred_element_type=F32)
        o = part if o is None else o + part
    den = o[:, HEAD_DIM:]
    if sink is not None:
        den = den + jnp.exp2(sink - m)
    return o[:, :HEAD_DIM] * (1.0 / den)


def _attn_prompt_kernel(sink_ref, q_ref, k_ref, v_ref, z_ref, *rest, n_heads, n_kv, use_sink):
    o_ref = rest[-1]
    grp = n_heads // n_kv
    for g in range(n_kv):
        kg = _head(k_ref, g)
        vg = _with_ones(_head(v_ref, g))
        for h in range(g * grp, (g + 1) * grp):
            s = _dot_nt(_head(q_ref, h), kg)
            sink = jnp.full((PROMPT_SEQ, HEAD_DIM), sink_ref[h], F32) if use_sink else None
            o = _softmax_pv([s], [vg], sink)
            o_ref[:, h * HEAD_DIM:(h + 1) * HEAD_DIM] = (o * _head(z_ref, h).astype(F32)).astype(BF16)


def _attn_prompt(sink, qz, k, v, n_kv, use_sink, name, run_after=None):
    ordering = [] if run_after is None else [run_after]
    width = qz.shape[1] // 2
    kern = functools.partial(_attn_prompt_kernel, n_heads=width // HEAD_DIM, n_kv=n_kv, use_sink=use_sink)
    cache_spec = pl.BlockSpec((PROMPT_SEQ, k.shape[1]), lambda b: (b, 0))
    return pl.pallas_call(
        kern,
        grid=(N_PROMPT,),
        in_specs=[pl.BlockSpec(memory_space=pltpu.SMEM),
                  pl.BlockSpec((PROMPT_SEQ, width), lambda b: (b, 0)),
                  cache_spec, cache_spec,
                  pl.BlockSpec((PROMPT_SEQ, width), lambda b: (b, 1)),
                  *[pl.BlockSpec(memory_space=pl.ANY) for _ in ordering]],
        out_specs=pl.BlockSpec((PROMPT_SEQ, width), lambda b: (b, 0)),
        out_shape=jax.ShapeDtypeStruct((PROMPT_TOKENS, width), BF16),
        compiler_params=_params(1),
        name=name,
    )(sink, qz, k, v, qz, *ordering)


WIN_BAND = 3 * WINDOW


def _attn_win_kernel(sink_ref, q_ref, k_ref, v_ref, kc_ref, vc_ref, z_ref, w_ref, o_ref, wbf_ref, *,
                     n_heads, n_kv):
    wbf_ref[...] = w_ref[...].astype(BF16)
    grp = n_heads // n_kv
    n = pl.program_id(1)
    start = pl.multiple_of(jnp.clip((n - 1) * WINDOW, 0, SAMPLE_SEQ - WIN_BAND), WINDOW)
    shape = (grp * WINDOW, WIN_BAND)
    qpos = n * WINDOW + (lax.broadcasted_iota(jnp.int32, shape, 0) & (WINDOW - 1))
    kpos = start + lax.broadcasted_iota(jnp.int32, shape, 1)
    valid = jnp.abs(kpos - qpos) <= WINDOW
    band = pl.ds(start, WIN_BAND)
    for g in range(n_kv):
        heads = [g * grp + t for t in range(grp)]
        qs = jnp.concatenate([_head(q_ref, h) for h in heads], axis=0)
        kb = _head(k_ref, g, band)
        vb = _with_ones(_head(v_ref, g, band))
        ctx_rows = pl.ds(g, PAST_LEN, stride=n_kv)
        kc = kc_ref[ctx_rows, :].astype(BF16)
        vc = _with_ones(vc_ref[ctx_rows, :].astype(BF16))
        s_band = jnp.where(valid, _dot_nt(qs, kb), NEG_INF)
        s_ctx = _dot_nt(qs, kc)
        sink = jnp.concatenate([jnp.full((WINDOW, HEAD_DIM), sink_ref[h], F32) for h in heads], axis=0)
        o = _softmax_pv([s_band, s_ctx], [vb, vc], sink)
        for t, h in enumerate(heads):
            oh = o[t * WINDOW:(t + 1) * WINDOW]
            o_ref[:, h * HEAD_DIM:(h + 1) * HEAD_DIM] = (oh * _head(z_ref, h).astype(F32)).astype(BF16)


def _attn_win(sink, qz, k, v, cache_k, cache_v, layer_in_kind, n_kv, w_out):
    width = qz.shape[1] // 2
    kv_width = k.shape[1]
    blocks_per_seq = SAMPLE_SEQ // WINDOW
    q_map = lambda b, n: (PROMPT_TOKENS // WINDOW + b * blocks_per_seq + n, 0)
    z_map = lambda b, n: (PROMPT_TOKENS // WINDOW + b * blocks_per_seq + n, 1)
    kv_map = lambda b, n: (PROMPT_TOKENS // SAMPLE_SEQ + b, 0)
    cache_map = lambda b, n: (b, layer_in_kind, 0, 0)
    kern = functools.partial(_attn_win_kernel, n_heads=width // HEAD_DIM, n_kv=n_kv)
    w_spec, wbf_spec, wbf_shape = _cast_slab_specs(w_out, layer_in_kind, N_SAMPLE * blocks_per_seq,
                                                   lambda b, n: b * blocks_per_seq + n)
    return pl.pallas_call(
        kern,
        grid=(N_SAMPLE, blocks_per_seq),
        in_specs=[pl.BlockSpec(memory_space=pltpu.SMEM),
                  pl.BlockSpec((WINDOW, width), q_map),
                  pl.BlockSpec((SAMPLE_SEQ, kv_width), kv_map),
                  pl.BlockSpec((SAMPLE_SEQ, kv_width), kv_map),
                  pl.BlockSpec((None, None, PAST_LEN * n_kv, HEAD_DIM), cache_map),
                  pl.BlockSpec((None, None, PAST_LEN * n_kv, HEAD_DIM), cache_map),
                  pl.BlockSpec((WINDOW, width), z_map),
                  w_spec],
        out_specs=[pl.BlockSpec((WINDOW, width), lambda b, n: (b * blocks_per_seq + n, 0)), wbf_spec],
        out_shape=[jax.ShapeDtypeStruct((SAMPLE_TOKENS, width), BF16), wbf_shape],
        compiler_params=_params(2),
        name="attn_win_latent",
    )(sink, qz, k, v, cache_k, cache_v, qz, w_out)


NAT_QROWS = 4
GRID_ROWS = SAMPLE_SEQ // GRID_W
N_DR = 2 * NAT_ROWS - 1
MASKED_TILE = N_DR


def _nat_row_start(qr):
    return min(max(qr - NAT_ROWS // 2, 0), GRID_ROWS - NAT_ROWS)


def _attn_nat_kernel(bias_ref, q_ref, k_ref, v_ref, kc_ref, vc_ref, z_ref, w_ref, o_ref, wbf_ref,
                     left_ref, right_ref):
    wbf_ref[...] = w_ref[...].astype(BF16)

    def build_bias_tiles():
        shape = (GRID_W, 2 * GRID_W)
        lane = lax.broadcasted_iota(jnp.int32, shape, 1)
        qc = lax.broadcasted_iota(jnp.int32, shape, 0)
        kc_ = lane & (GRID_W - 1)
        cs = jnp.clip(qc - NAT_COLS // 2, 0, GRID_W - NAT_COLS)
        col_ok = jnp.logical_and(kc_ >= cs, kc_ < cs + NAT_COLS)
        is_left = lane < GRID_W
        lanes = 2 * GRID_W
        for dri in range(N_DR):
            row = jnp.broadcast_to(bias_ref[dri:dri + 1, :], shape)
            on_left = pltpu.roll(row, lanes - (NAT_COLS - 1), 1, stride=1, stride_axis=0)
            on_right = pltpu.roll(row, GRID_W - (NAT_COLS - 1), 1, stride=1, stride_axis=0)
            left_ref[dri] = jnp.where(is_left, jnp.where(col_ok, on_left, NEG_INF), 0.0)
            right_ref[dri] = jnp.where(is_left, 0.0, jnp.where(col_ok, on_right, NEG_INF))
        left_ref[MASKED_TILE] = jnp.where(is_left, NEG_INF, 0.0)
        right_ref[MASKED_TILE] = jnp.where(is_left, 0.0, NEG_INF)

    build_bias_tiles()
    n_heads = kc_ref.shape[1] // PAST_LEN
    ctx_rows = pl.ds(pl.program_id(0), PAST_LEN, stride=n_heads)
    for b, qb in [(b, qb) for b in range(N_SAMPLE) for qb in range(GRID_ROWS // NAT_QROWS)]:
        if qb == 0:
            kc = kc_ref[b, ctx_rows, :].astype(BF16)
            vc = _with_ones(vc_ref[b, ctx_rows, :].astype(BF16))
        qrows = range(qb * NAT_QROWS, (qb + 1) * NAT_QROWS)
        krow0 = _nat_row_start(qrows[0]) // 2 * 2
        krow1 = -(-(_nat_row_start(qrows[-1]) + NAT_ROWS) // 2) * 2
        tok0 = b * SAMPLE_SEQ
        kwin = slice(tok0 + krow0 * GRID_W, tok0 + krow1 * GRID_W)
        qwin = slice(tok0 + qrows[0] * GRID_W, tok0 + (qrows[-1] + 1) * GRID_W)

        def tile_index(qr, kr):
            rs = _nat_row_start(qr)
            return kr - qr + (NAT_ROWS - 1) if rs <= kr < rs + NAT_ROWS else MASKED_TILE

        bias = jnp.concatenate(
            [jnp.concatenate([left_ref[tile_index(qr, kr)] + right_ref[tile_index(qr, kr + 1)]
                              for kr in range(krow0, krow1, 2)], axis=1)
             for qr in qrows], axis=0)

        q = q_ref[qwin, :]
        s_nb = _dot_nt(q, k_ref[kwin, :]) + bias
        s_ctx = _dot_nt(q, kc)
        o = _softmax_pv([s_nb, s_ctx], [_with_ones(v_ref[kwin, :]), vc], None)
        o_ref[qwin, :] = (o * z_ref[qwin, :].astype(F32)).astype(BF16)


def _attn_nat(rel_bias, qz, k, v, cache_k, cache_v, layer_in_kind, w_out):
    n_heads = qz.shape[1] // (2 * HEAD_DIM)
    bias_rows = jnp.pad(rel_bias, ((0, 0), (0, N_DR + 1 - rel_bias.shape[1]),
                                   (0, 2 * GRID_W - rel_bias.shape[2])))
    latent_tile = PROMPT_TOKENS // SAMPLE_TOKENS
    latent_spec = pl.BlockSpec((SAMPLE_TOKENS, HEAD_DIM), lambda h: (latent_tile, h))
    z_spec = pl.BlockSpec((SAMPLE_TOKENS, HEAD_DIM), lambda h: (latent_tile, n_heads + h))
    cache_spec = pl.BlockSpec((N_SAMPLE, None, PAST_LEN * n_heads, HEAD_DIM),
                              lambda h: (0, layer_in_kind, 0, 0))
    w_spec, wbf_spec, wbf_shape = _cast_slab_specs(w_out, layer_in_kind, n_heads, lambda h: h)
    return pl.pallas_call(
        _attn_nat_kernel,
        grid=(n_heads,),
        in_specs=[pl.BlockSpec((None, N_DR + 1, 2 * GRID_W), lambda h: (h, 0, 0)),
                  latent_spec, latent_spec, latent_spec, cache_spec, cache_spec, z_spec, w_spec],
        out_specs=[pl.BlockSpec((SAMPLE_TOKENS, HEAD_DIM), lambda h: (0, h)), wbf_spec],
        out_shape=[jax.ShapeDtypeStruct((SAMPLE_TOKENS, n_heads * HEAD_DIM), BF16), wbf_shape],
        scratch_shapes=[pltpu.VMEM((N_DR + 1, GRID_W, 2 * GRID_W), F32),
                        pltpu.VMEM((N_DR + 1, GRID_W, 2 * GRID_W), F32)],
        compiler_params=_params(1),
        name="attn_nat_latent",
    )(bias_rows, qz, k, v, cache_k, cache_v, qz, w_out)


SPATIAL_TOKENS = 2 * CHUNK


def _spatial_kernel(u_ref, v_ref, z_ref, g_ref, b_ref, ws_ref, bs_ref, o_ref):
    for c in range(SPATIAL_TOKENS // CHUNK):
        rows = slice(c * CHUNK, (c + 1) * CHUNK)
        v = v_ref[rows, :].astype(F32)
        mu = jnp.mean(v, axis=-1, keepdims=True)
        vc = v - mu
        var = jnp.mean(vc * vc, axis=-1, keepdims=True)
        vn = (vc * lax.rsqrt(var + EPS) * g_ref[...] + b_ref[...]).astype(BF16)
        for g in range(GMLP_GROUPS):
            sl = slice(g * GMLP_GROUP_WIDTH, (g + 1) * GMLP_GROUP_WIDTH)
            sv = jnp.dot(ws_ref[g], vn[:, sl], preferred_element_type=F32) + bs_ref[:, g:g + 1]
            o_ref[rows, sl] = u_ref[rows, sl] * sv.astype(BF16) * z_ref[rows, sl]


def _spatial(uvz, ln_g, ln_b, w_s, b_s):
    part = lambda p: pl.BlockSpec((SPATIAL_TOKENS, GMLP_WIDTH), lambda i: (i, p))
    row = part(0)
    vec = pl.BlockSpec((1, GMLP_WIDTH), lambda i: (0, 0))
    return pl.pallas_call(
        _spatial_kernel,
        grid=(TOKENS // SPATIAL_TOKENS,),
        in_specs=[part(0), part(1), part(2), vec, vec,
                  pl.BlockSpec((GMLP_GROUPS, CHUNK, CHUNK), lambda i: (0, 0, 0)),
                  pl.BlockSpec((CHUNK, GMLP_GROUPS), lambda i: (0, 0))],
        out_specs=row,
        out_shape=jax.ShapeDtypeStruct((TOKENS, GMLP_WIDTH), BF16),
        compiler_params=_params(1),
        name="gmlp_spatial",
    )(uvz, uvz, uvz, ln_g.reshape(1, -1), ln_b.reshape(1, -1), w_s.astype(BF16), b_s.T)


def kernel(x_prompt, x_sample, cache_win_k, cache_win_v, cache_nat_k, cache_nat_v, c, c_ctx,
           norm_g, w_ada, b_ada,
           win_w_in, win_q_norm, win_k_norm, win_sink, win_w_out,
           nat_w_in, nat_q_norm, nat_k_norm, nat_rel_bias, nat_w_out,
           gmlp_w_in, gmlp_ln_g, gmlp_ln_b, gmlp_w_s, gmlp_b_s, gmlp_w_out):
    depth = norm_g.shape[0]
    xp = x_prompt.reshape(PROMPT_TOKENS, D_MODEL)
    xs = x_sample.reshape(SAMPLE_TOKENS, D_MODEL)
    cond = jnp.zeros((COND_ROWS, D_MODEL), F32).at[:N_SAMPLE].set(c).at[CTX_COND_ROW].set(c_ctx)
    mod4 = _ada(cond, w_ada, b_ada).reshape(depth, COND_ROWS, 1, 3 * D_MODEL)
    norm_g3 = norm_g.reshape(depth, 1, D_MODEL)
    rope_tabs = _rope_tables()
    no_sink = jnp.zeros((1,), F32)

    n_win = win_w_in.shape[0]
    n_nat = nat_w_in.shape[0]
    win_kv_heads = cache_win_k.shape[3]
    win_kv_width = win_kv_heads * HEAD_DIM
    cwk = cache_win_k.reshape(N_SAMPLE, n_win, PAST_LEN * win_kv_heads, HEAD_DIM)
    cwv = cache_win_v.reshape(N_SAMPLE, n_win, PAST_LEN * win_kv_heads, HEAD_DIM)
    nat_heads = cache_nat_k.shape[3]
    cnk = cache_nat_k.reshape(N_SAMPLE, n_nat, PAST_LEN * nat_heads, HEAD_DIM)
    cnv = cache_nat_v.reshape(N_SAMPLE, n_nat, PAST_LEN * nat_heads, HEAD_DIM)

    new_win_k, new_win_v, new_nat_k, new_nat_v = [], [], [], []
    x = (xp, xs)
    hn = _norm_mod(xp, xs, norm_g3, mod4, 0)
    for layer in range(depth):
        kind = layer % 3
        li = layer // 3
        if kind == 0:
            nq = D_MODEL // PROJ_TN
            nkv = win_kv_width // PROJ_TN
            qz = _qz_proj(hn, win_w_in, li, 0, nq + 2 * nkv, nq, win_q_norm[li], rope_tabs, "win_qz")
            kp, ks, vp, vs = _kv_proj(hn, win_w_in, li, nq, nkv, win_k_norm[li], rope_tabs, True, "win_kv")
            new_win_k.append(kp)
            new_win_v.append(vp)
            sink = win_sink[li] * LOG2E
            a_latent, wbf = _attn_win(sink, qz, ks, vs, cwk, cwv, li, win_kv_heads, win_w_out)
            a = (_attn_prompt(sink, qz, ks, vs, win_kv_heads, True, "attn_win_prompt"), a_latent)
            out_tm, out_name = 512, "win_out"
        elif kind == 1:
            nq = D_MODEL // PROJ_TN
            qz = _qz_proj(hn, nat_w_in, li, 0, 3 * nq, nq, nat_q_norm[li], None, "nat_qz")
            kp, ks, vp, vs = _kv_proj(hn, nat_w_in, li, nq, nq, nat_k_norm[li], None, False, "nat_kv",
                                      run_after=qz)
            new_nat_k.append(kp)
            new_nat_v.append(vp)
            a_latent, wbf = _attn_nat(nat_rel_bias[li] * LOG2E, qz, ks, vs, cnk, cnv, li, nat_w_out)
            a = (_attn_prompt(no_sink, qz, ks, vs, nat_heads, False, "attn_nat_prompt", run_after=a_latent),
                 a_latent)
            out_tm, out_name = 512, "nat_out"
        else:
            nw = GMLP_WIDTH // PROJ_TN
            uvz, wbf = _uvz_proj(hn, gmlp_w_in, li, nw, "gmlp_uvz", gmlp_w_out)
            a = _spatial(uvz, gmlp_ln_g[li], gmlp_ln_b[li], gmlp_w_s[li], gmlp_b_s[li])
            out_tm, out_name = 256, "gmlp_out"
        if layer + 1 < depth:
            x, hn = _out_proj(a, wbf, x, mod4, layer, norm_g3, out_name, out_tm)
        else:
            (yp,) = _out_proj(a, wbf, x, mod4, layer, None, out_name + "_prompt", out_tm,
                              rows=(0, PROMPT_TOKENS))
            (ys,) = _out_proj(a, wbf, x, mod4, layer, None, out_name + "_latent", out_tm,
                              rows=(PROMPT_TOKENS, SAMPLE_TOKENS))

    cache_shape = lambda layers, heads: jnp.stack(
        [c_.reshape(N_PROMPT, PROMPT_SEQ, heads, HEAD_DIM) for c_ in layers], axis=1)
    return (yp.reshape(N_PROMPT, PROMPT_SEQ, D_MODEL), ys.reshape(N_SAMPLE, SAMPLE_SEQ, D_MODEL),
            cache_shape(new_win_k, win_kv_heads), cache_shape(new_win_v, win_kv_heads),
            cache_shape(new_nat_k, nat_heads), cache_shape(new_nat_v, nat_heads))
```

```python
import functools
import math
from typing import Callable, NamedTuple

import jax
import jax.numpy as jnp
import numpy as np
from jax import lax
from jax.experimental import pallas as pl
from jax.experimental.pallas import tpu as pltpu

F32 = jnp.float32
BF16 = jnp.bfloat16

D_MODEL = 2048
HEAD_DIM = 128
N_PROMPT = 16
PROMPT_SEQ = 256
N_SAMPLE = 2
SAMPLE_SEQ = 1024
PROMPT_TOKENS = N_PROMPT * PROMPT_SEQ
SAMPLE_TOKENS = N_SAMPLE * SAMPLE_SEQ
TOKENS = PROMPT_TOKENS + SAMPLE_TOKENS
PAST_LEN = 512
GRID_W = 64
EPS = 1e-6
NEG_INF = -1e30
ROPE_THETA = 10000.0
WINDOW = 128
NAT_ROWS = 8
NAT_COLS = 16
GMLP_WIDTH = 2 * D_MODEL
GMLP_GROUPS = 16
GMLP_GROUP_WIDTH = GMLP_WIDTH // GMLP_GROUPS
CHUNK = 128
CTX_COND_ROW = 2
COND_ROWS = 8
SM_SCALE = HEAD_DIM ** -0.5
LOG2E = math.log2(math.e)
QUERY_SCALE = SM_SCALE * LOG2E

VMEM_LIMIT = 56 * 1024 * 1024


def _params(n_axes, vmem=VMEM_LIMIT, flags=None):
    return pltpu.CompilerParams(dimension_semantics=("arbitrary",) * n_axes,
                                vmem_limit_bytes=vmem, flags=flags)


def _cond_row(tok0):
    return jnp.where(tok0 < PROMPT_TOKENS, CTX_COND_ROW, (tok0 - PROMPT_TOKENS) // SAMPLE_SEQ)


def _silu(x):
    return x * (0.5 + 0.5 * jnp.tanh(0.5 * x))


def _gelu_tanh(x):
    return 0.5 * x * (1.0 + jnp.tanh(math.sqrt(2.0 / math.pi) * (x + 0.044715 * (x * x * x))))


def _split_specs(tm, width, tile0=0):
    n_p = PROMPT_TOKENS // tm
    return [pl.BlockSpec((tm, width), lambda i: (jnp.minimum(i + tile0, n_p - 1), 0)),
            pl.BlockSpec((tm, width), lambda i: (jnp.maximum(i + tile0 - n_p, 0), 0))]


def _ada_kernel(cond_ref, w_ref, b_ref, o_ref):
    s = _silu(cond_ref[...]).astype(BF16)
    o_ref[...] = jnp.dot(s, w_ref[...].astype(BF16), preferred_element_type=F32) + b_ref[...]


def _ada(cond, w_ada, b_ada, tn=1024):
    depth = w_ada.shape[0]
    n = w_ada.shape[2]
    return pl.pallas_call(
        _ada_kernel,
        grid=(depth, n // tn),
        in_specs=[pl.BlockSpec((COND_ROWS, D_MODEL), lambda l, j: (0, 0)),
                  pl.BlockSpec((None, D_MODEL, tn), lambda l, j: (l, 0, j)),
                  pl.BlockSpec((None, 1, tn), lambda l, j: (l, 0, j))],
        out_specs=pl.BlockSpec((None, COND_ROWS, tn), lambda l, j: (l, 0, j)),
        out_shape=jax.ShapeDtypeStruct((depth, COND_ROWS, n), F32),
        compiler_params=_params(2),
        name="ada_mod",
    )(cond, w_ada, b_ada.reshape(depth, 1, n))


NORM_ROWS = 32


def _norm_kernel(xp_ref, xs_ref, g_ref, shift_ref, scale_ref, o_ref, *, tm):
    gmul = g_ref[...] * (1.0 + scale_ref[...])
    shift = shift_ref[...]

    def run(x_ref):
        def slab(r, carry):
            rows = pl.ds(pl.multiple_of(r * NORM_ROWS, NORM_ROWS), NORM_ROWS)
            x = x_ref[rows, :]
            rs = lax.rsqrt(jnp.mean(x * x, axis=-1, keepdims=True) + EPS)
            o_ref[rows, :] = (x * rs * gmul + shift).astype(BF16)
            return carry
        lax.fori_loop(0, tm // NORM_ROWS, slab, 0)

    is_prompt = pl.program_id(0) < PROMPT_TOKENS // tm
    pl.when(is_prompt)(lambda: run(xp_ref))
    pl.when(jnp.logical_not(is_prompt))(lambda: run(xs_ref))


def _norm_mod(xp, xs, norm_g3, mod4, layer, tm=512):
    return pl.pallas_call(
        functools.partial(_norm_kernel, tm=tm),
        grid=(TOKENS // tm,),
        in_specs=[*_split_specs(tm, D_MODEL),
                  pl.BlockSpec((None, 1, D_MODEL), lambda i: (layer, 0, 0)),
                  pl.BlockSpec((None, None, 1, D_MODEL), lambda i: (layer, _cond_row(i * tm), 0, 0)),
                  pl.BlockSpec((None, None, 1, D_MODEL), lambda i: (layer, _cond_row(i * tm), 0, 1))],
        out_specs=pl.BlockSpec((tm, D_MODEL), lambda i: (i, 0)),
        out_shape=jax.ShapeDtypeStruct((TOKENS, D_MODEL), BF16),
        compiler_params=_params(1),
        name="norm_mod",
    )(xp, xs, norm_g3, mod4, mod4)


PROJ_TM = 2048
PROJ_TN = 512
PROJ_TN_WIDE = 1024
PROJ_ROWS = 512
N_PROMPT_TILES = PROMPT_TOKENS // PROJ_TM


class _Segment(NamedTuple):
    col_blk0: int
    n_blk: int
    epilogue: Callable
    by_tile_kind: bool
    outs: tuple


def _segment_steps(segments, wide):
    spans, j0 = [], 0
    for seg in segments:
        assert seg.col_blk0 % wide == 0 and seg.n_blk % wide == 0
        spans.append((j0, seg.n_blk // wide))
        j0 += seg.n_blk // wide
    return spans


def _proj_kernel(*refs, segments, spans, n_extra, n_out, cast_steps):
    a_ref, w_ref = refs[0], refs[1]
    extras = refs[2:2 + n_extra]
    outs = refs[2 + n_extra:2 + n_extra + n_out]
    wbf_ref = refs[2 + n_extra + n_out]

    @pl.when(pl.program_id(1) == 0)
    def _():
        wbf_ref[...] = w_ref[...].astype(BF16)

    if cast_steps:
        step = pl.program_id(0) * pl.num_programs(1) + pl.program_id(1)

        @pl.when(step < cast_steps)
        def _():
            outs[-1][...] = extras[-1][...].astype(BF16)

    def run(seg):
        seg_outs = [outs[k] for k in seg.outs]

        def body(latent):
            for rc in range(PROJ_TM // PROJ_ROWS):
                rows = slice(rc * PROJ_ROWS, (rc + 1) * PROJ_ROWS)
                for col0 in range(0, wbf_ref.shape[1], PROJ_TN):
                    acc = jnp.dot(a_ref[rows, :], wbf_ref[:, col0:col0 + PROJ_TN],
                                  preferred_element_type=F32)
                    seg.epilogue(acc, extras, seg_outs, rc, col0, latent)

        if seg.by_tile_kind:
            is_latent = pl.program_id(1) >= N_PROMPT_TILES
            pl.when(is_latent)(lambda: body(True))
            pl.when(jnp.logical_not(is_latent))(lambda: body(False))
        else:
            body(None)

    j = pl.program_id(0)
    for seg, (j0, nj) in zip(segments, spans):
        if len(segments) == 1:
            run(seg)
        else:
            pl.when(jnp.logical_and(j >= j0, j < j0 + nj))(functools.partial(run, seg))


def _proj(a, w, layer, segments, out_shapes, out_specs, name, extras=(), extra_specs=(), tn=PROJ_TN,
          run_after=None, cast=None):
    m, k = a.shape
    if run_after is not None:
        extras = [*extras, run_after]
        extra_specs = [*extra_specs, pl.BlockSpec(memory_space=pl.ANY)]
    wide = tn // PROJ_TN
    spans = _segment_steps(segments, wide)
    cast_steps = 0
    if cast is not None:
        w_out, out_layer, cast_steps = cast
        n_tiles = m // PROJ_TM
        assert cast_steps <= n_tiles * sum(nj for _, nj in spans)
        w_spec, wbf_spec, wbf_shape = _cast_slab_specs(
            w_out, out_layer, cast_steps, lambda j, i: jnp.minimum(j * n_tiles + i, cast_steps - 1))
        extras, extra_specs = [*extras, w_out], [*extra_specs, w_spec]
        out_shapes, out_specs = [*out_shapes, wbf_shape], [*out_specs, wbf_spec]

    def w_col(j):
        col = None
        for seg, (j0, _) in zip(segments, spans):
            c = seg.col_blk0 // wide + j - j0
            col = c if col is None else jnp.where(j >= j0, c, col)
        return col

    kern = functools.partial(_proj_kernel, segments=segments, spans=spans, n_extra=len(extras),
                             n_out=len(out_shapes), cast_steps=cast_steps)
    return pl.pallas_call(
        kern,
        grid=(sum(nj for _, nj in spans), m // PROJ_TM),
        in_specs=[pl.BlockSpec((PROJ_TM, k), lambda j, i: (i, 0)),
                  pl.BlockSpec((None, k, tn), lambda j, i: (layer, 0, w_col(j))),
                  *extra_specs],
        out_specs=out_specs,
        out_shape=out_shapes,
        scratch_shapes=[pltpu.VMEM((k, tn), BF16)],
        compiler_params=_params(2),
        name=name,
    )(a, w, *extras)


def _full_out(n_blk):
    return ([jax.ShapeDtypeStruct((TOKENS, n_blk * PROJ_TN), BF16)],
            [pl.BlockSpec((PROJ_TM, PROJ_TN_WIDE), lambda j, i: (i, j))])


def _held(span, j, inside, before, after):
    j0, nj = span
    pick = lambda a, b, c: jnp.where(j < j0, b, jnp.where(j >= j0 + nj, c, a))
    return tuple(pick(a, b, c) for a, b, c in zip(inside, before, after))


def _split_out(span, n_blk, head_rows):
    prompt_tile = lambda i: jnp.minimum(i, N_PROMPT_TILES - 1)
    last_prompt, last_tile = N_PROMPT_TILES - 1, TOKENS // PROJ_TM - 1
    jj = lambda j: j - span[0]
    if head_rows:
        assert n_blk == 1
        heads = PROJ_TN // HEAD_DIM
        cache_shape = jax.ShapeDtypeStruct((PROMPT_TOKENS * heads, HEAD_DIM), F32)
        cache_spec = pl.BlockSpec(
            (PROJ_TM * heads, HEAD_DIM),
            lambda j, i: _held(span, j, (prompt_tile(i), 0), (0, 0), (last_prompt, 0)))
    else:
        cache_shape = jax.ShapeDtypeStruct((N_PROMPT, PROMPT_SEQ, n_blk * PROJ_TN), F32)
        cache_spec = pl.BlockSpec(
            (PROJ_TM // PROMPT_SEQ, PROMPT_SEQ, PROJ_TN),
            lambda j, i: _held(span, j, (prompt_tile(i), 0, jj(j)), (0, 0, 0), (last_prompt, 0, n_blk - 1)))
    shapes = [cache_shape, jax.ShapeDtypeStruct((TOKENS, n_blk * PROJ_TN), BF16)]
    specs = [cache_spec,
             pl.BlockSpec((PROJ_TM, PROJ_TN),
                          lambda j, i: _held(span, j, (i, jj(j)), (0, 0), (last_tile, n_blk - 1)))]
    return shapes, specs


def _chunk_rows(rc):
    return slice(rc * PROJ_ROWS, (rc + 1) * PROJ_ROWS)


def _epi_silu(acc, extras, outs, rc, col0, latent):
    outs[0][_chunk_rows(rc), col0:col0 + PROJ_TN] = _silu(acc).astype(BF16)


def _epi_gelu(acc, extras, outs, rc, col0, latent):
    outs[0][_chunk_rows(rc), col0:col0 + PROJ_TN] = _gelu_tanh(acc).astype(BF16)


def _store_split(y, outs, rc, sl, latent):
    outs[1][_chunk_rows(rc), sl] = y.astype(BF16)
    if latent:
        return
    if len(outs[0].shape) == 2:
        heads = PROJ_TN // HEAD_DIM
        col0 = sl.start or 0
        for c in range(y.shape[1] // HEAD_DIM):
            head = col0 // HEAD_DIM + c
            rows = pl.ds(rc * PROJ_ROWS * heads + head, PROJ_ROWS, stride=heads)
            outs[0][rows, :] = y[:, c * HEAD_DIM:(c + 1) * HEAD_DIM]
    else:
        seqs = PROJ_ROWS // PROMPT_SEQ
        outs[0][rc * seqs:(rc + 1) * seqs, :, sl] = y.reshape(seqs, PROMPT_SEQ, y.shape[1])


def _epi_value(acc, extras, outs, rc, col0, latent):
    _store_split(acc, outs, rc, slice(col0, col0 + PROJ_TN), latent)


def _head_rmsnorm(acc, hh, g):
    a = acc[:, hh * HEAD_DIM:(hh + 1) * HEAD_DIM]
    return a * lax.rsqrt(jnp.mean(a * a, axis=-1, keepdims=True) + EPS) * g


def _normed_heads(acc, extras, rc, rope):
    g = extras[0][...]
    ys = [_head_rmsnorm(acc, hh, g) for hh in range(PROJ_TN // HEAD_DIM)]
    if rope:
        rows = _chunk_rows(rc)
        cos, sin, swap = extras[1][rows, :], extras[2][rows, :], extras[3][...]
        for pair in range(len(ys) // 2):
            both = jnp.concatenate(ys[2 * pair:2 * pair + 2], axis=1).astype(BF16)
            partner = jnp.dot(both, swap, preferred_element_type=F32)
            for t in range(2):
                hh = 2 * pair + t
                ys[hh] = ys[hh] * cos + partner[:, t * HEAD_DIM:(t + 1) * HEAD_DIM] * sin
    for hh, y in enumerate(ys):
        yield slice(hh * HEAD_DIM, (hh + 1) * HEAD_DIM), y


def _epi_query(acc, extras, outs, rc, col0, latent, *, rope):
    for sl, y in _normed_heads(acc, extras, rc, rope and latent):
        outs[0][_chunk_rows(rc), col0 + sl.start:col0 + sl.stop] = (y * QUERY_SCALE).astype(BF16)


def _epi_key(acc, extras, outs, rc, col0, latent, *, rope):
    for sl, y in _normed_heads(acc, extras, rc, rope and latent):
        _store_split(y, outs, rc, slice(col0 + sl.start, col0 + sl.stop), latent)


def _rope_tables():
    nf = HEAD_DIM // 4
    t = np.arange(SAMPLE_TOKENS) % SAMPLE_SEQ
    row = (t // GRID_W).astype(np.float32)
    col = (t % GRID_W).astype(np.float32)
    inv = np.float32(ROPE_THETA) ** (-np.arange(nf, dtype=np.float32) / np.float32(nf))
    ang_r = row[:, None] * inv
    ang_c = col[:, None] * inv
    cos = np.concatenate([np.cos(ang_r), np.cos(ang_r), np.cos(ang_c), np.cos(ang_c)], axis=1)
    sin = np.concatenate([-np.sin(ang_r), np.sin(ang_r), -np.sin(ang_c), np.sin(ang_c)], axis=1)
    lanes = np.arange(2 * HEAD_DIM)
    swap = lanes[:, None] == (lanes[None, :] ^ (HEAD_DIM // 4))
    return jnp.asarray(cos, F32), jnp.asarray(sin, F32), jnp.asarray(swap, BF16)


def _gain_extras(gain, rope_tabs):
    extras = [gain.reshape(1, HEAD_DIM)]
    specs = [pl.BlockSpec((1, HEAD_DIM), lambda j, i: (0, 0))]
    if rope_tabs is not None:
        tab_map = lambda j, i: (jnp.maximum(i - N_PROMPT_TILES, 0), 0)
        extras += list(rope_tabs)
        specs += [pl.BlockSpec((PROJ_TM, HEAD_DIM), tab_map)] * 2
        specs.append(pl.BlockSpec((2 * HEAD_DIM, 2 * HEAD_DIM), lambda j, i: (0, 0)))
    return extras, specs


def _qz_proj(hn, w, layer, q_blk0, z_blk0, n_blk, gain, rope_tabs, name):
    extras, specs = _gain_extras(gain, rope_tabs)
    rope = rope_tabs is not None
    segments = (_Segment(q_blk0, n_blk, functools.partial(_epi_query, rope=rope), rope, (0,)),
                _Segment(z_blk0, n_blk, _epi_silu, False, (0,)))
    return _proj(hn, w, layer, segments, *_full_out(2 * n_blk), name, extras, specs, tn=PROJ_TN_WIDE)[0]


def _kv_proj(hn, w, layer, k_blk0, n_blk, gain, rope_tabs, head_rows, name, run_after=None):
    extras, specs = _gain_extras(gain, rope_tabs)
    segments = (_Segment(k_blk0, n_blk, functools.partial(_epi_key, rope=rope_tabs is not None), True, (0, 1)),
                _Segment(k_blk0 + n_blk, n_blk, _epi_value, True, (2, 3)))
    shapes, out_specs = [], []
    for span in _segment_steps(segments, 1):
        seg_shapes, seg_specs = _split_out(span, n_blk, head_rows)
        shapes += seg_shapes
        out_specs += seg_specs
    return _proj(hn, w, layer, segments, shapes, out_specs, name, extras, specs, run_after=run_after)


UVZ_CAST_STEPS = 32


def _uvz_proj(hn, w, layer, n_blk, name, w_out):
    segments = (_Segment(0, 2 * n_blk, _epi_gelu, False, (0,)),
                _Segment(2 * n_blk, n_blk, _epi_silu, False, (0,)))
    return _proj(hn, w, layer, segments, *_full_out(3 * n_blk), name, tn=PROJ_TN_WIDE,
                 cast=(w_out, layer, UVZ_CAST_STEPS))


OUT_COLS = 512


def _cast_slab_specs(w, layer, n_slabs, slab_of_step):
    k, n = w.shape[1], w.shape[2]
    rows = k // n_slabs
    return (pl.BlockSpec((None, rows, n), lambda *g: (layer, slab_of_step(*g), 0)),
            pl.BlockSpec((rows, n), lambda *g: (slab_of_step(*g), 0)),
            jax.ShapeDtypeStruct((k, n), BF16))


def _out_kernel(*refs, tile0, tm, split_a, split_x, fuse_norm):
    it = iter(refs)
    a_refs = [next(it), next(it)] if split_a else [next(it)]
    wbf_ref = next(it)
    x_refs = [next(it), next(it)] if split_x else [next(it)]
    gate_ref = next(it)
    if fuse_norm:
        g_ref, shift_ref, scale_ref = next(it), next(it), next(it)
    xnew_ref = next(it)
    hn_ref = next(it) if fuse_norm else None

    def body(a_ref, x_ref):
        ssq = jnp.zeros((tm, 1), F32)
        for cb in range(D_MODEL // OUT_COLS):
            sl = slice(cb * OUT_COLS, (cb + 1) * OUT_COLS)
            acc = jnp.dot(a_ref[...], wbf_ref[:, sl], preferred_element_type=F32)
            xn = x_ref[:, sl] + gate_ref[:, sl] * acc
            xnew_ref[:, sl] = xn
            if fuse_norm:
                ssq = ssq + jnp.sum(xn * xn, axis=-1, keepdims=True)
        if fuse_norm:
            rs = lax.rsqrt(ssq * (1.0 / D_MODEL) + EPS)
            for cb in range(D_MODEL // OUT_COLS):
                sl = slice(cb * OUT_COLS, (cb + 1) * OUT_COLS)
                gmul = g_ref[:, sl] * (1.0 + scale_ref[:, sl])
                hn_ref[:, sl] = (xnew_ref[:, sl] * rs * gmul + shift_ref[:, sl]).astype(BF16)

    if split_a or split_x:
        is_prompt = pl.program_id(0) + tile0 < PROMPT_TOKENS // tm
        pl.when(is_prompt)(lambda: body(a_refs[0], x_refs[0]))
        pl.when(jnp.logical_not(is_prompt))(lambda: body(a_refs[-1], x_refs[-1]))
    else:
        body(a_refs[0], x_refs[0])


def _out_proj(a, wbf, x, mod4, layer, norm_g3, name, tm, rows=None):
    split_a = isinstance(a, tuple)
    k = a[0].shape[1] if split_a else a.shape[1]
    tok0, n_tok = rows if rows is not None else (0, TOKENS)
    tile0 = tok0 // tm
    split_x = isinstance(x, tuple)
    fuse_norm = norm_g3 is not None
    mod_spec = lambda part, lyr: pl.BlockSpec(
        (None, None, 1, D_MODEL), lambda i: (lyr, _cond_row((i + tile0) * tm), 0, part))
    row_spec = lambda width: pl.BlockSpec((tm, width), lambda i: (i + tile0, 0))
    operands = [*a, wbf] if split_a else [a, wbf]
    in_specs = [*(_split_specs(tm, k, tile0) if split_a else [row_spec(k)]),
                pl.BlockSpec((k, D_MODEL), lambda i: (0, 0))]
    if split_x:
        assert rows is None
        operands += list(x)
        in_specs += _split_specs(tm, D_MODEL)
    else:
        operands.append(x)
        in_specs.append(row_spec(D_MODEL))
    operands.append(mod4)
    in_specs.append(mod_spec(2, layer))
    out_shapes = [jax.ShapeDtypeStruct((n_tok, D_MODEL), F32)]
    out_specs = [pl.BlockSpec((tm, D_MODEL), lambda i: (i, 0))]
    if fuse_norm:
        operands += [norm_g3, mod4, mod4]
        in_specs += [pl.BlockSpec((None, 1, D_MODEL), lambda i: (layer + 1, 0, 0)),
                     mod_spec(0, layer + 1), mod_spec(1, layer + 1)]
        out_shapes.append(jax.ShapeDtypeStruct((n_tok, D_MODEL), BF16))
        out_specs.append(pl.BlockSpec((tm, D_MODEL), lambda i: (i, 0)))
    kern = functools.partial(_out_kernel, tile0=tile0, tm=tm,
                             split_a=split_a, split_x=split_x, fuse_norm=fuse_norm)
    return pl.pallas_call(
        kern,
        grid=(n_tok // tm,),
        in_specs=in_specs,
        out_specs=out_specs,
        out_shape=out_shapes,
        compiler_params=_params(1),
        name=name,
    )(*operands)


def _dot_nt(a, b):
    return lax.dot_general(a, b, (((1,), (1,)), ((), ())), preferred_element_type=F32)


def _head(ref, h, rows=slice(None)):
    return ref[rows, h * HEAD_DIM:(h + 1) * HEAD_DIM]


def _with_ones(v):
    return jnp.concatenate([v, jnp.ones(v.shape, v.dtype)], axis=1)


def _lane_chunks(s):
    return [s[:, c * HEAD_DIM:(c + 1) * HEAD_DIM] for c in range(s.shape[1] // HEAD_DIM)]


def _softmax_pv(score_blocks, value_blocks, sink):
    rows = score_blocks[0].shape[0]
    mx = functools.reduce(jnp.maximum, [c for s in score_blocks for c in _lane_chunks(s)])
    m = jnp.broadcast_to(jnp.max(mx, axis=-1, keepdims=True), (rows, HEAD_DIM))
    if sink is not None:
        m = jnp.maximum(m, sink)
    o = None
    for s, v in zip(score_blocks, value_blocks):
        p = jnp.concatenate([jnp.exp2(c - m) for c in _lane_chunks(s)], axis=1).astype(BF16)
        part = jnp.dot(p, v, preferred_element_type=F32)
        o = part if o is None else o + part
    den = o[:, HEAD_DIM:]
    if sink is not None:
        den = den + jnp.exp2(sink - m)
    return o[:, :HEAD_DIM] * (1.0 / den)


def _attn_prompt_kernel(sink_ref, q_ref, k_ref, v_ref, z_ref, *rest, n_heads, n_kv, use_sink):
    o_ref = rest[-1]
    grp = n_heads // n_kv
    for g in range(n_kv):
        kg = _head(k_ref, g)
        vg = _with_ones(_head(v_ref, g))
        for h in range(g * grp, (g + 1) * grp):
            s = _dot_nt(_head(q_ref, h), kg)
            sink = jnp.full((PROMPT_SEQ, HEAD_DIM), sink_ref[h], F32) if use_sink else None
            o = _softmax_pv([s], [vg], sink)
            o_ref[:, h * HEAD_DIM:(h + 1) * HEAD_DIM] = (o * _head(z_ref, h).astype(F32)).astype(BF16)


def _attn_prompt(sink, qz, k, v, n_kv, use_sink, name, run_after=None):
    ordering = [] if run_after is None else [run_after]
    width = qz.shape[1] // 2
    kern = functools.partial(_attn_prompt_kernel, n_heads=width // HEAD_DIM, n_kv=n_kv, use_sink=use_sink)
    cache_spec = pl.BlockSpec((PROMPT_SEQ, k.shape[1]), lambda b: (b, 0))
    return pl.pallas_call(
        kern,
        grid=(N_PROMPT,),
        in_specs=[pl.BlockSpec(memory_space=pltpu.SMEM),
                  pl.BlockSpec((PROMPT_SEQ, width), lambda b: (b, 0)),
                  cache_spec, cache_spec,
                  pl.BlockSpec((PROMPT_SEQ, width), lambda b: (b, 1)),
                  *[pl.BlockSpec(memory_space=pl.ANY) for _ in ordering]],
        out_specs=pl.BlockSpec((PROMPT_SEQ, width), lambda b: (b, 0)),
        out_shape=jax.ShapeDtypeStruct((PROMPT_TOKENS, width), BF16),
        compiler_params=_params(1),
        name=name,
    )(sink, qz, k, v, qz, *ordering)


WIN_BAND = 3 * WINDOW


def _attn_win_kernel(sink_ref, q_ref, k_ref, v_ref, kc_ref, vc_ref, z_ref, w_ref, o_ref, wbf_ref, *,
                     n_heads, n_kv):
    wbf_ref[...] = w_ref[...].astype(BF16)
    grp = n_heads // n_kv
    n = pl.program_id(1)
    start = pl.multiple_of(jnp.clip((n - 1) * WINDOW, 0, SAMPLE_SEQ - WIN_BAND), WINDOW)
    shape = (grp * WINDOW, WIN_BAND)
    qpos = n * WINDOW + (lax.broadcasted_iota(jnp.int32, shape, 0) & (WINDOW - 1))
    kpos = start + lax.broadcasted_iota(jnp.int32, shape, 1)
    valid = jnp.abs(kpos - qpos) <= WINDOW
    band = pl.ds(start, WIN_BAND)
    for g in range(n_kv):
        heads = [g * grp + t for t in range(grp)]
        qs = jnp.concatenate([_head(q_ref, h) for h in heads], axis=0)
        kb = _head(k_ref, g, band)
        vb = _with_ones(_head(v_ref, g, band))
        ctx_rows = pl.ds(g, PAST_LEN, stride=n_kv)
        kc = kc_ref[ctx_rows, :].astype(BF16)
        vc = _with_ones(vc_ref[ctx_rows, :].astype(BF16))
        s_band = jnp.where(valid, _dot_nt(qs, kb), NEG_INF)
        s_ctx = _dot_nt(qs, kc)
        sink = jnp.concatenate([jnp.full((WINDOW, HEAD_DIM), sink_ref[h], F32) for h in heads], axis=0)
        o = _softmax_pv([s_band, s_ctx], [vb, vc], sink)
        for t, h in enumerate(heads):
            oh = o[t * WINDOW:(t + 1) * WINDOW]
            o_ref[:, h * HEAD_DIM:(h + 1) * HEAD_DIM] = (oh * _head(z_ref, h).astype(F32)).astype(BF16)


def _attn_win(sink, qz, k, v, cache_k, cache_v, layer_in_kind, n_kv, w_out):
    width = qz.shape[1] // 2
    kv_width = k.shape[1]
    blocks_per_seq = SAMPLE_SEQ // WINDOW
    q_map = lambda b, n: (PROMPT_TOKENS // WINDOW + b * blocks_per_seq + n, 0)
    z_map = lambda b, n: (PROMPT_TOKENS // WINDOW + b * blocks_per_seq + n, 1)
    kv_map = lambda b, n: (PROMPT_TOKENS // SAMPLE_SEQ + b, 0)
    cache_map = lambda b, n: (b, layer_in_kind, 0, 0)
    kern = functools.partial(_attn_win_kernel, n_heads=width // HEAD_DIM, n_kv=n_kv)
    w_spec, wbf_spec, wbf_shape = _cast_slab_specs(w_out, layer_in_kind, N_SAMPLE * blocks_per_seq,
                                                   lambda b, n: b * blocks_per_seq + n)
    return pl.pallas_call(
        kern,
        grid=(N_SAMPLE, blocks_per_seq),
        in_specs=[pl.BlockSpec(memory_space=pltpu.SMEM),
                  pl.BlockSpec((WINDOW, width), q_map),
                  pl.BlockSpec((SAMPLE_SEQ, kv_width), kv_map),
                  pl.BlockSpec((SAMPLE_SEQ, kv_width), kv_map),
                  pl.BlockSpec((None, None, PAST_LEN * n_kv, HEAD_DIM), cache_map),
                  pl.BlockSpec((None, None, PAST_LEN * n_kv, HEAD_DIM), cache_map),
                  pl.BlockSpec((WINDOW, width), z_map),
                  w_spec],
        out_specs=[pl.BlockSpec((WINDOW, width), lambda b, n: (b * blocks_per_seq + n, 0)), wbf_spec],
        out_shape=[jax.ShapeDtypeStruct((SAMPLE_TOKENS, width), BF16), wbf_shape],
        compiler_params=_params(2),
        name="attn_win_latent",
    )(sink, qz, k, v, cache_k, cache_v, qz, w_out)


NAT_QROWS = 4
GRID_ROWS = SAMPLE_SEQ // GRID_W
N_DR = 2 * NAT_ROWS - 1
MASKED_TILE = N_DR


def _nat_row_start(qr):
    return min(max(qr - NAT_ROWS // 2, 0), GRID_ROWS - NAT_ROWS)


def _attn_nat_kernel(bias_ref, q_ref, k_ref, v_ref, kc_ref, vc_ref, z_ref, w_ref, o_ref, wbf_ref,
                     left_ref, right_ref):
    wbf_ref[...] = w_ref[...].astype(BF16)

    def build_bias_tiles():
        shape = (GRID_W, 2 * GRID_W)
        lane = lax.broadcasted_iota(jnp.int32, shape, 1)
        qc = lax.broadcasted_iota(jnp.int32, shape, 0)
        kc_ = lane & (GRID_W - 1)
        cs = jnp.clip(qc - NAT_COLS // 2, 0, GRID_W - NAT_COLS)
        col_ok = jnp.logical_and(kc_ >= cs, kc_ < cs + NAT_COLS)
        is_left = lane < GRID_W
        lanes = 2 * GRID_W
        for dri in range(N_DR):
            row = jnp.broadcast_to(bias_ref[dri:dri + 1, :], shape)
            on_left = pltpu.roll(row, lanes - (NAT_COLS - 1), 1, stride=1, stride_axis=0)
            on_right = pltpu.roll(row, GRID_W - (NAT_COLS - 1), 1, stride=1, stride_axis=0)
            left_ref[dri] = jnp.where(is_left, jnp.where(col_ok, on_left, NEG_INF), 0.0)
            right_ref[dri] = jnp.where(is_left, 0.0, jnp.where(col_ok, on_right, NEG_INF))
        left_ref[MASKED_TILE] = jnp.where(is_left, NEG_INF, 0.0)
        right_ref[MASKED_TILE] = jnp.where(is_left, 0.0, NEG_INF)

    build_bias_tiles()
    n_heads = kc_ref.shape[1] // PAST_LEN
    ctx_rows = pl.ds(pl.program_id(0), PAST_LEN, stride=n_heads)
    for b, qb in [(b, qb) for b in range(N_SAMPLE) for qb in range(GRID_ROWS // NAT_QROWS)]:
        if qb == 0:
            kc = kc_ref[b, ctx_rows, :].astype(BF16)
            vc = _with_ones(vc_ref[b, ctx_rows, :].astype(BF16))
        qrows = range(qb * NAT_QROWS, (qb + 1) * NAT_QROWS)
        krow0 = _nat_row_start(qrows[0]) // 2 * 2
        krow1 = -(-(_nat_row_start(qrows[-1]) + NAT_ROWS) // 2) * 2
        tok0 = b * SAMPLE_SEQ
        kwin = slice(tok0 + krow0 * GRID_W, tok0 + krow1 * GRID_W)
        qwin = slice(tok0 + qrows[0] * GRID_W, tok0 + (qrows[-1] + 1) * GRID_W)

        def tile_index(qr, kr):
            rs = _nat_row_start(qr)
            return kr - qr + (NAT_ROWS - 1) if rs <= kr < rs + NAT_ROWS else MASKED_TILE

        bias = jnp.concatenate(
            [jnp.concatenate([left_ref[tile_index(qr, kr)] + right_ref[tile_index(qr, kr + 1)]
                              for kr in range(krow0, krow1, 2)], axis=1)
             for qr in qrows], axis=0)

        q = q_ref[qwin, :]
        s_nb = _dot_nt(q, k_ref[kwin, :]) + bias
        s_ctx = _dot_nt(q, kc)
        o = _softmax_pv([s_nb, s_ctx], [_with_ones(v_ref[kwin, :]), vc], None)
        o_ref[qwin, :] = (o * z_ref[qwin, :].astype(F32)).astype(BF16)


def _attn_nat(rel_bias, qz, k, v, cache_k, cache_v, layer_in_kind, w_out):
    n_heads = qz.shape[1] // (2 * HEAD_DIM)
    bias_rows = jnp.pad(rel_bias, ((0, 0), (0, N_DR + 1 - rel_bias.shape[1]),
                                   (0, 2 * GRID_W - rel_bias.shape[2])))
    latent_tile = PROMPT_TOKENS // SAMPLE_TOKENS
    latent_spec = pl.BlockSpec((SAMPLE_TOKENS, HEAD_DIM), lambda h: (latent_tile, h))
    z_spec = pl.BlockSpec((SAMPLE_TOKENS, HEAD_DIM), lambda h: (latent_tile, n_heads + h))
    cache_spec = pl.BlockSpec((N_SAMPLE, None, PAST_LEN * n_heads, HEAD_DIM),
                              lambda h: (0, layer_in_kind, 0, 0))
    w_spec, wbf_spec, wbf_shape = _cast_slab_specs(w_out, layer_in_kind, n_heads, lambda h: h)
    return pl.pallas_call(
        _attn_nat_kernel,
        grid=(n_heads,),
        in_specs=[pl.BlockSpec((None, N_DR + 1, 2 * GRID_W), lambda h: (h, 0, 0)),
                  latent_spec, latent_spec, latent_spec, cache_spec, cache_spec, z_spec, w_spec],
        out_specs=[pl.BlockSpec((SAMPLE_TOKENS, HEAD_DIM), lambda h: (0, h)), wbf_spec],
        out_shape=[jax.ShapeDtypeStruct((SAMPLE_TOKENS, n_heads * HEAD_DIM), BF16), wbf_shape],
        scratch_shapes=[pltpu.VMEM((N_DR + 1, GRID_W, 2 * GRID_W), F32),
                        pltpu.VMEM((N_DR + 1, GRID_W, 2 * GRID_W), F32)],
        compiler_params=_params(1),
        name="attn_nat_latent",
    )(bias_rows, qz, k, v, cache_k, cache_v, qz, w_out)


SPATIAL_TOKENS = 2 * CHUNK


def _spatial_kernel(u_ref, v_ref, z_ref, g_ref, b_ref, ws_ref, bs_ref, o_ref):
    for c in range(SPATIAL_TOKENS // CHUNK):
        rows = slice(c * CHUNK, (c + 1) * CHUNK)
        v = v_ref[rows, :].astype(F32)
        mu = jnp.mean(v, axis=-1, keepdims=True)
        vc = v - mu
        var = jnp.mean(vc * vc, axis=-1, keepdims=True)
        vn = (vc * lax.rsqrt(var + EPS) * g_ref[...] + b_ref[...]).astype(BF16)
        for g in range(GMLP_GROUPS):
            sl = slice(g * GMLP_GROUP_WIDTH, (g + 1) * GMLP_GROUP_WIDTH)
            sv = jnp.dot(ws_ref[g], vn[:, sl], preferred_element_type=F32) + bs_ref[:, g:g + 1]
            o_ref[rows, sl] = u_ref[rows, sl] * sv.astype(BF16) * z_ref[rows, sl]


def _spatial(uvz, ln_g, ln_b, w_s, b_s):
    part = lambda p: pl.BlockSpec((SPATIAL_TOKENS, GMLP_WIDTH), lambda i: (i, p))
    row = part(0)
    vec = pl.BlockSpec((1, GMLP_WIDTH), lambda i: (0, 0))
    return pl.pallas_call(
        _spatial_kernel,
        grid=(TOKENS // SPATIAL_TOKENS,),
        in_specs=[part(0), part(1), part(2), vec, vec,
                  pl.BlockSpec((GMLP_GROUPS, CHUNK, CHUNK), lambda i: (0, 0, 0)),
                  pl.BlockSpec((CHUNK, GMLP_GROUPS), lambda i: (0, 0))],
        out_specs=row,
        out_shape=jax.ShapeDtypeStruct((TOKENS, GMLP_WIDTH), BF16),
        compiler_params=_params(1),
        name="gmlp_spatial",
    )(uvz, uvz, uvz, ln_g.reshape(1, -1), ln_b.reshape(1, -1), w_s.astype(BF16), b_s.T)


def kernel(x_prompt, x_sample, cache_win_k, cache_win_v, cache_nat_k, cache_nat_v, c, c_ctx,
           norm_g, w_ada, b_ada,
           win_w_in, win_q_norm, win_k_norm, win_sink, win_w_out,
           nat_w_in, nat_q_norm, nat_k_norm, nat_rel_bias, nat_w_out,
           gmlp_w_in, gmlp_ln_g, gmlp_ln_b, gmlp_w_s, gmlp_b_s, gmlp_w_out):
    depth = norm_g.shape[0]
    xp = x_prompt.reshape(PROMPT_TOKENS, D_MODEL)
    xs = x_sample.reshape(SAMPLE_TOKENS, D_MODEL)
    cond = jnp.zeros((COND_ROWS, D_MODEL), F32).at[:N_SAMPLE].set(c).at[CTX_COND_ROW].set(c_ctx)
    mod4 = _ada(cond, w_ada, b_ada).reshape(depth, COND_ROWS, 1, 3 * D_MODEL)
    norm_g3 = norm_g.reshape(depth, 1, D_MODEL)
    rope_tabs = _rope_tables()
    no_sink = jnp.zeros((1,), F32)

    n_win = win_w_in.shape[0]
    n_nat = nat_w_in.shape[0]
    win_kv_heads = cache_win_k.shape[3]
    win_kv_width = win_kv_heads * HEAD_DIM
    cwk = cache_win_k.reshape(N_SAMPLE, n_win, PAST_LEN * win_kv_heads, HEAD_DIM)
    cwv = cache_win_v.reshape(N_SAMPLE, n_win, PAST_LEN * win_kv_heads, HEAD_DIM)
    nat_heads = cache_nat_k.shape[3]
    cnk = cache_nat_k.reshape(N_SAMPLE, n_nat, PAST_LEN * nat_heads, HEAD_DIM)
    cnv = cache_nat_v.reshape(N_SAMPLE, n_nat, PAST_LEN * nat_heads, HEAD_DIM)

    new_win_k, new_win_v, new_nat_k, new_nat_v = [], [], [], []
    x = (xp, xs)
    hn = _norm_mod(xp, xs, norm_g3, mod4, 0)
    for layer in range(depth):
        kind = layer % 3
        li = layer // 3
        if kind == 0:
            nq = D_MODEL // PROJ_TN
            nkv = win_kv_width // PROJ_TN
            qz = _qz_proj(hn, win_w_in, li, 0, nq + 2 * nkv, nq, win_q_norm[li], rope_tabs, "win_qz")
            kp, ks, vp, vs = _kv_proj(hn, win_w_in, li, nq, nkv, win_k_norm[li], rope_tabs, True, "win_kv",
                                      run_after=qz)
            new_win_k.append(kp)
            new_win_v.append(vp)
            sink = win_sink[li] * LOG2E
            a_latent, wbf = _attn_win(sink, qz, ks, vs, cwk, cwv, li, win_kv_heads, win_w_out)
            a = (_attn_prompt(sink, qz, ks, vs, win_kv_heads, True, "attn_win_prompt"), a_latent)
            out_tm, out_name = 512, "win_out"
        elif kind == 1:
            nq = D_MODEL // PROJ_TN
            qz = _qz_proj(hn, nat_w_in, li, 0, 3 * nq, nq, nat_q_norm[li], None, "nat_qz")
            kp, ks, vp, vs = _kv_proj(hn, nat_w_in, li, nq, nq, nat_k_norm[li], None, False, "nat_kv",
                                      run_after=qz)
            new_nat_k.append(kp)
            new_nat_v.append(vp)
            a_latent, wbf = _attn_nat(nat_rel_bias[li] * LOG2E, qz, ks, vs, cnk, cnv, li, nat_w_out)
            a = (_attn_prompt(no_sink, qz, ks, vs, nat_heads, False, "attn_nat_prompt", run_after=a_latent),
                 a_latent)
            out_tm, out_name = 512, "nat_out"
        else:
            nw = GMLP_WIDTH // PROJ_TN
            uvz, wbf = _uvz_proj(hn, gmlp_w_in, li, nw, "gmlp_uvz", gmlp_w_out)
            a = _spatial(uvz, gmlp_ln_g[li], gmlp_ln_b[li], gmlp_w_s[li], gmlp_b_s[li])
            out_tm, out_name = 256, "gmlp_out"
        if layer + 1 < depth:
            x, hn = _out_proj(a, wbf, x, mod4, layer, norm_g3, out_name, out_tm)
        else:
            (yp,) = _out_proj(a, wbf, x, mod4, layer, None, out_name + "_prompt", out_tm,
                              rows=(0, PROMPT_TOKENS))
            (ys,) = _out_proj(a, wbf, x, mod4, layer, None, out_name + "_latent", out_tm,
                              rows=(PROMPT_TOKENS, SAMPLE_TOKENS))

    cache_shape = lambda layers, heads: jnp.stack(
        [c_.reshape(N_PROMPT, PROMPT_SEQ, heads, HEAD_DIM) for c_ in layers], axis=1)
    return (yp.reshape(N_PROMPT, PROMPT_SEQ, D_MODEL), ys.reshape(N_SAMPLE, SAMPLE_SEQ, D_MODEL),
            cache_shape(new_win_k, win_kv_heads), cache_shape(new_win_v, win_kv_heads),
            cache_shape(new_nat_k, nat_heads), cache_shape(new_nat_v, nat_heads))
```

```python
import functools
import math
from typing import Callable, NamedTuple

import jax
import jax.numpy as jnp
import numpy as np
from jax import lax
from jax.experimental import pallas as pl
from jax.experimental.pallas import tpu as pltpu

F32 = jnp.float32
BF16 = jnp.bfloat16

D_MODEL = 2048
HEAD_DIM = 128
N_PROMPT = 16
PROMPT_SEQ = 256
N_SAMPLE = 2
SAMPLE_SEQ = 1024
PROMPT_TOKENS = N_PROMPT * PROMPT_SEQ
SAMPLE_TOKENS = N_SAMPLE * SAMPLE_SEQ
TOKENS = PROMPT_TOKENS + SAMPLE_TOKENS
PAST_LEN = 512
GRID_W = 64
EPS = 1e-6
NEG_INF = -1e30
ROPE_THETA = 10000.0
WINDOW = 128
NAT_ROWS = 8
NAT_COLS = 16
GMLP_WIDTH = 2 * D_MODEL
GMLP_GROUPS = 16
GMLP_GROUP_WIDTH = GMLP_WIDTH // GMLP_GROUPS
CHUNK = 128
CTX_COND_ROW = 2
COND_ROWS = 8
SM_SCALE = HEAD_DIM ** -0.5
LOG2E = math.log2(math.e)
QUERY_SCALE = SM_SCALE * LOG2E

VMEM_LIMIT = 56 * 1024 * 1024


def _params(n_axes, vmem=VMEM_LIMIT, flags=None):
    return pltpu.CompilerParams(dimension_semantics=("arbitrary",) * n_axes,
                                vmem_limit_bytes=vmem, flags=flags)


def _cond_row(tok0):
    return jnp.where(tok0 < PROMPT_TOKENS, CTX_COND_ROW, (tok0 - PROMPT_TOKENS) // SAMPLE_SEQ)


def _silu(x):
    return x * (0.5 + 0.5 * jnp.tanh(0.5 * x))


def _gelu_tanh(x):
    return 0.5 * x * (1.0 + jnp.tanh(math.sqrt(2.0 / math.pi) * (x + 0.044715 * (x * x * x))))


def _split_specs(tm, width, tile0=0):
    n_p = PROMPT_TOKENS // tm
    return [pl.BlockSpec((tm, width), lambda i: (jnp.minimum(i + tile0, n_p - 1), 0)),
            pl.BlockSpec((tm, width), lambda i: (jnp.maximum(i + tile0 - n_p, 0), 0))]


def _ada_kernel(cond_ref, w_ref, b_ref, o_ref):
    s = _silu(cond_ref[...]).astype(BF16)
    o_ref[...] = jnp.dot(s, w_ref[...].astype(BF16), preferred_element_type=F32) + b_ref[...]


def _ada(cond, w_ada, b_ada, tn=1024):
    depth = w_ada.shape[0]
    n = w_ada.shape[2]
    return pl.pallas_call(
        _ada_kernel,
        grid=(depth, n // tn),
        in_specs=[pl.BlockSpec((COND_ROWS, D_MODEL), lambda l, j: (0, 0)),
                  pl.BlockSpec((None, D_MODEL, tn), lambda l, j: (l, 0, j)),
                  pl.BlockSpec((None, 1, tn), lambda l, j: (l, 0, j))],
        out_specs=pl.BlockSpec((None, COND_ROWS, tn), lambda l, j: (l, 0, j)),
        out_shape=jax.ShapeDtypeStruct((depth, COND_ROWS, n), F32),
        compiler_params=_params(2),
        name="ada_mod",
    )(cond, w_ada, b_ada.reshape(depth, 1, n))


NORM_ROWS = 64


def _norm_kernel(xp_ref, xs_ref, g_ref, shift_ref, scale_ref, o_ref, *, tm):
    gmul = g_ref[...] * (1.0 + scale_ref[...])
    shift = shift_ref[...]

    def run(x_ref):
        for r in range(tm // NORM_ROWS):
            rows = slice(r * NORM_ROWS, (r + 1) * NORM_ROWS)
            x = x_ref[rows, :]
            rs = lax.rsqrt(jnp.mean(x * x, axis=-1, keepdims=True) + EPS)
            o_ref[rows, :] = (x * rs * gmul + shift).astype(BF16)

    is_prompt = pl.program_id(0) < PROMPT_TOKENS // tm
    pl.when(is_prompt)(lambda: run(xp_ref))
    pl.when(jnp.logical_not(is_prompt))(lambda: run(xs_ref))


def _norm_mod(xp, xs, norm_g3, mod4, layer, tm=1024):
    return pl.pallas_call(
        functools.partial(_norm_kernel, tm=tm),
        grid=(TOKENS // tm,),
        in_specs=[*_split_specs(tm, D_MODEL),
                  pl.BlockSpec((None, 1, D_MODEL), lambda i: (layer, 0, 0)),
                  pl.BlockSpec((None, None, 1, D_MODEL), lambda i: (layer, _cond_row(i * tm), 0, 0)),
                  pl.BlockSpec((None, None, 1, D_MODEL), lambda i: (layer, _cond_row(i * tm), 0, 1))],
        out_specs=pl.BlockSpec((tm, D_MODEL), lambda i: (i, 0)),
        out_shape=jax.ShapeDtypeStruct((TOKENS, D_MODEL), BF16),
        compiler_params=_params(1),
        name="norm_mod",
    )(xp, xs, norm_g3, mod4, mod4)


PROJ_TM = 2048
PROJ_TN = 512
PROJ_TN_WIDE = 1024
PROJ_ROWS = 512
N_PROMPT_TILES = PROMPT_TOKENS // PROJ_TM


class _Segment(NamedTuple):
    col_blk0: int
    n_blk: int
    epilogue: Callable
    by_tile_kind: bool
    outs: tuple


def _segment_steps(segments, wide):
    spans, j0 = [], 0
    for seg in segments:
        assert seg.col_blk0 % wide == 0 and seg.n_blk % wide == 0
        spans.append((j0, seg.n_blk // wide))
        j0 += seg.n_blk // wide
    return spans


def _proj_kernel(*refs, segments, spans, n_extra, n_out, cast_steps):
    a_ref, w_ref = refs[0], refs[1]
    extras = refs[2:2 + n_extra]
    outs = refs[2 + n_extra:2 + n_extra + n_out]
    wbf_ref = refs[2 + n_extra + n_out]

    @pl.when(pl.program_id(1) == 0)
    def _():
        wbf_ref[...] = w_ref[...].astype(BF16)

    if cast_steps:
        step = pl.program_id(0) * pl.num_programs(1) + pl.program_id(1)

        @pl.when(step < cast_steps)
        def _():
            outs[-1][...] = extras[-1][...].astype(BF16)

    def run(seg):
        seg_outs = [outs[k] for k in seg.outs]

        def body(latent):
            for rc in range(PROJ_TM // PROJ_ROWS):
                rows = slice(rc * PROJ_ROWS, (rc + 1) * PROJ_ROWS)
                for col0 in range(0, wbf_ref.shape[1], PROJ_TN):
                    acc = jnp.dot(a_ref[rows, :], wbf_ref[:, col0:col0 + PROJ_TN],
                                  preferred_element_type=F32)
                    seg.epilogue(acc, extras, seg_outs, rc, col0, latent)

        if seg.by_tile_kind:
            is_latent = pl.program_id(1) >= N_PROMPT_TILES
            pl.when(is_latent)(lambda: body(True))
            pl.when(jnp.logical_not(is_latent))(lambda: body(False))
        else:
            body(None)

    j = pl.program_id(0)
    for seg, (j0, nj) in zip(segments, spans):
        if len(segments) == 1:
            run(seg)
        else:
            pl.when(jnp.logical_and(j >= j0, j < j0 + nj))(functools.partial(run, seg))


def _proj(a, w, layer, segments, out_shapes, out_specs, name, extras=(), extra_specs=(), tn=PROJ_TN,
          run_after=None, cast=None):
    m, k = a.shape
    if run_after is not None:
        extras = [*extras, run_after]
        extra_specs = [*extra_specs, pl.BlockSpec(memory_space=pl.ANY)]
    wide = tn // PROJ_TN
    spans = _segment_steps(segments, wide)
    cast_steps = 0
    if cast is not None:
        w_out, out_layer, cast_steps = cast
        n_tiles = m // PROJ_TM
        assert cast_steps <= n_tiles * sum(nj for _, nj in spans)
        w_spec, wbf_spec, wbf_shape = _cast_slab_specs(
            w_out, out_layer, cast_steps, lambda j, i: jnp.minimum(j * n_tiles + i, cast_steps - 1))
        extras, extra_specs = [*extras, w_out], [*extra_specs, w_spec]
        out_shapes, out_specs = [*out_shapes, wbf_shape], [*out_specs, wbf_spec]

    def w_col(j):
        col = None
        for seg, (j0, _) in zip(segments, spans):
            c = seg.col_blk0 // wide + j - j0
            col = c if col is None else jnp.where(j >= j0, c, col)
        return col

    kern = functools.partial(_proj_kernel, segments=segments, spans=spans, n_extra=len(extras),
                             n_out=len(out_shapes), cast_steps=cast_steps)
    return pl.pallas_call(
        kern,
        grid=(sum(nj for _, nj in spans), m // PROJ_TM),
        in_specs=[pl.BlockSpec((PROJ_TM, k), lambda j, i: (i, 0)),
                  pl.BlockSpec((None, k, tn), lambda j, i: (layer, 0, w_col(j))),
                  *extra_specs],
        out_specs=out_specs,
        out_shape=out_shapes,
        scratch_shapes=[pltpu.VMEM((k, tn), BF16)],
        compiler_params=_params(2),
        name=name,
    )(a, w, *extras)


def _full_out(n_blk):
    return ([jax.ShapeDtypeStruct((TOKENS, n_blk * PROJ_TN), BF16)],
            [pl.BlockSpec((PROJ_TM, PROJ_TN_WIDE), lambda j, i: (i, j))])


def _held(span, j, inside, before, after):
    j0, nj = span
    pick = lambda a, b, c: jnp.where(j < j0, b, jnp.where(j >= j0 + nj, c, a))
    return tuple(pick(a, b, c) for a, b, c in zip(inside, before, after))


def _split_out(span, n_blk, head_rows):
    prompt_tile = lambda i: jnp.minimum(i, N_PROMPT_TILES - 1)
    last_prompt, last_tile = N_PROMPT_TILES - 1, TOKENS // PROJ_TM - 1
    jj = lambda j: j - span[0]
    if head_rows:
        assert n_blk == 1
        heads = PROJ_TN // HEAD_DIM
        cache_shape = jax.ShapeDtypeStruct((PROMPT_TOKENS * heads, HEAD_DIM), F32)
        cache_spec = pl.BlockSpec(
            (PROJ_TM * heads, HEAD_DIM),
            lambda j, i: _held(span, j, (prompt_tile(i), 0), (0, 0), (last_prompt, 0)))
    else:
        cache_shape = jax.ShapeDtypeStruct((N_PROMPT, PROMPT_SEQ, n_blk * PROJ_TN), F32)
        cache_spec = pl.BlockSpec(
            (PROJ_TM // PROMPT_SEQ, PROMPT_SEQ, PROJ_TN),
            lambda j, i: _held(span, j, (prompt_tile(i), 0, jj(j)), (0, 0, 0), (last_prompt, 0, n_blk - 1)))
    shapes = [cache_shape, jax.ShapeDtypeStruct((TOKENS, n_blk * PROJ_TN), BF16)]
    specs = [cache_spec,
             pl.BlockSpec((PROJ_TM, PROJ_TN),
                          lambda j, i: _held(span, j, (i, jj(j)), (0, 0), (last_tile, n_blk - 1)))]
    return shapes, specs


def _chunk_rows(rc):
    return slice(rc * PROJ_ROWS, (rc + 1) * PROJ_ROWS)


def _epi_silu(acc, extras, outs, rc, col0, latent):
    outs[0][_chunk_rows(rc), col0:col0 + PROJ_TN] = _silu(acc).astype(BF16)


def _epi_gelu(acc, extras, outs, rc, col0, latent):
    outs[0][_chunk_rows(rc), col0:col0 + PROJ_TN] = _gelu_tanh(acc).astype(BF16)


def _store_split(y, outs, rc, sl, latent):
    outs[1][_chunk_rows(rc), sl] = y.astype(BF16)
    if latent:
        return
    if len(outs[0].shape) == 2:
        heads = PROJ_TN // HEAD_DIM
        col0 = sl.start or 0
        for c in range(y.shape[1] // HEAD_DIM):
            head = col0 // HEAD_DIM + c
            rows = pl.ds(rc * PROJ_ROWS * heads + head, PROJ_ROWS, stride=heads)
            outs[0][rows, :] = y[:, c * HEAD_DIM:(c + 1) * HEAD_DIM]
    else:
        seqs = PROJ_ROWS // PROMPT_SEQ
        outs[0][rc * seqs:(rc + 1) * seqs, :, sl] = y.reshape(seqs, PROMPT_SEQ, y.shape[1])


def _epi_value(acc, extras, outs, rc, col0, latent):
    _store_split(acc, outs, rc, slice(col0, col0 + PROJ_TN), latent)


def _head_rmsnorm(acc, hh, g):
    a = acc[:, hh * HEAD_DIM:(hh + 1) * HEAD_DIM]
    return a * lax.rsqrt(jnp.mean(a * a, axis=-1, keepdims=True) + EPS) * g


def _normed_heads(acc, extras, rc, rope):
    g = extras[0][...]
    ys = [_head_rmsnorm(acc, hh, g) for hh in range(PROJ_TN // HEAD_DIM)]
    if rope:
        rows = _chunk_rows(rc)
        cos, sin, swap = extras[1][rows, :], extras[2][rows, :], extras[3][...]
        for pair in range(len(ys) // 2):
            both = jnp.concatenate(ys[2 * pair:2 * pair + 2], axis=1).astype(BF16)
            partner = jnp.dot(both, swap, preferred_element_type=F32)
            for t in range(2):
                hh = 2 * pair + t
                ys[hh] = ys[hh] * cos + partner[:, t * HEAD_DIM:(t + 1) * HEAD_DIM] * sin
    for hh, y in enumerate(ys):
        yield slice(hh * HEAD_DIM, (hh + 1) * HEAD_DIM), y


def _epi_query(acc, extras, outs, rc, col0, latent, *, rope):
    for sl, y in _normed_heads(acc, extras, rc, rope and latent):
        outs[0][_chunk_rows(rc), col0 + sl.start:col0 + sl.stop] = (y * QUERY_SCALE).astype(BF16)


def _epi_key(acc, extras, outs, rc, col0, latent, *, rope):
    for sl, y in _normed_heads(acc, extras, rc, rope and latent):
        _store_split(y, outs, rc, slice(col0 + sl.start, col0 + sl.stop), latent)


def _rope_tables():
    nf = HEAD_DIM // 4
    t = np.arange(SAMPLE_TOKENS) % SAMPLE_SEQ
    row = (t // GRID_W).astype(np.float32)
    col = (t % GRID_W).astype(np.float32)
    inv = np.float32(ROPE_THETA) ** (-np.arange(nf, dtype=np.float32) / np.float32(nf))
    ang_r = row[:, None] * inv
    ang_c = col[:, None] * inv
    cos = np.concatenate([np.cos(ang_r), np.cos(ang_r), np.cos(ang_c), np.cos(ang_c)], axis=1)
    sin = np.concatenate([-np.sin(ang_r), np.sin(ang_r), -np.sin(ang_c), np.sin(ang_c)], axis=1)
    lanes = np.arange(2 * HEAD_DIM)
    swap = lanes[:, None] == (lanes[None, :] ^ (HEAD_DIM // 4))
    return jnp.asarray(cos, F32), jnp.asarray(sin, F32), jnp.asarray(swap, BF16)


def _gain_extras(gain, rope_tabs):
    extras = [gain.reshape(1, HEAD_DIM)]
    specs = [pl.BlockSpec((1, HEAD_DIM), lambda j, i: (0, 0))]
    if rope_tabs is not None:
        tab_map = lambda j, i: (jnp.maximum(i - N_PROMPT_TILES, 0), 0)
        extras += list(rope_tabs)
        specs += [pl.BlockSpec((PROJ_TM, HEAD_DIM), tab_map)] * 2
        specs.append(pl.BlockSpec((2 * HEAD_DIM, 2 * HEAD_DIM), lambda j, i: (0, 0)))
    return extras, specs


def _qz_proj(hn, w, layer, q_blk0, z_blk0, n_blk, gain, rope_tabs, name):
    extras, specs = _gain_extras(gain, rope_tabs)
    rope = rope_tabs is not None
    segments = (_Segment(q_blk0, n_blk, functools.partial(_epi_query, rope=rope), rope, (0,)),
                _Segment(z_blk0, n_blk, _epi_silu, False, (0,)))
    return _proj(hn, w, layer, segments, *_full_out(2 * n_blk), name, extras, specs, tn=PROJ_TN_WIDE)[0]


def _kv_proj(hn, w, layer, k_blk0, n_blk, gain, rope_tabs, head_rows, name, run_after=None):
    extras, specs = _gain_extras(gain, rope_tabs)
    segments = (_Segment(k_blk0, n_blk, functools.partial(_epi_key, rope=rope_tabs is not None), True, (0, 1)),
                _Segment(k_blk0 + n_blk, n_blk, _epi_value, True, (2, 3)))
    shapes, out_specs = [], []
    for span in _segment_steps(segments, 1):
        seg_shapes, seg_specs = _split_out(span, n_blk, head_rows)
        shapes += seg_shapes
        out_specs += seg_specs
    return _proj(hn, w, layer, segments, shapes, out_specs, name, extras, specs, run_after=run_after)


UVZ_CAST_STEPS = 32


def _uvz_proj(hn, w, layer, n_blk, name, w_out):
    segments = (_Segment(0, 2 * n_blk, _epi_gelu, False, (0,)),
                _Segment(2 * n_blk, n_blk, _epi_silu, False, (0,)))
    return _proj(hn, w, layer, segments, *_full_out(3 * n_blk), name, tn=PROJ_TN_WIDE,
                 cast=(w_out, layer, UVZ_CAST_STEPS))


OUT_COLS = 512


def _cast_slab_specs(w, layer, n_slabs, slab_of_step):
    k, n = w.shape[1], w.shape[2]
    rows = k // n_slabs
    return (pl.BlockSpec((None, rows, n), lambda *g: (layer, slab_of_step(*g), 0)),
            pl.BlockSpec((rows, n), lambda *g: (slab_of_step(*g), 0)),
            jax.ShapeDtypeStruct((k, n), BF16))


def _out_kernel(*refs, tile0, tm, split_a, split_x, fuse_norm):
    it = iter(refs)
    a_refs = [next(it), next(it)] if split_a else [next(it)]
    wbf_ref = next(it)
    x_refs = [next(it), next(it)] if split_x else [next(it)]
    gate_ref = next(it)
    if fuse_norm:
        g_ref, shift_ref, scale_ref = next(it), next(it), next(it)
    xnew_ref = next(it)
    hn_ref = next(it) if fuse_norm else None

    def body(a_ref, x_ref):
        ssq = jnp.zeros((tm, 1), F32)
        for cb in range(D_MODEL // OUT_COLS):
            sl = slice(cb * OUT_COLS, (cb + 1) * OUT_COLS)
            acc = jnp.dot(a_ref[...], wbf_ref[:, sl], preferred_element_type=F32)
            xn = x_ref[:, sl] + gate_ref[:, sl] * acc
            xnew_ref[:, sl] = xn
            if fuse_norm:
                ssq = ssq + jnp.sum(xn * xn, axis=-1, keepdims=True)
        if fuse_norm:
            rs = lax.rsqrt(ssq * (1.0 / D_MODEL) + EPS)
            for cb in range(D_MODEL // OUT_COLS):
                sl = slice(cb * OUT_COLS, (cb + 1) * OUT_COLS)
                gmul = g_ref[:, sl] * (1.0 + scale_ref[:, sl])
                hn_ref[:, sl] = (xnew_ref[:, sl] * rs * gmul + shift_ref[:, sl]).astype(BF16)

    if split_a or split_x:
        is_prompt = pl.program_id(0) + tile0 < PROMPT_TOKENS // tm
        pl.when(is_prompt)(lambda: body(a_refs[0], x_refs[0]))
        pl.when(jnp.logical_not(is_prompt))(lambda: body(a_refs[-1], x_refs[-1]))
    else:
        body(a_refs[0], x_refs[0])


def _out_proj(a, wbf, x, mod4, layer, norm_g3, name, tm, rows=None):
    split_a = isinstance(a, tuple)
    k = a[0].shape[1] if split_a else a.shape[1]
    tok0, n_tok = rows if rows is not None else (0, TOKENS)
    tile0 = tok0 // tm
    split_x = isinstance(x, tuple)
    fuse_norm = norm_g3 is not None
    mod_spec = lambda part, lyr: pl.BlockSpec(
        (None, None, 1, D_MODEL), lambda i: (lyr, _cond_row((i + tile0) * tm), 0, part))
    row_spec = lambda width: pl.BlockSpec((tm, width), lambda i: (i + tile0, 0))
    operands = [*a, wbf] if split_a else [a, wbf]
    in_specs = [*(_split_specs(tm, k, tile0) if split_a else [row_spec(k)]),
                pl.BlockSpec((k, D_MODEL), lambda i: (0, 0))]
    if split_x:
        assert rows is None
        operands += list(x)
        in_specs += _split_specs(tm, D_MODEL)
    else:
        operands.append(x)
        in_specs.append(row_spec(D_MODEL))
    operands.append(mod4)
    in_specs.append(mod_spec(2, layer))
    out_shapes = [jax.ShapeDtypeStruct((n_tok, D_MODEL), F32)]
    out_specs = [pl.BlockSpec((tm, D_MODEL), lambda i: (i, 0))]
    if fuse_norm:
        operands += [norm_g3, mod4, mod4]
        in_specs += [pl.BlockSpec((None, 1, D_MODEL), lambda i: (layer + 1, 0, 0)),
                     mod_spec(0, layer + 1), mod_spec(1, layer + 1)]
        out_shapes.append(jax.ShapeDtypeStruct((n_tok, D_MODEL), BF16))
        out_specs.append(pl.BlockSpec((tm, D_MODEL), lambda i: (i, 0)))
    kern = functools.partial(_out_kernel, tile0=tile0, tm=tm,
                             split_a=split_a, split_x=split_x, fuse_norm=fuse_norm)
    return pl.pallas_call(
        kern,
        grid=(n_tok // tm,),
        in_specs=in_specs,
        out_specs=out_specs,
        out_shape=out_shapes,
        compiler_params=_params(1),
        name=name,
    )(*operands)


def _dot_nt(a, b):
    return lax.dot_general(a, b, (((1,), (1,)), ((), ())), preferred_element_type=F32)


def _head(ref, h, rows=slice(None)):
    return ref[rows, h * HEAD_DIM:(h + 1) * HEAD_DIM]


def _with_ones(v):
    return jnp.concatenate([v, jnp.ones(v.shape, v.dtype)], axis=1)


def _lane_chunks(s):
    return [s[:, c * HEAD_DIM:(c + 1) * HEAD_DIM] for c in range(s.shape[1] // HEAD_DIM)]


def _softmax_pv(score_blocks, value_blocks, sink):
    rows = score_blocks[0].shape[0]
    mx = functools.reduce(jnp.maximum, [c for s in score_blocks for c in _lane_chunks(s)])
    m = jnp.broadcast_to(jnp.max(mx, axis=-1, keepdims=True), (rows, HEAD_DIM))
    if sink is not None:
        m = jnp.maximum(m, sink)
    o = None
    for s, v in zip(score_blocks, value_blocks):
        p = jnp.concatenate([jnp.exp2(c - m) for c in _lane_chunks(s)], axis=1).astype(BF16)
        part = jnp.dot(p, v, preferred_element_type=F32)
        o = part if o is None else o + part
    den = o[:, HEAD_DIM:]
    if sink is not None:
        den = den + jnp.exp2(sink - m)
    return o[:, :HEAD_DIM] * (1.0 / den)


def _attn_prompt_kernel(sink_ref, q_ref, k_ref, v_ref, z_ref, *rest, n_heads, n_kv, use_sink):
    o_ref = rest[-1]
    grp = n_heads // n_kv
    for g in range(n_kv):
        kg = _head(k_ref, g)
        vg = _with_ones(_head(v_ref, g))
        for h in range(g * grp, (g + 1) * grp):
            s = _dot_nt(_head(q_ref, h), kg)
            sink = jnp.full((PROMPT_SEQ, HEAD_DIM), sink_ref[h], F32) if use_sink else None
            o = _softmax_pv([s], [vg], sink)
            o_ref[:, h * HEAD_DIM:(h + 1) * HEAD_DIM] = (o * _head(z_ref, h).astype(F32)).astype(BF16)


def _attn_prompt(sink, qz, k, v, n_kv, use_sink, name, run_after=None):
    ordering = [] if run_after is None else [run_after]
    width = qz.shape[1] // 2
    kern = functools.partial(_attn_prompt_kernel, n_heads=width // HEAD_DIM, n_kv=n_kv, use_sink=use_sink)
    cache_spec = pl.BlockSpec((PROMPT_SEQ, k.shape[1]), lambda b: (b, 0))
    return pl.pallas_call(
        kern,
        grid=(N_PROMPT,),
        in_specs=[pl.BlockSpec(memory_space=pltpu.SMEM),
                  pl.BlockSpec((PROMPT_SEQ, width), lambda b: (b, 0)),
                  cache_spec, cache_spec,
                  pl.BlockSpec((PROMPT_SEQ, width), lambda b: (b, 1)),
                  *[pl.BlockSpec(memory_space=pl.ANY) for _ in ordering]],
        out_specs=pl.BlockSpec((PROMPT_SEQ, width), lambda b: (b, 0)),
        out_shape=jax.ShapeDtypeStruct((PROMPT_TOKENS, width), BF16),
        compiler_params=_params(1),
        name=name,
    )(sink, qz, k, v, qz, *ordering)


WIN_BAND = 3 * WINDOW


def _attn_win_kernel(sink_ref, q_ref, k_ref, v_ref, kc_ref, vc_ref, z_ref, w_ref, o_ref, wbf_ref, *,
                     n_heads, n_kv):
    wbf_ref[...] = w_ref[...].astype(BF16)
    grp = n_heads // n_kv
    n = pl.program_id(1)
    start = pl.multiple_of(jnp.clip((n - 1) * WINDOW, 0, SAMPLE_SEQ - WIN_BAND), WINDOW)
    shape = (grp * WINDOW, WIN_BAND)
    qpos = n * WINDOW + (lax.broadcasted_iota(jnp.int32, shape, 0) & (WINDOW - 1))
    kpos = start + lax.broadcasted_iota(jnp.int32, shape, 1)
    valid = jnp.abs(kpos - qpos) <= WINDOW
    band = pl.ds(start, WIN_BAND)
    for g in range(n_kv):
        heads = [g * grp + t for t in range(grp)]
        qs = jnp.concatenate([_head(q_ref, h) for h in heads], axis=0)
        kb = _head(k_ref, g, band)
        vb = _with_ones(_head(v_ref, g, band))
        ctx_rows = pl.ds(g, PAST_LEN, stride=n_kv)
        kc = kc_ref[ctx_rows, :].astype(BF16)
        vc = _with_ones(vc_ref[ctx_rows, :].astype(BF16))
        s_band = jnp.where(valid, _dot_nt(qs, kb), NEG_INF)
        s_ctx = _dot_nt(qs, kc)
        sink = jnp.concatenate([jnp.full((WINDOW, HEAD_DIM), sink_ref[h], F32) for h in heads], axis=0)
        o = _softmax_pv([s_band, s_ctx], [vb, vc], sink)
        for t, h in enumerate(heads):
            oh = o[t * WINDOW:(t + 1) * WINDOW]
            o_ref[:, h * HEAD_DIM:(h + 1) * HEAD_DIM] = (oh * _head(z_ref, h).astype(F32)).astype(BF16)


def _attn_win(sink, qz, k, v, cache_k, cache_v, layer_in_kind, n_kv, w_out):
    width = qz.shape[1] // 2
    kv_width = k.shape[1]
    blocks_per_seq = SAMPLE_SEQ // WINDOW
    q_map = lambda b, n: (PROMPT_TOKENS // WINDOW + b * blocks_per_seq + n, 0)
    z_map = lambda b, n: (PROMPT_TOKENS // WINDOW + b * blocks_per_seq + n, 1)
    kv_map = lambda b, n: (PROMPT_TOKENS // SAMPLE_SEQ + b, 0)
    cache_map = lambda b, n: (b, layer_in_kind, 0, 0)
    kern = functools.partial(_attn_win_kernel, n_heads=width // HEAD_DIM, n_kv=n_kv)
    w_spec, wbf_spec, wbf_shape = _cast_slab_specs(w_out, layer_in_kind, N_SAMPLE * blocks_per_seq,
                                                   lambda b, n: b * blocks_per_seq + n)
    return pl.pallas_call(
        kern,
        grid=(N_SAMPLE, blocks_per_seq),
        in_specs=[pl.BlockSpec(memory_space=pltpu.SMEM),
                  pl.BlockSpec((WINDOW, width), q_map),
                  pl.BlockSpec((SAMPLE_SEQ, kv_width), kv_map),
                  pl.BlockSpec((SAMPLE_SEQ, kv_width), kv_map),
                  pl.BlockSpec((None, None, PAST_LEN * n_kv, HEAD_DIM), cache_map),
                  pl.BlockSpec((None, None, PAST_LEN * n_kv, HEAD_DIM), cache_map),
                  pl.BlockSpec((WINDOW, width), z_map),
                  w_spec],
        out_specs=[pl.BlockSpec((WINDOW, width), lambda b, n: (b * blocks_per_seq + n, 0)), wbf_spec],
        out_shape=[jax.ShapeDtypeStruct((SAMPLE_TOKENS, width), BF16), wbf_shape],
        compiler_params=_params(2),
        name="attn_win_latent",
    )(sink, qz, k, v, cache_k, cache_v, qz, w_out)


NAT_QROWS = 4
GRID_ROWS = SAMPLE_SEQ // GRID_W
N_DR = 2 * NAT_ROWS - 1
MASKED_TILE = N_DR


def _nat_row_start(qr):
    return min(max(qr - NAT_ROWS // 2, 0), GRID_ROWS - NAT_ROWS)


def _attn_nat_kernel(bias_ref, q_ref, k_ref, v_ref, kc_ref, vc_ref, z_ref, w_ref, o_ref, wbf_ref,
                     left_ref, right_ref):
    wbf_ref[...] = w_ref[...].astype(BF16)

    def build_bias_tiles():
        shape = (GRID_W, 2 * GRID_W)
        lane = lax.broadcasted_iota(jnp.int32, shape, 1)
        qc = lax.broadcasted_iota(jnp.int32, shape, 0)
        kc_ = lane & (GRID_W - 1)
        cs = jnp.clip(qc - NAT_COLS // 2, 0, GRID_W - NAT_COLS)
        col_ok = jnp.logical_and(kc_ >= cs, kc_ < cs + NAT_COLS)
        is_left = lane < GRID_W
        lanes = 2 * GRID_W
        for dri in range(N_DR):
            row = jnp.broadcast_to(bias_ref[dri:dri + 1, :], shape)
            on_left = pltpu.roll(row, lanes - (NAT_COLS - 1), 1, stride=1, stride_axis=0)
            on_right = pltpu.roll(row, GRID_W - (NAT_COLS - 1), 1, stride=1, stride_axis=0)
            left_ref[dri] = jnp.where(is_left, jnp.where(col_ok, on_left, NEG_INF), 0.0)
            right_ref[dri] = jnp.where(is_left, 0.0, jnp.where(col_ok, on_right, NEG_INF))
        left_ref[MASKED_TILE] = jnp.where(is_left, NEG_INF, 0.0)
        right_ref[MASKED_TILE] = jnp.where(is_left, 0.0, NEG_INF)

    build_bias_tiles()
    n_heads = kc_ref.shape[1] // PAST_LEN
    ctx_rows = pl.ds(pl.program_id(0), PAST_LEN, stride=n_heads)
    for b, qb in [(b, qb) for b in range(N_SAMPLE) for qb in range(GRID_ROWS // NAT_QROWS)]:
        if qb == 0:
            kc = kc_ref[b, ctx_rows, :].astype(BF16)
            vc = _with_ones(vc_ref[b, ctx_rows, :].astype(BF16))
        qrows = range(qb * NAT_QROWS, (qb + 1) * NAT_QROWS)
        krow0 = _nat_row_start(qrows[0]) // 2 * 2
        krow1 = -(-(_nat_row_start(qrows[-1]) + NAT_ROWS) // 2) * 2
        tok0 = b * SAMPLE_SEQ
        kwin = slice(tok0 + krow0 * GRID_W, tok0 + krow1 * GRID_W)
        qwin = slice(tok0 + qrows[0] * GRID_W, tok0 + (qrows[-1] + 1) * GRID_W)

        def tile_index(qr, kr):
            rs = _nat_row_start(qr)
            return kr - qr + (NAT_ROWS - 1) if rs <= kr < rs + NAT_ROWS else MASKED_TILE

        bias = jnp.concatenate(
            [jnp.concatenate([left_ref[tile_index(qr, kr)] + right_ref[tile_index(qr, kr + 1)]
                              for kr in range(krow0, krow1, 2)], axis=1)
             for qr in qrows], axis=0)

        q = q_ref[qwin, :]
        s_nb = _dot_nt(q, k_ref[kwin, :]) + bias
        s_ctx = _dot_nt(q, kc)
        o = _softmax_pv([s_nb, s_ctx], [_with_ones(v_ref[kwin, :]), vc], None)
        o_ref[qwin, :] = (o * z_ref[qwin, :].astype(F32)).astype(BF16)


def _attn_nat(rel_bias, qz, k, v, cache_k, cache_v, layer_in_kind, w_out):
    n_heads = qz.shape[1] // (2 * HEAD_DIM)
    bias_rows = jnp.pad(rel_bias, ((0, 0), (0, N_DR + 1 - rel_bias.shape[1]),
                                   (0, 2 * GRID_W - rel_bias.shape[2])))
    latent_tile = PROMPT_TOKENS // SAMPLE_TOKENS
    latent_spec = pl.BlockSpec((SAMPLE_TOKENS, HEAD_DIM), lambda h: (latent_tile, h))
    z_spec = pl.BlockSpec((SAMPLE_TOKENS, HEAD_DIM), lambda h: (latent_tile, n_heads + h))
    cache_spec = pl.BlockSpec((N_SAMPLE, None, PAST_LEN * n_heads, HEAD_DIM),
                              lambda h: (0, layer_in_kind, 0, 0))
    w_spec, wbf_spec, wbf_shape = _cast_slab_specs(w_out, layer_in_kind, n_heads, lambda h: h)
    return pl.pallas_call(
        _attn_nat_kernel,
        grid=(n_heads,),
        in_specs=[pl.BlockSpec((None, N_DR + 1, 2 * GRID_W), lambda h: (h, 0, 0)),
                  latent_spec, latent_spec, latent_spec, cache_spec, cache_spec, z_spec, w_spec],
        out_specs=[pl.BlockSpec((SAMPLE_TOKENS, HEAD_DIM), lambda h: (0, h)), wbf_spec],
        out_shape=[jax.ShapeDtypeStruct((SAMPLE_TOKENS, n_heads * HEAD_DIM), BF16), wbf_shape],
        scratch_shapes=[pltpu.VMEM((N_DR + 1, GRID_W, 2 * GRID_W), F32),
                        pltpu.VMEM((N_DR + 1, GRID_W, 2 * GRID_W), F32)],
        compiler_params=_params(1),
        name="attn_nat_latent",
    )(bias_rows, qz, k, v, cache_k, cache_v, qz, w_out)


SPATIAL_TOKENS = 2 * CHUNK


def _spatial_kernel(u_ref, v_ref, z_ref, g_ref, b_ref, ws_ref, bs_ref, o_ref):
    for c in range(SPATIAL_TOKENS // CHUNK):
        rows = slice(c * CHUNK, (c + 1) * CHUNK)
        v = v_ref[rows, :].astype(F32)
        mu = jnp.mean(v, axis=-1, keepdims=True)
        vc = v - mu
        var = jnp.mean(vc * vc, axis=-1, keepdims=True)
        vn = (vc * lax.rsqrt(var + EPS) * g_ref[...] + b_ref[...]).astype(BF16)
        for g in range(GMLP_GROUPS):
            sl = slice(g * GMLP_GROUP_WIDTH, (g + 1) * GMLP_GROUP_WIDTH)
            sv = jnp.dot(ws_ref[g], vn[:, sl], preferred_element_type=F32) + bs_ref[:, g:g + 1]
            o_ref[rows, sl] = u_ref[rows, sl] * sv.astype(BF16) * z_ref[rows, sl]


def _spatial(uvz, ln_g, ln_b, w_s, b_s):
    part = lambda p: pl.BlockSpec((SPATIAL_TOKENS, GMLP_WIDTH), lambda i: (i, p))
    row = part(0)
    vec = pl.BlockSpec((1, GMLP_WIDTH), lambda i: (0, 0))
    return pl.pallas_call(
        _spatial_kernel,
        grid=(TOKENS // SPATIAL_TOKENS,),
        in_specs=[part(0), part(1), part(2), vec, vec,
                  pl.BlockSpec((GMLP_GROUPS, CHUNK, CHUNK), lambda i: (0, 0, 0)),
                  pl.BlockSpec((CHUNK, GMLP_GROUPS), lambda i: (0, 0))],
        out_specs=row,
        out_shape=jax.ShapeDtypeStruct((TOKENS, GMLP_WIDTH), BF16),
        compiler_params=_params(1),
        name="gmlp_spatial",
    )(uvz, uvz, uvz, ln_g.reshape(1, -1), ln_b.reshape(1, -1), w_s.astype(BF16), b_s.T)


def kernel(x_prompt, x_sample, cache_win_k, cache_win_v, cache_nat_k, cache_nat_v, c, c_ctx,
           norm_g, w_ada, b_ada,
           win_w_in, win_q_norm, win_k_norm, win_sink, win_w_out,
           nat_w_in, nat_q_norm, nat_k_norm, nat_rel_bias, nat_w_out,
           gmlp_w_in, gmlp_ln_g, gmlp_ln_b, gmlp_w_s, gmlp_b_s, gmlp_w_out):
    depth = norm_g.shape[0]
    xp = x_prompt.reshape(PROMPT_TOKENS, D_MODEL)
    xs = x_sample.reshape(SAMPLE_TOKENS, D_MODEL)
    cond = jnp.zeros((COND_ROWS, D_MODEL), F32).at[:N_SAMPLE].set(c).at[CTX_COND_ROW].set(c_ctx)
    mod4 = _ada(cond, w_ada, b_ada).reshape(depth, COND_ROWS, 1, 3 * D_MODEL)
    norm_g3 = norm_g.reshape(depth, 1, D_MODEL)
    rope_tabs = _rope_tables()
    no_sink = jnp.zeros((1,), F32)

    n_win = win_w_in.shape[0]
    n_nat = nat_w_in.shape[0]
    win_kv_heads = cache_win_k.shape[3]
    win_kv_width = win_kv_heads * HEAD_DIM
    cwk = cache_win_k.reshape(N_SAMPLE, n_win, PAST_LEN * win_kv_heads, HEAD_DIM)
    cwv = cache_win_v.reshape(N_SAMPLE, n_win, PAST_LEN * win_kv_heads, HEAD_DIM)
    nat_heads = cache_nat_k.shape[3]
    cnk = cache_nat_k.reshape(N_SAMPLE, n_nat, PAST_LEN * nat_heads, HEAD_DIM)
    cnv = cache_nat_v.reshape(N_SAMPLE, n_nat, PAST_LEN * nat_heads, HEAD_DIM)

    new_win_k, new_win_v, new_nat_k, new_nat_v = [], [], [], []
    x = (xp, xs)
    hn = _norm_mod(xp, xs, norm_g3, mod4, 0)
    for layer in range(depth):
        kind = layer % 3
        li = layer // 3
        if kind == 0:
            nq = D_MODEL // PROJ_TN
            nkv = win_kv_width // PROJ_TN
            qz = _qz_proj(hn, win_w_in, li, 0, nq + 2 * nkv, nq, win_q_norm[li], rope_tabs, "win_qz")
            kp, ks, vp, vs = _kv_proj(hn, win_w_in, li, nq, nkv, win_k_norm[li], rope_tabs, True, "win_kv",
                                      run_after=qz)
            new_win_k.append(kp)
            new_win_v.append(vp)
            sink = win_sink[li] * LOG2E
            a_latent, wbf = _attn_win(sink, qz, ks, vs, cwk, cwv, li, win_kv_heads, win_w_out)
            a = (_attn_prompt(sink, qz, ks, vs, win_kv_heads, True, "attn_win_prompt"), a_latent)
            out_tm, out_name = 512, "win_out"
        elif kind == 1:
            nq = D_MODEL // PROJ_TN
            qz = _qz_proj(hn, nat_w_in, li, 0, 3 * nq, nq, nat_q_norm[li], None, "nat_qz")
            kp, ks, vp, vs = _kv_proj(hn, nat_w_in, li, nq, nq, nat_k_norm[li], None, False, "nat_kv",
                                      run_after=qz)
            new_nat_k.append(kp)
            new_nat_v.append(vp)
            a_latent, wbf = _attn_nat(nat_rel_bias[li] * LOG2E, qz, ks, vs, cnk, cnv, li, nat_w_out)
            a = (_attn_prompt(no_sink, qz, ks, vs, nat_heads, False, "attn_nat_prompt", run_after=a_latent),
                 a_latent)
            out_tm, out_name = 512, "nat_out"
        else:
            nw = GMLP_WIDTH // PROJ_TN
            uvz, wbf = _uvz_proj(hn, gmlp_w_in, li, nw, "gmlp_uvz", gmlp_w_out)
            a = _spatial(uvz, gmlp_ln_g[li], gmlp_ln_b[li], gmlp_w_s[li], gmlp_b_s[li])
            out_tm, out_name = 256, "gmlp_out"
        if layer + 1 < depth:
            x, hn = _out_proj(a, wbf, x, mod4, layer, norm_g3, out_name, out_tm)
        else:
            (yp,) = _out_proj(a, wbf, x, mod4, layer, None, out_name + "_prompt", out_tm,
                              rows=(0, PROMPT_TOKENS))
            (ys,) = _out_proj(a, wbf, x, mod4, layer, None, out_name + "_latent", out_tm,
                              rows=(PROMPT_TOKENS, SAMPLE_TOKENS))

    cache_shape = lambda layers, heads: jnp.stack(
        [c_.reshape(N_PROMPT, PROMPT_SEQ, heads, HEAD_DIM) for c_ in layers], axis=1)
    return (yp.reshape(N_PROMPT, PROMPT_SEQ, D_MODEL), ys.reshape(N_SAMPLE, SAMPLE_SEQ, D_MODEL),
            cache_shape(new_win_k, win_kv_heads), cache_shape(new_win_v, win_kv_heads),
            cache_shape(new_nat_k, nat_heads), cache_shape(new_nat_v, nat_heads))
```

```python
import functools
import math
from typing import Callable, NamedTuple

import jax
import jax.numpy as jnp
import numpy as np
from jax import lax
from jax.experimental import pallas as pl
from jax.experimental.pallas import tpu as pltpu

F32 = jnp.float32
BF16 = jnp.bfloat16

D_MODEL = 2048
HEAD_DIM = 128
N_PROMPT = 16
PROMPT_SEQ = 256
N_SAMPLE = 2
SAMPLE_SEQ = 1024
PROMPT_TOKENS = N_PROMPT * PROMPT_SEQ
SAMPLE_TOKENS = N_SAMPLE * SAMPLE_SEQ
TOKENS = PROMPT_TOKENS + SAMPLE_TOKENS
PAST_LEN = 512
GRID_W = 64
EPS = 1e-6
NEG_INF = -1e30
ROPE_THETA = 10000.0
WINDOW = 128
NAT_ROWS = 8
NAT_COLS = 16
GMLP_WIDTH = 2 * D_MODEL
GMLP_GROUPS = 16
GMLP_GROUP_WIDTH = GMLP_WIDTH // GMLP_GROUPS
CHUNK = 128
CTX_COND_ROW = 2
COND_ROWS = 8
SM_SCALE = HEAD_DIM ** -0.5
LOG2E = math.log2(math.e)
QUERY_SCALE = SM_SCALE * LOG2E

VMEM_LIMIT = 56 * 1024 * 1024


def _params(n_axes, vmem=VMEM_LIMIT, flags=None):
    return pltpu.CompilerParams(dimension_semantics=("arbitrary",) * n_axes,
                                vmem_limit_bytes=vmem, flags=flags)


def _cond_row(tok0):
    return jnp.where(tok0 < PROMPT_TOKENS, CTX_COND_ROW, (tok0 - PROMPT_TOKENS) // SAMPLE_SEQ)


def _silu(x):
    return x * (0.5 + 0.5 * jnp.tanh(0.5 * x))


def _gelu_tanh(x):
    return 0.5 * x * (1.0 + jnp.tanh(math.sqrt(2.0 / math.pi) * (x + 0.044715 * (x * x * x))))


def _split_specs(tm, width, tile0=0):
    n_p = PROMPT_TOKENS // tm
    return [pl.BlockSpec((tm, width), lambda i: (jnp.minimum(i + tile0, n_p - 1), 0)),
            pl.BlockSpec((tm, width), lambda i: (jnp.maximum(i + tile0 - n_p, 0), 0))]


def _ada_kernel(cond_ref, w_ref, b_ref, o_ref):
    s = _silu(cond_ref[...]).astype(BF16)
    o_ref[...] = jnp.dot(s, w_ref[...].astype(BF16), preferred_element_type=F32) + b_ref[...]


def _ada(cond, w_ada, b_ada, tn=1024):
    depth = w_ada.shape[0]
    n = w_ada.shape[2]
    return pl.pallas_call(
        _ada_kernel,
        grid=(depth, n // tn),
        in_specs=[pl.BlockSpec((COND_ROWS, D_MODEL), lambda l, j: (0, 0)),
                  pl.BlockSpec((None, D_MODEL, tn), lambda l, j: (l, 0, j)),
                  pl.BlockSpec((None, 1, tn), lambda l, j: (l, 0, j))],
        out_specs=pl.BlockSpec((None, COND_ROWS, tn), lambda l, j: (l, 0, j)),
        out_shape=jax.ShapeDtypeStruct((depth, COND_ROWS, n), F32),
        compiler_params=_params(2),
        name="ada_mod",
    )(cond, w_ada, b_ada.reshape(depth, 1, n))


NORM_ROWS = 64


def _norm_kernel(xp_ref, xs_ref, g_ref, shift_ref, scale_ref, o_ref, *, tm):
    gmul = g_ref[...] * (1.0 + scale_ref[...])
    shift = shift_ref[...]

    def run(x_ref):
        for r in range(tm // NORM_ROWS):
            rows = slice(r * NORM_ROWS, (r + 1) * NORM_ROWS)
            x = x_ref[rows, :]
            rs = lax.rsqrt(jnp.mean(x * x, axis=-1, keepdims=True) + EPS)
            o_ref[rows, :] = (x * rs * gmul + shift).astype(BF16)

    is_prompt = pl.program_id(0) < PROMPT_TOKENS // tm
    pl.when(is_prompt)(lambda: run(xp_ref))
    pl.when(jnp.logical_not(is_prompt))(lambda: run(xs_ref))


def _norm_mod(xp, xs, norm_g3, mod4, layer, tm=1024):
    return pl.pallas_call(
        functools.partial(_norm_kernel, tm=tm),
        grid=(TOKENS // tm,),
        in_specs=[*_split_specs(tm, D_MODEL),
                  pl.BlockSpec((None, 1, D_MODEL), lambda i: (layer, 0, 0)),
                  pl.BlockSpec((None, None, 1, D_MODEL), lambda i: (layer, _cond_row(i * tm), 0, 0)),
                  pl.BlockSpec((None, None, 1, D_MODEL), lambda i: (layer, _cond_row(i * tm), 0, 1))],
        out_specs=pl.BlockSpec((tm, D_MODEL), lambda i: (i, 0)),
        out_shape=jax.ShapeDtypeStruct((TOKENS, D_MODEL), BF16),
        compiler_params=_params(1),
        name="norm_mod",
    )(xp, xs, norm_g3, mod4, mod4)


PROJ_TM = 2048
PROJ_TN = 512
PROJ_TN_WIDE = 1024
PROJ_ROWS = 512
N_PROMPT_TILES = PROMPT_TOKENS // PROJ_TM


class _Segment(NamedTuple):
    col_blk0: int
    n_blk: int
    epilogue: Callable
    by_tile_kind: bool
    outs: tuple
    pair_blk0: int = -1


def _segment_steps(segments, wide):
    spans, j0 = [], 0
    for seg in segments:
        if seg.pair_blk0 >= 0:
            assert wide == 2
            nj = seg.n_blk
        else:
            assert seg.col_blk0 % wide == 0 and seg.n_blk % wide == 0
            nj = seg.n_blk // wide
        spans.append((j0, nj))
        j0 += nj
    return spans


def _proj_kernel(*refs, segments, spans, n_w, n_extra, n_out, cast_steps):
    a_ref, w_refs = refs[0], refs[1:1 + n_w]
    extras = refs[1 + n_w:1 + n_w + n_extra]
    outs = refs[1 + n_w + n_extra:1 + n_w + n_extra + n_out]
    wbf_ref = refs[1 + n_w + n_extra + n_out]

    @pl.when(pl.program_id(1) == 0)
    def _():
        for h, w_ref in enumerate(w_refs):
            wbf_ref[:, h * PROJ_TN:(h + 1) * PROJ_TN] = w_ref[...].astype(BF16)

    if cast_steps:
        step = pl.program_id(0) * pl.num_programs(1) + pl.program_id(1)

        @pl.when(step < cast_steps)
        def _():
            outs[-1][...] = extras[-1][...].astype(BF16)

    def run(seg):
        seg_outs = [outs[k] for k in seg.outs]

        def body(latent):
            for rc in range(PROJ_TM // PROJ_ROWS):
                rows = slice(rc * PROJ_ROWS, (rc + 1) * PROJ_ROWS)
                accs = [jnp.dot(a_ref[rows, :], wbf_ref[:, col0:col0 + PROJ_TN], preferred_element_type=F32)
                        for col0 in range(0, wbf_ref.shape[1], PROJ_TN)]
                if seg.pair_blk0 >= 0:
                    seg.epilogue(*accs, extras, seg_outs, rc, latent)
                else:
                    for c, acc in enumerate(accs):
                        seg.epilogue(acc, extras, seg_outs, rc, c * PROJ_TN, latent)

        if seg.by_tile_kind:
            is_latent = pl.program_id(1) >= N_PROMPT_TILES
            pl.when(is_latent)(lambda: body(True))
            pl.when(jnp.logical_not(is_latent))(lambda: body(False))
        else:
            body(None)

    j = pl.program_id(0)
    for seg, (j0, nj) in zip(segments, spans):
        if len(segments) == 1:
            run(seg)
        else:
            pl.when(jnp.logical_and(j >= j0, j < j0 + nj))(functools.partial(run, seg))


def _proj(a, w, layer, segments, out_shapes, out_specs, name, extras=(), extra_specs=(), tn=PROJ_TN,
          run_after=None, cast=None):
    m, k = a.shape
    if run_after is not None:
        extras = [*extras, run_after]
        extra_specs = [*extra_specs, pl.BlockSpec(memory_space=pl.ANY)]
    wide = tn // PROJ_TN
    spans = _segment_steps(segments, wide)
    cast_steps = 0
    if cast is not None:
        w_out, out_layer, cast_steps = cast
        n_tiles = m // PROJ_TM
        assert cast_steps <= n_tiles * sum(nj for _, nj in spans)
        w_spec, wbf_spec, wbf_shape = _cast_slab_specs(
            w_out, out_layer, cast_steps, lambda j, i: jnp.minimum(j * n_tiles + i, cast_steps - 1))
        extras, extra_specs = [*extras, w_out], [*extra_specs, w_spec]
        out_shapes, out_specs = [*out_shapes, wbf_shape], [*out_specs, wbf_spec]

    def w_col(j, half):
        col = None
        for seg, (j0, _) in zip(segments, spans):
            if seg.pair_blk0 >= 0:
                c = (seg.pair_blk0 if half else seg.col_blk0) + j - j0
            else:
                c = seg.col_blk0 + wide * (j - j0) + half
            col = c if col is None else jnp.where(j >= j0, c, col)
        return col

    w_specs = [pl.BlockSpec((None, k, PROJ_TN), lambda j, i, half=half: (layer, 0, w_col(j, half)))
               for half in range(wide)]
    kern = functools.partial(_proj_kernel, segments=segments, spans=spans, n_w=wide, n_extra=len(extras),
                             n_out=len(out_shapes), cast_steps=cast_steps)
    return pl.pallas_call(
        kern,
        grid=(sum(nj for _, nj in spans), m // PROJ_TM),
        in_specs=[pl.BlockSpec((PROJ_TM, k), lambda j, i: (i, 0)), *w_specs, *extra_specs],
        out_specs=out_specs,
        out_shape=out_shapes,
        scratch_shapes=[pltpu.VMEM((k, tn), BF16)],
        compiler_params=_params(2),
        name=name,
    )(a, *[w] * wide, *extras)


def _full_out(n_blk):
    return ([jax.ShapeDtypeStruct((TOKENS, n_blk * PROJ_TN), BF16)],
            [pl.BlockSpec((PROJ_TM, PROJ_TN_WIDE), lambda j, i: (i, j))])


def _held(span, j, inside, before, after):
    j0, nj = span
    pick = lambda a, b, c: jnp.where(j < j0, b, jnp.where(j >= j0 + nj, c, a))
    return tuple(pick(a, b, c) for a, b, c in zip(inside, before, after))


def _split_out(span, n_blk, head_rows):
    prompt_tile = lambda i: jnp.minimum(i, N_PROMPT_TILES - 1)
    last_prompt, last_tile = N_PROMPT_TILES - 1, TOKENS // PROJ_TM - 1
    jj = lambda j: j - span[0]
    if head_rows:
        assert n_blk == 1
        heads = PROJ_TN // HEAD_DIM
        cache_shape = jax.ShapeDtypeStruct((PROMPT_TOKENS * heads, HEAD_DIM), F32)
        cache_spec = pl.BlockSpec(
            (PROJ_TM * heads, HEAD_DIM),
            lambda j, i: _held(span, j, (prompt_tile(i), 0), (0, 0), (last_prompt, 0)))
    else:
        cache_shape = jax.ShapeDtypeStruct((N_PROMPT, PROMPT_SEQ, n_blk * PROJ_TN), F32)
        cache_spec = pl.BlockSpec(
            (PROJ_TM // PROMPT_SEQ, PROMPT_SEQ, PROJ_TN),
            lambda j, i: _held(span, j, (prompt_tile(i), 0, jj(j)), (0, 0, 0), (last_prompt, 0, n_blk - 1)))
    shapes = [cache_shape, jax.ShapeDtypeStruct((TOKENS, n_blk * PROJ_TN), BF16)]
    specs = [cache_spec,
             pl.BlockSpec((PROJ_TM, PROJ_TN),
                          lambda j, i: _held(span, j, (i, jj(j)), (0, 0), (last_tile, n_blk - 1)))]
    return shapes, specs


def _chunk_rows(rc):
    return slice(rc * PROJ_ROWS, (rc + 1) * PROJ_ROWS)


def _epi_silu(acc, extras, outs, rc, col0, latent):
    outs[0][_chunk_rows(rc), col0:col0 + PROJ_TN] = _silu(acc).astype(BF16)


def _epi_gelu(acc, extras, outs, rc, col0, latent):
    outs[0][_chunk_rows(rc), col0:col0 + PROJ_TN] = _gelu_tanh(acc).astype(BF16)


def _store_split(y, outs, rc, sl, latent):
    outs[1][_chunk_rows(rc), sl] = y.astype(BF16)
    if latent:
        return
    if len(outs[0].shape) == 2:
        heads = PROJ_TN // HEAD_DIM
        col0 = sl.start or 0
        for c in range(y.shape[1] // HEAD_DIM):
            head = col0 // HEAD_DIM + c
            rows = pl.ds(rc * PROJ_ROWS * heads + head, PROJ_ROWS, stride=heads)
            outs[0][rows, :] = y[:, c * HEAD_DIM:(c + 1) * HEAD_DIM]
    else:
        seqs = PROJ_ROWS // PROMPT_SEQ
        outs[0][rc * seqs:(rc + 1) * seqs, :, sl] = y.reshape(seqs, PROMPT_SEQ, y.shape[1])


def _epi_value(acc, extras, outs, rc, col0, latent):
    _store_split(acc, outs, rc, slice(col0, col0 + PROJ_TN), latent)


def _head_rmsnorm(acc, hh, g):
    a = acc[:, hh * HEAD_DIM:(hh + 1) * HEAD_DIM]
    return a * lax.rsqrt(jnp.mean(a * a, axis=-1, keepdims=True) + EPS) * g


def _normed_heads(acc, extras, rc, rope):
    g = extras[0][...]
    ys = [_head_rmsnorm(acc, hh, g) for hh in range(PROJ_TN // HEAD_DIM)]
    if rope:
        rows = _chunk_rows(rc)
        cos, sin, swap = extras[1][rows, :], extras[2][rows, :], extras[3][...]
        for pair in range(len(ys) // 2):
            both = jnp.concatenate(ys[2 * pair:2 * pair + 2], axis=1).astype(BF16)
            partner = jnp.dot(both, swap, preferred_element_type=F32)
            for t in range(2):
                hh = 2 * pair + t
                ys[hh] = ys[hh] * cos + partner[:, t * HEAD_DIM:(t + 1) * HEAD_DIM] * sin
    for hh, y in enumerate(ys):
        yield slice(hh * HEAD_DIM, (hh + 1) * HEAD_DIM), y


def _epi_query(acc, extras, outs, rc, col0, latent, *, rope):
    for sl, y in _normed_heads(acc, extras, rc, rope and latent):
        outs[0][_chunk_rows(rc), col0 + sl.start:col0 + sl.stop] = (y * QUERY_SCALE).astype(BF16)


def _epi_key(acc, extras, outs, rc, col0, latent, *, rope):
    for sl, y in _normed_heads(acc, extras, rc, rope and latent):
        _store_split(y, outs, rc, slice(col0 + sl.start, col0 + sl.stop), latent)


def _rope_tables():
    nf = HEAD_DIM // 4
    t = np.arange(SAMPLE_TOKENS) % SAMPLE_SEQ
    row = (t // GRID_W).astype(np.float32)
    col = (t % GRID_W).astype(np.float32)
    inv = np.float32(ROPE_THETA) ** (-np.arange(nf, dtype=np.float32) / np.float32(nf))
    ang_r = row[:, None] * inv
    ang_c = col[:, None] * inv
    cos = np.concatenate([np.cos(ang_r), np.cos(ang_r), np.cos(ang_c), np.cos(ang_c)], axis=1)
    sin = np.concatenate([-np.sin(ang_r), np.sin(ang_r), -np.sin(ang_c), np.sin(ang_c)], axis=1)
    lanes = np.arange(2 * HEAD_DIM)
    swap = lanes[:, None] == (lanes[None, :] ^ (HEAD_DIM // 4))
    return jnp.asarray(cos, F32), jnp.asarray(sin, F32), jnp.asarray(swap, BF16)


def _gain_extras(gain, rope_tabs):
    extras = [gain.reshape(1, HEAD_DIM)]
    specs = [pl.BlockSpec((1, HEAD_DIM), lambda j, i: (0, 0))]
    if rope_tabs is not None:
        tab_map = lambda j, i: (jnp.maximum(i - N_PROMPT_TILES, 0), 0)
        extras += list(rope_tabs)
        specs += [pl.BlockSpec((PROJ_TM, HEAD_DIM), tab_map)] * 2
        specs.append(pl.BlockSpec((2 * HEAD_DIM, 2 * HEAD_DIM), lambda j, i: (0, 0)))
    return extras, specs


def _qz_proj(hn, w, layer, q_blk0, z_blk0, n_blk, gain, rope_tabs, name):
    extras, specs = _gain_extras(gain, rope_tabs)
    rope = rope_tabs is not None
    segments = (_Segment(q_blk0, n_blk, functools.partial(_epi_query, rope=rope), rope, (0,)),
                _Segment(z_blk0, n_blk, _epi_silu, False, (0,)))
    return _proj(hn, w, layer, segments, *_full_out(2 * n_blk), name, extras, specs, tn=PROJ_TN_WIDE)[0]


def _kv_proj(hn, w, layer, k_blk0, n_blk, gain, rope_tabs, head_rows, name, run_after=None):
    extras, specs = _gain_extras(gain, rope_tabs)
    segments = (_Segment(k_blk0, n_blk, functools.partial(_epi_key, rope=rope_tabs is not None), True, (0, 1)),
                _Segment(k_blk0 + n_blk, n_blk, _epi_value, True, (2, 3)))
    shapes, out_specs = [], []
    for span in _segment_steps(segments, 1):
        seg_shapes, seg_specs = _split_out(span, n_blk, head_rows)
        shapes += seg_shapes
        out_specs += seg_specs
    return _proj(hn, w, layer, segments, shapes, out_specs, name, extras, specs, run_after=run_after)


UVZ_CAST_STEPS = 32


def _epi_uz(acc_u, acc_z, extras, outs, rc, latent):
    outs[0][_chunk_rows(rc), :] = (_gelu_tanh(acc_u) * _silu(acc_z)).astype(BF16)


def _uzv_proj(hn, w, layer, n_blk, name, w_out):
    segments = (_Segment(0, n_blk, _epi_uz, False, (0,), pair_blk0=2 * n_blk),
                _Segment(n_blk, n_blk, _epi_gelu, False, (1,)))
    (uz_span, v_span) = _segment_steps(segments, PROJ_TN_WIDE // PROJ_TN)
    last_tile = TOKENS // PROJ_TM - 1
    shape = jax.ShapeDtypeStruct((TOKENS, n_blk * PROJ_TN), BF16)
    out_specs = [
        pl.BlockSpec((PROJ_TM, PROJ_TN),
                     lambda j, i: _held(uz_span, j, (i, j - uz_span[0]), (0, 0), (last_tile, uz_span[1] - 1))),
        pl.BlockSpec((PROJ_TM, PROJ_TN_WIDE),
                     lambda j, i: _held(v_span, j, (i, j - v_span[0]), (0, 0), (last_tile, v_span[1] - 1)))]
    return _proj(hn, w, layer, segments, [shape, shape], out_specs, name, tn=PROJ_TN_WIDE,
                 cast=(w_out, layer, UVZ_CAST_STEPS))


OUT_COLS = 512


def _cast_slab_specs(w, layer, n_slabs, slab_of_step):
    k, n = w.shape[1], w.shape[2]
    rows = k // n_slabs
    return (pl.BlockSpec((None, rows, n), lambda *g: (layer, slab_of_step(*g), 0)),
            pl.BlockSpec((rows, n), lambda *g: (slab_of_step(*g), 0)),
            jax.ShapeDtypeStruct((k, n), BF16))


def _out_kernel(*refs, tile0, tm, split_a, split_x, fuse_norm):
    it = iter(refs)
    a_refs = [next(it), next(it)] if split_a else [next(it)]
    wbf_ref = next(it)
    x_refs = [next(it), next(it)] if split_x else [next(it)]
    gate_ref = next(it)
    if fuse_norm:
        g_ref, shift_ref, scale_ref = next(it), next(it), next(it)
    xnew_ref = next(it)
    hn_ref = next(it) if fuse_norm else None

    def body(a_ref, x_ref):
        ssq = jnp.zeros((tm, 1), F32)
        for cb in range(D_MODEL // OUT_COLS):
            sl = slice(cb * OUT_COLS, (cb + 1) * OUT_COLS)
            acc = jnp.dot(a_ref[...], wbf_ref[:, sl], preferred_element_type=F32)
            xn = x_ref[:, sl] + gate_ref[:, sl] * acc
            xnew_ref[:, sl] = xn
            if fuse_norm:
                ssq = ssq + jnp.sum(xn * xn, axis=-1, keepdims=True)
        if fuse_norm:
            rs = lax.rsqrt(ssq * (1.0 / D_MODEL) + EPS)
            for cb in range(D_MODEL // OUT_COLS):
                sl = slice(cb * OUT_COLS, (cb + 1) * OUT_COLS)
                gmul = g_ref[:, sl] * (1.0 + scale_ref[:, sl])
                hn_ref[:, sl] = (xnew_ref[:, sl] * rs * gmul + shift_ref[:, sl]).astype(BF16)

    if split_a or split_x:
        is_prompt = pl.program_id(0) + tile0 < PROMPT_TOKENS // tm
        pl.when(is_prompt)(lambda: body(a_refs[0], x_refs[0]))
        pl.when(jnp.logical_not(is_prompt))(lambda: body(a_refs[-1], x_refs[-1]))
    else:
        body(a_refs[0], x_refs[0])


def _out_proj(a, wbf, x, mod4, layer, norm_g3, name, tm, rows=None):
    split_a = isinstance(a, tuple)
    k = a[0].shape[1] if split_a else a.shape[1]
    tok0, n_tok = rows if rows is not None else (0, TOKENS)
    tile0 = tok0 // tm
    split_x = isinstance(x, tuple)
    fuse_norm = norm_g3 is not None
    mod_spec = lambda part, lyr: pl.BlockSpec(
        (None, None, 1, D_MODEL), lambda i: (lyr, _cond_row((i + tile0) * tm), 0, part))
    row_spec = lambda width: pl.BlockSpec((tm, width), lambda i: (i + tile0, 0))
    operands = [*a, wbf] if split_a else [a, wbf]
    in_specs = [*(_split_specs(tm, k, tile0) if split_a else [row_spec(k)]),
                pl.BlockSpec((k, D_MODEL), lambda i: (0, 0))]
    if split_x:
        assert rows is None
        operands += list(x)
        in_specs += _split_specs(tm, D_MODEL)
    else:
        operands.append(x)
        in_specs.append(row_spec(D_MODEL))
    operands.append(mod4)
    in_specs.append(mod_spec(2, layer))
    out_shapes = [jax.ShapeDtypeStruct((n_tok, D_MODEL), F32)]
    out_specs = [pl.BlockSpec((tm, D_MODEL), lambda i: (i, 0))]
    if fuse_norm:
        operands += [norm_g3, mod4, mod4]
        in_specs += [pl.BlockSpec((None, 1, D_MODEL), lambda i: (layer + 1, 0, 0)),
                     mod_spec(0, layer + 1), mod_spec(1, layer + 1)]
        out_shapes.append(jax.ShapeDtypeStruct((n_tok, D_MODEL), BF16))
        out_specs.append(pl.BlockSpec((tm, D_MODEL), lambda i: (i, 0)))
    kern = functools.partial(_out_kernel, tile0=tile0, tm=tm,
                             split_a=split_a, split_x=split_x, fuse_norm=fuse_norm)
    return pl.pallas_call(
        kern,
        grid=(n_tok // tm,),
        in_specs=in_specs,
        out_specs=out_specs,
        out_shape=out_shapes,
        compiler_params=_params(1),
        name=name,
    )(*operands)


def _dot_nt(a, b):
    return lax.dot_general(a, b, (((1,), (1,)), ((), ())), preferred_element_type=F32)


def _head(ref, h, rows=slice(None)):
    return ref[rows, h * HEAD_DIM:(h + 1) * HEAD_DIM]


def _with_ones(v):
    return jnp.concatenate([v, jnp.ones(v.shape, v.dtype)], axis=1)


def _lane_chunks(s):
    return [s[:, c * HEAD_DIM:(c + 1) * HEAD_DIM] for c in range(s.shape[1] // HEAD_DIM)]


def _softmax_pv(score_blocks, value_blocks, sink):
    rows = score_blocks[0].shape[0]
    mx = functools.reduce(jnp.maximum, [c for s in score_blocks for c in _lane_chunks(s)])
    m = jnp.broadcast_to(jnp.max(mx, axis=-1, keepdims=True), (rows, HEAD_DIM))
    if sink is not None:
        m = jnp.maximum(m, sink)
    o = None
    for s, v in zip(score_blocks, value_blocks):
        p = jnp.concatenate([jnp.exp2(c - m) for c in _lane_chunks(s)], axis=1).astype(BF16)
        part = jnp.dot(p, v, preferred_element_type=F32)
        o = part if o is None else o + part
    den = o[:, HEAD_DIM:]
    if sink is not None:
        den = den + jnp.exp2(sink - m)
    return o[:, :HEAD_DIM] * (1.0 / den)


def _attn_prompt_kernel(sink_ref, q_ref, k_ref, v_ref, z_ref, *rest, n_heads, n_kv, use_sink):
    o_ref = rest[-1]
    grp = n_heads // n_kv
    for g in range(n_kv):
        kg = _head(k_ref, g)
        vg = _with_ones(_head(v_ref, g))
        for h in range(g * grp, (g + 1) * grp):
            s = _dot_nt(_head(q_ref, h), kg)
            sink = jnp.full((PROMPT_SEQ, HEAD_DIM), sink_ref[h], F32) if use_sink else None
            o = _softmax_pv([s], [vg], sink)
            o_ref[:, h * HEAD_DIM:(h + 1) * HEAD_DIM] = (o * _head(z_ref, h).astype(F32)).astype(BF16)


def _attn_prompt(sink, qz, k, v, n_kv, use_sink, name, run_after=None):
    ordering = [] if run_after is None else [run_after]
    width = qz.shape[1] // 2
    kern = functools.partial(_attn_prompt_kernel, n_heads=width // HEAD_DIM, n_kv=n_kv, use_sink=use_sink)
    cache_spec = pl.BlockSpec((PROMPT_SEQ, k.shape[1]), lambda b: (b, 0))
    return pl.pallas_call(
        kern,
        grid=(N_PROMPT,),
        in_specs=[pl.BlockSpec(memory_space=pltpu.SMEM),
                  pl.BlockSpec((PROMPT_SEQ, width), lambda b: (b, 0)),
                  cache_spec, cache_spec,
                  pl.BlockSpec((PROMPT_SEQ, width), lambda b: (b, 1)),
                  *[pl.BlockSpec(memory_space=pl.ANY) for _ in ordering]],
        out_specs=pl.BlockSpec((PROMPT_SEQ, width), lambda b: (b, 0)),
        out_shape=jax.ShapeDtypeStruct((PROMPT_TOKENS, width), BF16),
        compiler_params=_params(1),
        name=name,
    )(sink, qz, k, v, qz, *ordering)


WIN_BAND = 3 * WINDOW


def _attn_win_kernel(sink_ref, q_ref, k_ref, v_ref, kc_ref, vc_ref, z_ref, w_ref, o_ref, wbf_ref, *,
                     n_heads, n_kv):
    wbf_ref[...] = w_ref[...].astype(BF16)
    grp = n_heads // n_kv
    n = pl.program_id(1)
    start = pl.multiple_of(jnp.clip((n - 1) * WINDOW, 0, SAMPLE_SEQ - WIN_BAND), WINDOW)
    shape = (grp * WINDOW, WIN_BAND)
    qpos = n * WINDOW + (lax.broadcasted_iota(jnp.int32, shape, 0) & (WINDOW - 1))
    kpos = start + lax.broadcasted_iota(jnp.int32, shape, 1)
    valid = jnp.abs(kpos - qpos) <= WINDOW
    band = pl.ds(start, WIN_BAND)
    for g in range(n_kv):
        heads = [g * grp + t for t in range(grp)]
        qs = jnp.concatenate([_head(q_ref, h) for h in heads], axis=0)
        kb = _head(k_ref, g, band)
        vb = _with_ones(_head(v_ref, g, band))
        ctx_rows = pl.ds(g, PAST_LEN, stride=n_kv)
        kc = kc_ref[ctx_rows, :].astype(BF16)
        vc = _with_ones(vc_ref[ctx_rows, :].astype(BF16))
        s_band = jnp.where(valid, _dot_nt(qs, kb), NEG_INF)
        s_ctx = _dot_nt(qs, kc)
        sink = jnp.concatenate([jnp.full((WINDOW, HEAD_DIM), sink_ref[h], F32) for h in heads], axis=0)
        o = _softmax_pv([s_band, s_ctx], [vb, vc], sink)
        for t, h in enumerate(heads):
            oh = o[t * WINDOW:(t + 1) * WINDOW]
            o_ref[:, h * HEAD_DIM:(h + 1) * HEAD_DIM] = (oh * _head(z_ref, h).astype(F32)).astype(BF16)


def _attn_win(sink, qz, k, v, cache_k, cache_v, layer_in_kind, n_kv, w_out):
    width = qz.shape[1] // 2
    kv_width = k.shape[1]
    blocks_per_seq = SAMPLE_SEQ // WINDOW
    q_map = lambda b, n: (PROMPT_TOKENS // WINDOW + b * blocks_per_seq + n, 0)
    z_map = lambda b, n: (PROMPT_TOKENS // WINDOW + b * blocks_per_seq + n, 1)
    kv_map = lambda b, n: (PROMPT_TOKENS // SAMPLE_SEQ + b, 0)
    cache_map = lambda b, n: (b, layer_in_kind, 0, 0)
    kern = functools.partial(_attn_win_kernel, n_heads=width // HEAD_DIM, n_kv=n_kv)
    w_spec, wbf_spec, wbf_shape = _cast_slab_specs(w_out, layer_in_kind, N_SAMPLE * blocks_per_seq,
                                                   lambda b, n: b * blocks_per_seq + n)
    return pl.pallas_call(
        kern,
        grid=(N_SAMPLE, blocks_per_seq),
        in_specs=[pl.BlockSpec(memory_space=pltpu.SMEM),
                  pl.BlockSpec((WINDOW, width), q_map),
                  pl.BlockSpec((SAMPLE_SEQ, kv_width), kv_map),
                  pl.BlockSpec((SAMPLE_SEQ, kv_width), kv_map),
                  pl.BlockSpec((None, None, PAST_LEN * n_kv, HEAD_DIM), cache_map),
                  pl.BlockSpec((None, None, PAST_LEN * n_kv, HEAD_DIM), cache_map),
                  pl.BlockSpec((WINDOW, width), z_map),
                  w_spec],
        out_specs=[pl.BlockSpec((WINDOW, width), lambda b, n: (b * blocks_per_seq + n, 0)), wbf_spec],
        out_shape=[jax.ShapeDtypeStruct((SAMPLE_TOKENS, width), BF16), wbf_shape],
        compiler_params=_params(2),
        name="attn_win_latent",
    )(sink, qz, k, v, cache_k, cache_v, qz, w_out)


NAT_QROWS = 4
GRID_ROWS = SAMPLE_SEQ // GRID_W
N_DR = 2 * NAT_ROWS - 1
MASKED_TILE = N_DR


def _nat_row_start(qr):
    return min(max(qr - NAT_ROWS // 2, 0), GRID_ROWS - NAT_ROWS)


def _attn_nat_kernel(bias_ref, q_ref, k_ref, v_ref, kc_ref, vc_ref, z_ref, w_ref, o_ref, wbf_ref,
                     left_ref, right_ref):
    wbf_ref[...] = w_ref[...].astype(BF16)

    def build_bias_tiles():
        shape = (GRID_W, 2 * GRID_W)
        lane = lax.broadcasted_iota(jnp.int32, shape, 1)
        qc = lax.broadcasted_iota(jnp.int32, shape, 0)
        kc_ = lane & (GRID_W - 1)
        cs = jnp.clip(qc - NAT_COLS // 2, 0, GRID_W - NAT_COLS)
        col_ok = jnp.logical_and(kc_ >= cs, kc_ < cs + NAT_COLS)
        is_left = lane < GRID_W
        lanes = 2 * GRID_W
        for dri in range(N_DR):
            row = jnp.broadcast_to(bias_ref[dri:dri + 1, :], shape)
            on_left = pltpu.roll(row, lanes - (NAT_COLS - 1), 1, stride=1, stride_axis=0)
            on_right = pltpu.roll(row, GRID_W - (NAT_COLS - 1), 1, stride=1, stride_axis=0)
            left_ref[dri] = jnp.where(is_left, jnp.where(col_ok, on_left, NEG_INF), 0.0)
            right_ref[dri] = jnp.where(is_left, 0.0, jnp.where(col_ok, on_right, NEG_INF))
        left_ref[MASKED_TILE] = jnp.where(is_left, NEG_INF, 0.0)
        right_ref[MASKED_TILE] = jnp.where(is_left, 0.0, NEG_INF)

    build_bias_tiles()
    n_heads = kc_ref.shape[1] // PAST_LEN
    ctx_rows = pl.ds(pl.program_id(0), PAST_LEN, stride=n_heads)
    for b, qb in [(b, qb) for b in range(N_SAMPLE) for qb in range(GRID_ROWS // NAT_QROWS)]:
        if qb == 0:
            kc = kc_ref[b, ctx_rows, :].astype(BF16)
            vc = _with_ones(vc_ref[b, ctx_rows, :].astype(BF16))
        qrows = range(qb * NAT_QROWS, (qb + 1) * NAT_QROWS)
        krow0 = _nat_row_start(qrows[0]) // 2 * 2
        krow1 = -(-(_nat_row_start(qrows[-1]) + NAT_ROWS) // 2) * 2
        tok0 = b * SAMPLE_SEQ
        kwin = slice(tok0 + krow0 * GRID_W, tok0 + krow1 * GRID_W)
        qwin = slice(tok0 + qrows[0] * GRID_W, tok0 + (qrows[-1] + 1) * GRID_W)

        def tile_index(qr, kr):
            rs = _nat_row_start(qr)
            return kr - qr + (NAT_ROWS - 1) if rs <= kr < rs + NAT_ROWS else MASKED_TILE

        bias = jnp.concatenate(
            [jnp.concatenate([left_ref[tile_index(qr, kr)] + right_ref[tile_index(qr, kr + 1)]
                              for kr in range(krow0, krow1, 2)], axis=1)
             for qr in qrows], axis=0)

        q = q_ref[qwin, :]
        s_nb = _dot_nt(q, k_ref[kwin, :]) + bias
        s_ctx = _dot_nt(q, kc)
        o = _softmax_pv([s_nb, s_ctx], [_with_ones(v_ref[kwin, :]), vc], None)
        o_ref[qwin, :] = (o * z_ref[qwin, :].astype(F32)).astype(BF16)


def _attn_nat(rel_bias, qz, k, v, cache_k, cache_v, layer_in_kind, w_out):
    n_heads = qz.shape[1] // (2 * HEAD_DIM)
    bias_rows = jnp.pad(rel_bias, ((0, 0), (0, N_DR + 1 - rel_bias.shape[1]),
                                   (0, 2 * GRID_W - rel_bias.shape[2])))
    latent_tile = PROMPT_TOKENS // SAMPLE_TOKENS
    latent_spec = pl.BlockSpec((SAMPLE_TOKENS, HEAD_DIM), lambda h: (latent_tile, h))
    z_spec = pl.BlockSpec((SAMPLE_TOKENS, HEAD_DIM), lambda h: (latent_tile, n_heads + h))
    cache_spec = pl.BlockSpec((N_SAMPLE, None, PAST_LEN * n_heads, HEAD_DIM),
                              lambda h: (0, layer_in_kind, 0, 0))
    w_spec, wbf_spec, wbf_shape = _cast_slab_specs(w_out, layer_in_kind, n_heads, lambda h: h)
    return pl.pallas_call(
        _attn_nat_kernel,
        grid=(n_heads,),
        in_specs=[pl.BlockSpec((None, N_DR + 1, 2 * GRID_W), lambda h: (h, 0, 0)),
                  latent_spec, latent_spec, latent_spec, cache_spec, cache_spec, z_spec, w_spec],
        out_specs=[pl.BlockSpec((SAMPLE_TOKENS, HEAD_DIM), lambda h: (0, h)), wbf_spec],
        out_shape=[jax.ShapeDtypeStruct((SAMPLE_TOKENS, n_heads * HEAD_DIM), BF16), wbf_shape],
        scratch_shapes=[pltpu.VMEM((N_DR + 1, GRID_W, 2 * GRID_W), F32),
                        pltpu.VMEM((N_DR + 1, GRID_W, 2 * GRID_W), F32)],
        compiler_params=_params(1),
        name="attn_nat_latent",
    )(bias_rows, qz, k, v, cache_k, cache_v, qz, w_out)


SPATIAL_TOKENS = 2 * CHUNK


def _spatial_kernel(uz_ref, v_ref, g_ref, b_ref, ws_ref, bs_ref, o_ref):
    for c in range(SPATIAL_TOKENS // CHUNK):
        rows = slice(c * CHUNK, (c + 1) * CHUNK)
        v = v_ref[rows, :].astype(F32)
        mu = jnp.mean(v, axis=-1, keepdims=True)
        vc = v - mu
        var = jnp.mean(vc * vc, axis=-1, keepdims=True)
        vn = (vc * lax.rsqrt(var + EPS) * g_ref[...] + b_ref[...]).astype(BF16)
        for g in range(GMLP_GROUPS):
            sl = slice(g * GMLP_GROUP_WIDTH, (g + 1) * GMLP_GROUP_WIDTH)
            sv = jnp.dot(ws_ref[g], vn[:, sl], preferred_element_type=F32) + bs_ref[:, g:g + 1]
            o_ref[rows, sl] = uz_ref[rows, sl] * sv.astype(BF16)


def _spatial(uz, v, ln_g, ln_b, w_s, b_s):
    row = pl.BlockSpec((SPATIAL_TOKENS, GMLP_WIDTH), lambda i: (i, 0))
    vec = pl.BlockSpec((1, GMLP_WIDTH), lambda i: (0, 0))
    return pl.pallas_call(
        _spatial_kernel,
        grid=(TOKENS // SPATIAL_TOKENS,),
        in_specs=[row, row, vec, vec,
                  pl.BlockSpec((GMLP_GROUPS, CHUNK, CHUNK), lambda i: (0, 0, 0)),
                  pl.BlockSpec((CHUNK, GMLP_GROUPS), lambda i: (0, 0))],
        out_specs=row,
        out_shape=jax.ShapeDtypeStruct((TOKENS, GMLP_WIDTH), BF16),
        compiler_params=_params(1),
        name="gmlp_spatial",
    )(uz, v, ln_g.reshape(1, -1), ln_b.reshape(1, -1), w_s.astype(BF16), b_s.T)


def kernel(x_prompt, x_sample, cache_win_k, cache_win_v, cache_nat_k, cache_nat_v, c, c_ctx,
           norm_g, w_ada, b_ada,
           win_w_in, win_q_norm, win_k_norm, win_sink, win_w_out,
           nat_w_in, nat_q_norm, nat_k_norm, nat_rel_bias, nat_w_out,
           gmlp_w_in, gmlp_ln_g, gmlp_ln_b, gmlp_w_s, gmlp_b_s, gmlp_w_out):
    depth = norm_g.shape[0]
    xp = x_prompt.reshape(PROMPT_TOKENS, D_MODEL)
    xs = x_sample.reshape(SAMPLE_TOKENS, D_MODEL)
    cond = jnp.zeros((COND_ROWS, D_MODEL), F32).at[:N_SAMPLE].set(c).at[CTX_COND_ROW].set(c_ctx)
    mod4 = _ada(cond, w_ada, b_ada).reshape(depth, COND_ROWS, 1, 3 * D_MODEL)
    norm_g3 = norm_g.reshape(depth, 1, D_MODEL)
    rope_tabs = _rope_tables()
    no_sink = jnp.zeros((1,), F32)

    n_win = win_w_in.shape[0]
    n_nat = nat_w_in.shape[0]
    win_kv_heads = cache_win_k.shape[3]
    win_kv_width = win_kv_heads * HEAD_DIM
    cwk = cache_win_k.reshape(N_SAMPLE, n_win, PAST_LEN * win_kv_heads, HEAD_DIM)
    cwv = cache_win_v.reshape(N_SAMPLE, n_win, PAST_LEN * win_kv_heads, HEAD_DIM)
    nat_heads = cache_nat_k.shape[3]
    cnk = cache_nat_k.reshape(N_SAMPLE, n_nat, PAST_LEN * nat_heads, HEAD_DIM)
    cnv = cache_nat_v.reshape(N_SAMPLE, n_nat, PAST_LEN * nat_heads, HEAD_DIM)

    new_win_k, new_win_v, new_nat_k, new_nat_v = [], [], [], []
    x = (xp, xs)
    hn = _norm_mod(xp, xs, norm_g3, mod4, 0)
    for layer in range(depth):
        kind = layer % 3
        li = layer // 3
        if kind == 0:
            nq = D_MODEL // PROJ_TN
            nkv = win_kv_width // PROJ_TN
            qz = _qz_proj(hn, win_w_in, li, 0, nq + 2 * nkv, nq, win_q_norm[li], rope_tabs, "win_qz")
            kp, ks, vp, vs = _kv_proj(hn, win_w_in, li, nq, nkv, win_k_norm[li], rope_tabs, True, "win_kv",
                                      run_after=qz)
            new_win_k.append(kp)
            new_win_v.append(vp)
            sink = win_sink[li] * LOG2E
            a_latent, wbf = _attn_win(sink, qz, ks, vs, cwk, cwv, li, win_kv_heads, win_w_out)
            a = (_attn_prompt(sink, qz, ks, vs, win_kv_heads, True, "attn_win_prompt"), a_latent)
            out_tm, out_name = 512, "win_out"
        elif kind == 1:
            nq = D_MODEL // PROJ_TN
            qz = _qz_proj(hn, nat_w_in, li, 0, 3 * nq, nq, nat_q_norm[li], None, "nat_qz")
            kp, ks, vp, vs = _kv_proj(hn, nat_w_in, li, nq, nq, nat_k_norm[li], None, False, "nat_kv",
                                      run_after=qz)
            new_nat_k.append(kp)
            new_nat_v.append(vp)
            a_latent, wbf = _attn_nat(nat_rel_bias[li] * LOG2E, qz, ks, vs, cnk, cnv, li, nat_w_out)
            a = (_attn_prompt(no_sink, qz, ks, vs, nat_heads, False, "attn_nat_prompt", run_after=a_latent),
                 a_latent)
            out_tm, out_name = 512, "nat_out"
        else:
            nw = GMLP_WIDTH // PROJ_TN
            uz, v, wbf = _uzv_proj(hn, gmlp_w_in, li, nw, "gmlp_uzv", gmlp_w_out)
            a = _spatial(uz, v, gmlp_ln_g[li], gmlp_ln_b[li], gmlp_w_s[li], gmlp_b_s[li])
            out_tm, out_name = 256, "gmlp_out"
        if layer + 1 < depth:
            x, hn = _out_proj(a, wbf, x, mod4, layer, norm_g3, out_name, out_tm)
        else:
            (yp,) = _out_proj(a, wbf, x, mod4, layer, None, out_name + "_prompt", out_tm,
                              rows=(0, PROMPT_TOKENS))
            (ys,) = _out_proj(a, wbf, x, mod4, layer, None, out_name + "_latent", out_tm,
                              rows=(PROMPT_TOKENS, SAMPLE_TOKENS))

    cache_shape = lambda layers, heads: jnp.stack(
        [c_.reshape(N_PROMPT, PROMPT_SEQ, heads, HEAD_DIM) for c_ in layers], axis=1)
    return (yp.reshape(N_PROMPT, PROMPT_SEQ, D_MODEL), ys.reshape(N_SAMPLE, SAMPLE_SEQ, D_MODEL),
            cache_shape(new_win_k, win_kv_heads), cache_shape(new_win_v, win_kv_heads),
            cache_shape(new_nat_k, nat_heads), cache_shape(new_nat_v, nat_heads))
```

```python
import functools
import math
from typing import Callable, NamedTuple

import jax
import jax.numpy as jnp
import numpy as np
from jax import lax
from jax.experimental import pallas as pl
from jax.experimental.pallas import tpu as pltpu

F32 = jnp.float32
BF16 = jnp.bfloat16

D_MODEL = 2048
HEAD_DIM = 128
N_PROMPT = 16
PROMPT_SEQ = 256
N_SAMPLE = 2
SAMPLE_SEQ = 1024
PROMPT_TOKENS = N_PROMPT * PROMPT_SEQ
SAMPLE_TOKENS = N_SAMPLE * SAMPLE_SEQ
TOKENS = PROMPT_TOKENS + SAMPLE_TOKENS
PAST_LEN = 512
GRID_W = 64
EPS = 1e-6
NEG_INF = -1e30
ROPE_THETA = 10000.0
WINDOW = 128
NAT_ROWS = 8
NAT_COLS = 16
GMLP_WIDTH = 2 * D_MODEL
GMLP_GROUPS = 16
GMLP_GROUP_WIDTH = GMLP_WIDTH // GMLP_GROUPS
CHUNK = 128
CTX_COND_ROW = 2
COND_ROWS = 8
SM_SCALE = HEAD_DIM ** -0.5
LOG2E = math.log2(math.e)
QUERY_SCALE = SM_SCALE * LOG2E

VMEM_LIMIT = 56 * 1024 * 1024


def _params(n_axes):
    return pltpu.CompilerParams(dimension_semantics=("arbitrary",) * n_axes,
                                vmem_limit_bytes=VMEM_LIMIT)


def _cond_row(tok0):
    return jnp.where(tok0 < PROMPT_TOKENS, CTX_COND_ROW, (tok0 - PROMPT_TOKENS) // SAMPLE_SEQ)


def _silu(x):
    return x * (0.5 + 0.5 * jnp.tanh(0.5 * x))


def _gelu_tanh(x):
    return 0.5 * x * (1.0 + jnp.tanh(math.sqrt(2.0 / math.pi) * (x + 0.044715 * (x * x * x))))


def _split_specs(tm, width, tile0=0):
    n_p = PROMPT_TOKENS // tm
    return [pl.BlockSpec((tm, width), lambda i: (jnp.minimum(i + tile0, n_p - 1), 0)),
            pl.BlockSpec((tm, width), lambda i: (jnp.maximum(i + tile0 - n_p, 0), 0))]


def _ada_kernel(cond_ref, w_ref, b_ref, o_ref):
    s = _silu(cond_ref[...]).astype(BF16)
    o_ref[...] = jnp.dot(s, w_ref[...].astype(BF16), preferred_element_type=F32) + b_ref[...]


def _ada(cond, w_ada, b_ada, tn=1024):
    depth = w_ada.shape[0]
    n = w_ada.shape[2]
    return pl.pallas_call(
        _ada_kernel,
        grid=(depth, n // tn),
        in_specs=[pl.BlockSpec((COND_ROWS, D_MODEL), lambda l, j: (0, 0)),
                  pl.BlockSpec((None, D_MODEL, tn), lambda l, j: (l, 0, j)),
                  pl.BlockSpec((None, 1, tn), lambda l, j: (l, 0, j))],
        out_specs=pl.BlockSpec((None, COND_ROWS, tn), lambda l, j: (l, 0, j)),
        out_shape=jax.ShapeDtypeStruct((depth, COND_ROWS, n), F32),
        compiler_params=_params(2),
        name="ada_mod",
    )(cond, w_ada, b_ada.reshape(depth, 1, n))


NORM_ROWS = 64


def _norm_kernel(xp_ref, xs_ref, g_ref, shift_ref, scale_ref, o_ref, *, tm):
    gmul = g_ref[...] * (1.0 + scale_ref[...])
    shift = shift_ref[...]

    def run(x_ref):
        for r in range(tm // NORM_ROWS):
            rows = slice(r * NORM_ROWS, (r + 1) * NORM_ROWS)
            x = x_ref[rows, :]
            rs = lax.rsqrt(jnp.mean(x * x, axis=-1, keepdims=True) + EPS)
            o_ref[rows, :] = (x * rs * gmul + shift).astype(BF16)

    is_prompt = pl.program_id(0) < PROMPT_TOKENS // tm
    pl.when(is_prompt)(lambda: run(xp_ref))
    pl.when(jnp.logical_not(is_prompt))(lambda: run(xs_ref))


def _norm_mod(xp, xs, norm_g3, mod4, layer, tm=1024):
    return pl.pallas_call(
        functools.partial(_norm_kernel, tm=tm),
        grid=(TOKENS // tm,),
        in_specs=[*_split_specs(tm, D_MODEL),
                  pl.BlockSpec((None, 1, D_MODEL), lambda i: (layer, 0, 0)),
                  pl.BlockSpec((None, None, 1, D_MODEL), lambda i: (layer, _cond_row(i * tm), 0, 0)),
                  pl.BlockSpec((None, None, 1, D_MODEL), lambda i: (layer, _cond_row(i * tm), 0, 1))],
        out_specs=pl.BlockSpec((tm, D_MODEL), lambda i: (i, 0)),
        out_shape=jax.ShapeDtypeStruct((TOKENS, D_MODEL), BF16),
        compiler_params=_params(1),
        name="norm_mod",
    )(xp, xs, norm_g3, mod4, mod4)


PROJ_TM = 2048
PROJ_TN = 512
PROJ_TN_WIDE = 1024
PROJ_ROWS = 512
N_PROMPT_TILES = PROMPT_TOKENS // PROJ_TM


class _Segment(NamedTuple):
    col_blk0: int
    n_blk: int
    epilogue: Callable
    by_tile_kind: bool
    outs: tuple
    pair_blk0: int = -1


def _segment_steps(segments, wide):
    spans, j0 = [], 0
    for seg in segments:
        if seg.pair_blk0 >= 0:
            assert wide == 2
            nj = seg.n_blk
        else:
            assert seg.col_blk0 % wide == 0 and seg.n_blk % wide == 0
            nj = seg.n_blk // wide
        spans.append((j0, nj))
        j0 += nj
    return spans


def _proj_kernel(*refs, segments, spans, n_w, n_extra, n_out, cast_steps):
    a_ref, w_refs = refs[0], refs[1:1 + n_w]
    extras = refs[1 + n_w:1 + n_w + n_extra]
    outs = refs[1 + n_w + n_extra:1 + n_w + n_extra + n_out]
    wbf_ref = refs[1 + n_w + n_extra + n_out]

    @pl.when(pl.program_id(1) == 0)
    def _():
        for h, w_ref in enumerate(w_refs):
            wbf_ref[:, h * PROJ_TN:(h + 1) * PROJ_TN] = w_ref[...].astype(BF16)

    if cast_steps:
        step = pl.program_id(0) * pl.num_programs(1) + pl.program_id(1)

        @pl.when(step < cast_steps)
        def _():
            outs[-1][...] = extras[-1][...].astype(BF16)

    def run(seg):
        seg_outs = [outs[k] for k in seg.outs]

        def body(latent):
            for rc in range(PROJ_TM // PROJ_ROWS):
                rows = slice(rc * PROJ_ROWS, (rc + 1) * PROJ_ROWS)
                accs = [jnp.dot(a_ref[rows, :], wbf_ref[:, col0:col0 + PROJ_TN], preferred_element_type=F32)
                        for col0 in range(0, wbf_ref.shape[1], PROJ_TN)]
                if seg.pair_blk0 >= 0:
                    seg.epilogue(*accs, extras, seg_outs, rc, latent)
                else:
                    for c, acc in enumerate(accs):
                        seg.epilogue(acc, extras, seg_outs, rc, c * PROJ_TN, latent)

        if seg.by_tile_kind:
            is_latent = pl.program_id(1) >= N_PROMPT_TILES
            pl.when(is_latent)(lambda: body(True))
            pl.when(jnp.logical_not(is_latent))(lambda: body(False))
        else:
            body(None)

    j = pl.program_id(0)
    for seg, (j0, nj) in zip(segments, spans):
        if len(segments) == 1:
            run(seg)
        else:
            pl.when(jnp.logical_and(j >= j0, j < j0 + nj))(functools.partial(run, seg))


def _proj(a, w, layer, segments, out_shapes, out_specs, name, extras=(), extra_specs=(), tn=PROJ_TN,
          run_after=None, cast=None):
    m, k = a.shape
    if run_after is not None:
        extras = [*extras, run_after]
        extra_specs = [*extra_specs, pl.BlockSpec(memory_space=pl.ANY)]
    wide = tn // PROJ_TN
    spans = _segment_steps(segments, wide)
    cast_steps = 0
    if cast is not None:
        w_out, out_layer, cast_steps = cast
        n_tiles = m // PROJ_TM
        assert cast_steps <= n_tiles * sum(nj for _, nj in spans)
        w_spec, wbf_spec, wbf_shape = _cast_slab_specs(
            w_out, out_layer, cast_steps, lambda j, i: jnp.minimum(j * n_tiles + i, cast_steps - 1))
        extras, extra_specs = [*extras, w_out], [*extra_specs, w_spec]
        out_shapes, out_specs = [*out_shapes, wbf_shape], [*out_specs, wbf_spec]

    def w_col(j, half):
        col = None
        for seg, (j0, _) in zip(segments, spans):
            if seg.pair_blk0 >= 0:
                c = (seg.pair_blk0 if half else seg.col_blk0) + j - j0
            else:
                c = seg.col_blk0 + wide * (j - j0) + half
            col = c if col is None else jnp.where(j >= j0, c, col)
        return col

    w_specs = [pl.BlockSpec((None, k, PROJ_TN), lambda j, i, half=half: (layer, 0, w_col(j, half)))
               for half in range(wide)]
    kern = functools.partial(_proj_kernel, segments=segments, spans=spans, n_w=wide, n_extra=len(extras),
                             n_out=len(out_shapes), cast_steps=cast_steps)
    return pl.pallas_call(
        kern,
        grid=(sum(nj for _, nj in spans), m // PROJ_TM),
        in_specs=[pl.BlockSpec((PROJ_TM, k), lambda j, i: (i, 0)), *w_specs, *extra_specs],
        out_specs=out_specs,
        out_shape=out_shapes,
        scratch_shapes=[pltpu.VMEM((k, tn), BF16)],
        compiler_params=_params(2),
        name=name,
    )(a, *[w] * wide, *extras)


def _full_out(n_blk):
    return ([jax.ShapeDtypeStruct((TOKENS, n_blk * PROJ_TN), BF16)],
            [pl.BlockSpec((PROJ_TM, PROJ_TN_WIDE), lambda j, i: (i, j))])


def _held(span, j, inside, before, after):
    j0, nj = span
    pick = lambda a, b, c: jnp.where(j < j0, b, jnp.where(j >= j0 + nj, c, a))
    return tuple(pick(a, b, c) for a, b, c in zip(inside, before, after))


def _split_out(span, n_blk, head_rows):
    prompt_tile = lambda i: jnp.minimum(i, N_PROMPT_TILES - 1)
    last_prompt, last_tile = N_PROMPT_TILES - 1, TOKENS // PROJ_TM - 1
    jj = lambda j: j - span[0]
    if head_rows:
        assert n_blk == 1
        heads = PROJ_TN // HEAD_DIM
        cache_shape = jax.ShapeDtypeStruct((PROMPT_TOKENS * heads, HEAD_DIM), F32)
        cache_spec = pl.BlockSpec(
            (PROJ_TM * heads, HEAD_DIM),
            lambda j, i: _held(span, j, (prompt_tile(i), 0), (0, 0), (last_prompt, 0)))
    else:
        cache_shape = jax.ShapeDtypeStruct((N_PROMPT, PROMPT_SEQ, n_blk * PROJ_TN), F32)
        cache_spec = pl.BlockSpec(
            (PROJ_TM // PROMPT_SEQ, PROMPT_SEQ, PROJ_TN),
            lambda j, i: _held(span, j, (prompt_tile(i), 0, jj(j)), (0, 0, 0), (last_prompt, 0, n_blk - 1)))
    shapes = [cache_shape, jax.ShapeDtypeStruct((TOKENS, n_blk * PROJ_TN), BF16)]
    specs = [cache_spec,
             pl.BlockSpec((PROJ_TM, PROJ_TN),
                          lambda j, i: _held(span, j, (i, jj(j)), (0, 0), (last_tile, n_blk - 1)))]
    return shapes, specs


def _chunk_rows(rc):
    return slice(rc * PROJ_ROWS, (rc + 1) * PROJ_ROWS)


def _epi_silu(acc, extras, outs, rc, col0, latent):
    outs[0][_chunk_rows(rc), col0:col0 + PROJ_TN] = _silu(acc).astype(BF16)


def _epi_gelu(acc, extras, outs, rc, col0, latent):
    outs[0][_chunk_rows(rc), col0:col0 + PROJ_TN] = _gelu_tanh(acc).astype(BF16)


def _store_split(y, outs, rc, sl, latent):
    outs[1][_chunk_rows(rc), sl] = y.astype(BF16)
    if latent:
        return
    if len(outs[0].shape) == 2:
        heads = PROJ_TN // HEAD_DIM
        col0 = sl.start or 0
        for c in range(y.shape[1] // HEAD_DIM):
            head = col0 // HEAD_DIM + c
            rows = pl.ds(rc * PROJ_ROWS * heads + head, PROJ_ROWS, stride=heads)
            outs[0][rows, :] = y[:, c * HEAD_DIM:(c + 1) * HEAD_DIM]
    else:
        seqs = PROJ_ROWS // PROMPT_SEQ
        outs[0][rc * seqs:(rc + 1) * seqs, :, sl] = y.reshape(seqs, PROMPT_SEQ, y.shape[1])


def _epi_value(acc, extras, outs, rc, col0, latent):
    _store_split(acc, outs, rc, slice(col0, col0 + PROJ_TN), latent)


def _head_rmsnorm(acc, hh, g):
    a = acc[:, hh * HEAD_DIM:(hh + 1) * HEAD_DIM]
    return a * lax.rsqrt(jnp.mean(a * a, axis=-1, keepdims=True) + EPS) * g


def _normed_heads(acc, extras, rc, rope):
    g = extras[0][...]
    ys = [_head_rmsnorm(acc, hh, g) for hh in range(PROJ_TN // HEAD_DIM)]
    if rope:
        rows = _chunk_rows(rc)
        cos, sin, swap = extras[1][rows, :], extras[2][rows, :], extras[3][...]
        for pair in range(len(ys) // 2):
            both = jnp.concatenate(ys[2 * pair:2 * pair + 2], axis=1).astype(BF16)
            partner = jnp.dot(both, swap, preferred_element_type=F32)
            for t in range(2):
                hh = 2 * pair + t
                ys[hh] = ys[hh] * cos + partner[:, t * HEAD_DIM:(t + 1) * HEAD_DIM] * sin
    for hh, y in enumerate(ys):
        yield slice(hh * HEAD_DIM, (hh + 1) * HEAD_DIM), y


def _epi_query(acc, extras, outs, rc, col0, latent, *, rope):
    for sl, y in _normed_heads(acc, extras, rc, rope and latent):
        outs[0][_chunk_rows(rc), col0 + sl.start:col0 + sl.stop] = (y * QUERY_SCALE).astype(BF16)


def _epi_key(acc, extras, outs, rc, col0, latent, *, rope):
    for sl, y in _normed_heads(acc, extras, rc, rope and latent):
        _store_split(y, outs, rc, slice(col0 + sl.start, col0 + sl.stop), latent)


def _rope_tables():
    nf = HEAD_DIM // 4
    t = np.arange(SAMPLE_TOKENS) % SAMPLE_SEQ
    row = (t // GRID_W).astype(np.float32)
    col = (t % GRID_W).astype(np.float32)
    inv = np.float32(ROPE_THETA) ** (-np.arange(nf, dtype=np.float32) / np.float32(nf))
    ang_r = row[:, None] * inv
    ang_c = col[:, None] * inv
    cos = np.concatenate([np.cos(ang_r), np.cos(ang_r), np.cos(ang_c), np.cos(ang_c)], axis=1)
    sin = np.concatenate([-np.sin(ang_r), np.sin(ang_r), -np.sin(ang_c), np.sin(ang_c)], axis=1)
    lanes = np.arange(2 * HEAD_DIM)
    swap = lanes[:, None] == (lanes[None, :] ^ (HEAD_DIM // 4))
    return jnp.asarray(cos, F32), jnp.asarray(sin, F32), jnp.asarray(swap, BF16)


def _gain_extras(gain, rope_tabs):
    extras = [gain.reshape(1, HEAD_DIM)]
    specs = [pl.BlockSpec((1, HEAD_DIM), lambda j, i: (0, 0))]
    if rope_tabs is not None:
        tab_map = lambda j, i: (jnp.maximum(i - N_PROMPT_TILES, 0), 0)
        extras += list(rope_tabs)
        specs += [pl.BlockSpec((PROJ_TM, HEAD_DIM), tab_map)] * 2
        specs.append(pl.BlockSpec((2 * HEAD_DIM, 2 * HEAD_DIM), lambda j, i: (0, 0)))
    return extras, specs


def _qz_proj(hn, w, layer, q_blk0, z_blk0, n_blk, gain, rope_tabs, name):
    extras, specs = _gain_extras(gain, rope_tabs)
    rope = rope_tabs is not None
    segments = (_Segment(q_blk0, n_blk, functools.partial(_epi_query, rope=rope), rope, (0,)),
                _Segment(z_blk0, n_blk, _epi_silu, False, (0,)))
    return _proj(hn, w, layer, segments, *_full_out(2 * n_blk), name, extras, specs, tn=PROJ_TN_WIDE)[0]


def _kv_proj(hn, w, layer, k_blk0, n_blk, gain, rope_tabs, head_rows, name, run_after=None):
    extras, specs = _gain_extras(gain, rope_tabs)
    segments = (_Segment(k_blk0, n_blk, functools.partial(_epi_key, rope=rope_tabs is not None), True, (0, 1)),
                _Segment(k_blk0 + n_blk, n_blk, _epi_value, True, (2, 3)))
    shapes, out_specs = [], []
    for span in _segment_steps(segments, 1):
        seg_shapes, seg_specs = _split_out(span, n_blk, head_rows)
        shapes += seg_shapes
        out_specs += seg_specs
    return _proj(hn, w, layer, segments, shapes, out_specs, name, extras, specs, run_after=run_after)


UVZ_CAST_STEPS = 32


def _epi_uz(acc_u, acc_z, extras, outs, rc, latent):
    outs[0][_chunk_rows(rc), :] = (_gelu_tanh(acc_u) * _silu(acc_z)).astype(BF16)


def _uzv_proj(hn, w, layer, n_blk, name, w_out):
    segments = (_Segment(0, n_blk, _epi_uz, False, (0,), pair_blk0=2 * n_blk),
                _Segment(n_blk, n_blk, _epi_gelu, False, (1,)))
    (uz_span, v_span) = _segment_steps(segments, PROJ_TN_WIDE // PROJ_TN)
    last_tile = TOKENS // PROJ_TM - 1
    shape = jax.ShapeDtypeStruct((TOKENS, n_blk * PROJ_TN), BF16)
    out_specs = [
        pl.BlockSpec((PROJ_TM, PROJ_TN),
                     lambda j, i: _held(uz_span, j, (i, j - uz_span[0]), (0, 0), (last_tile, uz_span[1] - 1))),
        pl.BlockSpec((PROJ_TM, PROJ_TN_WIDE),
                     lambda j, i: _held(v_span, j, (i, j - v_span[0]), (0, 0), (last_tile, v_span[1] - 1)))]
    return _proj(hn, w, layer, segments, [shape, shape], out_specs, name, tn=PROJ_TN_WIDE,
                 cast=(w_out, layer, UVZ_CAST_STEPS))


OUT_COLS = 512
OUT_TM = 512
OUT_TM_WIDE_K = 256


def _cast_slab_specs(w, layer, n_slabs, slab_of_step):
    k, n = w.shape[1], w.shape[2]
    rows = k // n_slabs
    return (pl.BlockSpec((None, rows, n), lambda *g: (layer, slab_of_step(*g), 0)),
            pl.BlockSpec((rows, n), lambda *g: (slab_of_step(*g), 0)),
            jax.ShapeDtypeStruct((k, n), BF16))


def _out_kernel(*refs, tile0, tm, split_a, split_x, fuse_norm):
    it = iter(refs)
    a_refs = [next(it), next(it)] if split_a else [next(it)]
    wbf_ref = next(it)
    x_refs = [next(it), next(it)] if split_x else [next(it)]
    gate_ref = next(it)
    if fuse_norm:
        g_ref, shift_ref, scale_ref = next(it), next(it), next(it)
    xnew_ref = next(it)
    hn_ref = next(it) if fuse_norm else None

    def body(a_ref, x_ref):
        ssq = jnp.zeros((tm, 1), F32)
        for cb in range(D_MODEL // OUT_COLS):
            sl = slice(cb * OUT_COLS, (cb + 1) * OUT_COLS)
            acc = jnp.dot(a_ref[...], wbf_ref[:, sl], preferred_element_type=F32)
            xn = x_ref[:, sl] + gate_ref[:, sl] * acc
            xnew_ref[:, sl] = xn
            if fuse_norm:
                ssq = ssq + jnp.sum(xn * xn, axis=-1, keepdims=True)
        if fuse_norm:
            rs = lax.rsqrt(ssq * (1.0 / D_MODEL) + EPS)
            for cb in range(D_MODEL // OUT_COLS):
                sl = slice(cb * OUT_COLS, (cb + 1) * OUT_COLS)
                gmul = g_ref[:, sl] * (1.0 + scale_ref[:, sl])
                hn_ref[:, sl] = (xnew_ref[:, sl] * rs * gmul + shift_ref[:, sl]).astype(BF16)

    if split_a or split_x:
        is_prompt = pl.program_id(0) + tile0 < PROMPT_TOKENS // tm
        pl.when(is_prompt)(lambda: body(a_refs[0], x_refs[0]))
        pl.when(jnp.logical_not(is_prompt))(lambda: body(a_refs[-1], x_refs[-1]))
    else:
        body(a_refs[0], x_refs[0])


def _out_proj(a, wbf, x, mod4, layer, norm_g3, name, tm, rows=None):
    split_a = isinstance(a, tuple)
    k = a[0].shape[1] if split_a else a.shape[1]
    tok0, n_tok = rows if rows is not None else (0, TOKENS)
    tile0 = tok0 // tm
    split_x = isinstance(x, tuple)
    fuse_norm = norm_g3 is not None
    mod_spec = lambda part, lyr: pl.BlockSpec(
        (None, None, 1, D_MODEL), lambda i: (lyr, _cond_row((i + tile0) * tm), 0, part))
    row_spec = lambda width: pl.BlockSpec((tm, width), lambda i: (i + tile0, 0))
    operands = [*a, wbf] if split_a else [a, wbf]
    in_specs = [*(_split_specs(tm, k, tile0) if split_a else [row_spec(k)]),
                pl.BlockSpec((k, D_MODEL), lambda i: (0, 0))]
    if split_x:
        assert rows is None
        operands += list(x)
        in_specs += _split_specs(tm, D_MODEL)
    else:
        operands.append(x)
        in_specs.append(row_spec(D_MODEL))
    operands.append(mod4)
    in_specs.append(mod_spec(2, layer))
    out_shapes = [jax.ShapeDtypeStruct((n_tok, D_MODEL), F32)]
    out_specs = [pl.BlockSpec((tm, D_MODEL), lambda i: (i, 0))]
    if fuse_norm:
        operands += [norm_g3, mod4, mod4]
        in_specs += [pl.BlockSpec((None, 1, D_MODEL), lambda i: (layer + 1, 0, 0)),
                     mod_spec(0, layer + 1), mod_spec(1, layer + 1)]
        out_shapes.append(jax.ShapeDtypeStruct((n_tok, D_MODEL), BF16))
        out_specs.append(pl.BlockSpec((tm, D_MODEL), lambda i: (i, 0)))
    kern = functools.partial(_out_kernel, tile0=tile0, tm=tm,
                             split_a=split_a, split_x=split_x, fuse_norm=fuse_norm)
    return pl.pallas_call(
        kern,
        grid=(n_tok // tm,),
        in_specs=in_specs,
        out_specs=out_specs,
        out_shape=out_shapes,
        compiler_params=_params(1),
        name=name,
    )(*operands)


def _dot_nt(a, b):
    return lax.dot_general(a, b, (((1,), (1,)), ((), ())), preferred_element_type=F32)


def _head(ref, h, rows=slice(None)):
    return ref[rows, h * HEAD_DIM:(h + 1) * HEAD_DIM]


def _with_ones(v):
    return jnp.concatenate([v, jnp.ones(v.shape, v.dtype)], axis=1)


def _lane_chunks(s):
    return [s[:, c * HEAD_DIM:(c + 1) * HEAD_DIM] for c in range(s.shape[1] // HEAD_DIM)]


def _softmax_pv(score_blocks, value_blocks, sink):
    rows = score_blocks[0].shape[0]
    mx = functools.reduce(jnp.maximum, [c for s in score_blocks for c in _lane_chunks(s)])
    m = jnp.broadcast_to(jnp.max(mx, axis=-1, keepdims=True), (rows, HEAD_DIM))
    if sink is not None:
        m = jnp.maximum(m, sink)
    o = None
    for s, v in zip(score_blocks, value_blocks):
        p = jnp.concatenate([jnp.exp2(c - m) for c in _lane_chunks(s)], axis=1).astype(BF16)
        part = jnp.dot(p, v, preferred_element_type=F32)
        o = part if o is None else o + part
    den = o[:, HEAD_DIM:]
    if sink is not None:
        den = den + jnp.exp2(sink - m)
    return o[:, :HEAD_DIM] * (1.0 / den)


PROMPT_REQUESTS = 2


def _attn_prompt_kernel(sink_ref, q_ref, k_ref, v_ref, z_ref, *rest, n_heads, n_kv, use_sink):
    o_ref = rest[-1]
    grp = n_heads // n_kv
    for r in range(PROMPT_REQUESTS):
        rows = slice(r * PROMPT_SEQ, (r + 1) * PROMPT_SEQ)
        for g in range(n_kv):
            kg = _head(k_ref, g, rows)
            vg = _with_ones(_head(v_ref, g, rows))
            for h in range(g * grp, (g + 1) * grp):
                s = _dot_nt(_head(q_ref, h, rows), kg)
                sink = jnp.full((PROMPT_SEQ, HEAD_DIM), sink_ref[h], F32) if use_sink else None
                o = _softmax_pv([s], [vg], sink)
                o_ref[rows, h * HEAD_DIM:(h + 1) * HEAD_DIM] = (
                    o * _head(z_ref, h, rows).astype(F32)).astype(BF16)


def _attn_prompt(sink, qz, k, v, n_kv, use_sink, name, run_after=None):
    ordering = [] if run_after is None else [run_after]
    width = qz.shape[1] // 2
    kern = functools.partial(_attn_prompt_kernel, n_heads=width // HEAD_DIM, n_kv=n_kv, use_sink=use_sink)
    rows = PROMPT_REQUESTS * PROMPT_SEQ
    cache_spec = pl.BlockSpec((rows, k.shape[1]), lambda b: (b, 0))
    return pl.pallas_call(
        kern,
        grid=(N_PROMPT // PROMPT_REQUESTS,),
        in_specs=[pl.BlockSpec(memory_space=pltpu.SMEM),
                  pl.BlockSpec((rows, width), lambda b: (b, 0)),
                  cache_spec, cache_spec,
                  pl.BlockSpec((rows, width), lambda b: (b, 1)),
                  *[pl.BlockSpec(memory_space=pl.ANY) for _ in ordering]],
        out_specs=pl.BlockSpec((rows, width), lambda b: (b, 0)),
        out_shape=jax.ShapeDtypeStruct((PROMPT_TOKENS, width), BF16),
        compiler_params=_params(1),
        name=name,
    )(sink, qz, k, v, qz, *ordering)


WIN_BAND = 3 * WINDOW


def _attn_win_kernel(sink_ref, q_ref, k_ref, v_ref, kc_ref, vc_ref, z_ref, w_ref, o_ref, wbf_ref, *,
                     n_heads, n_kv):
    wbf_ref[...] = w_ref[...].astype(BF16)
    grp = n_heads // n_kv
    n = pl.program_id(1)
    start = pl.multiple_of(jnp.clip((n - 1) * WINDOW, 0, SAMPLE_SEQ - WIN_BAND), WINDOW)
    shape = (grp * WINDOW, WIN_BAND)
    qpos = n * WINDOW + (lax.broadcasted_iota(jnp.int32, shape, 0) & (WINDOW - 1))
    kpos = start + lax.broadcasted_iota(jnp.int32, shape, 1)
    valid = jnp.abs(kpos - qpos) <= WINDOW
    band = pl.ds(start, WIN_BAND)
    for g in range(n_kv):
        heads = [g * grp + t for t in range(grp)]
        qs = jnp.concatenate([_head(q_ref, h) for h in heads], axis=0)
        kb = _head(k_ref, g, band)
        vb = _with_ones(_head(v_ref, g, band))
        ctx_rows = pl.ds(g, PAST_LEN, stride=n_kv)
        kc = kc_ref[ctx_rows, :].astype(BF16)
        vc = _with_ones(vc_ref[ctx_rows, :].astype(BF16))
        s_band = jnp.where(valid, _dot_nt(qs, kb), NEG_INF)
        s_ctx = _dot_nt(qs, kc)
        sink = jnp.concatenate([jnp.full((WINDOW, HEAD_DIM), sink_ref[h], F32) for h in heads], axis=0)
        o = _softmax_pv([s_band, s_ctx], [vb, vc], sink)
        for t, h in enumerate(heads):
            oh = o[t * WINDOW:(t + 1) * WINDOW]
            o_ref[:, h * HEAD_DIM:(h + 1) * HEAD_DIM] = (oh * _head(z_ref, h).astype(F32)).astype(BF16)


def _attn_win(sink, qz, k, v, cache_k, cache_v, layer_in_kind, n_kv, w_out):
    width = qz.shape[1] // 2
    kv_width = k.shape[1]
    blocks_per_seq = SAMPLE_SEQ // WINDOW
    q_map = lambda b, n: (PROMPT_TOKENS // WINDOW + b * blocks_per_seq + n, 0)
    z_map = lambda b, n: (PROMPT_TOKENS // WINDOW + b * blocks_per_seq + n, 1)
    kv_map = lambda b, n: (PROMPT_TOKENS // SAMPLE_SEQ + b, 0)
    cache_map = lambda b, n: (b, layer_in_kind, 0, 0)
    kern = functools.partial(_attn_win_kernel, n_heads=width // HEAD_DIM, n_kv=n_kv)
    w_spec, wbf_spec, wbf_shape = _cast_slab_specs(w_out, layer_in_kind, N_SAMPLE * blocks_per_seq,
                                                   lambda b, n: b * blocks_per_seq + n)
    return pl.pallas_call(
        kern,
        grid=(N_SAMPLE, blocks_per_seq),
        in_specs=[pl.BlockSpec(memory_space=pltpu.SMEM),
                  pl.BlockSpec((WINDOW, width), q_map),
                  pl.BlockSpec((SAMPLE_SEQ, kv_width), kv_map),
                  pl.BlockSpec((SAMPLE_SEQ, kv_width), kv_map),
                  pl.BlockSpec((None, None, PAST_LEN * n_kv, HEAD_DIM), cache_map),
                  pl.BlockSpec((None, None, PAST_LEN * n_kv, HEAD_DIM), cache_map),
                  pl.BlockSpec((WINDOW, width), z_map),
                  w_spec],
        out_specs=[pl.BlockSpec((WINDOW, width), lambda b, n: (b * blocks_per_seq + n, 0)), wbf_spec],
        out_shape=[jax.ShapeDtypeStruct((SAMPLE_TOKENS, width), BF16), wbf_shape],
        compiler_params=_params(2),
        name="attn_win_latent",
    )(sink, qz, k, v, cache_k, cache_v, qz, w_out)


NAT_QROWS = 4
GRID_ROWS = SAMPLE_SEQ // GRID_W
N_DR = 2 * NAT_ROWS - 1
MASKED_TILE = N_DR


def _nat_row_start(qr):
    return min(max(qr - NAT_ROWS // 2, 0), GRID_ROWS - NAT_ROWS)


def _attn_nat_kernel(bias_ref, q_ref, k_ref, v_ref, kc_ref, vc_ref, z_ref, w_ref, o_ref, wbf_ref,
                     left_ref, right_ref):
    wbf_ref[...] = w_ref[...].astype(BF16)

    def build_bias_tiles():
        shape = (GRID_W, 2 * GRID_W)
        lane = lax.broadcasted_iota(jnp.int32, shape, 1)
        qc = lax.broadcasted_iota(jnp.int32, shape, 0)
        kc_ = lane & (GRID_W - 1)
        cs = jnp.clip(qc - NAT_COLS // 2, 0, GRID_W - NAT_COLS)
        col_ok = jnp.logical_and(kc_ >= cs, kc_ < cs + NAT_COLS)
        is_left = lane < GRID_W
        lanes = 2 * GRID_W
        for dri in range(N_DR):
            row = jnp.broadcast_to(bias_ref[dri:dri + 1, :], shape)
            on_left = pltpu.roll(row, lanes - (NAT_COLS - 1), 1, stride=1, stride_axis=0)
            on_right = pltpu.roll(row, GRID_W - (NAT_COLS - 1), 1, stride=1, stride_axis=0)
            left_ref[dri] = jnp.where(is_left, jnp.where(col_ok, on_left, NEG_INF), 0.0)
            right_ref[dri] = jnp.where(is_left, 0.0, jnp.where(col_ok, on_right, NEG_INF))
        left_ref[MASKED_TILE] = jnp.where(is_left, NEG_INF, 0.0)
        right_ref[MASKED_TILE] = jnp.where(is_left, 0.0, NEG_INF)

    build_bias_tiles()
    n_heads = kc_ref.shape[1] // PAST_LEN
    ctx_rows = pl.ds(pl.program_id(0), PAST_LEN, stride=n_heads)
    for b, qb in [(b, qb) for b in range(N_SAMPLE) for qb in range(GRID_ROWS // NAT_QROWS)]:
        if qb == 0:
            kc = kc_ref[b, ctx_rows, :].astype(BF16)
            vc = _with_ones(vc_ref[b, ctx_rows, :].astype(BF16))
        qrows = range(qb * NAT_QROWS, (qb + 1) * NAT_QROWS)
        krow0 = _nat_row_start(qrows[0]) // 2 * 2
        krow1 = -(-(_nat_row_start(qrows[-1]) + NAT_ROWS) // 2) * 2
        tok0 = b * SAMPLE_SEQ
        kwin = slice(tok0 + krow0 * GRID_W, tok0 + krow1 * GRID_W)
        qwin = slice(tok0 + qrows[0] * GRID_W, tok0 + (qrows[-1] + 1) * GRID_W)

        def tile_index(qr, kr):
            rs = _nat_row_start(qr)
            return kr - qr + (NAT_ROWS - 1) if rs <= kr < rs + NAT_ROWS else MASKED_TILE

        bias = jnp.concatenate(
            [jnp.concatenate([left_ref[tile_index(qr, kr)] + right_ref[tile_index(qr, kr + 1)]
                              for kr in range(krow0, krow1, 2)], axis=1)
             for qr in qrows], axis=0)

        q = q_ref[qwin, :]
        s_nb = _dot_nt(q, k_ref[kwin, :]) + bias
        s_ctx = _dot_nt(q, kc)
        o = _softmax_pv([s_nb, s_ctx], [_with_ones(v_ref[kwin, :]), vc], None)
        o_ref[qwin, :] = (o * z_ref[qwin, :].astype(F32)).astype(BF16)


def _attn_nat(rel_bias, qz, k, v, cache_k, cache_v, layer_in_kind, w_out):
    n_heads = qz.shape[1] // (2 * HEAD_DIM)
    bias_rows = jnp.pad(rel_bias, ((0, 0), (0, N_DR + 1 - rel_bias.shape[1]),
                                   (0, 2 * GRID_W - rel_bias.shape[2])))
    latent_tile = PROMPT_TOKENS // SAMPLE_TOKENS
    latent_spec = pl.BlockSpec((SAMPLE_TOKENS, HEAD_DIM), lambda h: (latent_tile, h))
    z_spec = pl.BlockSpec((SAMPLE_TOKENS, HEAD_DIM), lambda h: (latent_tile, n_heads + h))
    cache_spec = pl.BlockSpec((N_SAMPLE, None, PAST_LEN * n_heads, HEAD_DIM),
                              lambda h: (0, layer_in_kind, 0, 0))
    w_spec, wbf_spec, wbf_shape = _cast_slab_specs(w_out, layer_in_kind, n_heads, lambda h: h)
    return pl.pallas_call(
        _attn_nat_kernel,
        grid=(n_heads,),
        in_specs=[pl.BlockSpec((None, N_DR + 1, 2 * GRID_W), lambda h: (h, 0, 0)),
                  latent_spec, latent_spec, latent_spec, cache_spec, cache_spec, z_spec, w_spec],
        out_specs=[pl.BlockSpec((SAMPLE_TOKENS, HEAD_DIM), lambda h: (0, h)), wbf_spec],
        out_shape=[jax.ShapeDtypeStruct((SAMPLE_TOKENS, n_heads * HEAD_DIM), BF16), wbf_shape],
        scratch_shapes=[pltpu.VMEM((N_DR + 1, GRID_W, 2 * GRID_W), F32),
                        pltpu.VMEM((N_DR + 1, GRID_W, 2 * GRID_W), F32)],
        compiler_params=_params(1),
        name="attn_nat_latent",
    )(bias_rows, qz, k, v, cache_k, cache_v, qz, w_out)


SPATIAL_TOKENS = 2 * CHUNK


def _spatial_kernel(uz_ref, v_ref, g_ref, b_ref, ws_ref, bs_ref, o_ref):
    for c in range(SPATIAL_TOKENS // CHUNK):
        rows = slice(c * CHUNK, (c + 1) * CHUNK)
        v = v_ref[rows, :].astype(F32)
        mu = jnp.mean(v, axis=-1, keepdims=True)
        vc = v - mu
        var = jnp.mean(vc * vc, axis=-1, keepdims=True)
        vn = (vc * lax.rsqrt(var + EPS) * g_ref[...] + b_ref[...]).astype(BF16)
        for g in range(GMLP_GROUPS):
            sl = slice(g * GMLP_GROUP_WIDTH, (g + 1) * GMLP_GROUP_WIDTH)
            sv = jnp.dot(ws_ref[g], vn[:, sl], preferred_element_type=F32) + bs_ref[:, g:g + 1]
            o_ref[rows, sl] = uz_ref[rows, sl] * sv.astype(BF16)


def _spatial(uz, v, ln_g, ln_b, w_s, b_s):
    row = pl.BlockSpec((SPATIAL_TOKENS, GMLP_WIDTH), lambda i: (i, 0))
    vec = pl.BlockSpec((1, GMLP_WIDTH), lambda i: (0, 0))
    return pl.pallas_call(
        _spatial_kernel,
        grid=(TOKENS // SPATIAL_TOKENS,),
        in_specs=[row, row, vec, vec,
                  pl.BlockSpec((GMLP_GROUPS, CHUNK, CHUNK), lambda i: (0, 0, 0)),
                  pl.BlockSpec((CHUNK, GMLP_GROUPS), lambda i: (0, 0))],
        out_specs=row,
        out_shape=jax.ShapeDtypeStruct((TOKENS, GMLP_WIDTH), BF16),
        compiler_params=_params(1),
        name="gmlp_spatial",
    )(uz, v, ln_g.reshape(1, -1), ln_b.reshape(1, -1), w_s.astype(BF16), b_s.T)


def kernel(x_prompt, x_sample, cache_win_k, cache_win_v, cache_nat_k, cache_nat_v, c, c_ctx,
           norm_g, w_ada, b_ada,
           win_w_in, win_q_norm, win_k_norm, win_sink, win_w_out,
           nat_w_in, nat_q_norm, nat_k_norm, nat_rel_bias, nat_w_out,
           gmlp_w_in, gmlp_ln_g, gmlp_ln_b, gmlp_w_s, gmlp_b_s, gmlp_w_out):
    depth = norm_g.shape[0]
    xp = x_prompt.reshape(PROMPT_TOKENS, D_MODEL)
    xs = x_sample.reshape(SAMPLE_TOKENS, D_MODEL)
    cond = jnp.zeros((COND_ROWS, D_MODEL), F32).at[:N_SAMPLE].set(c).at[CTX_COND_ROW].set(c_ctx)
    mod4 = _ada(cond, w_ada, b_ada).reshape(depth, COND_ROWS, 1, 3 * D_MODEL)
    norm_g3 = norm_g.reshape(depth, 1, D_MODEL)
    rope_tabs = _rope_tables()
    no_sink = jnp.zeros((1,), F32)

    n_win = win_w_in.shape[0]
    n_nat = nat_w_in.shape[0]
    win_kv_heads = cache_win_k.shape[3]
    win_kv_width = win_kv_heads * HEAD_DIM
    cwk = cache_win_k.reshape(N_SAMPLE, n_win, PAST_LEN * win_kv_heads, HEAD_DIM)
    cwv = cache_win_v.reshape(N_SAMPLE, n_win, PAST_LEN * win_kv_heads, HEAD_DIM)
    nat_heads = cache_nat_k.shape[3]
    cnk = cache_nat_k.reshape(N_SAMPLE, n_nat, PAST_LEN * nat_heads, HEAD_DIM)
    cnv = cache_nat_v.reshape(N_SAMPLE, n_nat, PAST_LEN * nat_heads, HEAD_DIM)

    new_win_k, new_win_v, new_nat_k, new_nat_v = [], [], [], []
    x = (xp, xs)
    hn = _norm_mod(xp, xs, norm_g3, mod4, 0)
    for layer in range(depth):
        kind = layer % 3
        li = layer // 3
        if kind == 0:
            nq = D_MODEL // PROJ_TN
            nkv = win_kv_width // PROJ_TN
            qz = _qz_proj(hn, win_w_in, li, 0, nq + 2 * nkv, nq, win_q_norm[li], rope_tabs, "win_qz")
            kp, ks, vp, vs = _kv_proj(hn, win_w_in, li, nq, nkv, win_k_norm[li], rope_tabs, True, "win_kv",
                                      run_after=qz)
            new_win_k.append(kp)
            new_win_v.append(vp)
            sink = win_sink[li] * LOG2E
            a_latent, wbf = _attn_win(sink, qz, ks, vs, cwk, cwv, li, win_kv_heads, win_w_out)
            a = (_attn_prompt(sink, qz, ks, vs, win_kv_heads, True, "attn_win_prompt"), a_latent)
            out_tm, out_name = OUT_TM, "win_out"
        elif kind == 1:
            nq = D_MODEL // PROJ_TN
            qz = _qz_proj(hn, nat_w_in, li, 0, 3 * nq, nq, nat_q_norm[li], None, "nat_qz")
            kp, ks, vp, vs = _kv_proj(hn, nat_w_in, li, nq, nq, nat_k_norm[li], None, False, "nat_kv",
                                      run_after=qz)
            new_nat_k.append(kp)
            new_nat_v.append(vp)
            a_latent, wbf = _attn_nat(nat_rel_bias[li] * LOG2E, qz, ks, vs, cnk, cnv, li, nat_w_out)
            a = (_attn_prompt(no_sink, qz, ks, vs, nat_heads, False, "attn_nat_prompt", run_after=a_latent),
                 a_latent)
            out_tm, out_name = OUT_TM, "nat_out"
        else:
            nw = GMLP_WIDTH // PROJ_TN
            uz, v, wbf = _uzv_proj(hn, gmlp_w_in, li, nw, "gmlp_uzv", gmlp_w_out)
            a = _spatial(uz, v, gmlp_ln_g[li], gmlp_ln_b[li], gmlp_w_s[li], gmlp_b_s[li])
            out_tm, out_name = OUT_TM_WIDE_K, "gmlp_out"
        if layer + 1 < depth:
            x, hn = _out_proj(a, wbf, x, mod4, layer, norm_g3, out_name, out_tm)
        else:
            (yp,) = _out_proj(a, wbf, x, mod4, layer, None, out_name + "_prompt", out_tm,
                              rows=(0, PROMPT_TOKENS))
            (ys,) = _out_proj(a, wbf, x, mod4, layer, None, out_name + "_latent", out_tm,
                              rows=(PROMPT_TOKENS, SAMPLE_TOKENS))

    cache_shape = lambda layers, heads: jnp.stack(
        [c_.reshape(N_PROMPT, PROMPT_SEQ, heads, HEAD_DIM) for c_ in layers], axis=1)
    return (yp.reshape(N_PROMPT, PROMPT_SEQ, D_MODEL), ys.reshape(N_SAMPLE, SAMPLE_SEQ, D_MODEL),
            cache_shape(new_win_k, win_kv_heads), cache_shape(new_win_v, win_kv_heads),
            cache_shape(new_nat_k, nat_heads), cache_shape(new_nat_v, nat_heads))
```

```python
import functools
import math
from typing import Callable, NamedTuple

import jax
import jax.numpy as jnp
import numpy as np
from jax import lax
from jax.experimental import pallas as pl
from jax.experimental.pallas import tpu as pltpu

F32 = jnp.float32
BF16 = jnp.bfloat16

D_MODEL = 2048
HEAD_DIM = 128
N_PROMPT = 16
PROMPT_SEQ = 256
N_SAMPLE = 2
SAMPLE_SEQ = 1024
PROMPT_TOKENS = N_PROMPT * PROMPT_SEQ
SAMPLE_TOKENS = N_SAMPLE * SAMPLE_SEQ
TOKENS = PROMPT_TOKENS + SAMPLE_TOKENS
PAST_LEN = 512
GRID_W = 64
EPS = 1e-6
NEG_INF = -1e30
ROPE_THETA = 10000.0
WINDOW = 128
NAT_ROWS = 8
NAT_COLS = 16
GMLP_WIDTH = 2 * D_MODEL
GMLP_GROUPS = 16
GMLP_GROUP_WIDTH = GMLP_WIDTH // GMLP_GROUPS
CHUNK = 128
CTX_COND_ROW = 2
COND_ROWS = 8
SM_SCALE = HEAD_DIM ** -0.5
LOG2E = math.log2(math.e)
QUERY_SCALE = SM_SCALE * LOG2E

VMEM_LIMIT = 56 * 1024 * 1024


def _params(n_axes):
    return pltpu.CompilerParams(dimension_semantics=("arbitrary",) * n_axes,
                                vmem_limit_bytes=VMEM_LIMIT)


def _cond_row(tok0):
    return jnp.where(tok0 < PROMPT_TOKENS, CTX_COND_ROW, (tok0 - PROMPT_TOKENS) // SAMPLE_SEQ)


def _silu(x):
    return x * (0.5 + 0.5 * jnp.tanh(0.5 * x))


def _gelu_tanh(x):
    return 0.5 * x * (1.0 + jnp.tanh(math.sqrt(2.0 / math.pi) * (x + 0.044715 * (x * x * x))))


def _split_specs(tm, width, tile0=0):
    n_p = PROMPT_TOKENS // tm
    return [pl.BlockSpec((tm, width), lambda i: (jnp.minimum(i + tile0, n_p - 1), 0)),
            pl.BlockSpec((tm, width), lambda i: (jnp.maximum(i + tile0 - n_p, 0), 0))]


def _ada_kernel(cond_ref, w_ref, b_ref, o_ref):
    s = _silu(cond_ref[...]).astype(BF16)
    o_ref[...] = jnp.dot(s, w_ref[...].astype(BF16), preferred_element_type=F32) + b_ref[...]


def _ada(cond, w_ada, b_ada, tn=2048):
    depth = w_ada.shape[0]
    n = w_ada.shape[2]
    return pl.pallas_call(
        _ada_kernel,
        grid=(depth, n // tn),
        in_specs=[pl.BlockSpec((COND_ROWS, D_MODEL), lambda l, j: (0, 0)),
                  pl.BlockSpec((None, D_MODEL, tn), lambda l, j: (l, 0, j)),
                  pl.BlockSpec((None, 1, tn), lambda l, j: (l, 0, j))],
        out_specs=pl.BlockSpec((None, COND_ROWS, tn), lambda l, j: (l, 0, j)),
        out_shape=jax.ShapeDtypeStruct((depth, COND_ROWS, n), F32),
        compiler_params=_params(2),
        name="ada_mod",
    )(cond, w_ada, b_ada.reshape(depth, 1, n))


NORM_ROWS = 64


def _norm_kernel(xp_ref, xs_ref, g_ref, shift_ref, scale_ref, o_ref, *, tm):
    gmul = g_ref[...] * (1.0 + scale_ref[...])
    shift = shift_ref[...]

    def run(x_ref):
        for r in range(tm // NORM_ROWS):
            rows = slice(r * NORM_ROWS, (r + 1) * NORM_ROWS)
            x = x_ref[rows, :]
            rs = lax.rsqrt(jnp.mean(x * x, axis=-1, keepdims=True) + EPS)
            o_ref[rows, :] = (x * rs * gmul + shift).astype(BF16)

    is_prompt = pl.program_id(0) < PROMPT_TOKENS // tm
    pl.when(is_prompt)(lambda: run(xp_ref))
    pl.when(jnp.logical_not(is_prompt))(lambda: run(xs_ref))


def _norm_mod(xp, xs, norm_g3, mod4, layer, tm=1024):
    return pl.pallas_call(
        functools.partial(_norm_kernel, tm=tm),
        grid=(TOKENS // tm,),
        in_specs=[*_split_specs(tm, D_MODEL),
                  pl.BlockSpec((None, 1, D_MODEL), lambda i: (layer, 0, 0)),
                  pl.BlockSpec((None, None, 1, D_MODEL), lambda i: (layer, _cond_row(i * tm), 0, 0)),
                  pl.BlockSpec((None, None, 1, D_MODEL), lambda i: (layer, _cond_row(i * tm), 0, 1))],
        out_specs=pl.BlockSpec((tm, D_MODEL), lambda i: (i, 0)),
        out_shape=jax.ShapeDtypeStruct((TOKENS, D_MODEL), BF16),
        compiler_params=_params(1),
        name="norm_mod",
    )(xp, xs, norm_g3, mod4, mod4)


PROJ_TM = 2048
PROJ_TN = 512
PROJ_TN_WIDE = 1024
PROJ_ROWS = 512
N_PROMPT_TILES = PROMPT_TOKENS // PROJ_TM


class _Segment(NamedTuple):
    col_blk0: int
    n_blk: int
    epilogue: Callable
    by_tile_kind: bool
    outs: tuple
    pair_blk0: int = -1


def _segment_steps(segments, wide):
    spans, j0 = [], 0
    for seg in segments:
        if seg.pair_blk0 >= 0:
            assert wide == 2
            nj = seg.n_blk
        else:
            assert seg.col_blk0 % wide == 0 and seg.n_blk % wide == 0
            nj = seg.n_blk // wide
        spans.append((j0, nj))
        j0 += nj
    return spans


def _proj_kernel(*refs, segments, spans, n_w, n_extra, n_out, cast_steps):
    a_ref, w_refs = refs[0], refs[1:1 + n_w]
    extras = refs[1 + n_w:1 + n_w + n_extra]
    outs = refs[1 + n_w + n_extra:1 + n_w + n_extra + n_out]
    wbf_ref = refs[1 + n_w + n_extra + n_out]

    @pl.when(pl.program_id(1) == 0)
    def _():
        for h, w_ref in enumerate(w_refs):
            wbf_ref[:, h * PROJ_TN:(h + 1) * PROJ_TN] = w_ref[...].astype(BF16)

    if cast_steps:
        step = pl.program_id(0) * pl.num_programs(1) + pl.program_id(1)

        @pl.when(step < cast_steps)
        def _():
            outs[-1][...] = extras[-1][...].astype(BF16)

    def run(seg):
        seg_outs = [outs[k] for k in seg.outs]

        def body(latent):
            for rc in range(PROJ_TM // PROJ_ROWS):
                rows = slice(rc * PROJ_ROWS, (rc + 1) * PROJ_ROWS)
                accs = [jnp.dot(a_ref[rows, :], wbf_ref[:, col0:col0 + PROJ_TN], preferred_element_type=F32)
                        for col0 in range(0, wbf_ref.shape[1], PROJ_TN)]
                if seg.pair_blk0 >= 0:
                    seg.epilogue(*accs, extras, seg_outs, rc, latent)
                else:
                    for c, acc in enumerate(accs):
                        seg.epilogue(acc, extras, seg_outs, rc, c * PROJ_TN, latent)

        if seg.by_tile_kind:
            is_latent = pl.program_id(1) >= N_PROMPT_TILES
            pl.when(is_latent)(lambda: body(True))
            pl.when(jnp.logical_not(is_latent))(lambda: body(False))
        else:
            body(None)

    j = pl.program_id(0)
    for seg, (j0, nj) in zip(segments, spans):
        if len(segments) == 1:
            run(seg)
        else:
            pl.when(jnp.logical_and(j >= j0, j < j0 + nj))(functools.partial(run, seg))


def _proj(a, w, layer, segments, out_shapes, out_specs, name, extras=(), extra_specs=(), tn=PROJ_TN,
          run_after=None, cast=None):
    m, k = a.shape
    if run_after is not None:
        extras = [*extras, run_after]
        extra_specs = [*extra_specs, pl.BlockSpec(memory_space=pl.ANY)]
    wide = tn // PROJ_TN
    spans = _segment_steps(segments, wide)
    cast_steps = 0
    if cast is not None:
        w_out, out_layer, cast_steps = cast
        n_tiles = m // PROJ_TM
        assert cast_steps <= n_tiles * sum(nj for _, nj in spans)
        w_spec, wbf_spec, wbf_shape = _cast_slab_specs(
            w_out, out_layer, cast_steps, lambda j, i: jnp.minimum(j * n_tiles + i, cast_steps - 1))
        extras, extra_specs = [*extras, w_out], [*extra_specs, w_spec]
        out_shapes, out_specs = [*out_shapes, wbf_shape], [*out_specs, wbf_spec]

    def w_col(j, half):
        col = None
        for seg, (j0, _) in zip(segments, spans):
            if seg.pair_blk0 >= 0:
                c = (seg.pair_blk0 if half else seg.col_blk0) + j - j0
            else:
                c = seg.col_blk0 + wide * (j - j0) + half
            col = c if col is None else jnp.where(j >= j0, c, col)
        return col

    w_specs = [pl.BlockSpec((None, k, PROJ_TN), lambda j, i, half=half: (layer, 0, w_col(j, half)))
               for half in range(wide)]
    kern = functools.partial(_proj_kernel, segments=segments, spans=spans, n_w=wide, n_extra=len(extras),
                             n_out=len(out_shapes), cast_steps=cast_steps)
    return pl.pallas_call(
        kern,
        grid=(sum(nj for _, nj in spans), m // PROJ_TM),
        in_specs=[pl.BlockSpec((PROJ_TM, k), lambda j, i: (i, 0)), *w_specs, *extra_specs],
        out_specs=out_specs,
        out_shape=out_shapes,
        scratch_shapes=[pltpu.VMEM((k, tn), BF16)],
        compiler_params=_params(2),
        name=name,
    )(a, *[w] * wide, *extras)


def _full_out(n_blk):
    return ([jax.ShapeDtypeStruct((TOKENS, n_blk * PROJ_TN), BF16)],
            [pl.BlockSpec((PROJ_TM, PROJ_TN_WIDE), lambda j, i: (i, j))])


def _held(span, j, inside, before, after):
    j0, nj = span
    pick = lambda a, b, c: jnp.where(j < j0, b, jnp.where(j >= j0 + nj, c, a))
    return tuple(pick(a, b, c) for a, b, c in zip(inside, before, after))


def _split_out(span, n_blk, head_rows):
    prompt_tile = lambda i: jnp.minimum(i, N_PROMPT_TILES - 1)
    last_prompt, last_tile = N_PROMPT_TILES - 1, TOKENS // PROJ_TM - 1
    jj = lambda j: j - span[0]
    if head_rows:
        assert n_blk == 1
        heads = PROJ_TN // HEAD_DIM
        cache_shape = jax.ShapeDtypeStruct((PROMPT_TOKENS * heads, HEAD_DIM), F32)
        cache_spec = pl.BlockSpec(
            (PROJ_TM * heads, HEAD_DIM),
            lambda j, i: _held(span, j, (prompt_tile(i), 0), (0, 0), (last_prompt, 0)))
    else:
        cache_shape = jax.ShapeDtypeStruct((N_PROMPT, PROMPT_SEQ, n_blk * PROJ_TN), F32)
        cache_spec = pl.BlockSpec(
            (PROJ_TM // PROMPT_SEQ, PROMPT_SEQ, PROJ_TN),
            lambda j, i: _held(span, j, (prompt_tile(i), 0, jj(j)), (0, 0, 0), (last_prompt, 0, n_blk - 1)))
    shapes = [cache_shape, jax.ShapeDtypeStruct((TOKENS, n_blk * PROJ_TN), BF16)]
    specs = [cache_spec,
             pl.BlockSpec((PROJ_TM, PROJ_TN),
                          lambda j, i: _held(span, j, (i, jj(j)), (0, 0), (last_tile, n_blk - 1)))]
    return shapes, specs


def _chunk_rows(rc):
    return slice(rc * PROJ_ROWS, (rc + 1) * PROJ_ROWS)


def _epi_silu(acc, extras, outs, rc, col0, latent):
    outs[0][_chunk_rows(rc), col0:col0 + PROJ_TN] = _silu(acc).astype(BF16)


def _epi_gelu(acc, extras, outs, rc, col0, latent):
    outs[0][_chunk_rows(rc), col0:col0 + PROJ_TN] = _gelu_tanh(acc).astype(BF16)


def _store_split(y, outs, rc, sl, latent):
    outs[1][_chunk_rows(rc), sl] = y.astype(BF16)
    if latent:
        return
    if len(outs[0].shape) == 2:
        heads = PROJ_TN // HEAD_DIM
        col0 = sl.start or 0
        for c in range(y.shape[1] // HEAD_DIM):
            head = col0 // HEAD_DIM + c
            rows = pl.ds(rc * PROJ_ROWS * heads + head, PROJ_ROWS, stride=heads)
            outs[0][rows, :] = y[:, c * HEAD_DIM:(c + 1) * HEAD_DIM]
    else:
        seqs = PROJ_ROWS // PROMPT_SEQ
        outs[0][rc * seqs:(rc + 1) * seqs, :, sl] = y.reshape(seqs, PROMPT_SEQ, y.shape[1])


def _epi_value(acc, extras, outs, rc, col0, latent):
    _store_split(acc, outs, rc, slice(col0, col0 + PROJ_TN), latent)


def _head_rmsnorm(acc, hh, g):
    a = acc[:, hh * HEAD_DIM:(hh + 1) * HEAD_DIM]
    return a * lax.rsqrt(jnp.mean(a * a, axis=-1, keepdims=True) + EPS) * g


def _normed_heads(acc, extras, rc, rope):
    g = extras[0][...]
    ys = [_head_rmsnorm(acc, hh, g) for hh in range(PROJ_TN // HEAD_DIM)]
    if rope:
        rows = _chunk_rows(rc)
        cos, sin, swap = extras[1][rows, :], extras[2][rows, :], extras[3][...]
        for pair in range(len(ys) // 2):
            both = jnp.concatenate(ys[2 * pair:2 * pair + 2], axis=1).astype(BF16)
            partner = jnp.dot(both, swap, preferred_element_type=F32)
            for t in range(2):
                hh = 2 * pair + t
                ys[hh] = ys[hh] * cos + partner[:, t * HEAD_DIM:(t + 1) * HEAD_DIM] * sin
    for hh, y in enumerate(ys):
        yield slice(hh * HEAD_DIM, (hh + 1) * HEAD_DIM), y


def _epi_query(acc, extras, outs, rc, col0, latent, *, rope):
    for sl, y in _normed_heads(acc, extras, rc, rope and latent):
        outs[0][_chunk_rows(rc), col0 + sl.start:col0 + sl.stop] = (y * QUERY_SCALE).astype(BF16)


def _epi_key(acc, extras, outs, rc, col0, latent, *, rope):
    for sl, y in _normed_heads(acc, extras, rc, rope and latent):
        _store_split(y, outs, rc, slice(col0 + sl.start, col0 + sl.stop), latent)


def _rope_tables():
    nf = HEAD_DIM // 4
    t = np.arange(SAMPLE_TOKENS) % SAMPLE_SEQ
    row = (t // GRID_W).astype(np.float32)
    col = (t % GRID_W).astype(np.float32)
    inv = np.float32(ROPE_THETA) ** (-np.arange(nf, dtype=np.float32) / np.float32(nf))
    ang_r = row[:, None] * inv
    ang_c = col[:, None] * inv
    cos = np.concatenate([np.cos(ang_r), np.cos(ang_r), np.cos(ang_c), np.cos(ang_c)], axis=1)
    sin = np.concatenate([-np.sin(ang_r), np.sin(ang_r), -np.sin(ang_c), np.sin(ang_c)], axis=1)
    lanes = np.arange(2 * HEAD_DIM)
    swap = lanes[:, None] == (lanes[None, :] ^ (HEAD_DIM // 4))
    return jnp.asarray(cos, F32), jnp.asarray(sin, F32), jnp.asarray(swap, BF16)


def _gain_extras(gain, rope_tabs):
    extras = [gain.reshape(1, HEAD_DIM)]
    specs = [pl.BlockSpec((1, HEAD_DIM), lambda j, i: (0, 0))]
    if rope_tabs is not None:
        tab_map = lambda j, i: (jnp.maximum(i - N_PROMPT_TILES, 0), 0)
        extras += list(rope_tabs)
        specs += [pl.BlockSpec((PROJ_TM, HEAD_DIM), tab_map)] * 2
        specs.append(pl.BlockSpec((2 * HEAD_DIM, 2 * HEAD_DIM), lambda j, i: (0, 0)))
    return extras, specs


def _qz_proj(hn, w, layer, q_blk0, z_blk0, n_blk, gain, rope_tabs, name):
    extras, specs = _gain_extras(gain, rope_tabs)
    rope = rope_tabs is not None
    segments = (_Segment(q_blk0, n_blk, functools.partial(_epi_query, rope=rope), rope, (0,)),
                _Segment(z_blk0, n_blk, _epi_silu, False, (0,)))
    return _proj(hn, w, layer, segments, *_full_out(2 * n_blk), name, extras, specs, tn=PROJ_TN_WIDE)[0]


def _kv_proj(hn, w, layer, k_blk0, n_blk, gain, rope_tabs, head_rows, name, run_after=None):
    extras, specs = _gain_extras(gain, rope_tabs)
    segments = (_Segment(k_blk0, n_blk, functools.partial(_epi_key, rope=rope_tabs is not None), True, (0, 1)),
                _Segment(k_blk0 + n_blk, n_blk, _epi_value, True, (2, 3)))
    shapes, out_specs = [], []
    for span in _segment_steps(segments, 1):
        seg_shapes, seg_specs = _split_out(span, n_blk, head_rows)
        shapes += seg_shapes
        out_specs += seg_specs
    return _proj(hn, w, layer, segments, shapes, out_specs, name, extras, specs, run_after=run_after)


UVZ_CAST_STEPS = 32


def _epi_uz(acc_u, acc_z, extras, outs, rc, latent):
    outs[0][_chunk_rows(rc), :] = (_gelu_tanh(acc_u) * _silu(acc_z)).astype(BF16)


def _uzv_proj(hn, w, layer, n_blk, name, w_out):
    segments = (_Segment(0, n_blk, _epi_uz, False, (0,), pair_blk0=2 * n_blk),
                _Segment(n_blk, n_blk, _epi_gelu, False, (1,)))
    (uz_span, v_span) = _segment_steps(segments, PROJ_TN_WIDE // PROJ_TN)
    last_tile = TOKENS // PROJ_TM - 1
    shape = jax.ShapeDtypeStruct((TOKENS, n_blk * PROJ_TN), BF16)
    out_specs = [
        pl.BlockSpec((PROJ_TM, PROJ_TN),
                     lambda j, i: _held(uz_span, j, (i, j - uz_span[0]), (0, 0), (last_tile, uz_span[1] - 1))),
        pl.BlockSpec((PROJ_TM, PROJ_TN_WIDE),
                     lambda j, i: _held(v_span, j, (i, j - v_span[0]), (0, 0), (last_tile, v_span[1] - 1)))]
    return _proj(hn, w, layer, segments, [shape, shape], out_specs, name, tn=PROJ_TN_WIDE,
                 cast=(w_out, layer, UVZ_CAST_STEPS))


OUT_COLS = 512
OUT_TM = 512
OUT_TM_WIDE_K = 256


def _cast_slab_specs(w, layer, n_slabs, slab_of_step):
    k, n = w.shape[1], w.shape[2]
    rows = k // n_slabs
    return (pl.BlockSpec((None, rows, n), lambda *g: (layer, slab_of_step(*g), 0)),
            pl.BlockSpec((rows, n), lambda *g: (slab_of_step(*g), 0)),
            jax.ShapeDtypeStruct((k, n), BF16))


def _out_kernel(*refs, tile0, tm, split_a, split_x, fuse_norm):
    it = iter(refs)
    a_refs = [next(it), next(it)] if split_a else [next(it)]
    wbf_ref = next(it)
    x_refs = [next(it), next(it)] if split_x else [next(it)]
    gate_ref = next(it)
    if fuse_norm:
        g_ref, shift_ref, scale_ref = next(it), next(it), next(it)
    xnew_ref = next(it)
    hn_ref = next(it) if fuse_norm else None

    def body(a_ref, x_ref):
        ssq = jnp.zeros((tm, 1), F32)
        for cb in range(D_MODEL // OUT_COLS):
            sl = slice(cb * OUT_COLS, (cb + 1) * OUT_COLS)
            acc = jnp.dot(a_ref[...], wbf_ref[:, sl], preferred_element_type=F32)
            xn = x_ref[:, sl] + gate_ref[:, sl] * acc
            xnew_ref[:, sl] = xn
            if fuse_norm:
                ssq = ssq + jnp.sum(xn * xn, axis=-1, keepdims=True)
        if fuse_norm:
            rs = lax.rsqrt(ssq * (1.0 / D_MODEL) + EPS)
            for cb in range(D_MODEL // OUT_COLS):
                sl = slice(cb * OUT_COLS, (cb + 1) * OUT_COLS)
                gmul = g_ref[:, sl] * (1.0 + scale_ref[:, sl])
                hn_ref[:, sl] = (xnew_ref[:, sl] * rs * gmul + shift_ref[:, sl]).astype(BF16)

    if split_a or split_x:
        is_prompt = pl.program_id(0) + tile0 < PROMPT_TOKENS // tm
        pl.when(is_prompt)(lambda: body(a_refs[0], x_refs[0]))
        pl.when(jnp.logical_not(is_prompt))(lambda: body(a_refs[-1], x_refs[-1]))
    else:
        body(a_refs[0], x_refs[0])


def _out_proj(a, wbf, x, mod4, layer, norm_g3, name, tm, rows=None):
    split_a = isinstance(a, tuple)
    k = a[0].shape[1] if split_a else a.shape[1]
    tok0, n_tok = rows if rows is not None else (0, TOKENS)
    tile0 = tok0 // tm
    split_x = isinstance(x, tuple)
    fuse_norm = norm_g3 is not None
    mod_spec = lambda part, lyr: pl.BlockSpec(
        (None, None, 1, D_MODEL), lambda i: (lyr, _cond_row((i + tile0) * tm), 0, part))
    row_spec = lambda width: pl.BlockSpec((tm, width), lambda i: (i + tile0, 0))
    operands = [*a, wbf] if split_a else [a, wbf]
    in_specs = [*(_split_specs(tm, k, tile0) if split_a else [row_spec(k)]),
                pl.BlockSpec((k, D_MODEL), lambda i: (0, 0))]
    if split_x:
        assert rows is None
        operands += list(x)
        in_specs += _split_specs(tm, D_MODEL)
    else:
        operands.append(x)
        in_specs.append(row_spec(D_MODEL))
    operands.append(mod4)
    in_specs.append(mod_spec(2, layer))
    out_shapes = [jax.ShapeDtypeStruct((n_tok, D_MODEL), F32)]
    out_specs = [pl.BlockSpec((tm, D_MODEL), lambda i: (i, 0))]
    if fuse_norm:
        operands += [norm_g3, mod4, mod4]
        in_specs += [pl.BlockSpec((None, 1, D_MODEL), lambda i: (layer + 1, 0, 0)),
                     mod_spec(0, layer + 1), mod_spec(1, layer + 1)]
        out_shapes.append(jax.ShapeDtypeStruct((n_tok, D_MODEL), BF16))
        out_specs.append(pl.BlockSpec((tm, D_MODEL), lambda i: (i, 0)))
    kern = functools.partial(_out_kernel, tile0=tile0, tm=tm,
                             split_a=split_a, split_x=split_x, fuse_norm=fuse_norm)
    return pl.pallas_call(
        kern,
        grid=(n_tok // tm,),
        in_specs=in_specs,
        out_specs=out_specs,
        out_shape=out_shapes,
        compiler_params=_params(1),
        name=name,
    )(*operands)


def _dot_nt(a, b):
    return lax.dot_general(a, b, (((1,), (1,)), ((), ())), preferred_element_type=F32)


def _head(ref, h, rows=slice(None)):
    return ref[rows, h * HEAD_DIM:(h + 1) * HEAD_DIM]


def _with_ones(v):
    return jnp.concatenate([v, jnp.ones(v.shape, v.dtype)], axis=1)


def _lane_chunks(s):
    return [s[:, c * HEAD_DIM:(c + 1) * HEAD_DIM] for c in range(s.shape[1] // HEAD_DIM)]


def _softmax_pv(score_blocks, value_blocks, sink):
    rows = score_blocks[0].shape[0]
    mx = functools.reduce(jnp.maximum, [c for s in score_blocks for c in _lane_chunks(s)])
    m = jnp.broadcast_to(jnp.max(mx, axis=-1, keepdims=True), (rows, HEAD_DIM))
    if sink is not None:
        m = jnp.maximum(m, sink)
    o = None
    for s, v in zip(score_blocks, value_blocks):
        p = jnp.concatenate([jnp.exp2(c - m) for c in _lane_chunks(s)], axis=1).astype(BF16)
        part = jnp.dot(p, v, preferred_element_type=F32)
        o = part if o is None else o + part
    den = o[:, HEAD_DIM:]
    if sink is not None:
        den = den + jnp.exp2(sink - m)
    return o[:, :HEAD_DIM] * (1.0 / den)


PROMPT_REQUESTS = 4


def _attn_prompt_kernel(sink_ref, q_ref, k_ref, v_ref, z_ref, *rest, n_heads, n_kv, use_sink):
    o_ref = rest[-1]
    grp = n_heads // n_kv
    for r in range(PROMPT_REQUESTS):
        rows = slice(r * PROMPT_SEQ, (r + 1) * PROMPT_SEQ)
        for g in range(n_kv):
            kg = _head(k_ref, g, rows)
            vg = _with_ones(_head(v_ref, g, rows))
            for h in range(g * grp, (g + 1) * grp):
                s = _dot_nt(_head(q_ref, h, rows), kg)
                sink = jnp.full((PROMPT_SEQ, HEAD_DIM), sink_ref[h], F32) if use_sink else None
                o = _softmax_pv([s], [vg], sink)
                o_ref[rows, h * HEAD_DIM:(h + 1) * HEAD_DIM] = (
                    o * _head(z_ref, h, rows).astype(F32)).astype(BF16)


def _attn_prompt(sink, qz, k, v, n_kv, use_sink, name, run_after=None):
    ordering = [] if run_after is None else [run_after]
    width = qz.shape[1] // 2
    kern = functools.partial(_attn_prompt_kernel, n_heads=width // HEAD_DIM, n_kv=n_kv, use_sink=use_sink)
    rows = PROMPT_REQUESTS * PROMPT_SEQ
    cache_spec = pl.BlockSpec((rows, k.shape[1]), lambda b: (b, 0))
    return pl.pallas_call(
        kern,
        grid=(N_PROMPT // PROMPT_REQUESTS,),
        in_specs=[pl.BlockSpec(memory_space=pltpu.SMEM),
                  pl.BlockSpec((rows, width), lambda b: (b, 0)),
                  cache_spec, cache_spec,
                  pl.BlockSpec((rows, width), lambda b: (b, 1)),
                  *[pl.BlockSpec(memory_space=pl.ANY) for _ in ordering]],
        out_specs=pl.BlockSpec((rows, width), lambda b: (b, 0)),
        out_shape=jax.ShapeDtypeStruct((PROMPT_TOKENS, width), BF16),
        compiler_params=_params(1),
        name=name,
    )(sink, qz, k, v, qz, *ordering)


WIN_BAND = 3 * WINDOW
WIN_BLOCKS = 2


def _attn_win_kernel(sink_ref, q_ref, k_ref, v_ref, kc_ref, vc_ref, z_ref, w_ref, o_ref, wbf_ref, *,
                     n_heads, n_kv):
    wbf_ref[...] = w_ref[...].astype(BF16)
    grp = n_heads // n_kv
    shape = (grp * WINDOW, WIN_BAND)
    row_in_block = lax.broadcasted_iota(jnp.int32, shape, 0) & (WINDOW - 1)
    col = lax.broadcasted_iota(jnp.int32, shape, 1)
    for g in range(n_kv):
        heads = [g * grp + t for t in range(grp)]
        ctx_rows = pl.ds(g, PAST_LEN, stride=n_kv)
        kc = kc_ref[ctx_rows, :].astype(BF16)
        vc = _with_ones(vc_ref[ctx_rows, :].astype(BF16))
        sink = jnp.concatenate([jnp.full((WINDOW, HEAD_DIM), sink_ref[h], F32) for h in heads], axis=0)
        for blk in range(WIN_BLOCKS):
            n = pl.program_id(1) * WIN_BLOCKS + blk
            rows = slice(blk * WINDOW, (blk + 1) * WINDOW)
            start = pl.multiple_of(jnp.clip((n - 1) * WINDOW, 0, SAMPLE_SEQ - WIN_BAND), WINDOW)
            valid = jnp.abs(start + col - (n * WINDOW + row_in_block)) <= WINDOW
            band = pl.ds(start, WIN_BAND)
            qs = jnp.concatenate([_head(q_ref, h, rows) for h in heads], axis=0)
            kb = _head(k_ref, g, band)
            vb = _with_ones(_head(v_ref, g, band))
            s_band = jnp.where(valid, _dot_nt(qs, kb), NEG_INF)
            s_ctx = _dot_nt(qs, kc)
            o = _softmax_pv([s_band, s_ctx], [vb, vc], sink)
            for t, h in enumerate(heads):
                oh = o[t * WINDOW:(t + 1) * WINDOW]
                o_ref[rows, h * HEAD_DIM:(h + 1) * HEAD_DIM] = (
                    oh * _head(z_ref, h, rows).astype(F32)).astype(BF16)


def _attn_win(sink, qz, k, v, cache_k, cache_v, layer_in_kind, n_kv, w_out):
    width = qz.shape[1] // 2
    kv_width = k.shape[1]
    q_rows = WIN_BLOCKS * WINDOW
    blocks_per_seq = SAMPLE_SEQ // q_rows
    q_map = lambda b, n: (PROMPT_TOKENS // q_rows + b * blocks_per_seq + n, 0)
    z_map = lambda b, n: (PROMPT_TOKENS // q_rows + b * blocks_per_seq + n, 1)
    kv_map = lambda b, n: (PROMPT_TOKENS // SAMPLE_SEQ + b, 0)
    cache_map = lambda b, n: (b, layer_in_kind, 0, 0)
    kern = functools.partial(_attn_win_kernel, n_heads=width // HEAD_DIM, n_kv=n_kv)
    w_spec, wbf_spec, wbf_shape = _cast_slab_specs(w_out, layer_in_kind, N_SAMPLE * blocks_per_seq,
                                                   lambda b, n: b * blocks_per_seq + n)
    return pl.pallas_call(
        kern,
        grid=(N_SAMPLE, blocks_per_seq),
        in_specs=[pl.BlockSpec(memory_space=pltpu.SMEM),
                  pl.BlockSpec((q_rows, width), q_map),
                  pl.BlockSpec((SAMPLE_SEQ, kv_width), kv_map),
                  pl.BlockSpec((SAMPLE_SEQ, kv_width), kv_map),
                  pl.BlockSpec((None, None, PAST_LEN * n_kv, HEAD_DIM), cache_map),
                  pl.BlockSpec((None, None, PAST_LEN * n_kv, HEAD_DIM), cache_map),
                  pl.BlockSpec((q_rows, width), z_map),
                  w_spec],
        out_specs=[pl.BlockSpec((q_rows, width), lambda b, n: (b * blocks_per_seq + n, 0)), wbf_spec],
        out_shape=[jax.ShapeDtypeStruct((SAMPLE_TOKENS, width), BF16), wbf_shape],
        compiler_params=_params(2),
        name="attn_win_latent",
    )(sink, qz, k, v, cache_k, cache_v, qz, w_out)


NAT_QROWS = 4
NAT_HEADS_PER_STEP = 2
GRID_ROWS = SAMPLE_SEQ // GRID_W
N_DR = 2 * NAT_ROWS - 1
MASKED_TILE = N_DR


def _nat_row_start(qr):
    return min(max(qr - NAT_ROWS // 2, 0), GRID_ROWS - NAT_ROWS)


def _attn_nat_kernel(bias_ref, q_ref, k_ref, v_ref, kc_ref, vc_ref, z_ref, w_ref, o_ref, wbf_ref,
                     left_ref, right_ref):
    wbf_ref[...] = w_ref[...].astype(BF16)
    n_heads = kc_ref.shape[1] // PAST_LEN
    for hl in range(NAT_HEADS_PER_STEP):
        _attn_nat_head(bias_ref.at[hl], q_ref, k_ref, v_ref, kc_ref, vc_ref, z_ref, o_ref, left_ref, right_ref,
                       lanes=slice(hl * HEAD_DIM, (hl + 1) * HEAD_DIM),
                       head=pl.program_id(0) * NAT_HEADS_PER_STEP + hl, n_heads=n_heads)


def _attn_nat_head(bias_ref, q_ref, k_ref, v_ref, kc_ref, vc_ref, z_ref, o_ref, left_ref, right_ref, *,
                   lanes, head, n_heads):
    def build_bias_tiles():
        shape = (GRID_W, 2 * GRID_W)
        lane = lax.broadcasted_iota(jnp.int32, shape, 1)
        qc = lax.broadcasted_iota(jnp.int32, shape, 0)
        kc_ = lane & (GRID_W - 1)
        cs = jnp.clip(qc - NAT_COLS // 2, 0, GRID_W - NAT_COLS)
        col_ok = jnp.logical_and(kc_ >= cs, kc_ < cs + NAT_COLS)
        is_left = lane < GRID_W
        lanes = 2 * GRID_W
        for dri in range(N_DR):
            row = jnp.broadcast_to(bias_ref[dri:dri + 1, :], shape)
            on_left = pltpu.roll(row, lanes - (NAT_COLS - 1), 1, stride=1, stride_axis=0)
            on_right = pltpu.roll(row, GRID_W - (NAT_COLS - 1), 1, stride=1, stride_axis=0)
            left_ref[dri] = jnp.where(is_left, jnp.where(col_ok, on_left, NEG_INF), 0.0)
            right_ref[dri] = jnp.where(is_left, 0.0, jnp.where(col_ok, on_right, NEG_INF))
        left_ref[MASKED_TILE] = jnp.where(is_left, NEG_INF, 0.0)
        right_ref[MASKED_TILE] = jnp.where(is_left, 0.0, NEG_INF)

    build_bias_tiles()
    ctx_rows = pl.ds(head, PAST_LEN, stride=n_heads)
    for b, qb in [(b, qb) for b in range(N_SAMPLE) for qb in range(GRID_ROWS // NAT_QROWS)]:
        if qb == 0:
            kc = kc_ref[b, ctx_rows, :].astype(BF16)
            vc = _with_ones(vc_ref[b, ctx_rows, :].astype(BF16))
        qrows = range(qb * NAT_QROWS, (qb + 1) * NAT_QROWS)
        krow0 = _nat_row_start(qrows[0]) // 2 * 2
        krow1 = -(-(_nat_row_start(qrows[-1]) + NAT_ROWS) // 2) * 2
        tok0 = b * SAMPLE_SEQ
        kwin = slice(tok0 + krow0 * GRID_W, tok0 + krow1 * GRID_W)
        qwin = slice(tok0 + qrows[0] * GRID_W, tok0 + (qrows[-1] + 1) * GRID_W)

        def tile_index(qr, kr):
            rs = _nat_row_start(qr)
            return kr - qr + (NAT_ROWS - 1) if rs <= kr < rs + NAT_ROWS else MASKED_TILE

        bias = jnp.concatenate(
            [jnp.concatenate([left_ref[tile_index(qr, kr)] + right_ref[tile_index(qr, kr + 1)]
                              for kr in range(krow0, krow1, 2)], axis=1)
             for qr in qrows], axis=0)

        q = q_ref[qwin, lanes]
        s_nb = _dot_nt(q, k_ref[kwin, lanes]) + bias
        s_ctx = _dot_nt(q, kc)
        o = _softmax_pv([s_nb, s_ctx], [_with_ones(v_ref[kwin, lanes]), vc], None)
        o_ref[qwin, lanes] = (o * z_ref[qwin, lanes].astype(F32)).astype(BF16)


def _attn_nat(rel_bias, qz, k, v, cache_k, cache_v, layer_in_kind, w_out):
    n_heads = qz.shape[1] // (2 * HEAD_DIM)
    bias_rows = jnp.pad(rel_bias, ((0, 0), (0, N_DR + 1 - rel_bias.shape[1]),
                                   (0, 2 * GRID_W - rel_bias.shape[2])))
    latent_tile = PROMPT_TOKENS // SAMPLE_TOKENS
    step_width = NAT_HEADS_PER_STEP * HEAD_DIM
    n_steps = n_heads // NAT_HEADS_PER_STEP
    latent_spec = pl.BlockSpec((SAMPLE_TOKENS, step_width), lambda h: (latent_tile, h))
    z_spec = pl.BlockSpec((SAMPLE_TOKENS, step_width), lambda h: (latent_tile, n_steps + h))
    cache_spec = pl.BlockSpec((N_SAMPLE, None, PAST_LEN * n_heads, HEAD_DIM),
                              lambda h: (0, layer_in_kind, 0, 0))
    w_spec, wbf_spec, wbf_shape = _cast_slab_specs(w_out, layer_in_kind, n_steps, lambda h: h)
    return pl.pallas_call(
        _attn_nat_kernel,
        grid=(n_steps,),
        in_specs=[pl.BlockSpec((NAT_HEADS_PER_STEP, N_DR + 1, 2 * GRID_W), lambda h: (h, 0, 0)),
                  latent_spec, latent_spec, latent_spec, cache_spec, cache_spec, z_spec, w_spec],
        out_specs=[pl.BlockSpec((SAMPLE_TOKENS, step_width), lambda h: (0, h)), wbf_spec],
        out_shape=[jax.ShapeDtypeStruct((SAMPLE_TOKENS, n_heads * HEAD_DIM), BF16), wbf_shape],
        scratch_shapes=[pltpu.VMEM((N_DR + 1, GRID_W, 2 * GRID_W), F32),
                        pltpu.VMEM((N_DR + 1, GRID_W, 2 * GRID_W), F32)],
        compiler_params=_params(1),
        name="attn_nat_latent",
    )(bias_rows, qz, k, v, cache_k, cache_v, qz, w_out)


SPATIAL_TOKENS = 4 * CHUNK


def _spatial_kernel(uz_ref, v_ref, g_ref, b_ref, ws_ref, bs_ref, o_ref):
    for c in range(SPATIAL_TOKENS // CHUNK):
        rows = slice(c * CHUNK, (c + 1) * CHUNK)
        v = v_ref[rows, :].astype(F32)
        mu = jnp.mean(v, axis=-1, keepdims=True)
        vc = v - mu
        var = jnp.mean(vc * vc, axis=-1, keepdims=True)
        vn = (vc * lax.rsqrt(var + EPS) * g_ref[...] + b_ref[...]).astype(BF16)
        for g in range(GMLP_GROUPS):
            sl = slice(g * GMLP_GROUP_WIDTH, (g + 1) * GMLP_GROUP_WIDTH)
            sv = jnp.dot(ws_ref[g], vn[:, sl], preferred_element_type=F32) + bs_ref[:, g:g + 1]
            o_ref[rows, sl] = uz_ref[rows, sl] * sv.astype(BF16)


def _spatial(uz, v, ln_g, ln_b, w_s, b_s):
    row = pl.BlockSpec((SPATIAL_TOKENS, GMLP_WIDTH), lambda i: (i, 0))
    vec = pl.BlockSpec((1, GMLP_WIDTH), lambda i: (0, 0))
    return pl.pallas_call(
        _spatial_kernel,
        grid=(TOKENS // SPATIAL_TOKENS,),
        in_specs=[row, row, vec, vec,
                  pl.BlockSpec((GMLP_GROUPS, CHUNK, CHUNK), lambda i: (0, 0, 0)),
                  pl.BlockSpec((CHUNK, GMLP_GROUPS), lambda i: (0, 0))],
        out_specs=row,
        out_shape=jax.ShapeDtypeStruct((TOKENS, GMLP_WIDTH), BF16),
        compiler_params=_params(1),
        name="gmlp_spatial",
    )(uz, v, ln_g.reshape(1, -1), ln_b.reshape(1, -1), w_s.astype(BF16), b_s.T)


def kernel(x_prompt, x_sample, cache_win_k, cache_win_v, cache_nat_k, cache_nat_v, c, c_ctx,
           norm_g, w_ada, b_ada,
           win_w_in, win_q_norm, win_k_norm, win_sink, win_w_out,
           nat_w_in, nat_q_norm, nat_k_norm, nat_rel_bias, nat_w_out,
           gmlp_w_in, gmlp_ln_g, gmlp_ln_b, gmlp_w_s, gmlp_b_s, gmlp_w_out):
    depth = norm_g.shape[0]
    xp = x_prompt.reshape(PROMPT_TOKENS, D_MODEL)
    xs = x_sample.reshape(SAMPLE_TOKENS, D_MODEL)
    cond = jnp.zeros((COND_ROWS, D_MODEL), F32).at[:N_SAMPLE].set(c).at[CTX_COND_ROW].set(c_ctx)
    mod4 = _ada(cond, w_ada, b_ada).reshape(depth, COND_ROWS, 1, 3 * D_MODEL)
    norm_g3 = norm_g.reshape(depth, 1, D_MODEL)
    rope_tabs = _rope_tables()
    no_sink = jnp.zeros((1,), F32)

    n_win = win_w_in.shape[0]
    n_nat = nat_w_in.shape[0]
    win_kv_heads = cache_win_k.shape[3]
    win_kv_width = win_kv_heads * HEAD_DIM
    cwk = cache_win_k.reshape(N_SAMPLE, n_win, PAST_LEN * win_kv_heads, HEAD_DIM)
    cwv = cache_win_v.reshape(N_SAMPLE, n_win, PAST_LEN * win_kv_heads, HEAD_DIM)
    nat_heads = cache_nat_k.shape[3]
    cnk = cache_nat_k.reshape(N_SAMPLE, n_nat, PAST_LEN * nat_heads, HEAD_DIM)
    cnv = cache_nat_v.reshape(N_SAMPLE, n_nat, PAST_LEN * nat_heads, HEAD_DIM)

    new_win_k, new_win_v, new_nat_k, new_nat_v = [], [], [], []
    x = (xp, xs)
    hn = _norm_mod(xp, xs, norm_g3, mod4, 0)
    for layer in range(depth):
        kind = layer % 3
        li = layer // 3
        if kind == 0:
            nq = D_MODEL // PROJ_TN
            nkv = win_kv_width // PROJ_TN
            qz = _qz_proj(hn, win_w_in, li, 0, nq + 2 * nkv, nq, win_q_norm[li], rope_tabs, "win_qz")
            kp, ks, vp, vs = _kv_proj(hn, win_w_in, li, nq, nkv, win_k_norm[li], rope_tabs, True, "win_kv",
                                      run_after=qz)
            new_win_k.append(kp)
            new_win_v.append(vp)
            sink = win_sink[li] * LOG2E
            a_latent, wbf = _attn_win(sink, qz, ks, vs, cwk, cwv, li, win_kv_heads, win_w_out)
            a = (_attn_prompt(sink, qz, ks, vs, win_kv_heads, True, "attn_win_prompt"), a_latent)
            out_tm, out_name = OUT_TM, "win_out"
        elif kind == 1:
            nq = D_MODEL // PROJ_TN
            qz = _qz_proj(hn, nat_w_in, li, 0, 3 * nq, nq, nat_q_norm[li], None, "nat_qz")
            kp, ks, vp, vs = _kv_proj(hn, nat_w_in, li, nq, nq, nat_k_norm[li], None, False, "nat_kv",
                                      run_after=qz)
            new_nat_k.append(kp)
            new_nat_v.append(vp)
            a_latent, wbf = _attn_nat(nat_rel_bias[li] * LOG2E, qz, ks, vs, cnk, cnv, li, nat_w_out)
            a = (_attn_prompt(no_sink, qz, ks, vs, nat_heads, False, "attn_nat_prompt", run_after=a_latent),
                 a_latent)
            out_tm, out_name = OUT_TM, "nat_out"
        else:
            nw = GMLP_WIDTH // PROJ_TN
            uz, v, wbf = _uzv_proj(hn, gmlp_w_in, li, nw, "gmlp_uzv", gmlp_w_out)
            a = _spatial(uz, v, gmlp_ln_g[li], gmlp_ln_b[li], gmlp_w_s[li], gmlp_b_s[li])
            out_tm, out_name = OUT_TM_WIDE_K, "gmlp_out"
        if layer + 1 < depth:
            x, hn = _out_proj(a, wbf, x, mod4, layer, norm_g3, out_name, out_tm)
        else:
            (yp,) = _out_proj(a, wbf, x, mod4, layer, None, out_name + "_prompt", out_tm,
                              rows=(0, PROMPT_TOKENS))
            (ys,) = _out_proj(a, wbf, x, mod4, layer, None, out_name + "_latent", out_tm,
                              rows=(PROMPT_TOKENS, SAMPLE_TOKENS))

    cache_shape = lambda layers, heads: jnp.stack(
        [c_.reshape(N_PROMPT, PROMPT_SEQ, heads, HEAD_DIM) for c_ in layers], axis=1)
    return (yp.reshape(N_PROMPT, PROMPT_SEQ, D_MODEL), ys.reshape(N_SAMPLE, SAMPLE_SEQ, D_MODEL),
            cache_shape(new_win_k, win_kv_heads), cache_shape(new_win_v, win_kv_heads),
            cache_shape(new_nat_k, nat_heads), cache_shape(new_nat_v, nat_heads))
```

```python
import functools
import math
from typing import Callable, NamedTuple

import jax
import jax.numpy as jnp
import numpy as np
from jax import lax
from jax.experimental import pallas as pl
from jax.experimental.pallas import tpu as pltpu

F32 = jnp.float32
BF16 = jnp.bfloat16

D_MODEL = 2048
HEAD_DIM = 128
N_PROMPT = 16
PROMPT_SEQ = 256
N_SAMPLE = 2
SAMPLE_SEQ = 1024
PROMPT_TOKENS = N_PROMPT * PROMPT_SEQ
SAMPLE_TOKENS = N_SAMPLE * SAMPLE_SEQ
TOKENS = PROMPT_TOKENS + SAMPLE_TOKENS
PAST_LEN = 512
GRID_W = 64
EPS = 1e-6
NEG_INF = -1e30
ROPE_THETA = 10000.0
WINDOW = 128
NAT_ROWS = 8
NAT_COLS = 16
GMLP_WIDTH = 2 * D_MODEL
GMLP_GROUPS = 16
GMLP_GROUP_WIDTH = GMLP_WIDTH // GMLP_GROUPS
CHUNK = 128
CTX_COND_ROW = 2
COND_ROWS = 8
SM_SCALE = HEAD_DIM ** -0.5
LOG2E = math.log2(math.e)
QUERY_SCALE = SM_SCALE * LOG2E

VMEM_LIMIT = 56 * 1024 * 1024


def _params(n_axes):
    return pltpu.CompilerParams(dimension_semantics=("arbitrary",) * n_axes,
                                vmem_limit_bytes=VMEM_LIMIT)


def _cond_row(tok0):
    return jnp.where(tok0 < PROMPT_TOKENS, CTX_COND_ROW, (tok0 - PROMPT_TOKENS) // SAMPLE_SEQ)


def _silu(x):
    return x * (0.5 + 0.5 * jnp.tanh(0.5 * x))


def _gelu_tanh(x):
    return 0.5 * x * (1.0 + jnp.tanh(math.sqrt(2.0 / math.pi) * (x + 0.044715 * (x * x * x))))


def _split_specs(tm, width, tile0=0):
    n_p = PROMPT_TOKENS // tm
    return [pl.BlockSpec((tm, width), lambda i: (jnp.minimum(i + tile0, n_p - 1), 0)),
            pl.BlockSpec((tm, width), lambda i: (jnp.maximum(i + tile0 - n_p, 0), 0))]


def _ada_kernel(cond_ref, w_ref, b_ref, o_ref):
    s = _silu(cond_ref[...]).astype(BF16)
    o_ref[...] = jnp.dot(s, w_ref[...].astype(BF16), preferred_element_type=F32) + b_ref[...]


def _ada(cond, w_ada, b_ada, tn=1024):
    depth = w_ada.shape[0]
    n = w_ada.shape[2]
    return pl.pallas_call(
        _ada_kernel,
        grid=(depth, n // tn),
        in_specs=[pl.BlockSpec((COND_ROWS, D_MODEL), lambda l, j: (0, 0)),
                  pl.BlockSpec((None, D_MODEL, tn), lambda l, j: (l, 0, j)),
                  pl.BlockSpec((None, 1, tn), lambda l, j: (l, 0, j))],
        out_specs=pl.BlockSpec((None, COND_ROWS, tn), lambda l, j: (l, 0, j)),
        out_shape=jax.ShapeDtypeStruct((depth, COND_ROWS, n), F32),
        compiler_params=_params(2),
        name="ada_mod",
    )(cond, w_ada, b_ada.reshape(depth, 1, n))


NORM_ROWS = 64


def _norm_kernel(xp_ref, xs_ref, g_ref, shift_ref, scale_ref, o_ref, *, tm):
    gmul = g_ref[...] * (1.0 + scale_ref[...])
    shift = shift_ref[...]

    def run(x_ref):
        for r in range(tm // NORM_ROWS):
            rows = slice(r * NORM_ROWS, (r + 1) * NORM_ROWS)
            x = x_ref[rows, :]
            rs = lax.rsqrt(jnp.mean(x * x, axis=-1, keepdims=True) + EPS)
            o_ref[rows, :] = (x * rs * gmul + shift).astype(BF16)

    is_prompt = pl.program_id(0) < PROMPT_TOKENS // tm
    pl.when(is_prompt)(lambda: run(xp_ref))
    pl.when(jnp.logical_not(is_prompt))(lambda: run(xs_ref))


def _norm_mod(xp, xs, norm_g3, mod4, layer, tm=1024):
    return pl.pallas_call(
        functools.partial(_norm_kernel, tm=tm),
        grid=(TOKENS // tm,),
        in_specs=[*_split_specs(tm, D_MODEL),
                  pl.BlockSpec((None, 1, D_MODEL), lambda i: (layer, 0, 0)),
                  pl.BlockSpec((None, None, 1, D_MODEL), lambda i: (layer, _cond_row(i * tm), 0, 0)),
                  pl.BlockSpec((None, None, 1, D_MODEL), lambda i: (layer, _cond_row(i * tm), 0, 1))],
        out_specs=pl.BlockSpec((tm, D_MODEL), lambda i: (i, 0)),
        out_shape=jax.ShapeDtypeStruct((TOKENS, D_MODEL), BF16),
        compiler_params=_params(1),
        name="norm_mod",
    )(xp, xs, norm_g3, mod4, mod4)


PROJ_TM = 2048
PROJ_TN = 512
PROJ_TN_WIDE = 1024
PROJ_ROWS = 512
N_PROMPT_TILES = PROMPT_TOKENS // PROJ_TM


class _Segment(NamedTuple):
    col_blk0: int
    n_blk: int
    epilogue: Callable
    by_tile_kind: bool
    outs: tuple
    pair_blk0: int = -1


def _segment_steps(segments, wide):
    spans, j0 = [], 0
    for seg in segments:
        if seg.pair_blk0 >= 0:
            assert wide == 2
            nj = seg.n_blk
        else:
            assert seg.col_blk0 % wide == 0 and seg.n_blk % wide == 0
            nj = seg.n_blk // wide
        spans.append((j0, nj))
        j0 += nj
    return spans


def _proj_kernel(*refs, segments, spans, n_w, n_extra, n_out, cast_steps):
    a_ref, w_refs = refs[0], refs[1:1 + n_w]
    extras = refs[1 + n_w:1 + n_w + n_extra]
    outs = refs[1 + n_w + n_extra:1 + n_w + n_extra + n_out]
    wbf_ref = refs[1 + n_w + n_extra + n_out]

    @pl.when(pl.program_id(1) == 0)
    def _():
        for h, w_ref in enumerate(w_refs):
            wbf_ref[:, h * PROJ_TN:(h + 1) * PROJ_TN] = w_ref[...].astype(BF16)

    if cast_steps:
        step = pl.program_id(0) * pl.num_programs(1) + pl.program_id(1)

        @pl.when(step < cast_steps)
        def _():
            outs[-1][...] = extras[-1][...].astype(BF16)

    def run(seg):
        seg_outs = [outs[k] for k in seg.outs]

        def body(latent):
            for rc in range(PROJ_TM // PROJ_ROWS):
                rows = slice(rc * PROJ_ROWS, (rc + 1) * PROJ_ROWS)
                accs = [jnp.dot(a_ref[rows, :], wbf_ref[:, col0:col0 + PROJ_TN], preferred_element_type=F32)
                        for col0 in range(0, wbf_ref.shape[1], PROJ_TN)]
                if seg.pair_blk0 >= 0:
                    seg.epilogue(*accs, extras, seg_outs, rc, latent)
                else:
                    for c, acc in enumerate(accs):
                        seg.epilogue(acc, extras, seg_outs, rc, c * PROJ_TN, latent)

        if seg.by_tile_kind:
            is_latent = pl.program_id(1) >= N_PROMPT_TILES
            pl.when(is_latent)(lambda: body(True))
            pl.when(jnp.logical_not(is_latent))(lambda: body(False))
        else:
            body(None)

    j = pl.program_id(0)
    for seg, (j0, nj) in zip(segments, spans):
        if len(segments) == 1:
            run(seg)
        else:
            pl.when(jnp.logical_and(j >= j0, j < j0 + nj))(functools.partial(run, seg))


def _proj(a, w, layer, segments, out_shapes, out_specs, name, extras=(), extra_specs=(), tn=PROJ_TN,
          run_after=None, cast=None):
    m, k = a.shape
    if run_after is not None:
        extras = [*extras, run_after]
        extra_specs = [*extra_specs, pl.BlockSpec(memory_space=pl.ANY)]
    wide = tn // PROJ_TN
    spans = _segment_steps(segments, wide)
    cast_steps = 0
    if cast is not None:
        w_out, out_layer, cast_steps = cast
        n_tiles = m // PROJ_TM
        assert cast_steps <= n_tiles * sum(nj for _, nj in spans)
        w_spec, wbf_spec, wbf_shape = _cast_slab_specs(
            w_out, out_layer, cast_steps, lambda j, i: jnp.minimum(j * n_tiles + i, cast_steps - 1))
        extras, extra_specs = [*extras, w_out], [*extra_specs, w_spec]
        out_shapes, out_specs = [*out_shapes, wbf_shape], [*out_specs, wbf_spec]

    def w_col(j, half):
        col = None
        for seg, (j0, _) in zip(segments, spans):
            if seg.pair_blk0 >= 0:
                c = (seg.pair_blk0 if half else seg.col_blk0) + j - j0
            else:
                c = seg.col_blk0 + wide * (j - j0) + half
            col = c if col is None else jnp.where(j >= j0, c, col)
        return col

    w_specs = [pl.BlockSpec((None, k, PROJ_TN), lambda j, i, half=half: (layer, 0, w_col(j, half)))
               for half in range(wide)]
    kern = functools.partial(_proj_kernel, segments=segments, spans=spans, n_w=wide, n_extra=len(extras),
                             n_out=len(out_shapes), cast_steps=cast_steps)
    return pl.pallas_call(
        kern,
        grid=(sum(nj for _, nj in spans), m // PROJ_TM),
        in_specs=[pl.BlockSpec((PROJ_TM, k), lambda j, i: (i, 0)), *w_specs, *extra_specs],
        out_specs=out_specs,
        out_shape=out_shapes,
        scratch_shapes=[pltpu.VMEM((k, tn), BF16)],
        compiler_params=_params(2),
        name=name,
    )(a, *[w] * wide, *extras)


def _full_out(n_blk):
    return ([jax.ShapeDtypeStruct((TOKENS, n_blk * PROJ_TN), BF16)],
            [pl.BlockSpec((PROJ_TM, PROJ_TN_WIDE), lambda j, i: (i, j))])


def _held(span, j, inside, before, after):
    j0, nj = span
    pick = lambda a, b, c: jnp.where(j < j0, b, jnp.where(j >= j0 + nj, c, a))
    return tuple(pick(a, b, c) for a, b, c in zip(inside, before, after))


def _split_out(span, n_blk, head_rows):
    prompt_tile = lambda i: jnp.minimum(i, N_PROMPT_TILES - 1)
    last_prompt, last_tile = N_PROMPT_TILES - 1, TOKENS // PROJ_TM - 1
    jj = lambda j: j - span[0]
    if head_rows:
        assert n_blk == 1
        heads = PROJ_TN // HEAD_DIM
        cache_shape = jax.ShapeDtypeStruct((PROMPT_TOKENS * heads, HEAD_DIM), F32)
        cache_spec = pl.BlockSpec(
            (PROJ_TM * heads, HEAD_DIM),
            lambda j, i: _held(span, j, (prompt_tile(i), 0), (0, 0), (last_prompt, 0)))
    else:
        cache_shape = jax.ShapeDtypeStruct((N_PROMPT, PROMPT_SEQ, n_blk * PROJ_TN), F32)
        cache_spec = pl.BlockSpec(
            (PROJ_TM // PROMPT_SEQ, PROMPT_SEQ, PROJ_TN),
            lambda j, i: _held(span, j, (prompt_tile(i), 0, jj(j)), (0, 0, 0), (last_prompt, 0, n_blk - 1)))
    shapes = [cache_shape, jax.ShapeDtypeStruct((TOKENS, n_blk * PROJ_TN), BF16)]
    specs = [cache_spec,
             pl.BlockSpec((PROJ_TM, PROJ_TN),
                          lambda j, i: _held(span, j, (i, jj(j)), (0, 0), (last_tile, n_blk - 1)))]
    return shapes, specs


def _chunk_rows(rc):
    return slice(rc * PROJ_ROWS, (rc + 1) * PROJ_ROWS)


def _epi_silu(acc, extras, outs, rc, col0, latent):
    outs[0][_chunk_rows(rc), col0:col0 + PROJ_TN] = _silu(acc).astype(BF16)


def _epi_gelu(acc, extras, outs, rc, col0, latent):
    outs[0][_chunk_rows(rc), col0:col0 + PROJ_TN] = _gelu_tanh(acc).astype(BF16)


def _store_split(y, outs, rc, sl, latent):
    outs[1][_chunk_rows(rc), sl] = y.astype(BF16)
    if latent:
        return
    if len(outs[0].shape) == 2:
        heads = PROJ_TN // HEAD_DIM
        col0 = sl.start or 0
        for c in range(y.shape[1] // HEAD_DIM):
            head = col0 // HEAD_DIM + c
            rows = pl.ds(rc * PROJ_ROWS * heads + head, PROJ_ROWS, stride=heads)
            outs[0][rows, :] = y[:, c * HEAD_DIM:(c + 1) * HEAD_DIM]
    else:
        seqs = PROJ_ROWS // PROMPT_SEQ
        outs[0][rc * seqs:(rc + 1) * seqs, :, sl] = y.reshape(seqs, PROMPT_SEQ, y.shape[1])


def _epi_value(acc, extras, outs, rc, col0, latent):
    _store_split(acc, outs, rc, slice(col0, col0 + PROJ_TN), latent)


def _head_rmsnorm(acc, hh, g):
    a = acc[:, hh * HEAD_DIM:(hh + 1) * HEAD_DIM]
    return a * lax.rsqrt(jnp.mean(a * a, axis=-1, keepdims=True) + EPS) * g


def _normed_heads(acc, extras, rc, rope):
    g = extras[0][...]
    ys = [_head_rmsnorm(acc, hh, g) for hh in range(PROJ_TN // HEAD_DIM)]
    if rope:
        rows = _chunk_rows(rc)
        cos, sin, swap = extras[1][rows, :], extras[2][rows, :], extras[3][...]
        for pair in range(len(ys) // 2):
            both = jnp.concatenate(ys[2 * pair:2 * pair + 2], axis=1).astype(BF16)
            partner = jnp.dot(both, swap, preferred_element_type=F32)
            for t in range(2):
                hh = 2 * pair + t
                ys[hh] = ys[hh] * cos + partner[:, t * HEAD_DIM:(t + 1) * HEAD_DIM] * sin
    for hh, y in enumerate(ys):
        yield slice(hh * HEAD_DIM, (hh + 1) * HEAD_DIM), y


def _epi_query(acc, extras, outs, rc, col0, latent, *, rope):
    for sl, y in _normed_heads(acc, extras, rc, rope and latent):
        outs[0][_chunk_rows(rc), col0 + sl.start:col0 + sl.stop] = (y * QUERY_SCALE).astype(BF16)


def _epi_key(acc, extras, outs, rc, col0, latent, *, rope):
    for sl, y in _normed_heads(acc, extras, rc, rope and latent):
        _store_split(y, outs, rc, slice(col0 + sl.start, col0 + sl.stop), latent)


def _rope_tables():
    nf = HEAD_DIM // 4
    t = np.arange(SAMPLE_TOKENS) % SAMPLE_SEQ
    row = (t // GRID_W).astype(np.float32)
    col = (t % GRID_W).astype(np.float32)
    inv = np.float32(ROPE_THETA) ** (-np.arange(nf, dtype=np.float32) / np.float32(nf))
    ang_r = row[:, None] * inv
    ang_c = col[:, None] * inv
    cos = np.concatenate([np.cos(ang_r), np.cos(ang_r), np.cos(ang_c), np.cos(ang_c)], axis=1)
    sin = np.concatenate([-np.sin(ang_r), np.sin(ang_r), -np.sin(ang_c), np.sin(ang_c)], axis=1)
    lanes = np.arange(2 * HEAD_DIM)
    swap = lanes[:, None] == (lanes[None, :] ^ (HEAD_DIM // 4))
    return jnp.asarray(cos, F32), jnp.asarray(sin, F32), jnp.asarray(swap, BF16)


def _gain_extras(gain, rope_tabs):
    extras = [gain.reshape(1, HEAD_DIM)]
    specs = [pl.BlockSpec((1, HEAD_DIM), lambda j, i: (0, 0))]
    if rope_tabs is not None:
        tab_map = lambda j, i: (jnp.maximum(i - N_PROMPT_TILES, 0), 0)
        extras += list(rope_tabs)
        specs += [pl.BlockSpec((PROJ_TM, HEAD_DIM), tab_map)] * 2
        specs.append(pl.BlockSpec((2 * HEAD_DIM, 2 * HEAD_DIM), lambda j, i: (0, 0)))
    return extras, specs


def _qz_proj(hn, w, layer, q_blk0, z_blk0, n_blk, gain, rope_tabs, name):
    extras, specs = _gain_extras(gain, rope_tabs)
    rope = rope_tabs is not None
    segments = (_Segment(q_blk0, n_blk, functools.partial(_epi_query, rope=rope), rope, (0,)),
                _Segment(z_blk0, n_blk, _epi_silu, False, (0,)))
    return _proj(hn, w, layer, segments, *_full_out(2 * n_blk), name, extras, specs, tn=PROJ_TN_WIDE)[0]


def _kv_proj(hn, w, layer, k_blk0, n_blk, gain, rope_tabs, head_rows, name, run_after=None):
    extras, specs = _gain_extras(gain, rope_tabs)
    segments = (_Segment(k_blk0, n_blk, functools.partial(_epi_key, rope=rope_tabs is not None), True, (0, 1)),
                _Segment(k_blk0 + n_blk, n_blk, _epi_value, True, (2, 3)))
    shapes, out_specs = [], []
    for span in _segment_steps(segments, 1):
        seg_shapes, seg_specs = _split_out(span, n_blk, head_rows)
        shapes += seg_shapes
        out_specs += seg_specs
    return _proj(hn, w, layer, segments, shapes, out_specs, name, extras, specs, run_after=run_after)


UVZ_CAST_STEPS = 32


def _epi_uz(acc_u, acc_z, extras, outs, rc, latent):
    outs[0][_chunk_rows(rc), :] = (_gelu_tanh(acc_u) * _silu(acc_z)).astype(BF16)


def _uzv_proj(hn, w, layer, n_blk, name, w_out):
    segments = (_Segment(0, n_blk, _epi_uz, False, (0,), pair_blk0=2 * n_blk),
                _Segment(n_blk, n_blk, _epi_gelu, False, (1,)))
    (uz_span, v_span) = _segment_steps(segments, PROJ_TN_WIDE // PROJ_TN)
    last_tile = TOKENS // PROJ_TM - 1
    shape = jax.ShapeDtypeStruct((TOKENS, n_blk * PROJ_TN), BF16)
    out_specs = [
        pl.BlockSpec((PROJ_TM, PROJ_TN),
                     lambda j, i: _held(uz_span, j, (i, j - uz_span[0]), (0, 0), (last_tile, uz_span[1] - 1))),
        pl.BlockSpec((PROJ_TM, PROJ_TN_WIDE),
                     lambda j, i: _held(v_span, j, (i, j - v_span[0]), (0, 0), (last_tile, v_span[1] - 1)))]
    return _proj(hn, w, layer, segments, [shape, shape], out_specs, name, tn=PROJ_TN_WIDE,
                 cast=(w_out, layer, UVZ_CAST_STEPS))


OUT_COLS = 512
OUT_TM = 512
OUT_TM_WIDE_K = 256


def _cast_slab_specs(w, layer, n_slabs, slab_of_step):
    k, n = w.shape[1], w.shape[2]
    rows = k // n_slabs
    return (pl.BlockSpec((None, rows, n), lambda *g: (layer, slab_of_step(*g), 0)),
            pl.BlockSpec((rows, n), lambda *g: (slab_of_step(*g), 0)),
            jax.ShapeDtypeStruct((k, n), BF16))


def _out_kernel(*refs, tile0, tm, split_a, split_x, fuse_norm):
    it = iter(refs)
    a_refs = [next(it), next(it)] if split_a else [next(it)]
    wbf_ref = next(it)
    x_refs = [next(it), next(it)] if split_x else [next(it)]
    gate_ref = next(it)
    if fuse_norm:
        g_ref, shift_ref, scale_ref = next(it), next(it), next(it)
    xnew_ref = next(it)
    hn_ref = next(it) if fuse_norm else None

    def body(a_ref, x_ref):
        ssq = jnp.zeros((tm, 1), F32)
        for cb in range(D_MODEL // OUT_COLS):
            sl = slice(cb * OUT_COLS, (cb + 1) * OUT_COLS)
            acc = jnp.dot(a_ref[...], wbf_ref[:, sl], preferred_element_type=F32)
            xn = x_ref[:, sl] + gate_ref[:, sl] * acc
            xnew_ref[:, sl] = xn
            if fuse_norm:
                ssq = ssq + jnp.sum(xn * xn, axis=-1, keepdims=True)
        if fuse_norm:
            rs = lax.rsqrt(ssq * (1.0 / D_MODEL) + EPS)
            for cb in range(D_MODEL // OUT_COLS):
                sl = slice(cb * OUT_COLS, (cb + 1) * OUT_COLS)
                gmul = g_ref[:, sl] * (1.0 + scale_ref[:, sl])
                hn_ref[:, sl] = (xnew_ref[:, sl] * rs * gmul + shift_ref[:, sl]).astype(BF16)

    if split_a or split_x:
        is_prompt = pl.program_id(0) + tile0 < PROMPT_TOKENS // tm
        pl.when(is_prompt)(lambda: body(a_refs[0], x_refs[0]))
        pl.when(jnp.logical_not(is_prompt))(lambda: body(a_refs[-1], x_refs[-1]))
    else:
        body(a_refs[0], x_refs[0])


def _out_proj(a, wbf, x, mod4, layer, norm_g3, name, tm, rows=None):
    split_a = isinstance(a, tuple)
    k = a[0].shape[1] if split_a else a.shape[1]
    tok0, n_tok = rows if rows is not None else (0, TOKENS)
    tile0 = tok0 // tm
    split_x = isinstance(x, tuple)
    fuse_norm = norm_g3 is not None
    mod_spec = lambda part, lyr: pl.BlockSpec(
        (None, None, 1, D_MODEL), lambda i: (lyr, _cond_row((i + tile0) * tm), 0, part))
    row_spec = lambda width: pl.BlockSpec((tm, width), lambda i: (i + tile0, 0))
    operands = [*a, wbf] if split_a else [a, wbf]
    in_specs = [*(_split_specs(tm, k, tile0) if split_a else [row_spec(k)]),
                pl.BlockSpec((k, D_MODEL), lambda i: (0, 0))]
    if split_x:
        assert rows is None
        operands += list(x)
        in_specs += _split_specs(tm, D_MODEL)
    else:
        operands.append(x)
        in_specs.append(row_spec(D_MODEL))
    operands.append(mod4)
    in_specs.append(mod_spec(2, layer))
    out_shapes = [jax.ShapeDtypeStruct((n_tok, D_MODEL), F32)]
    out_specs = [pl.BlockSpec((tm, D_MODEL), lambda i: (i, 0))]
    if fuse_norm:
        operands += [norm_g3, mod4, mod4]
        in_specs += [pl.BlockSpec((None, 1, D_MODEL), lambda i: (layer + 1, 0, 0)),
                     mod_spec(0, layer + 1), mod_spec(1, layer + 1)]
        out_shapes.append(jax.ShapeDtypeStruct((n_tok, D_MODEL), BF16))
        out_specs.append(pl.BlockSpec((tm, D_MODEL), lambda i: (i, 0)))
    kern = functools.partial(_out_kernel, tile0=tile0, tm=tm,
                             split_a=split_a, split_x=split_x, fuse_norm=fuse_norm)
    return pl.pallas_call(
        kern,
        grid=(n_tok // tm,),
        in_specs=in_specs,
        out_specs=out_specs,
        out_shape=out_shapes,
        compiler_params=_params(1),
        name=name,
    )(*operands)


def _dot_nt(a, b):
    return lax.dot_general(a, b, (((1,), (1,)), ((), ())), preferred_element_type=F32)


def _head(ref, h, rows=slice(None)):
    return ref[rows, h * HEAD_DIM:(h + 1) * HEAD_DIM]


def _with_ones(v):
    return jnp.concatenate([v, jnp.ones(v.shape, v.dtype)], axis=1)


def _lane_chunks(s):
    return [s[:, c * HEAD_DIM:(c + 1) * HEAD_DIM] for c in range(s.shape[1] // HEAD_DIM)]


def _softmax_pv(score_blocks, value_blocks, sink):
    rows = score_blocks[0].shape[0]
    mx = functools.reduce(jnp.maximum, [c for s in score_blocks for c in _lane_chunks(s)])
    m = jnp.broadcast_to(jnp.max(mx, axis=-1, keepdims=True), (rows, HEAD_DIM))
    if sink is not None:
        m = jnp.maximum(m, sink)
    o = None
    for s, v in zip(score_blocks, value_blocks):
        p = jnp.concatenate([jnp.exp2(c - m) for c in _lane_chunks(s)], axis=1).astype(BF16)
        part = jnp.dot(p, v, preferred_element_type=F32)
        o = part if o is None else o + part
    den = o[:, HEAD_DIM:]
    if sink is not None:
        den = den + jnp.exp2(sink - m)
    return o[:, :HEAD_DIM] * (1.0 / den)


PROMPT_REQUESTS = 4


def _attn_prompt_kernel(sink_ref, q_ref, k_ref, v_ref, z_ref, *rest, n_heads, n_kv, use_sink):
    o_ref = rest[-1]
    grp = n_heads // n_kv
    for r in range(PROMPT_REQUESTS):
        rows = slice(r * PROMPT_SEQ, (r + 1) * PROMPT_SEQ)
        for g in range(n_kv):
            kg = _head(k_ref, g, rows)
            vg = _with_ones(_head(v_ref, g, rows))
            for h in range(g * grp, (g + 1) * grp):
                s = _dot_nt(_head(q_ref, h, rows), kg)
                sink = jnp.full((PROMPT_SEQ, HEAD_DIM), sink_ref[h], F32) if use_sink else None
                o = _softmax_pv([s], [vg], sink)
                o_ref[rows, h * HEAD_DIM:(h + 1) * HEAD_DIM] = (
                    o * _head(z_ref, h, rows).astype(F32)).astype(BF16)


def _attn_prompt(sink, qz, k, v, n_kv, use_sink, name, run_after=None):
    ordering = [] if run_after is None else [run_after]
    width = qz.shape[1] // 2
    kern = functools.partial(_attn_prompt_kernel, n_heads=width // HEAD_DIM, n_kv=n_kv, use_sink=use_sink)
    rows = PROMPT_REQUESTS * PROMPT_SEQ
    cache_spec = pl.BlockSpec((rows, k.shape[1]), lambda b: (b, 0))
    return pl.pallas_call(
        kern,
        grid=(N_PROMPT // PROMPT_REQUESTS,),
        in_specs=[pl.BlockSpec(memory_space=pltpu.SMEM),
                  pl.BlockSpec((rows, width), lambda b: (b, 0)),
                  cache_spec, cache_spec,
                  pl.BlockSpec((rows, width), lambda b: (b, 1)),
                  *[pl.BlockSpec(memory_space=pl.ANY) for _ in ordering]],
        out_specs=pl.BlockSpec((rows, width), lambda b: (b, 0)),
        out_shape=jax.ShapeDtypeStruct((PROMPT_TOKENS, width), BF16),
        compiler_params=_params(1),
        name=name,
    )(sink, qz, k, v, qz, *ordering)


WIN_BAND = 3 * WINDOW
WIN_BLOCKS = 2


def _attn_win_kernel(sink_ref, q_ref, k_ref, v_ref, kc_ref, vc_ref, z_ref, w_ref, o_ref, wbf_ref, *,
                     n_heads, n_kv):
    wbf_ref[...] = w_ref[...].astype(BF16)
    grp = n_heads // n_kv
    shape = (grp * WINDOW, WIN_BAND)
    row_in_block = lax.broadcasted_iota(jnp.int32, shape, 0) & (WINDOW - 1)
    col = lax.broadcasted_iota(jnp.int32, shape, 1)
    for g in range(n_kv):
        heads = [g * grp + t for t in range(grp)]
        ctx_rows = pl.ds(g, PAST_LEN, stride=n_kv)
        kc = kc_ref[ctx_rows, :].astype(BF16)
        vc = _with_ones(vc_ref[ctx_rows, :].astype(BF16))
        sink = jnp.concatenate([jnp.full((WINDOW, HEAD_DIM), sink_ref[h], F32) for h in heads], axis=0)
        for blk in range(WIN_BLOCKS):
            n = pl.program_id(1) * WIN_BLOCKS + blk
            rows = slice(blk * WINDOW, (blk + 1) * WINDOW)
            start = pl.multiple_of(jnp.clip((n - 1) * WINDOW, 0, SAMPLE_SEQ - WIN_BAND), WINDOW)
            valid = jnp.abs(start + col - (n * WINDOW + row_in_block)) <= WINDOW
            band = pl.ds(start, WIN_BAND)
            qs = jnp.concatenate([_head(q_ref, h, rows) for h in heads], axis=0)
            kb = _head(k_ref, g, band)
            vb = _with_ones(_head(v_ref, g, band))
            s_band = jnp.where(valid, _dot_nt(qs, kb), NEG_INF)
            s_ctx = _dot_nt(qs, kc)
            o = _softmax_pv([s_band, s_ctx], [vb, vc], sink)
            for t, h in enumerate(heads):
                oh = o[t * WINDOW:(t + 1) * WINDOW]
                o_ref[rows, h * HEAD_DIM:(h + 1) * HEAD_DIM] = (
                    oh * _head(z_ref, h, rows).astype(F32)).astype(BF16)


def _attn_win(sink, qz, k, v, cache_k, cache_v, layer_in_kind, n_kv, w_out):
    width = qz.shape[1] // 2
    kv_width = k.shape[1]
    q_rows = WIN_BLOCKS * WINDOW
    blocks_per_seq = SAMPLE_SEQ // q_rows
    q_map = lambda b, n: (PROMPT_TOKENS // q_rows + b * blocks_per_seq + n, 0)
    z_map = lambda b, n: (PROMPT_TOKENS // q_rows + b * blocks_per_seq + n, 1)
    kv_map = lambda b, n: (PROMPT_TOKENS // SAMPLE_SEQ + b, 0)
    cache_map = lambda b, n: (b, layer_in_kind, 0, 0)
    kern = functools.partial(_attn_win_kernel, n_heads=width // HEAD_DIM, n_kv=n_kv)
    w_spec, wbf_spec, wbf_shape = _cast_slab_specs(w_out, layer_in_kind, N_SAMPLE * blocks_per_seq,
                                                   lambda b, n: b * blocks_per_seq + n)
    return pl.pallas_call(
        kern,
        grid=(N_SAMPLE, blocks_per_seq),
        in_specs=[pl.BlockSpec(memory_space=pltpu.SMEM),
                  pl.BlockSpec((q_rows, width), q_map),
                  pl.BlockSpec((SAMPLE_SEQ, kv_width), kv_map),
                  pl.BlockSpec((SAMPLE_SEQ, kv_width), kv_map),
                  pl.BlockSpec((None, None, PAST_LEN * n_kv, HEAD_DIM), cache_map),
                  pl.BlockSpec((None, None, PAST_LEN * n_kv, HEAD_DIM), cache_map),
                  pl.BlockSpec((q_rows, width), z_map),
                  w_spec],
        out_specs=[pl.BlockSpec((q_rows, width), lambda b, n: (b * blocks_per_seq + n, 0)), wbf_spec],
        out_shape=[jax.ShapeDtypeStruct((SAMPLE_TOKENS, width), BF16), wbf_shape],
        compiler_params=_params(2),
        name="attn_win_latent",
    )(sink, qz, k, v, cache_k, cache_v, qz, w_out)


NAT_QROWS = 4
NAT_HEADS_PER_STEP = 2
GRID_ROWS = SAMPLE_SEQ // GRID_W
N_DR = 2 * NAT_ROWS - 1
MASKED_TILE = N_DR


def _nat_row_start(qr):
    return min(max(qr - NAT_ROWS // 2, 0), GRID_ROWS - NAT_ROWS)


def _attn_nat_kernel(bias_ref, q_ref, k_ref, v_ref, kc_ref, vc_ref, z_ref, w_ref, o_ref, wbf_ref,
                     left_ref, right_ref):
    wbf_ref[...] = w_ref[...].astype(BF16)
    n_heads = kc_ref.shape[1] // PAST_LEN
    for hl in range(NAT_HEADS_PER_STEP):
        _attn_nat_head(bias_ref.at[hl], q_ref, k_ref, v_ref, kc_ref, vc_ref, z_ref, o_ref, left_ref, right_ref,
                       lanes=slice(hl * HEAD_DIM, (hl + 1) * HEAD_DIM),
                       head=pl.program_id(0) * NAT_HEADS_PER_STEP + hl, n_heads=n_heads)


def _attn_nat_head(bias_ref, q_ref, k_ref, v_ref, kc_ref, vc_ref, z_ref, o_ref, left_ref, right_ref, *,
                   lanes, head, n_heads):
    def build_bias_tiles():
        shape = (GRID_W, 2 * GRID_W)
        lane = lax.broadcasted_iota(jnp.int32, shape, 1)
        qc = lax.broadcasted_iota(jnp.int32, shape, 0)
        kc_ = lane & (GRID_W - 1)
        cs = jnp.clip(qc - NAT_COLS // 2, 0, GRID_W - NAT_COLS)
        col_ok = jnp.logical_and(kc_ >= cs, kc_ < cs + NAT_COLS)
        is_left = lane < GRID_W
        lanes = 2 * GRID_W
        for dri in range(N_DR):
            row = jnp.broadcast_to(bias_ref[dri:dri + 1, :], shape)
            on_left = pltpu.roll(row, lanes - (NAT_COLS - 1), 1, stride=1, stride_axis=0)
            on_right = pltpu.roll(row, GRID_W - (NAT_COLS - 1), 1, stride=1, stride_axis=0)
            left_ref[dri] = jnp.where(is_left, jnp.where(col_ok, on_left, NEG_INF), 0.0)
            right_ref[dri] = jnp.where(is_left, 0.0, jnp.where(col_ok, on_right, NEG_INF))
        left_ref[MASKED_TILE] = jnp.where(is_left, NEG_INF, 0.0)
        right_ref[MASKED_TILE] = jnp.where(is_left, 0.0, NEG_INF)

    build_bias_tiles()
    ctx_rows = pl.ds(head, PAST_LEN, stride=n_heads)
    for b, qb in [(b, qb) for b in range(N_SAMPLE) for qb in range(GRID_ROWS // NAT_QROWS)]:
        if qb == 0:
            kc = kc_ref[b, ctx_rows, :].astype(BF16)
            vc = _with_ones(vc_ref[b, ctx_rows, :].astype(BF16))
        qrows = range(qb * NAT_QROWS, (qb + 1) * NAT_QROWS)
        krow0 = _nat_row_start(qrows[0]) // 2 * 2
        krow1 = -(-(_nat_row_start(qrows[-1]) + NAT_ROWS) // 2) * 2
        tok0 = b * SAMPLE_SEQ
        kwin = slice(tok0 + krow0 * GRID_W, tok0 + krow1 * GRID_W)
        qwin = slice(tok0 + qrows[0] * GRID_W, tok0 + (qrows[-1] + 1) * GRID_W)

        def tile_index(qr, kr):
            rs = _nat_row_start(qr)
            return kr - qr + (NAT_ROWS - 1) if rs <= kr < rs + NAT_ROWS else MASKED_TILE

        bias = jnp.concatenate(
            [jnp.concatenate([left_ref[tile_index(qr, kr)] + right_ref[tile_index(qr, kr + 1)]
                              for kr in range(krow0, krow1, 2)], axis=1)
             for qr in qrows], axis=0)

        q = q_ref[qwin, lanes]
        s_nb = _dot_nt(q, k_ref[kwin, lanes]) + bias
        s_ctx = _dot_nt(q, kc)
        o = _softmax_pv([s_nb, s_ctx], [_with_ones(v_ref[kwin, lanes]), vc], None)
        o_ref[qwin, lanes] = (o * z_ref[qwin, lanes].astype(F32)).astype(BF16)


def _attn_nat(rel_bias, qz, k, v, cache_k, cache_v, layer_in_kind, w_out):
    n_heads = qz.shape[1] // (2 * HEAD_DIM)
    bias_rows = jnp.pad(rel_bias, ((0, 0), (0, N_DR + 1 - rel_bias.shape[1]),
                                   (0, 2 * GRID_W - rel_bias.shape[2])))
    latent_tile = PROMPT_TOKENS // SAMPLE_TOKENS
    step_width = NAT_HEADS_PER_STEP * HEAD_DIM
    n_steps = n_heads // NAT_HEADS_PER_STEP
    latent_spec = pl.BlockSpec((SAMPLE_TOKENS, step_width), lambda h: (latent_tile, h))
    z_spec = pl.BlockSpec((SAMPLE_TOKENS, step_width), lambda h: (latent_tile, n_steps + h))
    cache_spec = pl.BlockSpec((N_SAMPLE, None, PAST_LEN * n_heads, HEAD_DIM),
                              lambda h: (0, layer_in_kind, 0, 0))
    w_spec, wbf_spec, wbf_shape = _cast_slab_specs(w_out, layer_in_kind, n_steps, lambda h: h)
    return pl.pallas_call(
        _attn_nat_kernel,
        grid=(n_steps,),
        in_specs=[pl.BlockSpec((NAT_HEADS_PER_STEP, N_DR + 1, 2 * GRID_W), lambda h: (h, 0, 0)),
                  latent_spec, latent_spec, latent_spec, cache_spec, cache_spec, z_spec, w_spec],
        out_specs=[pl.BlockSpec((SAMPLE_TOKENS, step_width), lambda h: (0, h)), wbf_spec],
        out_shape=[jax.ShapeDtypeStruct((SAMPLE_TOKENS, n_heads * HEAD_DIM), BF16), wbf_shape],
        scratch_shapes=[pltpu.VMEM((N_DR + 1, GRID_W, 2 * GRID_W), F32),
                        pltpu.VMEM((N_DR + 1, GRID_W, 2 * GRID_W), F32)],
        compiler_params=_params(1),
        name="attn_nat_latent",
    )(bias_rows, qz, k, v, cache_k, cache_v, qz, w_out)


SPATIAL_TOKENS = 4 * CHUNK


def _spatial_kernel(uz_ref, v_ref, g_ref, b_ref, ws_ref, bs_ref, o_ref):
    for c in range(SPATIAL_TOKENS // CHUNK):
        rows = slice(c * CHUNK, (c + 1) * CHUNK)
        v = v_ref[rows, :].astype(F32)
        mu = jnp.mean(v, axis=-1, keepdims=True)
        vc = v - mu
        var = jnp.mean(vc * vc, axis=-1, keepdims=True)
        vn = (vc * lax.rsqrt(var + EPS) * g_ref[...] + b_ref[...]).astype(BF16)
        for g in range(GMLP_GROUPS):
            sl = slice(g * GMLP_GROUP_WIDTH, (g + 1) * GMLP_GROUP_WIDTH)
            sv = jnp.dot(ws_ref[g], vn[:, sl], preferred_element_type=F32) + bs_ref[:, g:g + 1]
            o_ref[rows, sl] = uz_ref[rows, sl] * sv.astype(BF16)


def _spatial(uz, v, ln_g, ln_b, w_s, b_s):
    row = pl.BlockSpec((SPATIAL_TOKENS, GMLP_WIDTH), lambda i: (i, 0))
    vec = pl.BlockSpec((1, GMLP_WIDTH), lambda i: (0, 0))
    return pl.pallas_call(
        _spatial_kernel,
        grid=(TOKENS // SPATIAL_TOKENS,),
        in_specs=[row, row, vec, vec,
                  pl.BlockSpec((GMLP_GROUPS, CHUNK, CHUNK), lambda i: (0, 0, 0)),
                  pl.BlockSpec((CHUNK, GMLP_GROUPS), lambda i: (0, 0))],
        out_specs=row,
        out_shape=jax.ShapeDtypeStruct((TOKENS, GMLP_WIDTH), BF16),
        compiler_params=_params(1),
        name="gmlp_spatial",
    )(uz, v, ln_g.reshape(1, -1), ln_b.reshape(1, -1), w_s.astype(BF16), b_s.T)


def kernel(x_prompt, x_sample, cache_win_k, cache_win_v, cache_nat_k, cache_nat_v, c, c_ctx,
           norm_g, w_ada, b_ada,
           win_w_in, win_q_norm, win_k_norm, win_sink, win_w_out,
           nat_w_in, nat_q_norm, nat_k_norm, nat_rel_bias, nat_w_out,
           gmlp_w_in, gmlp_ln_g, gmlp_ln_b, gmlp_w_s, gmlp_b_s, gmlp_w_out):
    depth = norm_g.shape[0]
    xp = x_prompt.reshape(PROMPT_TOKENS, D_MODEL)
    xs = x_sample.reshape(SAMPLE_TOKENS, D_MODEL)
    cond = jnp.zeros((COND_ROWS, D_MODEL), F32).at[:N_SAMPLE].set(c).at[CTX_COND_ROW].set(c_ctx)
    mod4 = _ada(cond, w_ada, b_ada).reshape(depth, COND_ROWS, 1, 3 * D_MODEL)
    norm_g3 = norm_g.reshape(depth, 1, D_MODEL)
    rope_tabs = _rope_tables()
    no_sink = jnp.zeros((1,), F32)

    n_win = win_w_in.shape[0]
    n_nat = nat_w_in.shape[0]
    win_kv_heads = cache_win_k.shape[3]
    win_kv_width = win_kv_heads * HEAD_DIM
    cwk = cache_win_k.reshape(N_SAMPLE, n_win, PAST_LEN * win_kv_heads, HEAD_DIM)
    cwv = cache_win_v.reshape(N_SAMPLE, n_win, PAST_LEN * win_kv_heads, HEAD_DIM)
    nat_heads = cache_nat_k.shape[3]
    cnk = cache_nat_k.reshape(N_SAMPLE, n_nat, PAST_LEN * nat_heads, HEAD_DIM)
    cnv = cache_nat_v.reshape(N_SAMPLE, n_nat, PAST_LEN * nat_heads, HEAD_DIM)

    new_win_k, new_win_v, new_nat_k, new_nat_v = [], [], [], []
    x = (xp, xs)
    hn = _norm_mod(xp, xs, norm_g3, mod4, 0)
    for layer in range(depth):
        kind = layer % 3
        li = layer // 3
        if kind == 0:
            nq = D_MODEL // PROJ_TN
            nkv = win_kv_width // PROJ_TN
            qz = _qz_proj(hn, win_w_in, li, 0, nq + 2 * nkv, nq, win_q_norm[li], rope_tabs, "win_qz")
            kp, ks, vp, vs = _kv_proj(hn, win_w_in, li, nq, nkv, win_k_norm[li], rope_tabs, True, "win_kv",
                                      run_after=qz)
            new_win_k.append(kp)
            new_win_v.append(vp)
            sink = win_sink[li] * LOG2E
            a_latent, wbf = _attn_win(sink, qz, ks, vs, cwk, cwv, li, win_kv_heads, win_w_out)
            a = (_attn_prompt(sink, qz, ks, vs, win_kv_heads, True, "attn_win_prompt"), a_latent)
            out_tm, out_name = OUT_TM, "win_out"
        elif kind == 1:
            nq = D_MODEL // PROJ_TN
            qz = _qz_proj(hn, nat_w_in, li, 0, 3 * nq, nq, nat_q_norm[li], None, "nat_qz")
            kp, ks, vp, vs = _kv_proj(hn, nat_w_in, li, nq, nq, nat_k_norm[li], None, False, "nat_kv",
                                      run_after=qz)
            new_nat_k.append(kp)
            new_nat_v.append(vp)
            a_latent, wbf = _attn_nat(nat_rel_bias[li] * LOG2E, qz, ks, vs, cnk, cnv, li, nat_w_out)
            a = (_attn_prompt(no_sink, qz, ks, vs, nat_heads, False, "attn_nat_prompt", run_after=a_latent),
                 a_latent)
            out_tm, out_name = OUT_TM, "nat_out"
        else:
            nw = GMLP_WIDTH // PROJ_TN
            uz, v, wbf = _uzv_proj(hn, gmlp_w_in, li, nw, "gmlp_uzv", gmlp_w_out)
            a = _spatial(uz, v, gmlp_ln_g[li], gmlp_ln_b[li], gmlp_w_s[li], gmlp_b_s[li])
            out_tm, out_name = OUT_TM_WIDE_K, "gmlp_out"
        if layer + 1 < depth:
            x, hn = _out_proj(a, wbf, x, mod4, layer, norm_g3, out_name, out_tm)
        else:
            (yp,) = _out_proj(a, wbf, x, mod4, layer, None, out_name + "_prompt", out_tm,
                              rows=(0, PROMPT_TOKENS))
            (ys,) = _out_proj(a, wbf, x, mod4, layer, None, out_name + "_latent", out_tm,
                              rows=(PROMPT_TOKENS, SAMPLE_TOKENS))

    cache_shape = lambda layers, heads: jnp.stack(
        [c_.reshape(N_PROMPT, PROMPT_SEQ, heads, HEAD_DIM) for c_ in layers], axis=1)
    return (yp.reshape(N_PROMPT, PROMPT_SEQ, D_MODEL), ys.reshape(N_SAMPLE, SAMPLE_SEQ, D_MODEL),
            cache_shape(new_win_k, win_kv_heads), cache_shape(new_win_v, win_kv_heads),
            cache_shape(new_nat_k, nat_heads), cache_shape(new_nat_v, nat_heads))
```

```python
import functools
import math
from typing import Callable, NamedTuple

import jax
import jax.numpy as jnp
import numpy as np
from jax import lax
from jax.experimental import pallas as pl
from jax.experimental.pallas import tpu as pltpu

F32 = jnp.float32
BF16 = jnp.bfloat16

D_MODEL = 2048
HEAD_DIM = 128
N_PROMPT = 16
PROMPT_SEQ = 256
N_SAMPLE = 2
SAMPLE_SEQ = 1024
PROMPT_TOKENS = N_PROMPT * PROMPT_SEQ
SAMPLE_TOKENS = N_SAMPLE * SAMPLE_SEQ
TOKENS = PROMPT_TOKENS + SAMPLE_TOKENS
PAST_LEN = 512
GRID_W = 64
EPS = 1e-6
NEG_INF = -1e30
ROPE_THETA = 10000.0
WINDOW = 128
NAT_ROWS = 8
NAT_COLS = 16
GMLP_WIDTH = 2 * D_MODEL
GMLP_GROUPS = 16
GMLP_GROUP_WIDTH = GMLP_WIDTH // GMLP_GROUPS
CHUNK = 128
CTX_COND_ROW = 2
COND_ROWS = 8
SM_SCALE = HEAD_DIM ** -0.5
LOG2E = math.log2(math.e)
QUERY_SCALE = SM_SCALE * LOG2E

VMEM_LIMIT = 56 * 1024 * 1024


def _params(n_axes):
    return pltpu.CompilerParams(dimension_semantics=("arbitrary",) * n_axes,
                                vmem_limit_bytes=VMEM_LIMIT)


def _cond_row(tok0):
    return jnp.where(tok0 < PROMPT_TOKENS, CTX_COND_ROW, (tok0 - PROMPT_TOKENS) // SAMPLE_SEQ)


def _silu(x):
    return x * (0.5 + 0.5 * jnp.tanh(0.5 * x))


def _gelu_tanh(x):
    return 0.5 * x * (1.0 + jnp.tanh(math.sqrt(2.0 / math.pi) * (x + 0.044715 * (x * x * x))))


def _split_specs(tm, width, tile0=0):
    n_p = PROMPT_TOKENS // tm
    return [pl.BlockSpec((tm, width), lambda i: (jnp.minimum(i + tile0, n_p - 1), 0)),
            pl.BlockSpec((tm, width), lambda i: (jnp.maximum(i + tile0 - n_p, 0), 0))]


def _ada_kernel(cond_ref, w_ref, b_ref, o_ref):
    s = _silu(cond_ref[...]).astype(BF16)
    o_ref[...] = jnp.dot(s, w_ref[...].astype(BF16), preferred_element_type=F32) + b_ref[...]


def _ada(cond, w_ada, b_ada, tn=1024):
    depth = w_ada.shape[0]
    n = w_ada.shape[2]
    return pl.pallas_call(
        _ada_kernel,
        grid=(depth, n // tn),
        in_specs=[pl.BlockSpec((COND_ROWS, D_MODEL), lambda l, j: (0, 0)),
                  pl.BlockSpec((None, D_MODEL, tn), lambda l, j: (l, 0, j)),
                  pl.BlockSpec((None, 1, tn), lambda l, j: (l, 0, j))],
        out_specs=pl.BlockSpec((None, COND_ROWS, tn), lambda l, j: (l, 0, j)),
        out_shape=jax.ShapeDtypeStruct((depth, COND_ROWS, n), F32),
        compiler_params=_params(2),
        name="ada_mod",
    )(cond, w_ada, b_ada.reshape(depth, 1, n))


NORM_ROWS = 64


def _norm_kernel(xp_ref, xs_ref, g_ref, shift_ref, scale_ref, o_ref, *, tm):
    gmul = g_ref[...] * (1.0 + scale_ref[...])
    shift = shift_ref[...]

    def run(x_ref):
        for r in range(tm // NORM_ROWS):
            rows = slice(r * NORM_ROWS, (r + 1) * NORM_ROWS)
            x = x_ref[rows, :]
            rs = lax.rsqrt(jnp.mean(x * x, axis=-1, keepdims=True) + EPS)
            o_ref[rows, :] = (x * rs * gmul + shift).astype(BF16)

    is_prompt = pl.program_id(0) < PROMPT_TOKENS // tm
    pl.when(is_prompt)(lambda: run(xp_ref))
    pl.when(jnp.logical_not(is_prompt))(lambda: run(xs_ref))


def _norm_mod(xp, xs, norm_g3, mod4, layer, tm=1024):
    return pl.pallas_call(
        functools.partial(_norm_kernel, tm=tm),
        grid=(TOKENS // tm,),
        in_specs=[*_split_specs(tm, D_MODEL),
                  pl.BlockSpec((None, 1, D_MODEL), lambda i: (layer, 0, 0)),
                  pl.BlockSpec((None, None, 1, D_MODEL), lambda i: (layer, _cond_row(i * tm), 0, 0)),
                  pl.BlockSpec((None, None, 1, D_MODEL), lambda i: (layer, _cond_row(i * tm), 0, 1))],
        out_specs=pl.BlockSpec((tm, D_MODEL), lambda i: (i, 0)),
        out_shape=jax.ShapeDtypeStruct((TOKENS, D_MODEL), BF16),
        compiler_params=_params(1),
        name="norm_mod",
    )(xp, xs, norm_g3, mod4, mod4)


PROJ_TM = 2048
PROJ_TN = 512
PROJ_TN_WIDE = 1024
PROJ_ROWS = 512
N_PROMPT_TILES = PROMPT_TOKENS // PROJ_TM


class _Segment(NamedTuple):
    col_blk0: int
    n_blk: int
    epilogue: Callable
    by_tile_kind: bool
    outs: tuple
    pair_blk0: int = -1


def _segment_steps(segments, wide):
    spans, j0 = [], 0
    for seg in segments:
        if seg.pair_blk0 >= 0:
            assert wide == 2
            nj = seg.n_blk
        else:
            assert seg.col_blk0 % wide == 0 and seg.n_blk % wide == 0
            nj = seg.n_blk // wide
        spans.append((j0, nj))
        j0 += nj
    return spans


def _proj_kernel(*refs, segments, spans, n_w, n_extra, n_out, cast_steps):
    a_ref, w_refs = refs[0], refs[1:1 + n_w]
    extras = refs[1 + n_w:1 + n_w + n_extra]
    outs = refs[1 + n_w + n_extra:1 + n_w + n_extra + n_out]
    wbf_ref = refs[1 + n_w + n_extra + n_out]

    @pl.when(pl.program_id(1) == 0)
    def _():
        for h, w_ref in enumerate(w_refs):
            wbf_ref[:, h * PROJ_TN:(h + 1) * PROJ_TN] = w_ref[...].astype(BF16)

    if cast_steps:
        step = pl.program_id(0) * pl.num_programs(1) + pl.program_id(1)

        @pl.when(step < cast_steps)
        def _():
            outs[-1][...] = extras[-1][...].astype(BF16)

    def run(seg):
        seg_outs = [outs[k] for k in seg.outs]

        def body(latent):
            for rc in range(PROJ_TM // PROJ_ROWS):
                rows = slice(rc * PROJ_ROWS, (rc + 1) * PROJ_ROWS)
                accs = [jnp.dot(a_ref[rows, :], wbf_ref[:, col0:col0 + PROJ_TN], preferred_element_type=F32)
                        for col0 in range(0, wbf_ref.shape[1], PROJ_TN)]
                if seg.pair_blk0 >= 0:
                    seg.epilogue(*accs, extras, seg_outs, rc, latent)
                else:
                    for c, acc in enumerate(accs):
                        seg.epilogue(acc, extras, seg_outs, rc, c * PROJ_TN, latent)

        if seg.by_tile_kind:
            is_latent = pl.program_id(1) >= N_PROMPT_TILES
            pl.when(is_latent)(lambda: body(True))
            pl.when(jnp.logical_not(is_latent))(lambda: body(False))
        else:
            body(None)

    j = pl.program_id(0)
    for seg, (j0, nj) in zip(segments, spans):
        if len(segments) == 1:
            run(seg)
        else:
            pl.when(jnp.logical_and(j >= j0, j < j0 + nj))(functools.partial(run, seg))


def _proj(a, w, layer, segments, out_shapes, out_specs, name, extras=(), extra_specs=(), tn=PROJ_TN,
          run_after=None, cast=None):
    m, k = a.shape
    if run_after is not None:
        extras = [*extras, run_after]
        extra_specs = [*extra_specs, pl.BlockSpec(memory_space=pl.ANY)]
    wide = tn // PROJ_TN
    spans = _segment_steps(segments, wide)
    cast_steps = 0
    if cast is not None:
        w_out, out_layer, cast_steps = cast
        n_tiles = m // PROJ_TM
        assert cast_steps <= n_tiles * sum(nj for _, nj in spans)
        w_spec, wbf_spec, wbf_shape = _cast_slab_specs(
            w_out, out_layer, cast_steps, lambda j, i: jnp.minimum(j * n_tiles + i, cast_steps - 1))
        extras, extra_specs = [*extras, w_out], [*extra_specs, w_spec]
        out_shapes, out_specs = [*out_shapes, wbf_shape], [*out_specs, wbf_spec]

    def w_col(j, half):
        col = None
        for seg, (j0, _) in zip(segments, spans):
            if seg.pair_blk0 >= 0:
                c = (seg.pair_blk0 if half else seg.col_blk0) + j - j0
            else:
                c = seg.col_blk0 + wide * (j - j0) + half
            col = c if col is None else jnp.where(j >= j0, c, col)
        return col

    w_specs = [pl.BlockSpec((None, k, PROJ_TN), lambda j, i, half=half: (layer, 0, w_col(j, half)))
               for half in range(wide)]
    kern = functools.partial(_proj_kernel, segments=segments, spans=spans, n_w=wide, n_extra=len(extras),
                             n_out=len(out_shapes), cast_steps=cast_steps)
    return pl.pallas_call(
        kern,
        grid=(sum(nj for _, nj in spans), m // PROJ_TM),
        in_specs=[pl.BlockSpec((PROJ_TM, k), lambda j, i: (i, 0)), *w_specs, *extra_specs],
        out_specs=out_specs,
        out_shape=out_shapes,
        scratch_shapes=[pltpu.VMEM((k, tn), BF16)],
        compiler_params=_params(2),
        name=name,
    )(a, *[w] * wide, *extras)


def _full_out(n_blk):
    return ([jax.ShapeDtypeStruct((TOKENS, n_blk * PROJ_TN), BF16)],
            [pl.BlockSpec((PROJ_TM, PROJ_TN_WIDE), lambda j, i: (i, j))])


def _held(span, j, inside, before, after):
    j0, nj = span
    pick = lambda a, b, c: jnp.where(j < j0, b, jnp.where(j >= j0 + nj, c, a))
    return tuple(pick(a, b, c) for a, b, c in zip(inside, before, after))


def _split_out(span, n_blk, head_rows):
    prompt_tile = lambda i: jnp.minimum(i, N_PROMPT_TILES - 1)
    last_prompt, last_tile = N_PROMPT_TILES - 1, TOKENS // PROJ_TM - 1
    jj = lambda j: j - span[0]
    if head_rows:
        assert n_blk == 1
        heads = PROJ_TN // HEAD_DIM
        cache_shape = jax.ShapeDtypeStruct((PROMPT_TOKENS * heads, HEAD_DIM), F32)
        cache_spec = pl.BlockSpec(
            (PROJ_TM * heads, HEAD_DIM),
            lambda j, i: _held(span, j, (prompt_tile(i), 0), (0, 0), (last_prompt, 0)))
    else:
        cache_shape = jax.ShapeDtypeStruct((N_PROMPT, PROMPT_SEQ, n_blk * PROJ_TN), F32)
        cache_spec = pl.BlockSpec(
            (PROJ_TM // PROMPT_SEQ, PROMPT_SEQ, PROJ_TN),
            lambda j, i: _held(span, j, (prompt_tile(i), 0, jj(j)), (0, 0, 0), (last_prompt, 0, n_blk - 1)))
    shapes = [cache_shape, jax.ShapeDtypeStruct((TOKENS, n_blk * PROJ_TN), BF16)]
    specs = [cache_spec,
             pl.BlockSpec((PROJ_TM, PROJ_TN),
                          lambda j, i: _held(span, j, (i, jj(j)), (0, 0), (last_tile, n_blk - 1)))]
    return shapes, specs


def _chunk_rows(rc):
    return slice(rc * PROJ_ROWS, (rc + 1) * PROJ_ROWS)


def _epi_silu(acc, extras, outs, rc, col0, latent):
    outs[0][_chunk_rows(rc), col0:col0 + PROJ_TN] = _silu(acc).astype(BF16)


def _epi_gelu(acc, extras, outs, rc, col0, latent):
    outs[0][_chunk_rows(rc), col0:col0 + PROJ_TN] = _gelu_tanh(acc).astype(BF16)


def _store_split(y, outs, rc, sl, latent):
    outs[1][_chunk_rows(rc), sl] = y.astype(BF16)
    if latent:
        return
    if len(outs[0].shape) == 2:
        heads = PROJ_TN // HEAD_DIM
        col0 = sl.start or 0
        for c in range(y.shape[1] // HEAD_DIM):
            head = col0 // HEAD_DIM + c
            rows = pl.ds(rc * PROJ_ROWS * heads + head, PROJ_ROWS, stride=heads)
            outs[0][rows, :] = y[:, c * HEAD_DIM:(c + 1) * HEAD_DIM]
    else:
        seqs = PROJ_ROWS // PROMPT_SEQ
        outs[0][rc * seqs:(rc + 1) * seqs, :, sl] = y.reshape(seqs, PROMPT_SEQ, y.shape[1])


def _epi_value(acc, extras, outs, rc, col0, latent):
    _store_split(acc, outs, rc, slice(col0, col0 + PROJ_TN), latent)


def _head_rmsnorm(acc, hh, g):
    a = acc[:, hh * HEAD_DIM:(hh + 1) * HEAD_DIM]
    return a * lax.rsqrt(jnp.mean(a * a, axis=-1, keepdims=True) + EPS) * g


def _normed_heads(acc, extras, rc, rope):
    g = extras[0][...]
    ys = [_head_rmsnorm(acc, hh, g) for hh in range(PROJ_TN // HEAD_DIM)]
    if rope:
        rows = _chunk_rows(rc)
        cos, sin, swap = extras[1][rows, :], extras[2][rows, :], extras[3][...]
        for pair in range(len(ys) // 2):
            both = jnp.concatenate(ys[2 * pair:2 * pair + 2], axis=1).astype(BF16)
            partner = jnp.dot(both, swap, preferred_element_type=F32)
            for t in range(2):
                hh = 2 * pair + t
                ys[hh] = ys[hh] * cos + partner[:, t * HEAD_DIM:(t + 1) * HEAD_DIM] * sin
    for hh, y in enumerate(ys):
        yield slice(hh * HEAD_DIM, (hh + 1) * HEAD_DIM), y


def _epi_query(acc, extras, outs, rc, col0, latent, *, rope):
    for sl, y in _normed_heads(acc, extras, rc, rope and latent):
        outs[0][_chunk_rows(rc), col0 + sl.start:col0 + sl.stop] = (y * QUERY_SCALE).astype(BF16)


def _epi_key(acc, extras, outs, rc, col0, latent, *, rope):
    for sl, y in _normed_heads(acc, extras, rc, rope and latent):
        _store_split(y, outs, rc, slice(col0 + sl.start, col0 + sl.stop), latent)


def _rope_tables():
    nf = HEAD_DIM // 4
    t = np.arange(SAMPLE_TOKENS) % SAMPLE_SEQ
    row = (t // GRID_W).astype(np.float32)
    col = (t % GRID_W).astype(np.float32)
    inv = np.float32(ROPE_THETA) ** (-np.arange(nf, dtype=np.float32) / np.float32(nf))
    ang_r = row[:, None] * inv
    ang_c = col[:, None] * inv
    cos = np.concatenate([np.cos(ang_r), np.cos(ang_r), np.cos(ang_c), np.cos(ang_c)], axis=1)
    sin = np.concatenate([-np.sin(ang_r), np.sin(ang_r), -np.sin(ang_c), np.sin(ang_c)], axis=1)
    lanes = np.arange(2 * HEAD_DIM)
    swap = lanes[:, None] == (lanes[None, :] ^ (HEAD_DIM // 4))
    return jnp.asarray(cos, F32), jnp.asarray(sin, F32), jnp.asarray(swap, BF16)


def _gain_extras(gain, rope_tabs):
    extras = [gain.reshape(1, HEAD_DIM)]
    specs = [pl.BlockSpec((1, HEAD_DIM), lambda j, i: (0, 0))]
    if rope_tabs is not None:
        tab_map = lambda j, i: (jnp.maximum(i - N_PROMPT_TILES, 0), 0)
        extras += list(rope_tabs)
        specs += [pl.BlockSpec((PROJ_TM, HEAD_DIM), tab_map)] * 2
        specs.append(pl.BlockSpec((2 * HEAD_DIM, 2 * HEAD_DIM), lambda j, i: (0, 0)))
    return extras, specs


def _qz_proj(hn, w, layer, q_blk0, z_blk0, n_blk, gain, rope_tabs, name):
    extras, specs = _gain_extras(gain, rope_tabs)
    rope = rope_tabs is not None
    segments = (_Segment(q_blk0, n_blk, functools.partial(_epi_query, rope=rope), rope, (0,)),
                _Segment(z_blk0, n_blk, _epi_silu, False, (0,)))
    return _proj(hn, w, layer, segments, *_full_out(2 * n_blk), name, extras, specs, tn=PROJ_TN_WIDE)[0]


def _kv_proj(hn, w, layer, k_blk0, n_blk, gain, rope_tabs, head_rows, name, run_after=None):
    extras, specs = _gain_extras(gain, rope_tabs)
    segments = (_Segment(k_blk0, n_blk, functools.partial(_epi_key, rope=rope_tabs is not None), True, (0, 1)),
                _Segment(k_blk0 + n_blk, n_blk, _epi_value, True, (2, 3)))
    shapes, out_specs = [], []
    for span in _segment_steps(segments, 1):
        seg_shapes, seg_specs = _split_out(span, n_blk, head_rows)
        shapes += seg_shapes
        out_specs += seg_specs
    return _proj(hn, w, layer, segments, shapes, out_specs, name, extras, specs, run_after=run_after)


UVZ_CAST_STEPS = 32


def _epi_uz(acc_u, acc_z, extras, outs, rc, latent):
    outs[0][_chunk_rows(rc), :] = (_gelu_tanh(acc_u) * _silu(acc_z)).astype(BF16)


def _uzv_proj(hn, w, layer, n_blk, name, w_out):
    segments = (_Segment(0, n_blk, _epi_uz, False, (0,), pair_blk0=2 * n_blk),
                _Segment(n_blk, n_blk, _epi_gelu, False, (1,)))
    (uz_span, v_span) = _segment_steps(segments, PROJ_TN_WIDE // PROJ_TN)
    last_tile = TOKENS // PROJ_TM - 1
    shape = jax.ShapeDtypeStruct((TOKENS, n_blk * PROJ_TN), BF16)
    out_specs = [
        pl.BlockSpec((PROJ_TM, PROJ_TN),
                     lambda j, i: _held(uz_span, j, (i, j - uz_span[0]), (0, 0), (last_tile, uz_span[1] - 1))),
        pl.BlockSpec((PROJ_TM, PROJ_TN_WIDE),
                     lambda j, i: _held(v_span, j, (i, j - v_span[0]), (0, 0), (last_tile, v_span[1] - 1)))]
    return _proj(hn, w, layer, segments, [shape, shape], out_specs, name, tn=PROJ_TN_WIDE,
                 cast=(w_out, layer, UVZ_CAST_STEPS))


OUT_COLS = 512
OUT_TM = 512
OUT_TM_WIDE_K = 512


def _cast_slab_specs(w, layer, n_slabs, slab_of_step):
    k, n = w.shape[1], w.shape[2]
    rows = k // n_slabs
    return (pl.BlockSpec((None, rows, n), lambda *g: (layer, slab_of_step(*g), 0)),
            pl.BlockSpec((rows, n), lambda *g: (slab_of_step(*g), 0)),
            jax.ShapeDtypeStruct((k, n), BF16))


def _out_kernel(*refs, tile0, tm, split_a, split_x, fuse_norm):
    it = iter(refs)
    a_refs = [next(it), next(it)] if split_a else [next(it)]
    wbf_ref = next(it)
    x_refs = [next(it), next(it)] if split_x else [next(it)]
    gate_ref = next(it)
    if fuse_norm:
        g_ref, shift_ref, scale_ref = next(it), next(it), next(it)
    xnew_ref = next(it)
    hn_ref = next(it) if fuse_norm else None

    def body(a_ref, x_ref):
        ssq = jnp.zeros((tm, 1), F32)
        for cb in range(D_MODEL // OUT_COLS):
            sl = slice(cb * OUT_COLS, (cb + 1) * OUT_COLS)
            acc = jnp.dot(a_ref[...], wbf_ref[:, sl], preferred_element_type=F32)
            xn = x_ref[:, sl] + gate_ref[:, sl] * acc
            xnew_ref[:, sl] = xn
            if fuse_norm:
                ssq = ssq + jnp.sum(xn * xn, axis=-1, keepdims=True)
        if fuse_norm:
            rs = lax.rsqrt(ssq * (1.0 / D_MODEL) + EPS)
            for cb in range(D_MODEL // OUT_COLS):
                sl = slice(cb * OUT_COLS, (cb + 1) * OUT_COLS)
                gmul = g_ref[:, sl] * (1.0 + scale_ref[:, sl])
                hn_ref[:, sl] = (xnew_ref[:, sl] * rs * gmul + shift_ref[:, sl]).astype(BF16)

    if split_a or split_x:
        is_prompt = pl.program_id(0) + tile0 < PROMPT_TOKENS // tm
        pl.when(is_prompt)(lambda: body(a_refs[0], x_refs[0]))
        pl.when(jnp.logical_not(is_prompt))(lambda: body(a_refs[-1], x_refs[-1]))
    else:
        body(a_refs[0], x_refs[0])


def _out_proj(a, wbf, x, mod4, layer, norm_g3, name, tm, rows=None):
    split_a = isinstance(a, tuple)
    k = a[0].shape[1] if split_a else a.shape[1]
    tok0, n_tok = rows if rows is not None else (0, TOKENS)
    tile0 = tok0 // tm
    split_x = isinstance(x, tuple)
    fuse_norm = norm_g3 is not None
    mod_spec = lambda part, lyr: pl.BlockSpec(
        (None, None, 1, D_MODEL), lambda i: (lyr, _cond_row((i + tile0) * tm), 0, part))
    row_spec = lambda width: pl.BlockSpec((tm, width), lambda i: (i + tile0, 0))
    operands = [*a, wbf] if split_a else [a, wbf]
    in_specs = [*(_split_specs(tm, k, tile0) if split_a else [row_spec(k)]),
                pl.BlockSpec((k, D_MODEL), lambda i: (0, 0), pipeline_mode=pl.Buffered(1))]
    if split_x:
        assert rows is None
        operands += list(x)
        in_specs += _split_specs(tm, D_MODEL)
    else:
        operands.append(x)
        in_specs.append(row_spec(D_MODEL))
    operands.append(mod4)
    in_specs.append(mod_spec(2, layer))
    out_shapes = [jax.ShapeDtypeStruct((n_tok, D_MODEL), F32)]
    out_specs = [pl.BlockSpec((tm, D_MODEL), lambda i: (i, 0))]
    if fuse_norm:
        operands += [norm_g3, mod4, mod4]
        in_specs += [pl.BlockSpec((None, 1, D_MODEL), lambda i: (layer + 1, 0, 0)),
                     mod_spec(0, layer + 1), mod_spec(1, layer + 1)]
        out_shapes.append(jax.ShapeDtypeStruct((n_tok, D_MODEL), BF16))
        out_specs.append(pl.BlockSpec((tm, D_MODEL), lambda i: (i, 0)))
    kern = functools.partial(_out_kernel, tile0=tile0, tm=tm,
                             split_a=split_a, split_x=split_x, fuse_norm=fuse_norm)
    return pl.pallas_call(
        kern,
        grid=(n_tok // tm,),
        in_specs=in_specs,
        out_specs=out_specs,
        out_shape=out_shapes,
        compiler_params=_params(1),
        name=name,
    )(*operands)


def _dot_nt(a, b):
    return lax.dot_general(a, b, (((1,), (1,)), ((), ())), preferred_element_type=F32)


def _head(ref, h, rows=slice(None)):
    return ref[rows, h * HEAD_DIM:(h + 1) * HEAD_DIM]


def _with_ones(v):
    return jnp.concatenate([v, jnp.ones(v.shape, v.dtype)], axis=1)


def _lane_chunks(s):
    return [s[:, c * HEAD_DIM:(c + 1) * HEAD_DIM] for c in range(s.shape[1] // HEAD_DIM)]


def _softmax_pv(score_blocks, value_blocks, sink):
    rows = score_blocks[0].shape[0]
    mx = functools.reduce(jnp.maximum, [c for s in score_blocks for c in _lane_chunks(s)])
    m = jnp.broadcast_to(jnp.max(mx, axis=-1, keepdims=True), (rows, HEAD_DIM))
    if sink is not None:
        m = jnp.maximum(m, sink)
    o = None
    for s, v in zip(score_blocks, value_blocks):
        p = jnp.concatenate([jnp.exp2(c - m) for c in _lane_chunks(s)], axis=1).astype(BF16)
        part = jnp.dot(p, v, preferred_element_type=F32)
        o = part if o is None else o + part
    den = o[:, HEAD_DIM:]
    if sink is not None:
        den = den + jnp.exp2(sink - m)
    return o[:, :HEAD_DIM] * (1.0 / den)


PROMPT_REQUESTS = 2


def _attn_prompt_kernel(sink_ref, q_ref, k_ref, v_ref, z_ref, *rest, n_heads, n_kv, use_sink):
    o_ref = rest[-1]
    grp = n_heads // n_kv
    for r in range(PROMPT_REQUESTS):
        rows = slice(r * PROMPT_SEQ, (r + 1) * PROMPT_SEQ)
        for g in range(n_kv):
            kg = _head(k_ref, g, rows)
            vg = _with_ones(_head(v_ref, g, rows))
            for h in range(g * grp, (g + 1) * grp):
                s = _dot_nt(_head(q_ref, h, rows), kg)
                sink = jnp.full((PROMPT_SEQ, HEAD_DIM), sink_ref[h], F32) if use_sink else None
                o = _softmax_pv([s], [vg], sink)
                o_ref[rows, h * HEAD_DIM:(h + 1) * HEAD_DIM] = (
                    o * _head(z_ref, h, rows).astype(F32)).astype(BF16)


def _attn_prompt(sink, qz, k, v, n_kv, use_sink, name, run_after=None):
    ordering = [] if run_after is None else [run_after]
    width = qz.shape[1] // 2
    kern = functools.partial(_attn_prompt_kernel, n_heads=width // HEAD_DIM, n_kv=n_kv, use_sink=use_sink)
    rows = PROMPT_REQUESTS * PROMPT_SEQ
    cache_spec = pl.BlockSpec((rows, k.shape[1]), lambda b: (b, 0))
    return pl.pallas_call(
        kern,
        grid=(N_PROMPT // PROMPT_REQUESTS,),
        in_specs=[pl.BlockSpec(memory_space=pltpu.SMEM),
                  pl.BlockSpec((rows, width), lambda b: (b, 0)),
                  cache_spec, cache_spec,
                  pl.BlockSpec((rows, width), lambda b: (b, 1)),
                  *[pl.BlockSpec(memory_space=pl.ANY) for _ in ordering]],
        out_specs=pl.BlockSpec((rows, width), lambda b: (b, 0)),
        out_shape=jax.ShapeDtypeStruct((PROMPT_TOKENS, width), BF16),
        compiler_params=_params(1),
        name=name,
    )(sink, qz, k, v, qz, *ordering)


WIN_BAND = 3 * WINDOW
WIN_BLOCKS = 2


def _attn_win_kernel(sink_ref, q_ref, k_ref, v_ref, kc_ref, vc_ref, z_ref, w_ref, o_ref, wbf_ref, *,
                     n_heads, n_kv):
    wbf_ref[...] = w_ref[...].astype(BF16)
    grp = n_heads // n_kv
    shape = (grp * WINDOW, WIN_BAND)
    row_in_block = lax.broadcasted_iota(jnp.int32, shape, 0) & (WINDOW - 1)
    col = lax.broadcasted_iota(jnp.int32, shape, 1)
    for g in range(n_kv):
        heads = [g * grp + t for t in range(grp)]
        ctx_rows = pl.ds(g, PAST_LEN, stride=n_kv)
        kc = kc_ref[ctx_rows, :].astype(BF16)
        vc = _with_ones(vc_ref[ctx_rows, :].astype(BF16))
        sink = jnp.concatenate([jnp.full((WINDOW, HEAD_DIM), sink_ref[h], F32) for h in heads], axis=0)
        for blk in range(WIN_BLOCKS):
            n = pl.program_id(1) * WIN_BLOCKS + blk
            rows = slice(blk * WINDOW, (blk + 1) * WINDOW)
            start = pl.multiple_of(jnp.clip((n - 1) * WINDOW, 0, SAMPLE_SEQ - WIN_BAND), WINDOW)
            valid = jnp.abs(start + col - (n * WINDOW + row_in_block)) <= WINDOW
            band = pl.ds(start, WIN_BAND)
            qs = jnp.concatenate([_head(q_ref, h, rows) for h in heads], axis=0)
            kb = _head(k_ref, g, band)
            vb = _with_ones(_head(v_ref, g, band))
            s_band = jnp.where(valid, _dot_nt(qs, kb), NEG_INF)
            s_ctx = _dot_nt(qs, kc)
            o = _softmax_pv([s_band, s_ctx], [vb, vc], sink)
            for t, h in enumerate(heads):
                oh = o[t * WINDOW:(t + 1) * WINDOW]
                o_ref[rows, h * HEAD_DIM:(h + 1) * HEAD_DIM] = (
                    oh * _head(z_ref, h, rows).astype(F32)).astype(BF16)


def _attn_win(sink, qz, k, v, cache_k, cache_v, layer_in_kind, n_kv, w_out):
    width = qz.shape[1] // 2
    kv_width = k.shape[1]
    q_rows = WIN_BLOCKS * WINDOW
    blocks_per_seq = SAMPLE_SEQ // q_rows
    q_map = lambda b, n: (PROMPT_TOKENS // q_rows + b * blocks_per_seq + n, 0)
    z_map = lambda b, n: (PROMPT_TOKENS // q_rows + b * blocks_per_seq + n, 1)
    kv_map = lambda b, n: (PROMPT_TOKENS // SAMPLE_SEQ + b, 0)
    cache_map = lambda b, n: (b, layer_in_kind, 0, 0)
    kern = functools.partial(_attn_win_kernel, n_heads=width // HEAD_DIM, n_kv=n_kv)
    w_spec, wbf_spec, wbf_shape = _cast_slab_specs(w_out, layer_in_kind, N_SAMPLE * blocks_per_seq,
                                                   lambda b, n: b * blocks_per_seq + n)
    return pl.pallas_call(
        kern,
        grid=(N_SAMPLE, blocks_per_seq),
        in_specs=[pl.BlockSpec(memory_space=pltpu.SMEM),
                  pl.BlockSpec((q_rows, width), q_map),
                  pl.BlockSpec((SAMPLE_SEQ, kv_width), kv_map),
                  pl.BlockSpec((SAMPLE_SEQ, kv_width), kv_map),
                  pl.BlockSpec((None, None, PAST_LEN * n_kv, HEAD_DIM), cache_map),
                  pl.BlockSpec((None, None, PAST_LEN * n_kv, HEAD_DIM), cache_map),
                  pl.BlockSpec((q_rows, width), z_map),
                  w_spec],
        out_specs=[pl.BlockSpec((q_rows, width), lambda b, n: (b * blocks_per_seq + n, 0)), wbf_spec],
        out_shape=[jax.ShapeDtypeStruct((SAMPLE_TOKENS, width), BF16), wbf_shape],
        compiler_params=_params(2),
        name="attn_win_latent",
    )(sink, qz, k, v, cache_k, cache_v, qz, w_out)


NAT_QROWS = 4
NAT_HEADS_PER_STEP = 2
GRID_ROWS = SAMPLE_SEQ // GRID_W
N_DR = 2 * NAT_ROWS - 1
MASKED_TILE = N_DR


def _nat_row_start(qr):
    return min(max(qr - NAT_ROWS // 2, 0), GRID_ROWS - NAT_ROWS)


def _attn_nat_kernel(bias_ref, q_ref, k_ref, v_ref, kc_ref, vc_ref, z_ref, w_ref, o_ref, wbf_ref,
                     left_ref, right_ref):
    wbf_ref[...] = w_ref[...].astype(BF16)
    n_heads = kc_ref.shape[1] // PAST_LEN
    for hl in range(NAT_HEADS_PER_STEP):
        _attn_nat_head(bias_ref.at[hl], q_ref, k_ref, v_ref, kc_ref, vc_ref, z_ref, o_ref, left_ref, right_ref,
                       lanes=slice(hl * HEAD_DIM, (hl + 1) * HEAD_DIM),
                       head=pl.program_id(0) * NAT_HEADS_PER_STEP + hl, n_heads=n_heads)


def _attn_nat_head(bias_ref, q_ref, k_ref, v_ref, kc_ref, vc_ref, z_ref, o_ref, left_ref, right_ref, *,
                   lanes, head, n_heads):
    def build_bias_tiles():
        shape = (GRID_W, 2 * GRID_W)
        lane = lax.broadcasted_iota(jnp.int32, shape, 1)
        qc = lax.broadcasted_iota(jnp.int32, shape, 0)
        kc_ = lane & (GRID_W - 1)
        cs = jnp.clip(qc - NAT_COLS // 2, 0, GRID_W - NAT_COLS)
        col_ok = jnp.logical_and(kc_ >= cs, kc_ < cs + NAT_COLS)
        is_left = lane < GRID_W
        lanes = 2 * GRID_W
        for dri in range(N_DR):
            row = jnp.broadcast_to(bias_ref[dri:dri + 1, :], shape)
            on_left = pltpu.roll(row, lanes - (NAT_COLS - 1), 1, stride=1, stride_axis=0)
            on_right = pltpu.roll(row, GRID_W - (NAT_COLS - 1), 1, stride=1, stride_axis=0)
            left_ref[dri] = jnp.where(is_left, jnp.where(col_ok, on_left, NEG_INF), 0.0)
            right_ref[dri] = jnp.where(is_left, 0.0, jnp.where(col_ok, on_right, NEG_INF))
        left_ref[MASKED_TILE] = jnp.where(is_left, NEG_INF, 0.0)
        right_ref[MASKED_TILE] = jnp.where(is_left, 0.0, NEG_INF)

    build_bias_tiles()
    ctx_rows = pl.ds(head, PAST_LEN, stride=n_heads)
    for b, qb in [(b, qb) for b in range(N_SAMPLE) for qb in range(GRID_ROWS // NAT_QROWS)]:
        if qb == 0:
            kc = kc_ref[b, ctx_rows, :].astype(BF16)
            vc = _with_ones(vc_ref[b, ctx_rows, :].astype(BF16))
        qrows = range(qb * NAT_QROWS, (qb + 1) * NAT_QROWS)
        krow0 = _nat_row_start(qrows[0]) // 2 * 2
        krow1 = -(-(_nat_row_start(qrows[-1]) + NAT_ROWS) // 2) * 2
        tok0 = b * SAMPLE_SEQ
        kwin = slice(tok0 + krow0 * GRID_W, tok0 + krow1 * GRID_W)
        qwin = slice(tok0 + qrows[0] * GRID_W, tok0 + (qrows[-1] + 1) * GRID_W)

        def tile_index(qr, kr):
            rs = _nat_row_start(qr)
            return kr - qr + (NAT_ROWS - 1) if rs <= kr < rs + NAT_ROWS else MASKED_TILE

        bias = jnp.concatenate(
            [jnp.concatenate([left_ref[tile_index(qr, kr)] + right_ref[tile_index(qr, kr + 1)]
                              for kr in range(krow0, krow1, 2)], axis=1)
             for qr in qrows], axis=0)

        q = q_ref[qwin, lanes]
        s_nb = _dot_nt(q, k_ref[kwin, lanes]) + bias
        s_ctx = _dot_nt(q, kc)
        o = _softmax_pv([s_nb, s_ctx], [_with_ones(v_ref[kwin, lanes]), vc], None)
        o_ref[qwin, lanes] = (o * z_ref[qwin, lanes].astype(F32)).astype(BF16)


def _attn_nat(rel_bias, qz, k, v, cache_k, cache_v, layer_in_kind, w_out):
    n_heads = qz.shape[1] // (2 * HEAD_DIM)
    bias_rows = jnp.pad(rel_bias, ((0, 0), (0, N_DR + 1 - rel_bias.shape[1]),
                                   (0, 2 * GRID_W - rel_bias.shape[2])))
    latent_tile = PROMPT_TOKENS // SAMPLE_TOKENS
    step_width = NAT_HEADS_PER_STEP * HEAD_DIM
    n_steps = n_heads // NAT_HEADS_PER_STEP
    latent_spec = pl.BlockSpec((SAMPLE_TOKENS, step_width), lambda h: (latent_tile, h))
    z_spec = pl.BlockSpec((SAMPLE_TOKENS, step_width), lambda h: (latent_tile, n_steps + h))
    cache_spec = pl.BlockSpec((N_SAMPLE, None, PAST_LEN * n_heads, HEAD_DIM),
                              lambda h: (0, layer_in_kind, 0, 0))
    w_spec, wbf_spec, wbf_shape = _cast_slab_specs(w_out, layer_in_kind, n_steps, lambda h: h)
    return pl.pallas_call(
        _attn_nat_kernel,
        grid=(n_steps,),
        in_specs=[pl.BlockSpec((NAT_HEADS_PER_STEP, N_DR + 1, 2 * GRID_W), lambda h: (h, 0, 0)),
                  latent_spec, latent_spec, latent_spec, cache_spec, cache_spec, z_spec, w_spec],
        out_specs=[pl.BlockSpec((SAMPLE_TOKENS, step_width), lambda h: (0, h)), wbf_spec],
        out_shape=[jax.ShapeDtypeStruct((SAMPLE_TOKENS, n_heads * HEAD_DIM), BF16), wbf_shape],
        scratch_shapes=[pltpu.VMEM((N_DR + 1, GRID_W, 2 * GRID_W), F32),
                        pltpu.VMEM((N_DR + 1, GRID_W, 2 * GRID_W), F32)],
        compiler_params=_params(1),
        name="attn_nat_latent",
    )(bias_rows, qz, k, v, cache_k, cache_v, qz, w_out)


SPATIAL_TOKENS = 4 * CHUNK


def _spatial_kernel(uz_ref, v_ref, g_ref, b_ref, ws_ref, bs_ref, o_ref):
    for c in range(SPATIAL_TOKENS // CHUNK):
        rows = slice(c * CHUNK, (c + 1) * CHUNK)
        v = v_ref[rows, :].astype(F32)
        mu = jnp.mean(v, axis=-1, keepdims=True)
        vc = v - mu
        var = jnp.mean(vc * vc, axis=-1, keepdims=True)
        vn = (vc * lax.rsqrt(var + EPS) * g_ref[...] + b_ref[...]).astype(BF16)
        for g in range(GMLP_GROUPS):
            sl = slice(g * GMLP_GROUP_WIDTH, (g + 1) * GMLP_GROUP_WIDTH)
            sv = jnp.dot(ws_ref[g], vn[:, sl], preferred_element_type=F32) + bs_ref[:, g:g + 1]
            o_ref[rows, sl] = uz_ref[rows, sl] * sv.astype(BF16)


def _spatial(uz, v, ln_g, ln_b, w_s, b_s):
    row = pl.BlockSpec((SPATIAL_TOKENS, GMLP_WIDTH), lambda i: (i, 0))
    vec = pl.BlockSpec((1, GMLP_WIDTH), lambda i: (0, 0))
    return pl.pallas_call(
        _spatial_kernel,
        grid=(TOKENS // SPATIAL_TOKENS,),
        in_specs=[row, row, vec, vec,
                  pl.BlockSpec((GMLP_GROUPS, CHUNK, CHUNK), lambda i: (0, 0, 0)),
                  pl.BlockSpec((CHUNK, GMLP_GROUPS), lambda i: (0, 0))],
        out_specs=row,
        out_shape=jax.ShapeDtypeStruct((TOKENS, GMLP_WIDTH), BF16),
        compiler_params=_params(1),
        name="gmlp_spatial",
    )(uz, v, ln_g.reshape(1, -1), ln_b.reshape(1, -1), w_s.astype(BF16), b_s.T)


def kernel(x_prompt, x_sample, cache_win_k, cache_win_v, cache_nat_k, cache_nat_v, c, c_ctx,
           norm_g, w_ada, b_ada,
           win_w_in, win_q_norm, win_k_norm, win_sink, win_w_out,
           nat_w_in, nat_q_norm, nat_k_norm, nat_rel_bias, nat_w_out,
           gmlp_w_in, gmlp_ln_g, gmlp_ln_b, gmlp_w_s, gmlp_b_s, gmlp_w_out):
    depth = norm_g.shape[0]
    xp = x_prompt.reshape(PROMPT_TOKENS, D_MODEL)
    xs = x_sample.reshape(SAMPLE_TOKENS, D_MODEL)
    cond = jnp.zeros((COND_ROWS, D_MODEL), F32).at[:N_SAMPLE].set(c).at[CTX_COND_ROW].set(c_ctx)
    mod4 = _ada(cond, w_ada, b_ada).reshape(depth, COND_ROWS, 1, 3 * D_MODEL)
    norm_g3 = norm_g.reshape(depth, 1, D_MODEL)
    rope_tabs = _rope_tables()
    no_sink = jnp.zeros((1,), F32)

    n_win = win_w_in.shape[0]
    n_nat = nat_w_in.shape[0]
    win_kv_heads = cache_win_k.shape[3]
    win_kv_width = win_kv_heads * HEAD_DIM
    cwk = cache_win_k.reshape(N_SAMPLE, n_win, PAST_LEN * win_kv_heads, HEAD_DIM)
    cwv = cache_win_v.reshape(N_SAMPLE, n_win, PAST_LEN * win_kv_heads, HEAD_DIM)
    nat_heads = cache_nat_k.shape[3]
    cnk = cache_nat_k.reshape(N_SAMPLE, n_nat, PAST_LEN * nat_heads, HEAD_DIM)
    cnv = cache_nat_v.reshape(N_SAMPLE, n_nat, PAST_LEN * nat_heads, HEAD_DIM)

    new_win_k, new_win_v, new_nat_k, new_nat_v = [], [], [], []
    x = (xp, xs)
    hn = _norm_mod(xp, xs, norm_g3, mod4, 0)
    for layer in range(depth):
        kind = layer % 3
        li = layer // 3
        if kind == 0:
            nq = D_MODEL // PROJ_TN
            nkv = win_kv_width // PROJ_TN
            qz = _qz_proj(hn, win_w_in, li, 0, nq + 2 * nkv, nq, win_q_norm[li], rope_tabs, "win_qz")
            kp, ks, vp, vs = _kv_proj(hn, win_w_in, li, nq, nkv, win_k_norm[li], rope_tabs, True, "win_kv",
                                      run_after=qz)
            new_win_k.append(kp)
            new_win_v.append(vp)
            sink = win_sink[li] * LOG2E
            a_latent, wbf = _attn_win(sink, qz, ks, vs, cwk, cwv, li, win_kv_heads, win_w_out)
            a = (_attn_prompt(sink, qz, ks, vs, win_kv_heads, True, "attn_win_prompt"), a_latent)
            out_tm, out_name = OUT_TM, "win_out"
        elif kind == 1:
            nq = D_MODEL // PROJ_TN
            qz = _qz_proj(hn, nat_w_in, li, 0, 3 * nq, nq, nat_q_norm[li], None, "nat_qz")
            kp, ks, vp, vs = _kv_proj(hn, nat_w_in, li, nq, nq, nat_k_norm[li], None, False, "nat_kv",
                                      run_after=qz)
            new_nat_k.append(kp)
            new_nat_v.append(vp)
            a_latent, wbf = _attn_nat(nat_rel_bias[li] * LOG2E, qz, ks, vs, cnk, cnv, li, nat_w_out)
            a = (_attn_prompt(no_sink, qz, ks, vs, nat_heads, False, "attn_nat_prompt", run_after=a_latent),
                 a_latent)
            out_tm, out_name = OUT_TM, "nat_out"
        else:
            nw = GMLP_WIDTH // PROJ_TN
            uz, v, wbf = _uzv_proj(hn, gmlp_w_in, li, nw, "gmlp_uzv", gmlp_w_out)
            a = _spatial(uz, v, gmlp_ln_g[li], gmlp_ln_b[li], gmlp_w_s[li], gmlp_b_s[li])
            out_tm, out_name = OUT_TM_WIDE_K, "gmlp_out"
        if layer + 1 < depth:
            x, hn = _out_proj(a, wbf, x, mod4, layer, norm_g3, out_name, out_tm)
        else:
            (yp,) = _out_proj(a, wbf, x, mod4, layer, None, out_name + "_prompt", out_tm,
                              rows=(0, PROMPT_TOKENS))
            (ys,) = _out_proj(a, wbf, x, mod4, layer, None, out_name + "_latent", out_tm,
                              rows=(PROMPT_TOKENS, SAMPLE_TOKENS))

    cache_shape = lambda layers, heads: jnp.stack(
        [c_.reshape(N_PROMPT, PROMPT_SEQ, heads, HEAD_DIM) for c_ in layers], axis=1)
    return (yp.reshape(N_PROMPT, PROMPT_SEQ, D_MODEL), ys.reshape(N_SAMPLE, SAMPLE_SEQ, D_MODEL),
            cache_shape(new_win_k, win_kv_heads), cache_shape(new_win_v, win_kv_heads),
            cache_shape(new_nat_k, nat_heads), cache_shape(new_nat_v, nat_heads))
```

```python
import functools
import math
from typing import Callable, NamedTuple

import jax
import jax.numpy as jnp
import numpy as np
from jax import lax
from jax.experimental import pallas as pl
from jax.experimental.pallas import tpu as pltpu

F32 = jnp.float32
BF16 = jnp.bfloat16

D_MODEL = 2048
HEAD_DIM = 128
N_PROMPT = 16
PROMPT_SEQ = 256
N_SAMPLE = 2
SAMPLE_SEQ = 1024
PROMPT_TOKENS = N_PROMPT * PROMPT_SEQ
SAMPLE_TOKENS = N_SAMPLE * SAMPLE_SEQ
TOKENS = PROMPT_TOKENS + SAMPLE_TOKENS
PAST_LEN = 512
GRID_W = 64
EPS = 1e-6
NEG_INF = -1e30
ROPE_THETA = 10000.0
WINDOW = 128
NAT_ROWS = 8
NAT_COLS = 16
GMLP_WIDTH = 2 * D_MODEL
GMLP_GROUPS = 16
GMLP_GROUP_WIDTH = GMLP_WIDTH // GMLP_GROUPS
CHUNK = 128
CTX_COND_ROW = 2
COND_ROWS = 8
SM_SCALE = HEAD_DIM ** -0.5
LOG2E = math.log2(math.e)
QUERY_SCALE = SM_SCALE * LOG2E

VMEM_LIMIT = 56 * 1024 * 1024


def _params(n_axes):
    return pltpu.CompilerParams(dimension_semantics=("arbitrary",) * n_axes,
                                vmem_limit_bytes=VMEM_LIMIT)


def _cond_row(tok0):
    return jnp.where(tok0 < PROMPT_TOKENS, CTX_COND_ROW, (tok0 - PROMPT_TOKENS) // SAMPLE_SEQ)


def _silu(x):
    return x * (0.5 + 0.5 * jnp.tanh(0.5 * x))


def _gelu_tanh(x):
    return 0.5 * x * (1.0 + jnp.tanh(math.sqrt(2.0 / math.pi) * (x + 0.044715 * (x * x * x))))


def _split_specs(tm, width, tile0=0):
    n_p = PROMPT_TOKENS // tm
    return [pl.BlockSpec((tm, width), lambda i: (jnp.minimum(i + tile0, n_p - 1), 0)),
            pl.BlockSpec((tm, width), lambda i: (jnp.maximum(i + tile0 - n_p, 0), 0))]


def _ada_kernel(cond_ref, w_ref, b_ref, o_ref):
    s = _silu(cond_ref[...]).astype(BF16)
    o_ref[...] = jnp.dot(s, w_ref[...].astype(BF16), preferred_element_type=F32) + b_ref[...]


def _ada(cond, w_ada, b_ada, tn=1024):
    depth = w_ada.shape[0]
    n = w_ada.shape[2]
    return pl.pallas_call(
        _ada_kernel,
        grid=(depth, n // tn),
        in_specs=[pl.BlockSpec((COND_ROWS, D_MODEL), lambda l, j: (0, 0)),
                  pl.BlockSpec((None, D_MODEL, tn), lambda l, j: (l, 0, j)),
                  pl.BlockSpec((None, 1, tn), lambda l, j: (l, 0, j))],
        out_specs=pl.BlockSpec((None, COND_ROWS, tn), lambda l, j: (l, 0, j)),
        out_shape=jax.ShapeDtypeStruct((depth, COND_ROWS, n), F32),
        compiler_params=_params(2),
        name="ada_mod",
    )(cond, w_ada, b_ada.reshape(depth, 1, n))


NORM_ROWS = 64


def _norm_kernel(xp_ref, xs_ref, g_ref, shift_ref, scale_ref, o_ref, *, tm):
    gmul = g_ref[...] * (1.0 + scale_ref[...])
    shift = shift_ref[...]

    def run(x_ref):
        for r in range(tm // NORM_ROWS):
            rows = slice(r * NORM_ROWS, (r + 1) * NORM_ROWS)
            x = x_ref[rows, :]
            rs = lax.rsqrt(jnp.mean(x * x, axis=-1, keepdims=True) + EPS)
            o_ref[rows, :] = (x * rs * gmul + shift).astype(BF16)

    is_prompt = pl.program_id(0) < PROMPT_TOKENS // tm
    pl.when(is_prompt)(lambda: run(xp_ref))
    pl.when(jnp.logical_not(is_prompt))(lambda: run(xs_ref))


def _norm_mod(xp, xs, norm_g3, mod4, layer, tm=1024):
    return pl.pallas_call(
        functools.partial(_norm_kernel, tm=tm),
        grid=(TOKENS // tm,),
        in_specs=[*_split_specs(tm, D_MODEL),
                  pl.BlockSpec((None, 1, D_MODEL), lambda i: (layer, 0, 0)),
                  pl.BlockSpec((None, None, 1, D_MODEL), lambda i: (layer, _cond_row(i * tm), 0, 0)),
                  pl.BlockSpec((None, None, 1, D_MODEL), lambda i: (layer, _cond_row(i * tm), 0, 1))],
        out_specs=pl.BlockSpec((tm, D_MODEL), lambda i: (i, 0)),
        out_shape=jax.ShapeDtypeStruct((TOKENS, D_MODEL), BF16),
        compiler_params=_params(1),
        name="norm_mod",
    )(xp, xs, norm_g3, mod4, mod4)


PROJ_TM = 2048
PROJ_TN = 512
PROJ_TN_WIDE = 1024
PROJ_ROWS = 512
N_PROMPT_TILES = PROMPT_TOKENS // PROJ_TM


class _Segment(NamedTuple):
    col_blk0: int
    n_blk: int
    epilogue: Callable
    by_tile_kind: bool
    outs: tuple
    pair_blk0: int = -1


def _segment_steps(segments, wide):
    spans, j0 = [], 0
    for seg in segments:
        if seg.pair_blk0 >= 0:
            assert wide == 2
            nj = seg.n_blk
        else:
            assert seg.col_blk0 % wide == 0 and seg.n_blk % wide == 0
            nj = seg.n_blk // wide
        spans.append((j0, nj))
        j0 += nj
    return spans


def _proj_kernel(*refs, segments, spans, n_w, n_extra, n_out, cast_steps):
    a_ref, w_refs = refs[0], refs[1:1 + n_w]
    extras = refs[1 + n_w:1 + n_w + n_extra]
    outs = refs[1 + n_w + n_extra:1 + n_w + n_extra + n_out]
    wbf_ref = refs[1 + n_w + n_extra + n_out]

    @pl.when(pl.program_id(1) == 0)
    def _():
        for h, w_ref in enumerate(w_refs):
            wbf_ref[:, h * PROJ_TN:(h + 1) * PROJ_TN] = w_ref[...].astype(BF16)

    if cast_steps:
        step = pl.program_id(0) * pl.num_programs(1) + pl.program_id(1)

        @pl.when(step < cast_steps)
        def _():
            outs[-1][...] = extras[-1][...].astype(BF16)

    def run(seg):
        seg_outs = [outs[k] for k in seg.outs]

        def body(latent):
            for rc in range(PROJ_TM // PROJ_ROWS):
                rows = slice(rc * PROJ_ROWS, (rc + 1) * PROJ_ROWS)
                accs = [jnp.dot(a_ref[rows, :], wbf_ref[:, col0:col0 + PROJ_TN], preferred_element_type=F32)
                        for col0 in range(0, wbf_ref.shape[1], PROJ_TN)]
                if seg.pair_blk0 >= 0:
                    seg.epilogue(*accs, extras, seg_outs, rc, latent)
                else:
                    for c, acc in enumerate(accs):
                        seg.epilogue(acc, extras, seg_outs, rc, c * PROJ_TN, latent)

        if seg.by_tile_kind:
            is_latent = pl.program_id(1) >= N_PROMPT_TILES
            pl.when(is_latent)(lambda: body(True))
            pl.when(jnp.logical_not(is_latent))(lambda: body(False))
        else:
            body(None)

    j = pl.program_id(0)
    for seg, (j0, nj) in zip(segments, spans):
        if len(segments) == 1:
            run(seg)
        else:
            pl.when(jnp.logical_and(j >= j0, j < j0 + nj))(functools.partial(run, seg))


def _proj(a, w, layer, segments, out_shapes, out_specs, name, extras=(), extra_specs=(), tn=PROJ_TN,
          run_after=None, cast=None, in_place=()):
    m, k = a.shape
    if run_after is not None:
        extras = [*extras, run_after]
        extra_specs = [*extra_specs, pl.BlockSpec(memory_space=pl.ANY)]
    wide = tn // PROJ_TN
    aliases = {}
    for array, out_index in in_place:
        aliases[1 + wide + len(extras)] = out_index
        extras = [*extras, array]
        extra_specs = [*extra_specs, pl.BlockSpec(memory_space=pl.ANY)]
    spans = _segment_steps(segments, wide)
    cast_steps = 0
    if cast is not None:
        w_out, out_layer, cast_steps = cast
        n_tiles = m // PROJ_TM
        assert cast_steps <= n_tiles * sum(nj for _, nj in spans)
        w_spec, wbf_spec, wbf_shape = _cast_slab_specs(
            w_out, out_layer, cast_steps, lambda j, i: jnp.minimum(j * n_tiles + i, cast_steps - 1))
        extras, extra_specs = [*extras, w_out], [*extra_specs, w_spec]
        out_shapes, out_specs = [*out_shapes, wbf_shape], [*out_specs, wbf_spec]

    def w_col(j, half):
        col = None
        for seg, (j0, _) in zip(segments, spans):
            if seg.pair_blk0 >= 0:
                c = (seg.pair_blk0 if half else seg.col_blk0) + j - j0
            else:
                c = seg.col_blk0 + wide * (j - j0) + half
            col = c if col is None else jnp.where(j >= j0, c, col)
        return col

    w_specs = [pl.BlockSpec((None, k, PROJ_TN), lambda j, i, half=half: (layer, 0, w_col(j, half)))
               for half in range(wide)]
    kern = functools.partial(_proj_kernel, segments=segments, spans=spans, n_w=wide, n_extra=len(extras),
                             n_out=len(out_shapes), cast_steps=cast_steps)
    return pl.pallas_call(
        kern,
        grid=(sum(nj for _, nj in spans), m // PROJ_TM),
        in_specs=[pl.BlockSpec((PROJ_TM, k), lambda j, i: (i, 0)), *w_specs, *extra_specs],
        out_specs=out_specs,
        out_shape=out_shapes,
        scratch_shapes=[pltpu.VMEM((k, tn), BF16)],
        input_output_aliases=aliases,
        compiler_params=_params(2),
        name=name,
    )(a, *[w] * wide, *extras)


def _full_out(n_blk):
    return ([jax.ShapeDtypeStruct((TOKENS, n_blk * PROJ_TN), BF16)],
            [pl.BlockSpec((PROJ_TM, PROJ_TN_WIDE), lambda j, i: (i, j))])


def _held(span, j, inside, before, after):
    j0, nj = span
    pick = lambda a, b, c: jnp.where(j < j0, b, jnp.where(j >= j0 + nj, c, a))
    return tuple(pick(a, b, c) for a, b, c in zip(inside, before, after))


def _split_out(span, n_blk, head_rows, layers=(1, 0)):
    prompt_tile = lambda i: jnp.minimum(i, N_PROMPT_TILES - 1)
    last_prompt, last_tile = N_PROMPT_TILES - 1, TOKENS // PROJ_TM - 1
    jj = lambda j: j - span[0]
    if head_rows:
        assert n_blk == 1
        heads = PROJ_TN // HEAD_DIM
        n_layers, layer = layers
        cache_shape = jax.ShapeDtypeStruct((N_PROMPT, n_layers * PROMPT_SEQ * heads, HEAD_DIM), F32)
        cache_spec = pl.BlockSpec(
            (PROJ_TM // PROMPT_SEQ, PROMPT_SEQ * heads, HEAD_DIM),
            lambda j, i: _held(span, j, (prompt_tile(i), layer, 0), (0, layer, 0), (last_prompt, layer, 0)))
    else:
        cache_shape = jax.ShapeDtypeStruct((N_PROMPT, PROMPT_SEQ, n_blk * PROJ_TN), F32)
        cache_spec = pl.BlockSpec(
            (PROJ_TM // PROMPT_SEQ, PROMPT_SEQ, PROJ_TN),
            lambda j, i: _held(span, j, (prompt_tile(i), 0, jj(j)), (0, 0, 0), (last_prompt, 0, n_blk - 1)))
    shapes = [cache_shape, jax.ShapeDtypeStruct((TOKENS, n_blk * PROJ_TN), BF16)]
    specs = [cache_spec,
             pl.BlockSpec((PROJ_TM, PROJ_TN),
                          lambda j, i: _held(span, j, (i, jj(j)), (0, 0), (last_tile, n_blk - 1)))]
    return shapes, specs


def _chunk_rows(rc):
    return slice(rc * PROJ_ROWS, (rc + 1) * PROJ_ROWS)


def _epi_silu(acc, extras, outs, rc, col0, latent):
    outs[0][_chunk_rows(rc), col0:col0 + PROJ_TN] = _silu(acc).astype(BF16)


def _epi_gelu(acc, extras, outs, rc, col0, latent):
    outs[0][_chunk_rows(rc), col0:col0 + PROJ_TN] = _gelu_tanh(acc).astype(BF16)


def _store_split(y, outs, rc, sl, latent):
    outs[1][_chunk_rows(rc), sl] = y.astype(BF16)
    if latent:
        return
    seqs = PROJ_ROWS // PROMPT_SEQ
    if outs[0].shape[-1] == HEAD_DIM:
        heads = PROJ_TN // HEAD_DIM
        col0 = sl.start or 0
        for c in range(y.shape[1] // HEAD_DIM):
            head_rows = pl.ds(col0 // HEAD_DIM + c, PROMPT_SEQ, stride=heads)
            for s in range(seqs):
                outs[0][rc * seqs + s, head_rows, :] = y[s * PROMPT_SEQ:(s + 1) * PROMPT_SEQ,
                                                         c * HEAD_DIM:(c + 1) * HEAD_DIM]
    else:
        outs[0][rc * seqs:(rc + 1) * seqs, :, sl] = y.reshape(seqs, PROMPT_SEQ, y.shape[1])


def _epi_value(acc, extras, outs, rc, col0, latent):
    _store_split(acc, outs, rc, slice(col0, col0 + PROJ_TN), latent)


def _head_rmsnorm(acc, hh, g):
    a = acc[:, hh * HEAD_DIM:(hh + 1) * HEAD_DIM]
    return a * lax.rsqrt(jnp.mean(a * a, axis=-1, keepdims=True) + EPS) * g


def _normed_heads(acc, extras, rc, rope):
    g = extras[0][...]
    ys = [_head_rmsnorm(acc, hh, g) for hh in range(PROJ_TN // HEAD_DIM)]
    if rope:
        rows = _chunk_rows(rc)
        cos, sin, swap = extras[1][rows, :], extras[2][rows, :], extras[3][...]
        for pair in range(len(ys) // 2):
            both = jnp.concatenate(ys[2 * pair:2 * pair + 2], axis=1).astype(BF16)
            partner = jnp.dot(both, swap, preferred_element_type=F32)
            for t in range(2):
                hh = 2 * pair + t
                ys[hh] = ys[hh] * cos + partner[:, t * HEAD_DIM:(t + 1) * HEAD_DIM] * sin
    for hh, y in enumerate(ys):
        yield slice(hh * HEAD_DIM, (hh + 1) * HEAD_DIM), y


def _epi_query(acc, extras, outs, rc, col0, latent, *, rope):
    for sl, y in _normed_heads(acc, extras, rc, rope and latent):
        outs[0][_chunk_rows(rc), col0 + sl.start:col0 + sl.stop] = (y * QUERY_SCALE).astype(BF16)


def _epi_key(acc, extras, outs, rc, col0, latent, *, rope):
    for sl, y in _normed_heads(acc, extras, rc, rope and latent):
        _store_split(y, outs, rc, slice(col0 + sl.start, col0 + sl.stop), latent)


def _rope_tables():
    nf = HEAD_DIM // 4
    t = np.arange(SAMPLE_TOKENS) % SAMPLE_SEQ
    row = (t // GRID_W).astype(np.float32)
    col = (t % GRID_W).astype(np.float32)
    inv = np.float32(ROPE_THETA) ** (-np.arange(nf, dtype=np.float32) / np.float32(nf))
    ang_r = row[:, None] * inv
    ang_c = col[:, None] * inv
    cos = np.concatenate([np.cos(ang_r), np.cos(ang_r), np.cos(ang_c), np.cos(ang_c)], axis=1)
    sin = np.concatenate([-np.sin(ang_r), np.sin(ang_r), -np.sin(ang_c), np.sin(ang_c)], axis=1)
    lanes = np.arange(2 * HEAD_DIM)
    swap = lanes[:, None] == (lanes[None, :] ^ (HEAD_DIM // 4))
    return jnp.asarray(cos, F32), jnp.asarray(sin, F32), jnp.asarray(swap, BF16)


def _gain_extras(gain, rope_tabs):
    extras = [gain.reshape(1, HEAD_DIM)]
    specs = [pl.BlockSpec((1, HEAD_DIM), lambda j, i: (0, 0))]
    if rope_tabs is not None:
        tab_map = lambda j, i: (jnp.maximum(i - N_PROMPT_TILES, 0), 0)
        extras += list(rope_tabs)
        specs += [pl.BlockSpec((PROJ_TM, HEAD_DIM), tab_map)] * 2
        specs.append(pl.BlockSpec((2 * HEAD_DIM, 2 * HEAD_DIM), lambda j, i: (0, 0)))
    return extras, specs


def _qz_proj(hn, w, layer, q_blk0, z_blk0, n_blk, gain, rope_tabs, name):
    extras, specs = _gain_extras(gain, rope_tabs)
    rope = rope_tabs is not None
    segments = (_Segment(q_blk0, n_blk, functools.partial(_epi_query, rope=rope), rope, (0,)),
                _Segment(z_blk0, n_blk, _epi_silu, False, (0,)))
    return _proj(hn, w, layer, segments, *_full_out(2 * n_blk), name, extras, specs, tn=PROJ_TN_WIDE)[0]


def _kv_proj(hn, w, layer, k_blk0, n_blk, gain, rope_tabs, head_rows, name, run_after=None, caches=None):
    n_layers = 1 if caches is None else caches[0].shape[1] // (PROMPT_SEQ * n_blk * PROJ_TN // HEAD_DIM)
    extras, specs = _gain_extras(gain, rope_tabs)
    segments = (_Segment(k_blk0, n_blk, functools.partial(_epi_key, rope=rope_tabs is not None), True, (0, 1)),
                _Segment(k_blk0 + n_blk, n_blk, _epi_value, True, (2, 3)))
    shapes, out_specs = [], []
    for span in _segment_steps(segments, 1):
        seg_shapes, seg_specs = _split_out(span, n_blk, head_rows, (n_layers, layer))
        shapes += seg_shapes
        out_specs += seg_specs
    in_place = () if caches is None else ((caches[0], 0), (caches[1], 2))
    return _proj(hn, w, layer, segments, shapes, out_specs, name, extras, specs, run_after=run_after,
                 in_place=in_place)


UVZ_CAST_STEPS = 32


def _epi_uz(acc_u, acc_z, extras, outs, rc, latent):
    outs[0][_chunk_rows(rc), :] = (_gelu_tanh(acc_u) * _silu(acc_z)).astype(BF16)


def _uzv_proj(hn, w, layer, n_blk, name, w_out):
    segments = (_Segment(0, n_blk, _epi_uz, False, (0,), pair_blk0=2 * n_blk),
                _Segment(n_blk, n_blk, _epi_gelu, False, (1,)))
    (uz_span, v_span) = _segment_steps(segments, PROJ_TN_WIDE // PROJ_TN)
    last_tile = TOKENS // PROJ_TM - 1
    shape = jax.ShapeDtypeStruct((TOKENS, n_blk * PROJ_TN), BF16)
    out_specs = [
        pl.BlockSpec((PROJ_TM, PROJ_TN),
                     lambda j, i: _held(uz_span, j, (i, j - uz_span[0]), (0, 0), (last_tile, uz_span[1] - 1))),
        pl.BlockSpec((PROJ_TM, PROJ_TN_WIDE),
                     lambda j, i: _held(v_span, j, (i, j - v_span[0]), (0, 0), (last_tile, v_span[1] - 1)))]
    return _proj(hn, w, layer, segments, [shape, shape], out_specs, name, tn=PROJ_TN_WIDE,
                 cast=(w_out, layer, UVZ_CAST_STEPS))


OUT_COLS = 512
OUT_TM = 512
OUT_TM_WIDE_K = 256


def _cast_slab_specs(w, layer, n_slabs, slab_of_step):
    k, n = w.shape[1], w.shape[2]
    rows = k // n_slabs
    return (pl.BlockSpec((None, rows, n), lambda *g: (layer, slab_of_step(*g), 0)),
            pl.BlockSpec((rows, n), lambda *g: (slab_of_step(*g), 0)),
            jax.ShapeDtypeStruct((k, n), BF16))


def _out_kernel(*refs, tile0, tm, split_a, split_x, fuse_norm):
    it = iter(refs)
    a_refs = [next(it), next(it)] if split_a else [next(it)]
    wbf_ref = next(it)
    x_refs = [next(it), next(it)] if split_x else [next(it)]
    gate_ref = next(it)
    if fuse_norm:
        g_ref, shift_ref, scale_ref = next(it), next(it), next(it)
    xnew_ref = next(it)
    hn_ref = next(it) if fuse_norm else None

    def body(a_ref, x_ref):
        ssq = jnp.zeros((tm, 1), F32)
        for cb in range(D_MODEL // OUT_COLS):
            sl = slice(cb * OUT_COLS, (cb + 1) * OUT_COLS)
            acc = jnp.dot(a_ref[...], wbf_ref[:, sl], preferred_element_type=F32)
            xn = x_ref[:, sl] + gate_ref[:, sl] * acc
            xnew_ref[:, sl] = xn
            if fuse_norm:
                ssq = ssq + jnp.sum(xn * xn, axis=-1, keepdims=True)
        if fuse_norm:
            rs = lax.rsqrt(ssq * (1.0 / D_MODEL) + EPS)
            for cb in range(D_MODEL // OUT_COLS):
                sl = slice(cb * OUT_COLS, (cb + 1) * OUT_COLS)
                gmul = g_ref[:, sl] * (1.0 + scale_ref[:, sl])
                hn_ref[:, sl] = (xnew_ref[:, sl] * rs * gmul + shift_ref[:, sl]).astype(BF16)

    if split_a or split_x:
        is_prompt = pl.program_id(0) + tile0 < PROMPT_TOKENS // tm
        pl.when(is_prompt)(lambda: body(a_refs[0], x_refs[0]))
        pl.when(jnp.logical_not(is_prompt))(lambda: body(a_refs[-1], x_refs[-1]))
    else:
        body(a_refs[0], x_refs[0])


def _out_proj(a, wbf, x, mod4, layer, norm_g3, name, tm, rows=None):
    split_a = isinstance(a, tuple)
    k = a[0].shape[1] if split_a else a.shape[1]
    tok0, n_tok = rows if rows is not None else (0, TOKENS)
    tile0 = tok0 // tm
    split_x = isinstance(x, tuple)
    fuse_norm = norm_g3 is not None
    mod_spec = lambda part, lyr: pl.BlockSpec(
        (None, None, 1, D_MODEL), lambda i: (lyr, _cond_row((i + tile0) * tm), 0, part))
    row_spec = lambda width: pl.BlockSpec((tm, width), lambda i: (i + tile0, 0))
    operands = [*a, wbf] if split_a else [a, wbf]
    in_specs = [*(_split_specs(tm, k, tile0) if split_a else [row_spec(k)]),
                pl.BlockSpec((k, D_MODEL), lambda i: (0, 0))]
    if split_x:
        assert rows is None
        operands += list(x)
        in_specs += _split_specs(tm, D_MODEL)
    else:
        operands.append(x)
        in_specs.append(row_spec(D_MODEL))
    operands.append(mod4)
    in_specs.append(mod_spec(2, layer))
    out_shapes = [jax.ShapeDtypeStruct((n_tok, D_MODEL), F32)]
    out_specs = [pl.BlockSpec((tm, D_MODEL), lambda i: (i, 0))]
    if fuse_norm:
        operands += [norm_g3, mod4, mod4]
        in_specs += [pl.BlockSpec((None, 1, D_MODEL), lambda i: (layer + 1, 0, 0)),
                     mod_spec(0, layer + 1), mod_spec(1, layer + 1)]
        out_shapes.append(jax.ShapeDtypeStruct((n_tok, D_MODEL), BF16))
        out_specs.append(pl.BlockSpec((tm, D_MODEL), lambda i: (i, 0)))
    kern = functools.partial(_out_kernel, tile0=tile0, tm=tm,
                             split_a=split_a, split_x=split_x, fuse_norm=fuse_norm)
    return pl.pallas_call(
        kern,
        grid=(n_tok // tm,),
        in_specs=in_specs,
        out_specs=out_specs,
        out_shape=out_shapes,
        compiler_params=_params(1),
        name=name,
    )(*operands)


def _dot_nt(a, b):
    return lax.dot_general(a, b, (((1,), (1,)), ((), ())), preferred_element_type=F32)


def _head(ref, h, rows=slice(None)):
    return ref[rows, h * HEAD_DIM:(h + 1) * HEAD_DIM]


def _with_ones(v):
    return jnp.concatenate([v, jnp.ones(v.shape, v.dtype)], axis=1)


def _lane_chunks(s):
    return [s[:, c * HEAD_DIM:(c + 1) * HEAD_DIM] for c in range(s.shape[1] // HEAD_DIM)]


def _softmax_pv(score_blocks, value_blocks, sink):
    rows = score_blocks[0].shape[0]
    mx = functools.reduce(jnp.maximum, [c for s in score_blocks for c in _lane_chunks(s)])
    m = jnp.broadcast_to(jnp.max(mx, axis=-1, keepdims=True), (rows, HEAD_DIM))
    if sink is not None:
        m = jnp.maximum(m, sink)
    o = None
    for s, v in zip(score_blocks, value_blocks):
        p = jnp.concatenate([jnp.exp2(c - m) for c in _lane_chunks(s)], axis=1).astype(BF16)
        part = jnp.dot(p, v, preferred_element_type=F32)
        o = part if o is None else o + part
    den = o[:, HEAD_DIM:]
    if sink is not None:
        den = den + jnp.exp2(sink - m)
    return o[:, :HEAD_DIM] * (1.0 / den)


PROMPT_REQUESTS = 2


def _attn_prompt_kernel(sink_ref, q_ref, k_ref, v_ref, z_ref, *rest, n_heads, n_kv, use_sink):
    o_ref = rest[-1]
    grp = n_heads // n_kv
    for r in range(PROMPT_REQUESTS):
        rows = slice(r * PROMPT_SEQ, (r + 1) * PROMPT_SEQ)
        for g in range(n_kv):
            kg = _head(k_ref, g, rows)
            vg = _with_ones(_head(v_ref, g, rows))
            for h in range(g * grp, (g + 1) * grp):
                s = _dot_nt(_head(q_ref, h, rows), kg)
                sink = jnp.full((PROMPT_SEQ, HEAD_DIM), sink_ref[h], F32) if use_sink else None
                o = _softmax_pv([s], [vg], sink)
                o_ref[rows, h * HEAD_DIM:(h + 1) * HEAD_DIM] = (
                    o * _head(z_ref, h, rows).astype(F32)).astype(BF16)


def _attn_prompt(sink, qz, k, v, n_kv, use_sink, name, run_after=None):
    ordering = [] if run_after is None else [run_after]
    width = qz.shape[1] // 2
    kern = functools.partial(_attn_prompt_kernel, n_heads=width // HEAD_DIM, n_kv=n_kv, use_sink=use_sink)
    rows = PROMPT_REQUESTS * PROMPT_SEQ
    cache_spec = pl.BlockSpec((rows, k.shape[1]), lambda b: (b, 0))
    return pl.pallas_call(
        kern,
        grid=(N_PROMPT // PROMPT_REQUESTS,),
        in_specs=[pl.BlockSpec(memory_space=pltpu.SMEM),
                  pl.BlockSpec((rows, width), lambda b: (b, 0)),
                  cache_spec, cache_spec,
                  pl.BlockSpec((rows, width), lambda b: (b, 1)),
                  *[pl.BlockSpec(memory_space=pl.ANY) for _ in ordering]],
        out_specs=pl.BlockSpec((rows, width), lambda b: (b, 0)),
        out_shape=jax.ShapeDtypeStruct((PROMPT_TOKENS, width), BF16),
        compiler_params=_params(1),
        name=name,
    )(sink, qz, k, v, qz, *ordering)


WIN_BAND = 3 * WINDOW


def _attn_win_kernel(sink_ref, q_ref, k_ref, v_ref, kc_ref, vc_ref, z_ref, w_ref, o_ref, wbf_ref, *,
                     n_heads, n_kv):
    wbf_ref[...] = w_ref[...].astype(BF16)
    grp = n_heads // n_kv
    n = pl.program_id(1)
    start = pl.multiple_of(jnp.clip((n - 1) * WINDOW, 0, SAMPLE_SEQ - WIN_BAND), WINDOW)
    shape = (grp * WINDOW, WIN_BAND)
    qpos = n * WINDOW + (lax.broadcasted_iota(jnp.int32, shape, 0) & (WINDOW - 1))
    kpos = start + lax.broadcasted_iota(jnp.int32, shape, 1)
    valid = jnp.abs(kpos - qpos) <= WINDOW
    band = pl.ds(start, WIN_BAND)
    for g in range(n_kv):
        heads = [g * grp + t for t in range(grp)]
        qs = jnp.concatenate([_head(q_ref, h) for h in heads], axis=0)
        kb = _head(k_ref, g, band)
        vb = _with_ones(_head(v_ref, g, band))
        ctx_rows = pl.ds(g, PAST_LEN, stride=n_kv)
        kc = kc_ref[ctx_rows, :].astype(BF16)
        vc = _with_ones(vc_ref[ctx_rows, :].astype(BF16))
        s_band = jnp.where(valid, _dot_nt(qs, kb), NEG_INF)
        s_ctx = _dot_nt(qs, kc)
        sink = jnp.concatenate([jnp.full((WINDOW, HEAD_DIM), sink_ref[h], F32) for h in heads], axis=0)
        o = _softmax_pv([s_band, s_ctx], [vb, vc], sink)
        for t, h in enumerate(heads):
            oh = o[t * WINDOW:(t + 1) * WINDOW]
            o_ref[:, h * HEAD_DIM:(h + 1) * HEAD_DIM] = (oh * _head(z_ref, h).astype(F32)).astype(BF16)


def _attn_win(sink, qz, k, v, cache_k, cache_v, layer_in_kind, n_kv, w_out):
    width = qz.shape[1] // 2
    kv_width = k.shape[1]
    blocks_per_seq = SAMPLE_SEQ // WINDOW
    q_map = lambda b, n: (PROMPT_TOKENS // WINDOW + b * blocks_per_seq + n, 0)
    z_map = lambda b, n: (PROMPT_TOKENS // WINDOW + b * blocks_per_seq + n, 1)
    kv_map = lambda b, n: (PROMPT_TOKENS // SAMPLE_SEQ + b, 0)
    cache_map = lambda b, n: (b, layer_in_kind, 0, 0)
    kern = functools.partial(_attn_win_kernel, n_heads=width // HEAD_DIM, n_kv=n_kv)
    w_spec, wbf_spec, wbf_shape = _cast_slab_specs(w_out, layer_in_kind, N_SAMPLE * blocks_per_seq,
                                                   lambda b, n: b * blocks_per_seq + n)
    return pl.pallas_call(
        kern,
        grid=(N_SAMPLE, blocks_per_seq),
        in_specs=[pl.BlockSpec(memory_space=pltpu.SMEM),
                  pl.BlockSpec((WINDOW, width), q_map),
                  pl.BlockSpec((SAMPLE_SEQ, kv_width), kv_map),
                  pl.BlockSpec((SAMPLE_SEQ, kv_width), kv_map),
                  pl.BlockSpec((None, None, PAST_LEN * n_kv, HEAD_DIM), cache_map),
                  pl.BlockSpec((None, None, PAST_LEN * n_kv, HEAD_DIM), cache_map),
                  pl.BlockSpec((WINDOW, width), z_map),
                  w_spec],
        out_specs=[pl.BlockSpec((WINDOW, width), lambda b, n: (b * blocks_per_seq + n, 0)), wbf_spec],
        out_shape=[jax.ShapeDtypeStruct((SAMPLE_TOKENS, width), BF16), wbf_shape],
        compiler_params=_params(2),
        name="attn_win_latent",
    )(sink, qz, k, v, cache_k, cache_v, qz, w_out)


NAT_QROWS = 4
GRID_ROWS = SAMPLE_SEQ // GRID_W
N_DR = 2 * NAT_ROWS - 1
MASKED_TILE = N_DR


def _nat_row_start(qr):
    return min(max(qr - NAT_ROWS // 2, 0), GRID_ROWS - NAT_ROWS)


def _attn_nat_kernel(bias_ref, q_ref, k_ref, v_ref, kc_ref, vc_ref, z_ref, w_ref, o_ref, wbf_ref,
                     left_ref, right_ref):
    wbf_ref[...] = w_ref[...].astype(BF16)

    def build_bias_tiles():
        shape = (GRID_W, 2 * GRID_W)
        lane = lax.broadcasted_iota(jnp.int32, shape, 1)
        qc = lax.broadcasted_iota(jnp.int32, shape, 0)
        kc_ = lane & (GRID_W - 1)
        cs = jnp.clip(qc - NAT_COLS // 2, 0, GRID_W - NAT_COLS)
        col_ok = jnp.logical_and(kc_ >= cs, kc_ < cs + NAT_COLS)
        is_left = lane < GRID_W
        lanes = 2 * GRID_W
        for dri in range(N_DR):
            row = jnp.broadcast_to(bias_ref[dri:dri + 1, :], shape)
            on_left = pltpu.roll(row, lanes - (NAT_COLS - 1), 1, stride=1, stride_axis=0)
            on_right = pltpu.roll(row, GRID_W - (NAT_COLS - 1), 1, stride=1, stride_axis=0)
            left_ref[dri] = jnp.where(is_left, jnp.where(col_ok, on_left, NEG_INF), 0.0)
            right_ref[dri] = jnp.where(is_left, 0.0, jnp.where(col_ok, on_right, NEG_INF))
        left_ref[MASKED_TILE] = jnp.where(is_left, NEG_INF, 0.0)
        right_ref[MASKED_TILE] = jnp.where(is_left, 0.0, NEG_INF)

    build_bias_tiles()
    n_heads = kc_ref.shape[1] // PAST_LEN
    ctx_rows = pl.ds(pl.program_id(0), PAST_LEN, stride=n_heads)
    for b, qb in [(b, qb) for b in range(N_SAMPLE) for qb in range(GRID_ROWS // NAT_QROWS)]:
        if qb == 0:
            kc = kc_ref[b, ctx_rows, :].astype(BF16)
            vc = _with_ones(vc_ref[b, ctx_rows, :].astype(BF16))
        qrows = range(qb * NAT_QROWS, (qb + 1) * NAT_QROWS)
        krow0 = _nat_row_start(qrows[0]) // 2 * 2
        krow1 = -(-(_nat_row_start(qrows[-1]) + NAT_ROWS) // 2) * 2
        tok0 = b * SAMPLE_SEQ
        kwin = slice(tok0 + krow0 * GRID_W, tok0 + krow1 * GRID_W)
        qwin = slice(tok0 + qrows[0] * GRID_W, tok0 + (qrows[-1] + 1) * GRID_W)

        def tile_index(qr, kr):
            rs = _nat_row_start(qr)
            return kr - qr + (NAT_ROWS - 1) if rs <= kr < rs + NAT_ROWS else MASKED_TILE

        bias = jnp.concatenate(
            [jnp.concatenate([left_ref[tile_index(qr, kr)] + right_ref[tile_index(qr, kr + 1)]
                              for kr in range(krow0, krow1, 2)], axis=1)
             for qr in qrows], axis=0)

        q = q_ref[qwin, :]
        s_nb = _dot_nt(q, k_ref[kwin, :]) + bias
        s_ctx = _dot_nt(q, kc)
        o = _softmax_pv([s_nb, s_ctx], [_with_ones(v_ref[kwin, :]), vc], None)
        o_ref[qwin, :] = (o * z_ref[qwin, :].astype(F32)).astype(BF16)


def _attn_nat(rel_bias, qz, k, v, cache_k, cache_v, layer_in_kind, w_out):
    n_heads = qz.shape[1] // (2 * HEAD_DIM)
    bias_rows = jnp.pad(rel_bias, ((0, 0), (0, N_DR + 1 - rel_bias.shape[1]),
                                   (0, 2 * GRID_W - rel_bias.shape[2])))
    latent_tile = PROMPT_TOKENS // SAMPLE_TOKENS
    latent_spec = pl.BlockSpec((SAMPLE_TOKENS, HEAD_DIM), lambda h: (latent_tile, h))
    z_spec = pl.BlockSpec((SAMPLE_TOKENS, HEAD_DIM), lambda h: (latent_tile, n_heads + h))
    cache_spec = pl.BlockSpec((N_SAMPLE, None, PAST_LEN * n_heads, HEAD_DIM),
                              lambda h: (0, layer_in_kind, 0, 0))
    w_spec, wbf_spec, wbf_shape = _cast_slab_specs(w_out, layer_in_kind, n_heads, lambda h: h)
    return pl.pallas_call(
        _attn_nat_kernel,
        grid=(n_heads,),
        in_specs=[pl.BlockSpec((None, N_DR + 1, 2 * GRID_W), lambda h: (h, 0, 0)),
                  latent_spec, latent_spec, latent_spec, cache_spec, cache_spec, z_spec, w_spec],
        out_specs=[pl.BlockSpec((SAMPLE_TOKENS, HEAD_DIM), lambda h: (0, h)), wbf_spec],
        out_shape=[jax.ShapeDtypeStruct((SAMPLE_TOKENS, n_heads * HEAD_DIM), BF16), wbf_shape],
        scratch_shapes=[pltpu.VMEM((N_DR + 1, GRID_W, 2 * GRID_W), F32),
                        pltpu.VMEM((N_DR + 1, GRID_W, 2 * GRID_W), F32)],
        compiler_params=_params(1),
        name="attn_nat_latent",
    )(bias_rows, qz, k, v, cache_k, cache_v, qz, w_out)


SPATIAL_TOKENS = 2 * CHUNK


def _spatial_kernel(uz_ref, v_ref, g_ref, b_ref, ws_ref, bs_ref, o_ref):
    for c in range(SPATIAL_TOKENS // CHUNK):
        rows = slice(c * CHUNK, (c + 1) * CHUNK)
        v = v_ref[rows, :].astype(F32)
        mu = jnp.mean(v, axis=-1, keepdims=True)
        vc = v - mu
        var = jnp.mean(vc * vc, axis=-1, keepdims=True)
        vn = (vc * lax.rsqrt(var + EPS) * g_ref[...] + b_ref[...]).astype(BF16)
        for g in range(GMLP_GROUPS):
            sl = slice(g * GMLP_GROUP_WIDTH, (g + 1) * GMLP_GROUP_WIDTH)
            sv = jnp.dot(ws_ref[g], vn[:, sl], preferred_element_type=F32) + bs_ref[:, g:g + 1]
            o_ref[rows, sl] = uz_ref[rows, sl] * sv.astype(BF16)


def _spatial(uz, v, ln_g, ln_b, w_s, b_s):
    row = pl.BlockSpec((SPATIAL_TOKENS, GMLP_WIDTH), lambda i: (i, 0))
    vec = pl.BlockSpec((1, GMLP_WIDTH), lambda i: (0, 0))
    return pl.pallas_call(
        _spatial_kernel,
        grid=(TOKENS // SPATIAL_TOKENS,),
        in_specs=[row, row, vec, vec,
                  pl.BlockSpec((GMLP_GROUPS, CHUNK, CHUNK), lambda i: (0, 0, 0)),
                  pl.BlockSpec((CHUNK, GMLP_GROUPS), lambda i: (0, 0))],
        out_specs=row,
        out_shape=jax.ShapeDtypeStruct((TOKENS, GMLP_WIDTH), BF16),
        compiler_params=_params(1),
        name="gmlp_spatial",
    )(uz, v, ln_g.reshape(1, -1), ln_b.reshape(1, -1), w_s.astype(BF16), b_s.T)


def kernel(x_prompt, x_sample, cache_win_k, cache_win_v, cache_nat_k, cache_nat_v, c, c_ctx,
           norm_g, w_ada, b_ada,
           win_w_in, win_q_norm, win_k_norm, win_sink, win_w_out,
           nat_w_in, nat_q_norm, nat_k_norm, nat_rel_bias, nat_w_out,
           gmlp_w_in, gmlp_ln_g, gmlp_ln_b, gmlp_w_s, gmlp_b_s, gmlp_w_out):
    depth = norm_g.shape[0]
    xp = x_prompt.reshape(PROMPT_TOKENS, D_MODEL)
    xs = x_sample.reshape(SAMPLE_TOKENS, D_MODEL)
    cond = jnp.zeros((COND_ROWS, D_MODEL), F32).at[:N_SAMPLE].set(c).at[CTX_COND_ROW].set(c_ctx)
    mod4 = _ada(cond, w_ada, b_ada).reshape(depth, COND_ROWS, 1, 3 * D_MODEL)
    norm_g3 = norm_g.reshape(depth, 1, D_MODEL)
    rope_tabs = _rope_tables()
    no_sink = jnp.zeros((1,), F32)

    n_win = win_w_in.shape[0]
    n_nat = nat_w_in.shape[0]
    win_kv_heads = cache_win_k.shape[3]
    win_kv_width = win_kv_heads * HEAD_DIM
    cwk = cache_win_k.reshape(N_SAMPLE, n_win, PAST_LEN * win_kv_heads, HEAD_DIM)
    cwv = cache_win_v.reshape(N_SAMPLE, n_win, PAST_LEN * win_kv_heads, HEAD_DIM)
    nat_heads = cache_nat_k.shape[3]
    cnk = cache_nat_k.reshape(N_SAMPLE, n_nat, PAST_LEN * nat_heads, HEAD_DIM)
    cnv = cache_nat_v.reshape(N_SAMPLE, n_nat, PAST_LEN * nat_heads, HEAD_DIM)

    new_nat_k, new_nat_v = [], []
    win_cache = jax.ShapeDtypeStruct((N_PROMPT, n_win * PROMPT_SEQ * win_kv_heads, HEAD_DIM), F32)
    new_win_k = jnp.zeros(win_cache.shape, win_cache.dtype)
    new_win_v = jnp.zeros(win_cache.shape, win_cache.dtype)
    x = (xp, xs)
    hn = _norm_mod(xp, xs, norm_g3, mod4, 0)
    for layer in range(depth):
        kind = layer % 3
        li = layer // 3
        if kind == 0:
            nq = D_MODEL // PROJ_TN
            nkv = win_kv_width // PROJ_TN
            qz = _qz_proj(hn, win_w_in, li, 0, nq + 2 * nkv, nq, win_q_norm[li], rope_tabs, "win_qz")
            new_win_k, ks, new_win_v, vs = _kv_proj(hn, win_w_in, li, nq, nkv, win_k_norm[li], rope_tabs, True,
                                                    "win_kv", run_after=qz, caches=(new_win_k, new_win_v))
            sink = win_sink[li] * LOG2E
            a_latent, wbf = _attn_win(sink, qz, ks, vs, cwk, cwv, li, win_kv_heads, win_w_out)
            a = (_attn_prompt(sink, qz, ks, vs, win_kv_heads, True, "attn_win_prompt"), a_latent)
            out_tm, out_name = OUT_TM, "win_out"
        elif kind == 1:
            nq = D_MODEL // PROJ_TN
            qz = _qz_proj(hn, nat_w_in, li, 0, 3 * nq, nq, nat_q_norm[li], None, "nat_qz")
            kp, ks, vp, vs = _kv_proj(hn, nat_w_in, li, nq, nq, nat_k_norm[li], None, False, "nat_kv",
                                      run_after=qz)
            new_nat_k.append(kp)
            new_nat_v.append(vp)
            a_latent, wbf = _attn_nat(nat_rel_bias[li] * LOG2E, qz, ks, vs, cnk, cnv, li, nat_w_out)
            a = (_attn_prompt(no_sink, qz, ks, vs, nat_heads, False, "attn_nat_prompt", run_after=a_latent),
                 a_latent)
            out_tm, out_name = OUT_TM, "nat_out"
        else:
            nw = GMLP_WIDTH // PROJ_TN
            uz, v, wbf = _uzv_proj(hn, gmlp_w_in, li, nw, "gmlp_uzv", gmlp_w_out)
            a = _spatial(uz, v, gmlp_ln_g[li], gmlp_ln_b[li], gmlp_w_s[li], gmlp_b_s[li])
            out_tm, out_name = OUT_TM_WIDE_K, "gmlp_out"
        if layer + 1 < depth:
            x, hn = _out_proj(a, wbf, x, mod4, layer, norm_g3, out_name, out_tm)
        else:
            (yp,) = _out_proj(a, wbf, x, mod4, layer, None, out_name + "_prompt", out_tm,
                              rows=(0, PROMPT_TOKENS))
            (ys,) = _out_proj(a, wbf, x, mod4, layer, None, out_name + "_latent", out_tm,
                              rows=(PROMPT_TOKENS, SAMPLE_TOKENS))

    cache_shape = lambda layers, heads: jnp.stack(
        [c_.reshape(N_PROMPT, PROMPT_SEQ, heads, HEAD_DIM) for c_ in layers], axis=1)
    win_shape = (N_PROMPT, n_win, PROMPT_SEQ, win_kv_heads, HEAD_DIM)
    return (yp.reshape(N_PROMPT, PROMPT_SEQ, D_MODEL), ys.reshape(N_SAMPLE, SAMPLE_SEQ, D_MODEL),
            new_win_k.reshape(win_shape), new_win_v.reshape(win_shape),
            cache_shape(new_nat_k, nat_heads), cache_shape(new_nat_v, nat_heads))
```

```python
import functools
import math
from typing import Callable, NamedTuple

import jax
import jax.numpy as jnp
import numpy as np
from jax import lax
from jax.experimental import pallas as pl
from jax.experimental.pallas import tpu as pltpu

F32 = jnp.float32
BF16 = jnp.bfloat16

D_MODEL = 2048
HEAD_DIM = 128
N_PROMPT = 16
PROMPT_SEQ = 256
N_SAMPLE = 2
SAMPLE_SEQ = 1024
PROMPT_TOKENS = N_PROMPT * PROMPT_SEQ
SAMPLE_TOKENS = N_SAMPLE * SAMPLE_SEQ
TOKENS = PROMPT_TOKENS + SAMPLE_TOKENS
PAST_LEN = 512
GRID_W = 64
EPS = 1e-6
NEG_INF = -1e30
ROPE_THETA = 10000.0
WINDOW = 128
NAT_ROWS = 8
NAT_COLS = 16
GMLP_WIDTH = 2 * D_MODEL
GMLP_GROUPS = 16
GMLP_GROUP_WIDTH = GMLP_WIDTH // GMLP_GROUPS
CHUNK = 128
CTX_COND_ROW = 2
COND_ROWS = 8
SM_SCALE = HEAD_DIM ** -0.5
LOG2E = math.log2(math.e)
QUERY_SCALE = SM_SCALE * LOG2E

VMEM_LIMIT = 56 * 1024 * 1024


def _params(n_axes):
    return pltpu.CompilerParams(dimension_semantics=("arbitrary",) * n_axes,
                                vmem_limit_bytes=VMEM_LIMIT)


def _cond_row(tok0):
    return jnp.where(tok0 < PROMPT_TOKENS, CTX_COND_ROW, (tok0 - PROMPT_TOKENS) // SAMPLE_SEQ)


def _silu(x):
    return x * (0.5 + 0.5 * jnp.tanh(0.5 * x))


def _gelu_tanh(x):
    return 0.5 * x * (1.0 + jnp.tanh(math.sqrt(2.0 / math.pi) * (x + 0.044715 * (x * x * x))))


def _split_specs(tm, width, tile0=0):
    n_p = PROMPT_TOKENS // tm
    return [pl.BlockSpec((tm, width), lambda i: (jnp.minimum(i + tile0, n_p - 1), 0)),
            pl.BlockSpec((tm, width), lambda i: (jnp.maximum(i + tile0 - n_p, 0), 0))]


def _ada_kernel(cond_ref, w_ref, b_ref, o_ref):
    s = _silu(cond_ref[...]).astype(BF16)
    o_ref[...] = jnp.dot(s, w_ref[...].astype(BF16), preferred_element_type=F32) + b_ref[...]


def _ada(cond, w_ada, b_ada, tn=1024):
    depth = w_ada.shape[0]
    n = w_ada.shape[2]
    return pl.pallas_call(
        _ada_kernel,
        grid=(depth, n // tn),
        in_specs=[pl.BlockSpec((COND_ROWS, D_MODEL), lambda l, j: (0, 0)),
                  pl.BlockSpec((None, D_MODEL, tn), lambda l, j: (l, 0, j)),
                  pl.BlockSpec((None, 1, tn), lambda l, j: (l, 0, j))],
        out_specs=pl.BlockSpec((None, COND_ROWS, tn), lambda l, j: (l, 0, j)),
        out_shape=jax.ShapeDtypeStruct((depth, COND_ROWS, n), F32),
        compiler_params=_params(2),
        name="ada_mod",
    )(cond, w_ada, b_ada.reshape(depth, 1, n))


NORM_ROWS = 64


def _norm_kernel(xp_ref, xs_ref, g_ref, shift_ref, scale_ref, o_ref, *, tm):
    gmul = g_ref[...] * (1.0 + scale_ref[...])
    shift = shift_ref[...]

    def run(x_ref):
        for r in range(tm // NORM_ROWS):
            rows = slice(r * NORM_ROWS, (r + 1) * NORM_ROWS)
            x = x_ref[rows, :]
            rs = lax.rsqrt(jnp.mean(x * x, axis=-1, keepdims=True) + EPS)
            o_ref[rows, :] = (x * rs * gmul + shift).astype(BF16)

    is_prompt = pl.program_id(0) < PROMPT_TOKENS // tm
    pl.when(is_prompt)(lambda: run(xp_ref))
    pl.when(jnp.logical_not(is_prompt))(lambda: run(xs_ref))


def _norm_mod(xp, xs, norm_g3, mod4, layer, tm=1024):
    return pl.pallas_call(
        functools.partial(_norm_kernel, tm=tm),
        grid=(TOKENS // tm,),
        in_specs=[*_split_specs(tm, D_MODEL),
                  pl.BlockSpec((None, 1, D_MODEL), lambda i: (layer, 0, 0)),
                  pl.BlockSpec((None, None, 1, D_MODEL), lambda i: (layer, _cond_row(i * tm), 0, 0)),
                  pl.BlockSpec((None, None, 1, D_MODEL), lambda i: (layer, _cond_row(i * tm), 0, 1))],
        out_specs=pl.BlockSpec((tm, D_MODEL), lambda i: (i, 0)),
        out_shape=jax.ShapeDtypeStruct((TOKENS, D_MODEL), BF16),
        compiler_params=_params(1),
        name="norm_mod",
    )(xp, xs, norm_g3, mod4, mod4)


PROJ_TM = 2048
PROJ_TN = 512
PROJ_TN_WIDE = 1024
PROJ_ROWS = 512
N_PROMPT_TILES = PROMPT_TOKENS // PROJ_TM


class _Segment(NamedTuple):
    col_blk0: int
    n_blk: int
    epilogue: Callable
    by_tile_kind: bool
    outs: tuple
    pair_blk0: int = -1


def _segment_steps(segments, wide):
    spans, j0 = [], 0
    for seg in segments:
        if seg.pair_blk0 >= 0:
            assert wide == 2
            nj = seg.n_blk
        else:
            assert seg.col_blk0 % wide == 0 and seg.n_blk % wide == 0
            nj = seg.n_blk // wide
        spans.append((j0, nj))
        j0 += nj
    return spans


def _proj_kernel(*refs, segments, spans, n_w, n_extra, n_out, cast_steps):
    a_ref, w_refs = refs[0], refs[1:1 + n_w]
    extras = refs[1 + n_w:1 + n_w + n_extra]
    outs = refs[1 + n_w + n_extra:1 + n_w + n_extra + n_out]
    wbf_ref = refs[1 + n_w + n_extra + n_out]

    @pl.when(pl.program_id(1) == 0)
    def _():
        for h, w_ref in enumerate(w_refs):
            wbf_ref[:, h * PROJ_TN:(h + 1) * PROJ_TN] = w_ref[...].astype(BF16)

    if cast_steps:
        step = pl.program_id(0) * pl.num_programs(1) + pl.program_id(1)

        @pl.when(step < cast_steps)
        def _():
            outs[-1][...] = extras[-1][...].astype(BF16)

    def run(seg):
        seg_outs = [outs[k] for k in seg.outs]

        def body(latent):
            for rc in range(PROJ_TM // PROJ_ROWS):
                rows = slice(rc * PROJ_ROWS, (rc + 1) * PROJ_ROWS)
                accs = [jnp.dot(a_ref[rows, :], wbf_ref[:, col0:col0 + PROJ_TN], preferred_element_type=F32)
                        for col0 in range(0, wbf_ref.shape[1], PROJ_TN)]
                if seg.pair_blk0 >= 0:
                    seg.epilogue(*accs, extras, seg_outs, rc, latent)
                else:
                    for c, acc in enumerate(accs):
                        seg.epilogue(acc, extras, seg_outs, rc, c * PROJ_TN, latent)

        if seg.by_tile_kind:
            is_latent = pl.program_id(1) >= N_PROMPT_TILES
            pl.when(is_latent)(lambda: body(True))
            pl.when(jnp.logical_not(is_latent))(lambda: body(False))
        else:
            body(None)

    j = pl.program_id(0)
    for seg, (j0, nj) in zip(segments, spans):
        if len(segments) == 1:
            run(seg)
        else:
            pl.when(jnp.logical_and(j >= j0, j < j0 + nj))(functools.partial(run, seg))


def _proj(a, w, layer, segments, out_shapes, out_specs, name, extras=(), extra_specs=(), tn=PROJ_TN,
          run_after=None, cast=None, in_place=()):
    m, k = a.shape
    if run_after is not None:
        extras = [*extras, run_after]
        extra_specs = [*extra_specs, pl.BlockSpec(memory_space=pl.ANY)]
    wide = tn // PROJ_TN
    aliases = {}
    for array, out_index in in_place:
        aliases[1 + wide + len(extras)] = out_index
        extras = [*extras, array]
        extra_specs = [*extra_specs, pl.BlockSpec(memory_space=pl.ANY)]
    spans = _segment_steps(segments, wide)
    cast_steps = 0
    if cast is not None:
        w_out, out_layer, cast_steps = cast
        n_tiles = m // PROJ_TM
        assert cast_steps <= n_tiles * sum(nj for _, nj in spans)
        w_spec, wbf_spec, wbf_shape = _cast_slab_specs(
            w_out, out_layer, cast_steps, lambda j, i: jnp.minimum(j * n_tiles + i, cast_steps - 1))
        extras, extra_specs = [*extras, w_out], [*extra_specs, w_spec]
        out_shapes, out_specs = [*out_shapes, wbf_shape], [*out_specs, wbf_spec]

    def w_col(j, half):
        col = None
        for seg, (j0, _) in zip(segments, spans):
            if seg.pair_blk0 >= 0:
                c = (seg.pair_blk0 if half else seg.col_blk0) + j - j0
            else:
                c = seg.col_blk0 + wide * (j - j0) + half
            col = c if col is None else jnp.where(j >= j0, c, col)
        return col

    w_specs = [pl.BlockSpec((None, k, PROJ_TN), lambda j, i, half=half: (layer, 0, w_col(j, half)))
               for half in range(wide)]
    kern = functools.partial(_proj_kernel, segments=segments, spans=spans, n_w=wide, n_extra=len(extras),
                             n_out=len(out_shapes), cast_steps=cast_steps)
    return pl.pallas_call(
        kern,
        grid=(sum(nj for _, nj in spans), m // PROJ_TM),
        in_specs=[pl.BlockSpec((PROJ_TM, k), lambda j, i: (i, 0)), *w_specs, *extra_specs],
        out_specs=out_specs,
        out_shape=out_shapes,
        scratch_shapes=[pltpu.VMEM((k, tn), BF16)],
        input_output_aliases=aliases,
        compiler_params=_params(2),
        name=name,
    )(a, *[w] * wide, *extras)


def _full_out(n_blk):
    return ([jax.ShapeDtypeStruct((TOKENS, n_blk * PROJ_TN), BF16)],
            [pl.BlockSpec((PROJ_TM, PROJ_TN_WIDE), lambda j, i: (i, j))])


def _held(span, j, inside, before, after):
    j0, nj = span
    pick = lambda a, b, c: jnp.where(j < j0, b, jnp.where(j >= j0 + nj, c, a))
    return tuple(pick(a, b, c) for a, b, c in zip(inside, before, after))


def _split_out(span, n_blk, head_rows, layers=(1, 0)):
    prompt_tile = lambda i: jnp.minimum(i, N_PROMPT_TILES - 1)
    last_prompt, last_tile = N_PROMPT_TILES - 1, TOKENS // PROJ_TM - 1
    jj = lambda j: j - span[0]
    if head_rows:
        assert n_blk == 1
        heads = PROJ_TN // HEAD_DIM
        n_layers, layer = layers
        cache_shape = jax.ShapeDtypeStruct((N_PROMPT, n_layers * PROMPT_SEQ * heads, HEAD_DIM), F32)
        cache_spec = pl.BlockSpec(
            (PROJ_TM // PROMPT_SEQ, PROMPT_SEQ * heads, HEAD_DIM),
            lambda j, i: _held(span, j, (prompt_tile(i), layer, 0), (0, layer, 0), (last_prompt, layer, 0)))
    else:
        cache_shape = jax.ShapeDtypeStruct((N_PROMPT, PROMPT_SEQ, n_blk * PROJ_TN), F32)
        cache_spec = pl.BlockSpec(
            (PROJ_TM // PROMPT_SEQ, PROMPT_SEQ, PROJ_TN),
            lambda j, i: _held(span, j, (prompt_tile(i), 0, jj(j)), (0, 0, 0), (last_prompt, 0, n_blk - 1)))
    shapes = [cache_shape, jax.ShapeDtypeStruct((TOKENS, n_blk * PROJ_TN), BF16)]
    specs = [cache_spec,
             pl.BlockSpec((PROJ_TM, PROJ_TN),
                          lambda j, i: _held(span, j, (i, jj(j)), (0, 0), (last_tile, n_blk - 1)))]
    return shapes, specs


def _chunk_rows(rc):
    return slice(rc * PROJ_ROWS, (rc + 1) * PROJ_ROWS)


def _epi_silu(acc, extras, outs, rc, col0, latent):
    outs[0][_chunk_rows(rc), col0:col0 + PROJ_TN] = _silu(acc).astype(BF16)


def _epi_gelu(acc, extras, outs, rc, col0, latent):
    outs[0][_chunk_rows(rc), col0:col0 + PROJ_TN] = _gelu_tanh(acc).astype(BF16)


def _store_split(y, outs, rc, sl, latent):
    outs[1][_chunk_rows(rc), sl] = y.astype(BF16)
    if latent:
        return
    seqs = PROJ_ROWS // PROMPT_SEQ
    if outs[0].shape[-1] == HEAD_DIM:
        heads = PROJ_TN // HEAD_DIM
        col0 = sl.start or 0
        for c in range(y.shape[1] // HEAD_DIM):
            head_rows = pl.ds(col0 // HEAD_DIM + c, PROMPT_SEQ, stride=heads)
            for s in range(seqs):
                outs[0][rc * seqs + s, head_rows, :] = y[s * PROMPT_SEQ:(s + 1) * PROMPT_SEQ,
                                                         c * HEAD_DIM:(c + 1) * HEAD_DIM]
    else:
        outs[0][rc * seqs:(rc + 1) * seqs, :, sl] = y.reshape(seqs, PROMPT_SEQ, y.shape[1])


def _epi_value(acc, extras, outs, rc, col0, latent):
    _store_split(acc, outs, rc, slice(col0, col0 + PROJ_TN), latent)


def _head_rmsnorm(acc, hh, g):
    a = acc[:, hh * HEAD_DIM:(hh + 1) * HEAD_DIM]
    return a * lax.rsqrt(jnp.mean(a * a, axis=-1, keepdims=True) + EPS) * g


def _normed_heads(acc, extras, rc, rope):
    g = extras[0][...]
    ys = [_head_rmsnorm(acc, hh, g) for hh in range(PROJ_TN // HEAD_DIM)]
    if rope:
        rows = _chunk_rows(rc)
        cos, sin, swap = extras[1][rows, :], extras[2][rows, :], extras[3][...]
        for pair in range(len(ys) // 2):
            both = jnp.concatenate(ys[2 * pair:2 * pair + 2], axis=1).astype(BF16)
            partner = jnp.dot(both, swap, preferred_element_type=F32)
            for t in range(2):
                hh = 2 * pair + t
                ys[hh] = ys[hh] * cos + partner[:, t * HEAD_DIM:(t + 1) * HEAD_DIM] * sin
    for hh, y in enumerate(ys):
        yield slice(hh * HEAD_DIM, (hh + 1) * HEAD_DIM), y


def _epi_query(acc, extras, outs, rc, col0, latent, *, rope):
    for sl, y in _normed_heads(acc, extras, rc, rope and latent):
        outs[0][_chunk_rows(rc), col0 + sl.start:col0 + sl.stop] = (y * QUERY_SCALE).astype(BF16)


def _epi_key(acc, extras, outs, rc, col0, latent, *, rope):
    for sl, y in _normed_heads(acc, extras, rc, rope and latent):
        _store_split(y, outs, rc, slice(col0 + sl.start, col0 + sl.stop), latent)


def _rope_tables():
    nf = HEAD_DIM // 4
    t = np.arange(SAMPLE_TOKENS) % SAMPLE_SEQ
    row = (t // GRID_W).astype(np.float32)
    col = (t % GRID_W).astype(np.float32)
    inv = np.float32(ROPE_THETA) ** (-np.arange(nf, dtype=np.float32) / np.float32(nf))
    ang_r = row[:, None] * inv
    ang_c = col[:, None] * inv
    cos = np.concatenate([np.cos(ang_r), np.cos(ang_r), np.cos(ang_c), np.cos(ang_c)], axis=1)
    sin = np.concatenate([-np.sin(ang_r), np.sin(ang_r), -np.sin(ang_c), np.sin(ang_c)], axis=1)
    lanes = np.arange(2 * HEAD_DIM)
    swap = lanes[:, None] == (lanes[None, :] ^ (HEAD_DIM // 4))
    return jnp.asarray(cos, F32), jnp.asarray(sin, F32), jnp.asarray(swap, BF16)


def _gain_extras(gain, rope_tabs):
    extras = [gain.reshape(1, HEAD_DIM)]
    specs = [pl.BlockSpec((1, HEAD_DIM), lambda j, i: (0, 0))]
    if rope_tabs is not None:
        tab_map = lambda j, i: (jnp.maximum(i - N_PROMPT_TILES, 0), 0)
        extras += list(rope_tabs)
        specs += [pl.BlockSpec((PROJ_TM, HEAD_DIM), tab_map)] * 2
        specs.append(pl.BlockSpec((2 * HEAD_DIM, 2 * HEAD_DIM), lambda j, i: (0, 0)))
    return extras, specs


def _qz_proj(hn, w, layer, q_blk0, z_blk0, n_blk, gain, rope_tabs, name):
    extras, specs = _gain_extras(gain, rope_tabs)
    rope = rope_tabs is not None
    segments = (_Segment(q_blk0, n_blk, functools.partial(_epi_query, rope=rope), rope, (0,)),
                _Segment(z_blk0, n_blk, _epi_silu, False, (0,)))
    return _proj(hn, w, layer, segments, *_full_out(2 * n_blk), name, extras, specs, tn=PROJ_TN_WIDE)[0]


def _kv_proj(hn, w, layer, k_blk0, n_blk, gain, rope_tabs, head_rows, name, run_after=None, caches=None):
    n_layers = 1 if caches is None else caches[0].shape[1] // (PROMPT_SEQ * n_blk * PROJ_TN // HEAD_DIM)
    extras, specs = _gain_extras(gain, rope_tabs)
    segments = (_Segment(k_blk0, n_blk, functools.partial(_epi_key, rope=rope_tabs is not None), True, (0, 1)),
                _Segment(k_blk0 + n_blk, n_blk, _epi_value, True, (2, 3)))
    shapes, out_specs = [], []
    for span in _segment_steps(segments, 1):
        seg_shapes, seg_specs = _split_out(span, n_blk, head_rows, (n_layers, layer))
        shapes += seg_shapes
        out_specs += seg_specs
    in_place = () if caches is None else ((caches[0], 0), (caches[1], 2))
    return _proj(hn, w, layer, segments, shapes, out_specs, name, extras, specs, run_after=run_after,
                 in_place=in_place)


UVZ_CAST_STEPS = 32


def _epi_uz(acc_u, acc_z, extras, outs, rc, latent):
    outs[0][_chunk_rows(rc), :] = (_gelu_tanh(acc_u) * _silu(acc_z)).astype(BF16)


def _uzv_proj(hn, w, layer, n_blk, name, w_out):
    segments = (_Segment(0, n_blk, _epi_uz, False, (0,), pair_blk0=2 * n_blk),
                _Segment(n_blk, n_blk, _epi_gelu, False, (1,)))
    (uz_span, v_span) = _segment_steps(segments, PROJ_TN_WIDE // PROJ_TN)
    last_tile = TOKENS // PROJ_TM - 1
    shape = jax.ShapeDtypeStruct((TOKENS, n_blk * PROJ_TN), BF16)
    out_specs = [
        pl.BlockSpec((PROJ_TM, PROJ_TN),
                     lambda j, i: _held(uz_span, j, (i, j - uz_span[0]), (0, 0), (last_tile, uz_span[1] - 1))),
        pl.BlockSpec((PROJ_TM, PROJ_TN_WIDE),
                     lambda j, i: _held(v_span, j, (i, j - v_span[0]), (0, 0), (last_tile, v_span[1] - 1)))]
    return _proj(hn, w, layer, segments, [shape, shape], out_specs, name, tn=PROJ_TN_WIDE,
                 cast=(w_out, layer, UVZ_CAST_STEPS))


OUT_COLS = 512
OUT_TM = 512
OUT_TM_WIDE_K = 256


def _cast_slab_specs(w, layer, n_slabs, slab_of_step):
    k, n = w.shape[1], w.shape[2]
    rows = k // n_slabs
    return (pl.BlockSpec((None, rows, n), lambda *g: (layer, slab_of_step(*g), 0)),
            pl.BlockSpec((rows, n), lambda *g: (slab_of_step(*g), 0)),
            jax.ShapeDtypeStruct((k, n), BF16))


def _out_kernel(*refs, tile0, tm, split_a, split_x, fuse_norm):
    it = iter(refs)
    a_refs = [next(it), next(it)] if split_a else [next(it)]
    wbf_ref = next(it)
    x_refs = [next(it), next(it)] if split_x else [next(it)]
    gate_ref = next(it)
    if fuse_norm:
        g_ref, shift_ref, scale_ref = next(it), next(it), next(it)
    xnew_ref = next(it)
    hn_ref = next(it) if fuse_norm else None

    def body(a_ref, x_ref):
        ssq = jnp.zeros((tm, 1), F32)
        for cb in range(D_MODEL // OUT_COLS):
            sl = slice(cb * OUT_COLS, (cb + 1) * OUT_COLS)
            acc = jnp.dot(a_ref[...], wbf_ref[:, sl], preferred_element_type=F32)
            xn = x_ref[:, sl] + gate_ref[:, sl] * acc
            xnew_ref[:, sl] = xn
            if fuse_norm:
                ssq = ssq + jnp.sum(xn * xn, axis=-1, keepdims=True)
        if fuse_norm:
            rs = lax.rsqrt(ssq * (1.0 / D_MODEL) + EPS)
            for cb in range(D_MODEL // OUT_COLS):
                sl = slice(cb * OUT_COLS, (cb + 1) * OUT_COLS)
                gmul = g_ref[:, sl] * (1.0 + scale_ref[:, sl])
                hn_ref[:, sl] = (xnew_ref[:, sl] * rs * gmul + shift_ref[:, sl]).astype(BF16)

    if split_a or split_x:
        is_prompt = pl.program_id(0) + tile0 < PROMPT_TOKENS // tm
        pl.when(is_prompt)(lambda: body(a_refs[0], x_refs[0]))
        pl.when(jnp.logical_not(is_prompt))(lambda: body(a_refs[-1], x_refs[-1]))
    else:
        body(a_refs[0], x_refs[0])


def _out_proj(a, wbf, x, mod4, layer, norm_g3, name, tm, rows=None):
    split_a = isinstance(a, tuple)
    k = a[0].shape[1] if split_a else a.shape[1]
    tok0, n_tok = rows if rows is not None else (0, TOKENS)
    tile0 = tok0 // tm
    split_x = isinstance(x, tuple)
    fuse_norm = norm_g3 is not None
    mod_spec = lambda part, lyr: pl.BlockSpec(
        (None, None, 1, D_MODEL), lambda i: (lyr, _cond_row((i + tile0) * tm), 0, part))
    row_spec = lambda width: pl.BlockSpec((tm, width), lambda i: (i + tile0, 0))
    operands = [*a, wbf] if split_a else [a, wbf]
    in_specs = [*(_split_specs(tm, k, tile0) if split_a else [row_spec(k)]),
                pl.BlockSpec((k, D_MODEL), lambda i: (0, 0))]
    if split_x:
        assert rows is None
        operands += list(x)
        in_specs += _split_specs(tm, D_MODEL)
    else:
        operands.append(x)
        in_specs.append(row_spec(D_MODEL))
    operands.append(mod4)
    in_specs.append(mod_spec(2, layer))
    out_shapes = [jax.ShapeDtypeStruct((n_tok, D_MODEL), F32)]
    out_specs = [pl.BlockSpec((tm, D_MODEL), lambda i: (i, 0))]
    if fuse_norm:
        operands += [norm_g3, mod4, mod4]
        in_specs += [pl.BlockSpec((None, 1, D_MODEL), lambda i: (layer + 1, 0, 0)),
                     mod_spec(0, layer + 1), mod_spec(1, layer + 1)]
        out_shapes.append(jax.ShapeDtypeStruct((n_tok, D_MODEL), BF16))
        out_specs.append(pl.BlockSpec((tm, D_MODEL), lambda i: (i, 0)))
    kern = functools.partial(_out_kernel, tile0=tile0, tm=tm,
                             split_a=split_a, split_x=split_x, fuse_norm=fuse_norm)
    return pl.pallas_call(
        kern,
        grid=(n_tok // tm,),
        in_specs=in_specs,
        out_specs=out_specs,
        out_shape=out_shapes,
        compiler_params=_params(1),
        name=name,
    )(*operands)


def _dot_nt(a, b):
    return lax.dot_general(a, b, (((1,), (1,)), ((), ())), preferred_element_type=F32)


def _head(ref, h, rows=slice(None)):
    return ref[rows, h * HEAD_DIM:(h + 1) * HEAD_DIM]


def _with_ones(v):
    return jnp.concatenate([v, jnp.ones(v.shape, v.dtype)], axis=1)


def _lane_chunks(s):
    return [s[:, c * HEAD_DIM:(c + 1) * HEAD_DIM] for c in range(s.shape[1] // HEAD_DIM)]


def _softmax_pv(score_blocks, value_blocks, sink):
    rows = score_blocks[0].shape[0]
    mx = functools.reduce(jnp.maximum, [c for s in score_blocks for c in _lane_chunks(s)])
    m = jnp.broadcast_to(jnp.max(mx, axis=-1, keepdims=True), (rows, HEAD_DIM))
    if sink is not None:
        m = jnp.maximum(m, sink)
    o = None
    for s, v in zip(score_blocks, value_blocks):
        p = jnp.concatenate([jnp.exp2(c - m) for c in _lane_chunks(s)], axis=1).astype(BF16)
        part = jnp.dot(p, v, preferred_element_type=F32)
        o = part if o is None else o + part
    den = o[:, HEAD_DIM:]
    if sink is not None:
        den = den + jnp.exp2(sink - m)
    return o[:, :HEAD_DIM] * (1.0 / den)


PROMPT_REQUESTS = 2


def _attn_prompt_kernel(sink_ref, q_ref, k_ref, v_ref, z_ref, *rest, n_heads, n_kv, use_sink):
    o_ref = rest[-1]
    grp = n_heads // n_kv
    for r in range(PROMPT_REQUESTS):
        rows = slice(r * PROMPT_SEQ, (r + 1) * PROMPT_SEQ)
        for g in range(n_kv):
            kg = _head(k_ref, g, rows)
            vg = _with_ones(_head(v_ref, g, rows))
            for h in range(g * grp, (g + 1) * grp):
                s = _dot_nt(_head(q_ref, h, rows), kg)
                sink = jnp.full((PROMPT_SEQ, HEAD_DIM), sink_ref[h], F32) if use_sink else None
                o = _softmax_pv([s], [vg], sink)
                o_ref[rows, h * HEAD_DIM:(h + 1) * HEAD_DIM] = (
                    o * _head(z_ref, h, rows).astype(F32)).astype(BF16)


def _attn_prompt(sink, qz, k, v, n_kv, use_sink, name, run_after=None):
    ordering = [] if run_after is None else [run_after]
    width = qz.shape[1] // 2
    kern = functools.partial(_attn_prompt_kernel, n_heads=width // HEAD_DIM, n_kv=n_kv, use_sink=use_sink)
    rows = PROMPT_REQUESTS * PROMPT_SEQ
    cache_spec = pl.BlockSpec((rows, k.shape[1]), lambda b: (b, 0))
    return pl.pallas_call(
        kern,
        grid=(N_PROMPT // PROMPT_REQUESTS,),
        in_specs=[pl.BlockSpec(memory_space=pltpu.SMEM),
                  pl.BlockSpec((rows, width), lambda b: (b, 0)),
                  cache_spec, cache_spec,
                  pl.BlockSpec((rows, width), lambda b: (b, 1)),
                  *[pl.BlockSpec(memory_space=pl.ANY) for _ in ordering]],
        out_specs=pl.BlockSpec((rows, width), lambda b: (b, 0)),
        out_shape=jax.ShapeDtypeStruct((PROMPT_TOKENS, width), BF16),
        compiler_params=_params(1),
        name=name,
    )(sink, qz, k, v, qz, *ordering)


WIN_BAND = 3 * WINDOW


def _attn_win_kernel(sink_ref, q_ref, k_ref, v_ref, kc_ref, vc_ref, z_ref, w_ref, o_ref, wbf_ref, *,
                     n_heads, n_kv):
    wbf_ref[...] = w_ref[...].astype(BF16)
    grp = n_heads // n_kv
    n = pl.program_id(1)
    start = pl.multiple_of(jnp.clip((n - 1) * WINDOW, 0, SAMPLE_SEQ - WIN_BAND), WINDOW)
    shape = (grp * WINDOW, WIN_BAND)
    qpos = n * WINDOW + (lax.broadcasted_iota(jnp.int32, shape, 0) & (WINDOW - 1))
    kpos = start + lax.broadcasted_iota(jnp.int32, shape, 1)
    valid = jnp.abs(kpos - qpos) <= WINDOW
    band = pl.ds(start, WIN_BAND)
    for g in range(n_kv):
        heads = [g * grp + t for t in range(grp)]
        qs = jnp.concatenate([_head(q_ref, h) for h in heads], axis=0)
        kb = _head(k_ref, g, band)
        vb = _with_ones(_head(v_ref, g, band))
        ctx_rows = pl.ds(g, PAST_LEN, stride=n_kv)
        kc = kc_ref[ctx_rows, :].astype(BF16)
        vc = _with_ones(vc_ref[ctx_rows, :].astype(BF16))
        s_band = jnp.where(valid, _dot_nt(qs, kb), NEG_INF)
        s_ctx = _dot_nt(qs, kc)
        sink = jnp.concatenate([jnp.full((WINDOW, HEAD_DIM), sink_ref[h], F32) for h in heads], axis=0)
        o = _softmax_pv([s_band, s_ctx], [vb, vc], sink)
        for t, h in enumerate(heads):
            oh = o[t * WINDOW:(t + 1) * WINDOW]
            o_ref[:, h * HEAD_DIM:(h + 1) * HEAD_DIM] = (oh * _head(z_ref, h).astype(F32)).astype(BF16)


def _attn_win(sink, qz, k, v, cache_k, cache_v, layer_in_kind, n_kv, w_out):
    width = qz.shape[1] // 2
    kv_width = k.shape[1]
    blocks_per_seq = SAMPLE_SEQ // WINDOW
    q_map = lambda b, n: (PROMPT_TOKENS // WINDOW + b * blocks_per_seq + n, 0)
    z_map = lambda b, n: (PROMPT_TOKENS // WINDOW + b * blocks_per_seq + n, 1)
    kv_map = lambda b, n: (PROMPT_TOKENS // SAMPLE_SEQ + b, 0)
    cache_map = lambda b, n: (b, layer_in_kind, 0, 0)
    kern = functools.partial(_attn_win_kernel, n_heads=width // HEAD_DIM, n_kv=n_kv)
    w_spec, wbf_spec, wbf_shape = _cast_slab_specs(w_out, layer_in_kind, N_SAMPLE * blocks_per_seq,
                                                   lambda b, n: b * blocks_per_seq + n)
    return pl.pallas_call(
        kern,
        grid=(N_SAMPLE, blocks_per_seq),
        in_specs=[pl.BlockSpec(memory_space=pltpu.SMEM),
                  pl.BlockSpec((WINDOW, width), q_map),
                  pl.BlockSpec((SAMPLE_SEQ, kv_width), kv_map),
                  pl.BlockSpec((SAMPLE_SEQ, kv_width), kv_map),
                  pl.BlockSpec((None, None, PAST_LEN * n_kv, HEAD_DIM), cache_map),
                  pl.BlockSpec((None, None, PAST_LEN * n_kv, HEAD_DIM), cache_map),
                  pl.BlockSpec((WINDOW, width), z_map),
                  w_spec],
        out_specs=[pl.BlockSpec((WINDOW, width), lambda b, n: (b * blocks_per_seq + n, 0)), wbf_spec],
        out_shape=[jax.ShapeDtypeStruct((SAMPLE_TOKENS, width), BF16), wbf_shape],
        compiler_params=_params(2),
        name="attn_win_latent",
    )(sink, qz, k, v, cache_k, cache_v, qz, w_out)


NAT_QROWS = 4
GRID_ROWS = SAMPLE_SEQ // GRID_W
N_DR = 2 * NAT_ROWS - 1
MASKED_TILE = N_DR


def _nat_row_start(qr):
    return min(max(qr - NAT_ROWS // 2, 0), GRID_ROWS - NAT_ROWS)


def _attn_nat_kernel(bias_ref, q_ref, k_ref, v_ref, kc_ref, vc_ref, z_ref, w_ref, o_ref, wbf_ref,
                     left_ref, right_ref):
    wbf_ref[...] = w_ref[...].astype(BF16)

    def build_bias_tiles():
        shape = (GRID_W, 2 * GRID_W)
        lane = lax.broadcasted_iota(jnp.int32, shape, 1)
        qc = lax.broadcasted_iota(jnp.int32, shape, 0)
        kc_ = lane & (GRID_W - 1)
        cs = jnp.clip(qc - NAT_COLS // 2, 0, GRID_W - NAT_COLS)
        col_ok = jnp.logical_and(kc_ >= cs, kc_ < cs + NAT_COLS)
        is_left = lane < GRID_W
        lanes = 2 * GRID_W
        for dri in range(N_DR):
            row = jnp.broadcast_to(bias_ref[dri:dri + 1, :], shape)
            on_left = pltpu.roll(row, lanes - (NAT_COLS - 1), 1, stride=1, stride_axis=0)
            on_right = pltpu.roll(row, GRID_W - (NAT_COLS - 1), 1, stride=1, stride_axis=0)
            left_ref[dri] = jnp.where(is_left, jnp.where(col_ok, on_left, NEG_INF), 0.0)
            right_ref[dri] = jnp.where(is_left, 0.0, jnp.where(col_ok, on_right, NEG_INF))
        left_ref[MASKED_TILE] = jnp.where(is_left, NEG_INF, 0.0)
        right_ref[MASKED_TILE] = jnp.where(is_left, 0.0, NEG_INF)

    build_bias_tiles()
    n_heads = kc_ref.shape[1] // PAST_LEN
    ctx_rows = pl.ds(pl.program_id(0), PAST_LEN, stride=n_heads)
    for b, qb in [(b, qb) for b in range(N_SAMPLE) for qb in range(GRID_ROWS // NAT_QROWS)]:
        if qb == 0:
            kc = kc_ref[b, ctx_rows, :].astype(BF16)
            vc = _with_ones(vc_ref[b, ctx_rows, :].astype(BF16))
        qrows = range(qb * NAT_QROWS, (qb + 1) * NAT_QROWS)
        krow0 = _nat_row_start(qrows[0]) // 2 * 2
        krow1 = -(-(_nat_row_start(qrows[-1]) + NAT_ROWS) // 2) * 2
        tok0 = b * SAMPLE_SEQ
        kwin = slice(tok0 + krow0 * GRID_W, tok0 + krow1 * GRID_W)
        qwin = slice(tok0 + qrows[0] * GRID_W, tok0 + (qrows[-1] + 1) * GRID_W)

        def tile_index(qr, kr):
            rs = _nat_row_start(qr)
            return kr - qr + (NAT_ROWS - 1) if rs <= kr < rs + NAT_ROWS else MASKED_TILE

        bias = jnp.concatenate(
            [jnp.concatenate([left_ref[tile_index(qr, kr)] + right_ref[tile_index(qr, kr + 1)]
                              for kr in range(krow0, krow1, 2)], axis=1)
             for qr in qrows], axis=0)

        q = q_ref[qwin, :]
        s_nb = _dot_nt(q, k_ref[kwin, :]) + bias
        s_ctx = _dot_nt(q, kc)
        o = _softmax_pv([s_nb, s_ctx], [_with_ones(v_ref[kwin, :]), vc], None)
        o_ref[qwin, :] = (o * z_ref[qwin, :].astype(F32)).astype(BF16)


def _attn_nat(rel_bias, qz, k, v, cache_k, cache_v, layer_in_kind, w_out):
    n_heads = qz.shape[1] // (2 * HEAD_DIM)
    bias_rows = jnp.pad(rel_bias, ((0, 0), (0, N_DR + 1 - rel_bias.shape[1]),
                                   (0, 2 * GRID_W - rel_bias.shape[2])))
    latent_tile = PROMPT_TOKENS // SAMPLE_TOKENS
    latent_spec = pl.BlockSpec((SAMPLE_TOKENS, HEAD_DIM), lambda h: (latent_tile, h))
    z_spec = pl.BlockSpec((SAMPLE_TOKENS, HEAD_DIM), lambda h: (latent_tile, n_heads + h))
    cache_spec = pl.BlockSpec((N_SAMPLE, None, PAST_LEN * n_heads, HEAD_DIM),
                              lambda h: (0, layer_in_kind, 0, 0))
    w_spec, wbf_spec, wbf_shape = _cast_slab_specs(w_out, layer_in_kind, n_heads, lambda h: h)
    return pl.pallas_call(
        _attn_nat_kernel,
        grid=(n_heads,),
        in_specs=[pl.BlockSpec((None, N_DR + 1, 2 * GRID_W), lambda h: (h, 0, 0)),
                  latent_spec, latent_spec, latent_spec, cache_spec, cache_spec, z_spec, w_spec],
        out_specs=[pl.BlockSpec((SAMPLE_TOKENS, HEAD_DIM), lambda h: (0, h)), wbf_spec],
        out_shape=[jax.ShapeDtypeStruct((SAMPLE_TOKENS, n_heads * HEAD_DIM), BF16), wbf_shape],
        scratch_shapes=[pltpu.VMEM((N_DR + 1, GRID_W, 2 * GRID_W), F32),
                        pltpu.VMEM((N_DR + 1, GRID_W, 2 * GRID_W), F32)],
        compiler_params=_params(1),
        name="attn_nat_latent",
    )(bias_rows, qz, k, v, cache_k, cache_v, qz, w_out)


SPATIAL_TOKENS = 4 * CHUNK


def _spatial_kernel(uz_ref, v_ref, g_ref, b_ref, ws_ref, bs_ref, o_ref):
    for c in range(SPATIAL_TOKENS // CHUNK):
        rows = slice(c * CHUNK, (c + 1) * CHUNK)
        v = v_ref[rows, :].astype(F32)
        mu = jnp.mean(v, axis=-1, keepdims=True)
        vc = v - mu
        var = jnp.mean(vc * vc, axis=-1, keepdims=True)
        vn = (vc * lax.rsqrt(var + EPS) * g_ref[...] + b_ref[...]).astype(BF16)
        for g in range(GMLP_GROUPS):
            sl = slice(g * GMLP_GROUP_WIDTH, (g + 1) * GMLP_GROUP_WIDTH)
            sv = jnp.dot(ws_ref[g], vn[:, sl], preferred_element_type=F32) + bs_ref[:, g:g + 1]
            o_ref[rows, sl] = uz_ref[rows, sl] * sv.astype(BF16)


def _spatial(uz, v, ln_g, ln_b, w_s, b_s):
    row = pl.BlockSpec((SPATIAL_TOKENS, GMLP_WIDTH), lambda i: (i, 0))
    vec = pl.BlockSpec((1, GMLP_WIDTH), lambda i: (0, 0))
    return pl.pallas_call(
        _spatial_kernel,
        grid=(TOKENS // SPATIAL_TOKENS,),
        in_specs=[row, row, vec, vec,
                  pl.BlockSpec((GMLP_GROUPS, CHUNK, CHUNK), lambda i: (0, 0, 0)),
                  pl.BlockSpec((CHUNK, GMLP_GROUPS), lambda i: (0, 0))],
        out_specs=row,
        out_shape=jax.ShapeDtypeStruct((TOKENS, GMLP_WIDTH), BF16),
        compiler_params=_params(1),
        name="gmlp_spatial",
    )(uz, v, ln_g.reshape(1, -1), ln_b.reshape(1, -1), w_s.astype(BF16), b_s.T)


def kernel(x_prompt, x_sample, cache_win_k, cache_win_v, cache_nat_k, cache_nat_v, c, c_ctx,
           norm_g, w_ada, b_ada,
           win_w_in, win_q_norm, win_k_norm, win_sink, win_w_out,
           nat_w_in, nat_q_norm, nat_k_norm, nat_rel_bias, nat_w_out,
           gmlp_w_in, gmlp_ln_g, gmlp_ln_b, gmlp_w_s, gmlp_b_s, gmlp_w_out):
    depth = norm_g.shape[0]
    xp = x_prompt.reshape(PROMPT_TOKENS, D_MODEL)
    xs = x_sample.reshape(SAMPLE_TOKENS, D_MODEL)
    cond = jnp.zeros((COND_ROWS, D_MODEL), F32).at[:N_SAMPLE].set(c).at[CTX_COND_ROW].set(c_ctx)
    mod4 = _ada(cond, w_ada, b_ada).reshape(depth, COND_ROWS, 1, 3 * D_MODEL)
    norm_g3 = norm_g.reshape(depth, 1, D_MODEL)
    rope_tabs = _rope_tables()
    no_sink = jnp.zeros((1,), F32)

    n_win = win_w_in.shape[0]
    n_nat = nat_w_in.shape[0]
    win_kv_heads = cache_win_k.shape[3]
    win_kv_width = win_kv_heads * HEAD_DIM
    cwk = cache_win_k.reshape(N_SAMPLE, n_win, PAST_LEN * win_kv_heads, HEAD_DIM)
    cwv = cache_win_v.reshape(N_SAMPLE, n_win, PAST_LEN * win_kv_heads, HEAD_DIM)
    nat_heads = cache_nat_k.shape[3]
    cnk = cache_nat_k.reshape(N_SAMPLE, n_nat, PAST_LEN * nat_heads, HEAD_DIM)
    cnv = cache_nat_v.reshape(N_SAMPLE, n_nat, PAST_LEN * nat_heads, HEAD_DIM)

    new_nat_k, new_nat_v = [], []
    win_cache = jax.ShapeDtypeStruct((N_PROMPT, n_win * PROMPT_SEQ * win_kv_heads, HEAD_DIM), F32)
    new_win_k = jnp.zeros(win_cache.shape, win_cache.dtype)
    new_win_v = jnp.zeros(win_cache.shape, win_cache.dtype)
    x = (xp, xs)
    hn = _norm_mod(xp, xs, norm_g3, mod4, 0)
    for layer in range(depth):
        kind = layer % 3
        li = layer // 3
        if kind == 0:
            nq = D_MODEL // PROJ_TN
            nkv = win_kv_width // PROJ_TN
            qz = _qz_proj(hn, win_w_in, li, 0, nq + 2 * nkv, nq, win_q_norm[li], rope_tabs, "win_qz")
            new_win_k, ks, new_win_v, vs = _kv_proj(hn, win_w_in, li, nq, nkv, win_k_norm[li], rope_tabs, True,
                                                    "win_kv", run_after=qz, caches=(new_win_k, new_win_v))
            sink = win_sink[li] * LOG2E
            a_latent, wbf = _attn_win(sink, qz, ks, vs, cwk, cwv, li, win_kv_heads, win_w_out)
            a = (_attn_prompt(sink, qz, ks, vs, win_kv_heads, True, "attn_win_prompt"), a_latent)
            out_tm, out_name = OUT_TM, "win_out"
        elif kind == 1:
            nq = D_MODEL // PROJ_TN
            qz = _qz_proj(hn, nat_w_in, li, 0, 3 * nq, nq, nat_q_norm[li], None, "nat_qz")
            kp, ks, vp, vs = _kv_proj(hn, nat_w_in, li, nq, nq, nat_k_norm[li], None, False, "nat_kv",
                                      run_after=qz)
            new_nat_k.append(kp)
            new_nat_v.append(vp)
            a_latent, wbf = _attn_nat(nat_rel_bias[li] * LOG2E, qz, ks, vs, cnk, cnv, li, nat_w_out)
            a = (_attn_prompt(no_sink, qz, ks, vs, nat_heads, False, "attn_nat_prompt", run_after=a_latent),
                 a_latent)
            out_tm, out_name = OUT_TM, "nat_out"
        else:
            nw = GMLP_WIDTH // PROJ_TN
            uz, v, wbf = _uzv_proj(hn, gmlp_w_in, li, nw, "gmlp_uzv", gmlp_w_out)
            a = _spatial(uz, v, gmlp_ln_g[li], gmlp_ln_b[li], gmlp_w_s[li], gmlp_b_s[li])
            out_tm, out_name = OUT_TM_WIDE_K, "gmlp_out"
        if layer + 1 < depth:
            x, hn = _out_proj(a, wbf, x, mod4, layer, norm_g3, out_name, out_tm)
        else:
            (yp,) = _out_proj(a, wbf, x, mod4, layer, None, out_name + "_prompt", out_tm,
                              rows=(0, PROMPT_TOKENS))
            (ys,) = _out_proj(a, wbf, x, mod4, layer, None, out_name + "_latent", out_tm,
                              rows=(PROMPT_TOKENS, SAMPLE_TOKENS))

    cache_shape = lambda layers, heads: jnp.stack(
        [c_.reshape(N_PROMPT, PROMPT_SEQ, heads, HEAD_DIM) for c_ in layers], axis=1)
    win_shape = (N_PROMPT, n_win, PROMPT_SEQ, win_kv_heads, HEAD_DIM)
    return (yp.reshape(N_PROMPT, PROMPT_SEQ, D_MODEL), ys.reshape(N_SAMPLE, SAMPLE_SEQ, D_MODEL),
            new_win_k.reshape(win_shape), new_win_v.reshape(win_shape),
            cache_shape(new_nat_k, nat_heads), cache_shape(new_nat_v, nat_heads))
```

```python
import functools
import math
from typing import Callable, NamedTuple

import jax
import jax.numpy as jnp
import numpy as np
from jax import lax
from jax.experimental import pallas as pl
from jax.experimental.pallas import tpu as pltpu

F32 = jnp.float32
BF16 = jnp.bfloat16

D_MODEL = 2048
HEAD_DIM = 128
N_PROMPT = 16
PROMPT_SEQ = 256
N_SAMPLE = 2
SAMPLE_SEQ = 1024
PROMPT_TOKENS = N_PROMPT * PROMPT_SEQ
SAMPLE_TOKENS = N_SAMPLE * SAMPLE_SEQ
TOKENS = PROMPT_TOKENS + SAMPLE_TOKENS
PAST_LEN = 512
GRID_W = 64
EPS = 1e-6
NEG_INF = -1e30
ROPE_THETA = 10000.0
WINDOW = 128
NAT_ROWS = 8
NAT_COLS = 16
GMLP_WIDTH = 2 * D_MODEL
GMLP_GROUPS = 16
GMLP_GROUP_WIDTH = GMLP_WIDTH // GMLP_GROUPS
CHUNK = 128
CTX_COND_ROW = 2
COND_ROWS = 8
SM_SCALE = HEAD_DIM ** -0.5
LOG2E = math.log2(math.e)
QUERY_SCALE = SM_SCALE * LOG2E

VMEM_LIMIT = 56 * 1024 * 1024


def _params(n_axes):
    return pltpu.CompilerParams(dimension_semantics=("arbitrary",) * n_axes,
                                vmem_limit_bytes=VMEM_LIMIT)


def _cond_row(tok0):
    return jnp.where(tok0 < PROMPT_TOKENS, CTX_COND_ROW, (tok0 - PROMPT_TOKENS) // SAMPLE_SEQ)


def _silu(x):
    return x * (0.5 + 0.5 * jnp.tanh(0.5 * x))


def _gelu_tanh(x):
    return 0.5 * x * (1.0 + jnp.tanh(math.sqrt(2.0 / math.pi) * (x + 0.044715 * (x * x * x))))


def _split_specs(tm, width, tile0=0):
    n_p = PROMPT_TOKENS // tm
    return [pl.BlockSpec((tm, width), lambda i: (jnp.minimum(i + tile0, n_p - 1), 0)),
            pl.BlockSpec((tm, width), lambda i: (jnp.maximum(i + tile0 - n_p, 0), 0))]


ADA_BUFFERS = 3


def _ada_kernel(cond_ref, w_hbm, b_ref, o_ref, buf, sem, *, tn, n_blk, n_steps):
    step = pl.program_id(0) * n_blk + pl.program_id(1)

    def copy(t):
        if isinstance(t, int):
            layer, blk, slot = t // n_blk, t % n_blk, t % ADA_BUFFERS
        else:
            layer, blk, slot = lax.div(t, n_blk), lax.rem(t, n_blk), lax.rem(t, ADA_BUFFERS)
        return pltpu.make_async_copy(w_hbm.at[layer, :, pl.ds(blk * tn, tn)],
                                     buf.at[slot], sem.at[slot])

    @pl.when(step == 0)
    def _():
        for t in range(ADA_BUFFERS - 1):
            copy(t).start()

    ahead = step + (ADA_BUFFERS - 1)
    pl.when(ahead < n_steps)(lambda: copy(ahead).start())

    copy(step).wait()
    s = _silu(cond_ref[...]).astype(BF16)
    w = buf[lax.rem(step, ADA_BUFFERS)].astype(BF16)
    o_ref[...] = jnp.dot(s, w, preferred_element_type=F32) + b_ref[...]


def _ada(cond, w_ada, b_ada, tn=1024):
    depth = w_ada.shape[0]
    n = w_ada.shape[2]
    n_blk = n // tn
    assert depth * n_blk >= ADA_BUFFERS
    return pl.pallas_call(
        functools.partial(_ada_kernel, tn=tn, n_blk=n_blk, n_steps=depth * n_blk),
        grid=(depth, n_blk),
        in_specs=[pl.BlockSpec((COND_ROWS, D_MODEL), lambda l, j: (0, 0)),
                  pl.BlockSpec(memory_space=pl.ANY),
                  pl.BlockSpec((None, 1, tn), lambda l, j: (l, 0, j))],
        out_specs=pl.BlockSpec((None, COND_ROWS, tn), lambda l, j: (l, 0, j)),
        out_shape=jax.ShapeDtypeStruct((depth, COND_ROWS, n), F32),
        scratch_shapes=[pltpu.VMEM((ADA_BUFFERS, D_MODEL, tn), F32),
                        pltpu.SemaphoreType.DMA((ADA_BUFFERS,))],
        compiler_params=_params(2),
        name="ada_mod",
    )(cond, w_ada, b_ada.reshape(depth, 1, n))


NORM_ROWS = 64


def _norm_kernel(xp_ref, xs_ref, g_ref, shift_ref, scale_ref, o_ref, *, tm):
    gmul = g_ref[...] * (1.0 + scale_ref[...])
    shift = shift_ref[...]

    def run(x_ref):
        for r in range(tm // NORM_ROWS):
            rows = slice(r * NORM_ROWS, (r + 1) * NORM_ROWS)
            x = x_ref[rows, :]
            rs = lax.rsqrt(jnp.mean(x * x, axis=-1, keepdims=True) + EPS)
            o_ref[rows, :] = (x * rs * gmul + shift).astype(BF16)

    is_prompt = pl.program_id(0) < PROMPT_TOKENS // tm
    pl.when(is_prompt)(lambda: run(xp_ref))
    pl.when(jnp.logical_not(is_prompt))(lambda: run(xs_ref))


def _norm_mod(xp, xs, norm_g3, mod4, layer, tm=1024):
    return pl.pallas_call(
        functools.partial(_norm_kernel, tm=tm),
        grid=(TOKENS // tm,),
        in_specs=[*_split_specs(tm, D_MODEL),
                  pl.BlockSpec((None, 1, D_MODEL), lambda i: (layer, 0, 0)),
                  pl.BlockSpec((None, None, 1, D_MODEL), lambda i: (layer, _cond_row(i * tm), 0, 0)),
                  pl.BlockSpec((None, None, 1, D_MODEL), lambda i: (layer, _cond_row(i * tm), 0, 1))],
        out_specs=pl.BlockSpec((tm, D_MODEL), lambda i: (i, 0)),
        out_shape=jax.ShapeDtypeStruct((TOKENS, D_MODEL), BF16),
        compiler_params=_params(1),
        name="norm_mod",
    )(xp, xs, norm_g3, mod4, mod4)


PROJ_TM = 2048
PROJ_TN = 512
PROJ_TN_WIDE = 1024
PROJ_ROWS = 512
N_PROMPT_TILES = PROMPT_TOKENS // PROJ_TM


class _Segment(NamedTuple):
    col_blk0: int
    n_blk: int
    epilogue: Callable
    by_tile_kind: bool
    outs: tuple
    pair_blk0: int = -1


def _segment_steps(segments, wide):
    spans, j0 = [], 0
    for seg in segments:
        if seg.pair_blk0 >= 0:
            assert wide == 2
            nj = seg.n_blk
        else:
            assert seg.col_blk0 % wide == 0 and seg.n_blk % wide == 0
            nj = seg.n_blk // wide
        spans.append((j0, nj))
        j0 += nj
    return spans


def _proj_kernel(*refs, segments, spans, n_w, n_extra, n_out, cast_steps):
    a_ref, w_refs = refs[0], refs[1:1 + n_w]
    extras = refs[1 + n_w:1 + n_w + n_extra]
    outs = refs[1 + n_w + n_extra:1 + n_w + n_extra + n_out]
    wbf_ref = refs[1 + n_w + n_extra + n_out]

    @pl.when(pl.program_id(1) == 0)
    def _():
        for h, w_ref in enumerate(w_refs):
            wbf_ref[:, h * PROJ_TN:(h + 1) * PROJ_TN] = w_ref[...].astype(BF16)

    if cast_steps:
        step = pl.program_id(0) * pl.num_programs(1) + pl.program_id(1)

        @pl.when(step < cast_steps)
        def _():
            outs[-1][...] = extras[-1][...].astype(BF16)

    def run(seg):
        seg_outs = [outs[k] for k in seg.outs]

        def body(latent):
            for rc in range(PROJ_TM // PROJ_ROWS):
                rows = slice(rc * PROJ_ROWS, (rc + 1) * PROJ_ROWS)
                accs = [jnp.dot(a_ref[rows, :], wbf_ref[:, col0:col0 + PROJ_TN], preferred_element_type=F32)
                        for col0 in range(0, wbf_ref.shape[1], PROJ_TN)]
                if seg.pair_blk0 >= 0:
                    seg.epilogue(*accs, extras, seg_outs, rc, latent)
                else:
                    for c, acc in enumerate(accs):
                        seg.epilogue(acc, extras, seg_outs, rc, c * PROJ_TN, latent)

        if seg.by_tile_kind:
            is_latent = pl.program_id(1) >= N_PROMPT_TILES
            pl.when(is_latent)(lambda: body(True))
            pl.when(jnp.logical_not(is_latent))(lambda: body(False))
        else:
            body(None)

    j = pl.program_id(0)
    for seg, (j0, nj) in zip(segments, spans):
        if len(segments) == 1:
            run(seg)
        else:
            pl.when(jnp.logical_and(j >= j0, j < j0 + nj))(functools.partial(run, seg))


def _proj(a, w, layer, segments, out_shapes, out_specs, name, extras=(), extra_specs=(), tn=PROJ_TN,
          run_after=None, cast=None, in_place=()):
    m, k = a.shape
    if run_after is not None:
        extras = [*extras, run_after]
        extra_specs = [*extra_specs, pl.BlockSpec(memory_space=pl.ANY)]
    wide = tn // PROJ_TN
    aliases = {}
    for array, out_index in in_place:
        aliases[1 + wide + len(extras)] = out_index
        extras = [*extras, array]
        extra_specs = [*extra_specs, pl.BlockSpec(memory_space=pl.ANY)]
    spans = _segment_steps(segments, wide)
    cast_steps = 0
    if cast is not None:
        w_out, out_layer, cast_steps = cast
        n_tiles = m // PROJ_TM
        assert cast_steps <= n_tiles * sum(nj for _, nj in spans)
        w_spec, wbf_spec, wbf_shape = _cast_slab_specs(
            w_out, out_layer, cast_steps, lambda j, i: jnp.minimum(j * n_tiles + i, cast_steps - 1))
        extras, extra_specs = [*extras, w_out], [*extra_specs, w_spec]
        out_shapes, out_specs = [*out_shapes, wbf_shape], [*out_specs, wbf_spec]

    def w_col(j, half):
        col = None
        for seg, (j0, _) in zip(segments, spans):
            if seg.pair_blk0 >= 0:
                c = (seg.pair_blk0 if half else seg.col_blk0) + j - j0
            else:
                c = seg.col_blk0 + wide * (j - j0) + half
            col = c if col is None else jnp.where(j >= j0, c, col)
        return col

    w_specs = [pl.BlockSpec((None, k, PROJ_TN), lambda j, i, half=half: (layer, 0, w_col(j, half)))
               for half in range(wide)]
    kern = functools.partial(_proj_kernel, segments=segments, spans=spans, n_w=wide, n_extra=len(extras),
                             n_out=len(out_shapes), cast_steps=cast_steps)
    return pl.pallas_call(
        kern,
        grid=(sum(nj for _, nj in spans), m // PROJ_TM),
        in_specs=[pl.BlockSpec((PROJ_TM, k), lambda j, i: (i, 0)), *w_specs, *extra_specs],
        out_specs=out_specs,
        out_shape=out_shapes,
        scratch_shapes=[pltpu.VMEM((k, tn), BF16)],
        input_output_aliases=aliases,
        compiler_params=_params(2),
        name=name,
    )(a, *[w] * wide, *extras)


def _full_out(n_blk):
    return ([jax.ShapeDtypeStruct((TOKENS, n_blk * PROJ_TN), BF16)],
            [pl.BlockSpec((PROJ_TM, PROJ_TN_WIDE), lambda j, i: (i, j))])


def _held(span, j, inside, before, after):
    j0, nj = span
    pick = lambda a, b, c: jnp.where(j < j0, b, jnp.where(j >= j0 + nj, c, a))
    return tuple(pick(a, b, c) for a, b, c in zip(inside, before, after))


def _split_out(span, n_blk, head_rows, layers=(1, 0)):
    prompt_tile = lambda i: jnp.minimum(i, N_PROMPT_TILES - 1)
    last_prompt, last_tile = N_PROMPT_TILES - 1, TOKENS // PROJ_TM - 1
    jj = lambda j: j - span[0]
    if head_rows:
        assert n_blk == 1
        heads = PROJ_TN // HEAD_DIM
        n_layers, layer = layers
        cache_shape = jax.ShapeDtypeStruct((N_PROMPT, n_layers * PROMPT_SEQ * heads, HEAD_DIM), F32)
        cache_spec = pl.BlockSpec(
            (PROJ_TM // PROMPT_SEQ, PROMPT_SEQ * heads, HEAD_DIM),
            lambda j, i: _held(span, j, (prompt_tile(i), layer, 0), (0, layer, 0), (last_prompt, layer, 0)))
    else:
        cache_shape = jax.ShapeDtypeStruct((N_PROMPT, PROMPT_SEQ, n_blk * PROJ_TN), F32)
        cache_spec = pl.BlockSpec(
            (PROJ_TM // PROMPT_SEQ, PROMPT_SEQ, PROJ_TN),
            lambda j, i: _held(span, j, (prompt_tile(i), 0, jj(j)), (0, 0, 0), (last_prompt, 0, n_blk - 1)))
    shapes = [cache_shape, jax.ShapeDtypeStruct((TOKENS, n_blk * PROJ_TN), BF16)]
    specs = [cache_spec,
             pl.BlockSpec((PROJ_TM, PROJ_TN),
                          lambda j, i: _held(span, j, (i, jj(j)), (0, 0), (last_tile, n_blk - 1)))]
    return shapes, specs


def _chunk_rows(rc):
    return slice(rc * PROJ_ROWS, (rc + 1) * PROJ_ROWS)


def _epi_silu(acc, extras, outs, rc, col0, latent):
    outs[0][_chunk_rows(rc), col0:col0 + PROJ_TN] = _silu(acc).astype(BF16)


def _epi_gelu(acc, extras, outs, rc, col0, latent):
    outs[0][_chunk_rows(rc), col0:col0 + PROJ_TN] = _gelu_tanh(acc).astype(BF16)


def _store_split(y, outs, rc, sl, latent):
    outs[1][_chunk_rows(rc), sl] = y.astype(BF16)
    if latent:
        return
    seqs = PROJ_ROWS // PROMPT_SEQ
    if outs[0].shape[-1] == HEAD_DIM:
        heads = PROJ_TN // HEAD_DIM
        col0 = sl.start or 0
        for c in range(y.shape[1] // HEAD_DIM):
            head_rows = pl.ds(col0 // HEAD_DIM + c, PROMPT_SEQ, stride=heads)
            for s in range(seqs):
                outs[0][rc * seqs + s, head_rows, :] = y[s * PROMPT_SEQ:(s + 1) * PROMPT_SEQ,
                                                         c * HEAD_DIM:(c + 1) * HEAD_DIM]
    else:
        outs[0][rc * seqs:(rc + 1) * seqs, :, sl] = y.reshape(seqs, PROMPT_SEQ, y.shape[1])


def _epi_value(acc, extras, outs, rc, col0, latent):
    _store_split(acc, outs, rc, slice(col0, col0 + PROJ_TN), latent)


def _head_rmsnorm(acc, hh, g):
    a = acc[:, hh * HEAD_DIM:(hh + 1) * HEAD_DIM]
    return a * lax.rsqrt(jnp.mean(a * a, axis=-1, keepdims=True) + EPS) * g


def _normed_heads(acc, extras, rc, rope):
    g = extras[0][...]
    ys = [_head_rmsnorm(acc, hh, g) for hh in range(PROJ_TN // HEAD_DIM)]
    if rope:
        rows = _chunk_rows(rc)
        cos, sin, swap = extras[1][rows, :], extras[2][rows, :], extras[3][...]
        for pair in range(len(ys) // 2):
            both = jnp.concatenate(ys[2 * pair:2 * pair + 2], axis=1).astype(BF16)
            partner = jnp.dot(both, swap, preferred_element_type=F32)
            for t in range(2):
                hh = 2 * pair + t
                ys[hh] = ys[hh] * cos + partner[:, t * HEAD_DIM:(t + 1) * HEAD_DIM] * sin
    for hh, y in enumerate(ys):
        yield slice(hh * HEAD_DIM, (hh + 1) * HEAD_DIM), y


def _epi_query(acc, extras, outs, rc, col0, latent, *, rope):
    for sl, y in _normed_heads(acc, extras, rc, rope and latent):
        outs[0][_chunk_rows(rc), col0 + sl.start:col0 + sl.stop] = (y * QUERY_SCALE).astype(BF16)


def _epi_key(acc, extras, outs, rc, col0, latent, *, rope):
    for sl, y in _normed_heads(acc, extras, rc, rope and latent):
        _store_split(y, outs, rc, slice(col0 + sl.start, col0 + sl.stop), latent)


def _rope_tables():
    nf = HEAD_DIM // 4
    t = np.arange(SAMPLE_TOKENS) % SAMPLE_SEQ
    row = (t // GRID_W).astype(np.float32)
    col = (t % GRID_W).astype(np.float32)
    inv = np.float32(ROPE_THETA) ** (-np.arange(nf, dtype=np.float32) / np.float32(nf))
    ang_r = row[:, None] * inv
    ang_c = col[:, None] * inv
    cos = np.concatenate([np.cos(ang_r), np.cos(ang_r), np.cos(ang_c), np.cos(ang_c)], axis=1)
    sin = np.concatenate([-np.sin(ang_r), np.sin(ang_r), -np.sin(ang_c), np.sin(ang_c)], axis=1)
    lanes = np.arange(2 * HEAD_DIM)
    swap = lanes[:, None] == (lanes[None, :] ^ (HEAD_DIM // 4))
    return jnp.asarray(cos, F32), jnp.asarray(sin, F32), jnp.asarray(swap, BF16)


def _gain_extras(gain, rope_tabs):
    extras = [gain.reshape(1, HEAD_DIM)]
    specs = [pl.BlockSpec((1, HEAD_DIM), lambda j, i: (0, 0))]
    if rope_tabs is not None:
        tab_map = lambda j, i: (jnp.maximum(i - N_PROMPT_TILES, 0), 0)
        extras += list(rope_tabs)
        specs += [pl.BlockSpec((PROJ_TM, HEAD_DIM), tab_map)] * 2
        specs.append(pl.BlockSpec((2 * HEAD_DIM, 2 * HEAD_DIM), lambda j, i: (0, 0)))
    return extras, specs


def _qz_proj(hn, w, layer, q_blk0, z_blk0, n_blk, gain, rope_tabs, name):
    extras, specs = _gain_extras(gain, rope_tabs)
    rope = rope_tabs is not None
    segments = (_Segment(q_blk0, n_blk, functools.partial(_epi_query, rope=rope), rope, (0,)),
                _Segment(z_blk0, n_blk, _epi_silu, False, (0,)))
    return _proj(hn, w, layer, segments, *_full_out(2 * n_blk), name, extras, specs, tn=PROJ_TN_WIDE)[0]


def _kv_proj(hn, w, layer, k_blk0, n_blk, gain, rope_tabs, head_rows, name, run_after=None, caches=None):
    n_layers = 1 if caches is None else caches[0].shape[1] // (PROMPT_SEQ * n_blk * PROJ_TN // HEAD_DIM)
    extras, specs = _gain_extras(gain, rope_tabs)
    segments = (_Segment(k_blk0, n_blk, functools.partial(_epi_key, rope=rope_tabs is not None), True, (0, 1)),
                _Segment(k_blk0 + n_blk, n_blk, _epi_value, True, (2, 3)))
    shapes, out_specs = [], []
    for span in _segment_steps(segments, 1):
        seg_shapes, seg_specs = _split_out(span, n_blk, head_rows, (n_layers, layer))
        shapes += seg_shapes
        out_specs += seg_specs
    in_place = () if caches is None else ((caches[0], 0), (caches[1], 2))
    return _proj(hn, w, layer, segments, shapes, out_specs, name, extras, specs, run_after=run_after,
                 in_place=in_place)


UVZ_CAST_STEPS = 32


def _epi_uz(acc_u, acc_z, extras, outs, rc, latent):
    outs[0][_chunk_rows(rc), :] = (_gelu_tanh(acc_u) * _silu(acc_z)).astype(BF16)


def _uzv_proj(hn, w, layer, n_blk, name, w_out):
    segments = (_Segment(0, n_blk, _epi_uz, False, (0,), pair_blk0=2 * n_blk),
                _Segment(n_blk, n_blk, _epi_gelu, False, (1,)))
    (uz_span, v_span) = _segment_steps(segments, PROJ_TN_WIDE // PROJ_TN)
    last_tile = TOKENS // PROJ_TM - 1
    shape = jax.ShapeDtypeStruct((TOKENS, n_blk * PROJ_TN), BF16)
    out_specs = [
        pl.BlockSpec((PROJ_TM, PROJ_TN),
                     lambda j, i: _held(uz_span, j, (i, j - uz_span[0]), (0, 0), (last_tile, uz_span[1] - 1))),
        pl.BlockSpec((PROJ_TM, PROJ_TN_WIDE),
                     lambda j, i: _held(v_span, j, (i, j - v_span[0]), (0, 0), (last_tile, v_span[1] - 1)))]
    return _proj(hn, w, layer, segments, [shape, shape], out_specs, name, tn=PROJ_TN_WIDE,
                 cast=(w_out, layer, UVZ_CAST_STEPS))


OUT_COLS = 512
OUT_TM = 512
OUT_TM_WIDE_K = 256


def _cast_slab_specs(w, layer, n_slabs, slab_of_step):
    k, n = w.shape[1], w.shape[2]
    rows = k // n_slabs
    return (pl.BlockSpec((None, rows, n), lambda *g: (layer, slab_of_step(*g), 0)),
            pl.BlockSpec((rows, n), lambda *g: (slab_of_step(*g), 0)),
            jax.ShapeDtypeStruct((k, n), BF16))


def _out_kernel(*refs, tile0, tm, split_a, split_x, fuse_norm):
    it = iter(refs)
    a_refs = [next(it), next(it)] if split_a else [next(it)]
    wbf_ref = next(it)
    x_refs = [next(it), next(it)] if split_x else [next(it)]
    gate_ref = next(it)
    if fuse_norm:
        g_ref, shift_ref, scale_ref = next(it), next(it), next(it)
    xnew_ref = next(it)
    hn_ref = next(it) if fuse_norm else None

    def body(a_ref, x_ref):
        ssq = jnp.zeros((tm, 1), F32)
        for cb in range(D_MODEL // OUT_COLS):
            sl = slice(cb * OUT_COLS, (cb + 1) * OUT_COLS)
            acc = jnp.dot(a_ref[...], wbf_ref[:, sl], preferred_element_type=F32)
            xn = x_ref[:, sl] + gate_ref[:, sl] * acc
            xnew_ref[:, sl] = xn
            if fuse_norm:
                ssq = ssq + jnp.sum(xn * xn, axis=-1, keepdims=True)
        if fuse_norm:
            rs = lax.rsqrt(ssq * (1.0 / D_MODEL) + EPS)
            for cb in range(D_MODEL // OUT_COLS):
                sl = slice(cb * OUT_COLS, (cb + 1) * OUT_COLS)
                gmul = g_ref[:, sl] * (1.0 + scale_ref[:, sl])
                hn_ref[:, sl] = (xnew_ref[:, sl] * rs * gmul + shift_ref[:, sl]).astype(BF16)

    if split_a or split_x:
        is_prompt = pl.program_id(0) + tile0 < PROMPT_TOKENS // tm
        pl.when(is_prompt)(lambda: body(a_refs[0], x_refs[0]))
        pl.when(jnp.logical_not(is_prompt))(lambda: body(a_refs[-1], x_refs[-1]))
    else:
        body(a_refs[0], x_refs[0])


def _out_proj(a, wbf, x, mod4, layer, norm_g3, name, tm, rows=None):
    split_a = isinstance(a, tuple)
    k = a[0].shape[1] if split_a else a.shape[1]
    tok0, n_tok = rows if rows is not None else (0, TOKENS)
    tile0 = tok0 // tm
    split_x = isinstance(x, tuple)
    fuse_norm = norm_g3 is not None
    mod_spec = lambda part, lyr: pl.BlockSpec(
        (None, None, 1, D_MODEL), lambda i: (lyr, _cond_row((i + tile0) * tm), 0, part))
    row_spec = lambda width: pl.BlockSpec((tm, width), lambda i: (i + tile0, 0))
    operands = [*a, wbf] if split_a else [a, wbf]
    in_specs = [*(_split_specs(tm, k, tile0) if split_a else [row_spec(k)]),
                pl.BlockSpec((k, D_MODEL), lambda i: (0, 0))]
    if split_x:
        assert rows is None
        operands += list(x)
        in_specs += _split_specs(tm, D_MODEL)
    else:
        operands.append(x)
        in_specs.append(row_spec(D_MODEL))
    operands.append(mod4)
    in_specs.append(mod_spec(2, layer))
    out_shapes = [jax.ShapeDtypeStruct((n_tok, D_MODEL), F32)]
    out_specs = [pl.BlockSpec((tm, D_MODEL), lambda i: (i, 0))]
    if fuse_norm:
        operands += [norm_g3, mod4, mod4]
        in_specs += [pl.BlockSpec((None, 1, D_MODEL), lambda i: (layer + 1, 0, 0)),
                     mod_spec(0, layer + 1), mod_spec(1, layer + 1)]
        out_shapes.append(jax.ShapeDtypeStruct((n_tok, D_MODEL), BF16))
        out_specs.append(pl.BlockSpec((tm, D_MODEL), lambda i: (i, 0)))
    kern = functools.partial(_out_kernel, tile0=tile0, tm=tm,
                             split_a=split_a, split_x=split_x, fuse_norm=fuse_norm)
    return pl.pallas_call(
        kern,
        grid=(n_tok // tm,),
        in_specs=in_specs,
        out_specs=out_specs,
        out_shape=out_shapes,
        compiler_params=_params(1),
        name=name,
    )(*operands)


def _dot_nt(a, b):
    return lax.dot_general(a, b, (((1,), (1,)), ((), ())), preferred_element_type=F32)


def _head(ref, h, rows=slice(None)):
    return ref[rows, h * HEAD_DIM:(h + 1) * HEAD_DIM]


def _with_ones(v):
    return jnp.concatenate([v, jnp.ones(v.shape, v.dtype)], axis=1)


def _lane_chunks(s):
    return [s[:, c * HEAD_DIM:(c + 1) * HEAD_DIM] for c in range(s.shape[1] // HEAD_DIM)]


def _softmax_pv(score_blocks, value_blocks, sink):
    rows = score_blocks[0].shape[0]
    mx = functools.reduce(jnp.maximum, [c for s in score_blocks for c in _lane_chunks(s)])
    m = jnp.broadcast_to(jnp.max(mx, axis=-1, keepdims=True), (rows, HEAD_DIM))
    if sink is not None:
        m = jnp.maximum(m, sink)
    o = None
    for s, v in zip(score_blocks, value_blocks):
        p = jnp.concatenate([jnp.exp2(c - m) for c in _lane_chunks(s)], axis=1).astype(BF16)
        part = jnp.dot(p, v, preferred_element_type=F32)
        o = part if o is None else o + part
    den = o[:, HEAD_DIM:]
    if sink is not None:
        den = den + jnp.exp2(sink - m)
    return o[:, :HEAD_DIM] * (1.0 / den)


PROMPT_REQUESTS = 2


def _attn_prompt_kernel(sink_ref, q_ref, k_ref, v_ref, z_ref, *rest, n_heads, n_kv, use_sink):
    o_ref = rest[-1]
    grp = n_heads // n_kv
    for r in range(PROMPT_REQUESTS):
        rows = slice(r * PROMPT_SEQ, (r + 1) * PROMPT_SEQ)
        for g in range(n_kv):
            kg = _head(k_ref, g, rows)
            vg = _with_ones(_head(v_ref, g, rows))
            for h in range(g * grp, (g + 1) * grp):
                s = _dot_nt(_head(q_ref, h, rows), kg)
                sink = jnp.full((PROMPT_SEQ, HEAD_DIM), sink_ref[h], F32) if use_sink else None
                o = _softmax_pv([s], [vg], sink)
                o_ref[rows, h * HEAD_DIM:(h + 1) * HEAD_DIM] = (
                    o * _head(z_ref, h, rows).astype(F32)).astype(BF16)


def _attn_prompt(sink, qz, k, v, n_kv, use_sink, name, run_after=None):
    ordering = [] if run_after is None else [run_after]
    width = qz.shape[1] // 2
    kern = functools.partial(_attn_prompt_kernel, n_heads=width // HEAD_DIM, n_kv=n_kv, use_sink=use_sink)
    rows = PROMPT_REQUESTS * PROMPT_SEQ
    cache_spec = pl.BlockSpec((rows, k.shape[1]), lambda b: (b, 0))
    return pl.pallas_call(
        kern,
        grid=(N_PROMPT // PROMPT_REQUESTS,),
        in_specs=[pl.BlockSpec(memory_space=pltpu.SMEM),
                  pl.BlockSpec((rows, width), lambda b: (b, 0)),
                  cache_spec, cache_spec,
                  pl.BlockSpec((rows, width), lambda b: (b, 1)),
                  *[pl.BlockSpec(memory_space=pl.ANY) for _ in ordering]],
        out_specs=pl.BlockSpec((rows, width), lambda b: (b, 0)),
        out_shape=jax.ShapeDtypeStruct((PROMPT_TOKENS, width), BF16),
        compiler_params=_params(1),
        name=name,
    )(sink, qz, k, v, qz, *ordering)


WIN_BAND = 3 * WINDOW


def _attn_win_kernel(sink_ref, q_ref, k_ref, v_ref, kc_ref, vc_ref, z_ref, w_ref, o_ref, wbf_ref, *,
                     n_heads, n_kv):
    wbf_ref[...] = w_ref[...].astype(BF16)
    grp = n_heads // n_kv
    n = pl.program_id(1)
    start = pl.multiple_of(jnp.clip((n - 1) * WINDOW, 0, SAMPLE_SEQ - WIN_BAND), WINDOW)
    shape = (grp * WINDOW, WIN_BAND)
    qpos = n * WINDOW + (lax.broadcasted_iota(jnp.int32, shape, 0) & (WINDOW - 1))
    kpos = start + lax.broadcasted_iota(jnp.int32, shape, 1)
    valid = jnp.abs(kpos - qpos) <= WINDOW
    band = pl.ds(start, WIN_BAND)
    for g in range(n_kv):
        heads = [g * grp + t for t in range(grp)]
        qs = jnp.concatenate([_head(q_ref, h) for h in heads], axis=0)
        kb = _head(k_ref, g, band)
        vb = _with_ones(_head(v_ref, g, band))
        ctx_rows = pl.ds(g, PAST_LEN, stride=n_kv)
        kc = kc_ref[ctx_rows, :].astype(BF16)
        vc = _with_ones(vc_ref[ctx_rows, :].astype(BF16))
        s_band = jnp.where(valid, _dot_nt(qs, kb), NEG_INF)
        s_ctx = _dot_nt(qs, kc)
        sink = jnp.concatenate([jnp.full((WINDOW, HEAD_DIM), sink_ref[h], F32) for h in heads], axis=0)
        o = _softmax_pv([s_band, s_ctx], [vb, vc], sink)
        for t, h in enumerate(heads):
            oh = o[t * WINDOW:(t + 1) * WINDOW]
            o_ref[:, h * HEAD_DIM:(h + 1) * HEAD_DIM] = (oh * _head(z_ref, h).astype(F32)).astype(BF16)


def _attn_win(sink, qz, k, v, cache_k, cache_v, layer_in_kind, n_kv, w_out):
    width = qz.shape[1] // 2
    kv_width = k.shape[1]
    blocks_per_seq = SAMPLE_SEQ // WINDOW
    q_map = lambda b, n: (PROMPT_TOKENS // WINDOW + b * blocks_per_seq + n, 0)
    z_map = lambda b, n: (PROMPT_TOKENS // WINDOW + b * blocks_per_seq + n, 1)
    kv_map = lambda b, n: (PROMPT_TOKENS // SAMPLE_SEQ + b, 0)
    cache_map = lambda b, n: (b, layer_in_kind, 0, 0)
    kern = functools.partial(_attn_win_kernel, n_heads=width // HEAD_DIM, n_kv=n_kv)
    w_spec, wbf_spec, wbf_shape = _cast_slab_specs(w_out, layer_in_kind, N_SAMPLE * blocks_per_seq,
                                                   lambda b, n: b * blocks_per_seq + n)
    return pl.pallas_call(
        kern,
        grid=(N_SAMPLE, blocks_per_seq),
        in_specs=[pl.BlockSpec(memory_space=pltpu.SMEM),
                  pl.BlockSpec((WINDOW, width), q_map),
                  pl.BlockSpec((SAMPLE_SEQ, kv_width), kv_map),
                  pl.BlockSpec((SAMPLE_SEQ, kv_width), kv_map),
                  pl.BlockSpec((None, None, PAST_LEN * n_kv, HEAD_DIM), cache_map),
                  pl.BlockSpec((None, None, PAST_LEN * n_kv, HEAD_DIM), cache_map),
                  pl.BlockSpec((WINDOW, width), z_map),
                  w_spec],
        out_specs=[pl.BlockSpec((WINDOW, width), lambda b, n: (b * blocks_per_seq + n, 0)), wbf_spec],
        out_shape=[jax.ShapeDtypeStruct((SAMPLE_TOKENS, width), BF16), wbf_shape],
        compiler_params=_params(2),
        name="attn_win_latent",
    )(sink, qz, k, v, cache_k, cache_v, qz, w_out)


NAT_QROWS = 4
GRID_ROWS = SAMPLE_SEQ // GRID_W
N_DR = 2 * NAT_ROWS - 1
MASKED_TILE = N_DR


def _nat_row_start(qr):
    return min(max(qr - NAT_ROWS // 2, 0), GRID_ROWS - NAT_ROWS)


def _attn_nat_kernel(bias_ref, q_ref, k_ref, v_ref, kc_ref, vc_ref, z_ref, w_ref, o_ref, wbf_ref,
                     left_ref, right_ref):
    wbf_ref[...] = w_ref[...].astype(BF16)

    def build_bias_tiles():
        shape = (GRID_W, 2 * GRID_W)
        lane = lax.broadcasted_iota(jnp.int32, shape, 1)
        qc = lax.broadcasted_iota(jnp.int32, shape, 0)
        kc_ = lane & (GRID_W - 1)
        cs = jnp.clip(qc - NAT_COLS // 2, 0, GRID_W - NAT_COLS)
        col_ok = jnp.logical_and(kc_ >= cs, kc_ < cs + NAT_COLS)
        is_left = lane < GRID_W
        lanes = 2 * GRID_W
        for dri in range(N_DR):
            row = jnp.broadcast_to(bias_ref[dri:dri + 1, :], shape)
            on_left = pltpu.roll(row, lanes - (NAT_COLS - 1), 1, stride=1, stride_axis=0)
            on_right = pltpu.roll(row, GRID_W - (NAT_COLS - 1), 1, stride=1, stride_axis=0)
            left_ref[dri] = jnp.where(is_left, jnp.where(col_ok, on_left, NEG_INF), 0.0)
            right_ref[dri] = jnp.where(is_left, 0.0, jnp.where(col_ok, on_right, NEG_INF))
        left_ref[MASKED_TILE] = jnp.where(is_left, NEG_INF, 0.0)
        right_ref[MASKED_TILE] = jnp.where(is_left, 0.0, NEG_INF)

    build_bias_tiles()
    n_heads = kc_ref.shape[1] // PAST_LEN
    ctx_rows = pl.ds(pl.program_id(0), PAST_LEN, stride=n_heads)
    for b, qb in [(b, qb) for b in range(N_SAMPLE) for qb in range(GRID_ROWS // NAT_QROWS)]:
        if qb == 0:
            kc = kc_ref[b, ctx_rows, :].astype(BF16)
            vc = _with_ones(vc_ref[b, ctx_rows, :].astype(BF16))
        qrows = range(qb * NAT_QROWS, (qb + 1) * NAT_QROWS)
        krow0 = _nat_row_start(qrows[0]) // 2 * 2
        krow1 = -(-(_nat_row_start(qrows[-1]) + NAT_ROWS) // 2) * 2
        tok0 = b * SAMPLE_SEQ
        kwin = slice(tok0 + krow0 * GRID_W, tok0 + krow1 * GRID_W)
        qwin = slice(tok0 + qrows[0] * GRID_W, tok0 + (qrows[-1] + 1) * GRID_W)

        def tile_index(qr, kr):
            rs = _nat_row_start(qr)
            return kr - qr + (NAT_ROWS - 1) if rs <= kr < rs + NAT_ROWS else MASKED_TILE

        bias = jnp.concatenate(
            [jnp.concatenate([left_ref[tile_index(qr, kr)] + right_ref[tile_index(qr, kr + 1)]
                              for kr in range(krow0, krow1, 2)], axis=1)
             for qr in qrows], axis=0)

        q = q_ref[qwin, :]
        s_nb = _dot_nt(q, k_ref[kwin, :]) + bias
        s_ctx = _dot_nt(q, kc)
        o = _softmax_pv([s_nb, s_ctx], [_with_ones(v_ref[kwin, :]), vc], None)
        o_ref[qwin, :] = (o * z_ref[qwin, :].astype(F32)).astype(BF16)


def _attn_nat(rel_bias, qz, k, v, cache_k, cache_v, layer_in_kind, w_out):
    n_heads = qz.shape[1] // (2 * HEAD_DIM)
    bias_rows = jnp.pad(rel_bias, ((0, 0), (0, N_DR + 1 - rel_bias.shape[1]),
                                   (0, 2 * GRID_W - rel_bias.shape[2])))
    latent_tile = PROMPT_TOKENS // SAMPLE_TOKENS
    latent_spec = pl.BlockSpec((SAMPLE_TOKENS, HEAD_DIM), lambda h: (latent_tile, h))
    z_spec = pl.BlockSpec((SAMPLE_TOKENS, HEAD_DIM), lambda h: (latent_tile, n_heads + h))
    cache_spec = pl.BlockSpec((N_SAMPLE, None, PAST_LEN * n_heads, HEAD_DIM),
                              lambda h: (0, layer_in_kind, 0, 0))
    w_spec, wbf_spec, wbf_shape = _cast_slab_specs(w_out, layer_in_kind, n_heads, lambda h: h)
    return pl.pallas_call(
        _attn_nat_kernel,
        grid=(n_heads,),
        in_specs=[pl.BlockSpec((None, N_DR + 1, 2 * GRID_W), lambda h: (h, 0, 0)),
                  latent_spec, latent_spec, latent_spec, cache_spec, cache_spec, z_spec, w_spec],
        out_specs=[pl.BlockSpec((SAMPLE_TOKENS, HEAD_DIM), lambda h: (0, h)), wbf_spec],
        out_shape=[jax.ShapeDtypeStruct((SAMPLE_TOKENS, n_heads * HEAD_DIM), BF16), wbf_shape],
        scratch_shapes=[pltpu.VMEM((N_DR + 1, GRID_W, 2 * GRID_W), F32),
                        pltpu.VMEM((N_DR + 1, GRID_W, 2 * GRID_W), F32)],
        compiler_params=_params(1),
        name="attn_nat_latent",
    )(bias_rows, qz, k, v, cache_k, cache_v, qz, w_out)


SPATIAL_TOKENS = 4 * CHUNK


def _spatial_kernel(uz_ref, v_ref, g_ref, b_ref, ws_ref, bs_ref, o_ref):
    for c in range(SPATIAL_TOKENS // CHUNK):
        rows = slice(c * CHUNK, (c + 1) * CHUNK)
        v = v_ref[rows, :].astype(F32)
        mu = jnp.mean(v, axis=-1, keepdims=True)
        vc = v - mu
        var = jnp.mean(vc * vc, axis=-1, keepdims=True)
        vn = (vc * lax.rsqrt(var + EPS) * g_ref[...] + b_ref[...]).astype(BF16)
        for g in range(GMLP_GROUPS):
            sl = slice(g * GMLP_GROUP_WIDTH, (g + 1) * GMLP_GROUP_WIDTH)
            sv = jnp.dot(ws_ref[g], vn[:, sl], preferred_element_type=F32) + bs_ref[:, g:g + 1]
            o_ref[rows, sl] = uz_ref[rows, sl] * sv.astype(BF16)


def _spatial(uz, v, ln_g, ln_b, w_s, b_s):
    row = pl.BlockSpec((SPATIAL_TOKENS, GMLP_WIDTH), lambda i: (i, 0))
    vec = pl.BlockSpec((1, GMLP_WIDTH), lambda i: (0, 0))
    return pl.pallas_call(
        _spatial_kernel,
        grid=(TOKENS // SPATIAL_TOKENS,),
        in_specs=[row, row, vec, vec,
                  pl.BlockSpec((GMLP_GROUPS, CHUNK, CHUNK), lambda i: (0, 0, 0)),
                  pl.BlockSpec((CHUNK, GMLP_GROUPS), lambda i: (0, 0))],
        out_specs=row,
        out_shape=jax.ShapeDtypeStruct((TOKENS, GMLP_WIDTH), BF16),
        compiler_params=_params(1),
        name="gmlp_spatial",
    )(uz, v, ln_g.reshape(1, -1), ln_b.reshape(1, -1), w_s.astype(BF16), b_s.T)


def kernel(x_prompt, x_sample, cache_win_k, cache_win_v, cache_nat_k, cache_nat_v, c, c_ctx,
           norm_g, w_ada, b_ada,
           win_w_in, win_q_norm, win_k_norm, win_sink, win_w_out,
           nat_w_in, nat_q_norm, nat_k_norm, nat_rel_bias, nat_w_out,
           gmlp_w_in, gmlp_ln_g, gmlp_ln_b, gmlp_w_s, gmlp_b_s, gmlp_w_out):
    depth = norm_g.shape[0]
    xp = x_prompt.reshape(PROMPT_TOKENS, D_MODEL)
    xs = x_sample.reshape(SAMPLE_TOKENS, D_MODEL)
    cond = jnp.zeros((COND_ROWS, D_MODEL), F32).at[:N_SAMPLE].set(c).at[CTX_COND_ROW].set(c_ctx)
    mod4 = _ada(cond, w_ada, b_ada).reshape(depth, COND_ROWS, 1, 3 * D_MODEL)
    norm_g3 = norm_g.reshape(depth, 1, D_MODEL)
    rope_tabs = _rope_tables()
    no_sink = jnp.zeros((1,), F32)

    n_win = win_w_in.shape[0]
    n_nat = nat_w_in.shape[0]
    win_kv_heads = cache_win_k.shape[3]
    win_kv_width = win_kv_heads * HEAD_DIM
    cwk = cache_win_k.reshape(N_SAMPLE, n_win, PAST_LEN * win_kv_heads, HEAD_DIM)
    cwv = cache_win_v.reshape(N_SAMPLE, n_win, PAST_LEN * win_kv_heads, HEAD_DIM)
    nat_heads = cache_nat_k.shape[3]
    cnk = cache_nat_k.reshape(N_SAMPLE, n_nat, PAST_LEN * nat_heads, HEAD_DIM)
    cnv = cache_nat_v.reshape(N_SAMPLE, n_nat, PAST_LEN * nat_heads, HEAD_DIM)

    new_nat_k, new_nat_v = [], []
    win_cache = jax.ShapeDtypeStruct((N_PROMPT, n_win * PROMPT_SEQ * win_kv_heads, HEAD_DIM), F32)
    new_win_k = jnp.zeros(win_cache.shape, win_cache.dtype)
    new_win_v = jnp.zeros(win_cache.shape, win_cache.dtype)
    x = (xp, xs)
    hn = _norm_mod(xp, xs, norm_g3, mod4, 0)
    for layer in range(depth):
        kind = layer % 3
        li = layer // 3
        if kind == 0:
            nq = D_MODEL // PROJ_TN
            nkv = win_kv_width // PROJ_TN
            qz = _qz_proj(hn, win_w_in, li, 0, nq + 2 * nkv, nq, win_q_norm[li], rope_tabs, "win_qz")
            new_win_k, ks, new_win_v, vs = _kv_proj(hn, win_w_in, li, nq, nkv, win_k_norm[li], rope_tabs, True,
                                                    "win_kv", run_after=qz, caches=(new_win_k, new_win_v))
            sink = win_sink[li] * LOG2E
            a_latent, wbf = _attn_win(sink, qz, ks, vs, cwk, cwv, li, win_kv_heads, win_w_out)
            a = (_attn_prompt(sink, qz, ks, vs, win_kv_heads, True, "attn_win_prompt"), a_latent)
            out_tm, out_name = OUT_TM, "win_out"
        elif kind == 1:
            nq = D_MODEL // PROJ_TN
            qz = _qz_proj(hn, nat_w_in, li, 0, 3 * nq, nq, nat_q_norm[li], None, "nat_qz")
            kp, ks, vp, vs = _kv_proj(hn, nat_w_in, li, nq, nq, nat_k_norm[li], None, False, "nat_kv",
                                      run_after=qz)
            new_nat_k.append(kp)
            new_nat_v.append(vp)
            a_latent, wbf = _attn_nat(nat_rel_bias[li] * LOG2E, qz, ks, vs, cnk, cnv, li, nat_w_out)
            a = (_attn_prompt(no_sink, qz, ks, vs, nat_heads, False, "attn_nat_prompt", run_after=a_latent),
                 a_latent)
            out_tm, out_name = OUT_TM, "nat_out"
        else:
            nw = GMLP_WIDTH // PROJ_TN
            uz, v, wbf = _uzv_proj(hn, gmlp_w_in, li, nw, "gmlp_uzv", gmlp_w_out)
            a = _spatial(uz, v, gmlp_ln_g[li], gmlp_ln_b[li], gmlp_w_s[li], gmlp_b_s[li])
            out_tm, out_name = OUT_TM_WIDE_K, "gmlp_out"
        if layer + 1 < depth:
            x, hn = _out_proj(a, wbf, x, mod4, layer, norm_g3, out_name, out_tm)
        else:
            (yp,) = _out_proj(a, wbf, x, mod4, layer, None, out_name + "_prompt", out_tm,
                              rows=(0, PROMPT_TOKENS))
            (ys,) = _out_proj(a, wbf, x, mod4, layer, None, out_name + "_latent", out_tm,
                              rows=(PROMPT_TOKENS, SAMPLE_TOKENS))

    cache_shape = lambda layers, heads: jnp.stack(
        [c_.reshape(N_PROMPT, PROMPT_SEQ, heads, HEAD_DIM) for c_ in layers], axis=1)
    win_shape = (N_PROMPT, n_win, PROMPT_SEQ, win_kv_heads, HEAD_DIM)
    return (yp.reshape(N_PROMPT, PROMPT_SEQ, D_MODEL), ys.reshape(N_SAMPLE, SAMPLE_SEQ, D_MODEL),
            new_win_k.reshape(win_shape), new_win_v.reshape(win_shape),
            cache_shape(new_nat_k, nat_heads), cache_shape(new_nat_v, nat_heads))
```
